```python
import math
import jax
import jax.numpy as jnp
from jax import lax
import numpy as np

D_MODEL = 1024
BATCH = 4
SEQ = 8192
DEPTH = 2

HEAD_DIM = 64
Q_BLOCK = 128
N_BRANCH = 4
BRANCH_WIDTH = 256

MOBA_HEADS = 4
MOBA_BLOCK = 256
MOBA_TOPK = 3

DIFF_HEADS = 4
DIFF_QK_DIM = 32
DIFF_V_DIM = 2 * DIFF_QK_DIM

SB_HEADS = 4

SWA_Q_HEADS = 4
SWA_KV_HEADS = 2
SWA_WINDOW = 128

REL_BUCKETS = 32
REL_MAX_DIST = 128
N_BIAS_HEADS = MOBA_HEADS + DIFF_HEADS + SWA_Q_HEADS

N_GROUPS = 4
EXPERTS_PER_GROUP = 8
N_EXPERTS = N_GROUPS * EXPERTS_PER_GROUP
EXPERT_TOPK = 2
D_EXPERT = 512
MOE_BLOCK = 128

NORM_EPS = 1e-6

MOBA_COLS = 3 * MOBA_HEADS * HEAD_DIM
DIFF_COLS = DIFF_HEADS * (4 * DIFF_QK_DIM + DIFF_V_DIM)
SB_COLS = 3 * SB_HEADS * HEAD_DIM
SWA_COLS = (SWA_Q_HEADS + 2 * SWA_KV_HEADS) * HEAD_DIM
IN_COLS = MOBA_COLS + DIFF_COLS + SB_COLS + SWA_COLS

kernel_name = "hybrid_gated_moba_diff_stickbreak_swa_hmoe"


def rmsnorm(x, g):
    x32 = x.astype(jnp.float32)
    y = x32 * lax.rsqrt(jnp.mean(x32 * x32, axis=-1, keepdims=True) + NORM_EPS)
    return (y * g.astype(jnp.float32)).astype(x.dtype)


def rel_bucket(dist):
    n = jnp.maximum(dist, 0)
    max_exact = REL_BUCKETS // 2
    nf = jnp.maximum(n, 1).astype(jnp.float32)
    large = max_exact + (jnp.log(nf / max_exact) / math.log(REL_MAX_DIST / max_exact)
                         * (REL_BUCKETS - max_exact)).astype(jnp.int32)
    large = jnp.minimum(large, REL_BUCKETS - 1)
    return jnp.where(n < max_exact, n, large)


def head_bias(tab, dist):
    return jnp.take(tab, rel_bucket(dist), axis=1).astype(jnp.float32)


def moba_attention(q, k, v, tab):
    B, S, H, d = q.shape
    nb = -(-S // MOBA_BLOCK)
    pad = nb * MOBA_BLOCK - S
    q = q.transpose(0, 2, 1, 3)
    k = jnp.pad(k.transpose(0, 2, 1, 3), ((0, 0), (0, 0), (0, pad), (0, 0)))
    v = jnp.pad(v.transpose(0, 2, 1, 3), ((0, 0), (0, 0), (0, pad), (0, 0)))
    kblk = k.reshape(B, H, nb, MOBA_BLOCK, d)
    vblk = v.reshape(B, H, nb, MOBA_BLOCK, d)
    kmean = jnp.mean(kblk.astype(jnp.float32), axis=3).astype(k.dtype)
    topk = min(MOBA_TOPK, nb)
    scale = d ** -0.5
    bi = jnp.arange(B)[:, None, None, None]
    hi = jnp.arange(H)[None, :, None, None]
    in_blk = jnp.arange(MOBA_BLOCK)

    def chunk(c):
        t0 = c * Q_BLOCK
        qc = lax.dynamic_slice_in_dim(q, t0, Q_BLOCK, axis=2)
        tpos = t0 + jnp.arange(Q_BLOCK)
        own = t0 // MOBA_BLOCK
        gate = jnp.einsum('bhqd,bhnd->bhqn', qc, kmean).astype(jnp.float32)
        gate = jnp.where(jnp.arange(nb) < own, gate, -jnp.inf)
        gval, gidx = lax.top_k(gate, topk)
        ks = kblk[bi, hi, gidx]
        vs = vblk[bi, hi, gidx]
        s_sel = jnp.einsum('bhqd,bhqjkd->bhqjk', qc, ks).astype(jnp.float32) * scale
        kpos = gidx[..., None] * MOBA_BLOCK + in_blk
        s_sel = s_sel + tab[hi[..., None], rel_bucket(tpos[:, None, None] - kpos)].astype(jnp.float32)
        s_sel = jnp.where(jnp.isfinite(gval)[..., None], s_sel, -jnp.inf)
        ko = lax.dynamic_slice_in_dim(k, own * MOBA_BLOCK, MOBA_BLOCK, axis=2)
        vo = lax.dynamic_slice_in_dim(v, own * MOBA_BLOCK, MOBA_BLOCK, axis=2)
        dist = tpos[:, None] - (own * MOBA_BLOCK + in_blk)[None, :]
        s_own = jnp.einsum('bhqd,bhkd->bhqk', qc, ko).astype(jnp.float32) * scale + head_bias(tab, dist)[None]
        s_own = jnp.where(dist >= 0, s_own, -jnp.inf)
        logits = jnp.concatenate([s_sel.reshape(B, H, Q_BLOCK, topk * MOBA_BLOCK), s_own], axis=-1)
        p = jax.nn.softmax(logits, axis=-1).astype(v.dtype)
        p_sel = p[..., :topk * MOBA_BLOCK].reshape(B, H, Q_BLOCK, topk, MOBA_BLOCK)
        p_own = p[..., topk * MOBA_BLOCK:]
        return (jnp.einsum('bhqjk,bhqjkd->bhqd', p_sel, vs)
                + jnp.einsum('bhqk,bhkd->bhqd', p_own, vo))

    out = lax.map(chunk, jnp.arange(S // Q_BLOCK))
    return out.transpose(1, 0, 3, 2, 4).reshape(B, S, H * d)


def diff_attention(q1, q2, k1, k2, v, lam, lam_init, subln_g, tab):
    B, S, H, dq = q1.shape
    dv = v.shape[-1]
    scale = dq ** -0.5
    kpos = jnp.arange(S)

    def chunk(c):
        t0 = c * Q_BLOCK
        dist = (t0 + jnp.arange(Q_BLOCK))[:, None] - kpos[None, :]
        causal = dist >= 0
        bias = head_bias(tab, dist)

        def probs(qa, ka):
            qc = lax.dynamic_slice_in_dim(qa, t0, Q_BLOCK, axis=1)
            s = jnp.einsum('bqhd,bkhd->bhqk', qc, ka).astype(jnp.float32) * scale + bias
            return jax.nn.softmax(jnp.where(causal, s, -jnp.inf), axis=-1)

        a = probs(q1, k1) - lam * probs(q2, k2)
        return jnp.einsum('bhqk,bkhd->bqhd', a.astype(v.dtype), v)

    out = lax.map(chunk, jnp.arange(S // Q_BLOCK))
    out = out.transpose(1, 0, 2, 3, 4).reshape(B, S, H, dv)
    out = rmsnorm(out, subln_g) * (1.0 - lam_init)
    return out.reshape(B, S, H * dv)


def stick_breaking_attention(q, k, v):
    B, S, H, d = q.shape
    scale = d ** -0.5
    kpos = jnp.arange(S)

    def chunk(c):
        t0 = c * Q_BLOCK
        qc = lax.dynamic_slice_in_dim(q, t0, Q_BLOCK, axis=1)
        strict = (t0 + jnp.arange(Q_BLOCK))[:, None] > kpos[None, :]
        z = jnp.einsum('bqhd,bkhd->bhqk', qc, k).astype(jnp.float32) * scale
        log_keep = jnp.where(strict, jax.nn.log_sigmoid(-z), 0.0)
        later = lax.cumsum(log_keep, axis=3, reverse=True) - log_keep
        w = jnp.where(strict, jnp.exp(jax.nn.log_sigmoid(z) + later), 0.0)
        return jnp.einsum('bhqk,bkhd->bqhd', w.astype(v.dtype), v)

    out = lax.map(chunk, jnp.arange(S // Q_BLOCK))
    return out.transpose(1, 0, 2, 3, 4).reshape(B, S, H * d)


def sliding_window_attention(q, k, v, sinks, tab):
    B, S, Hq, d = q.shape
    Hkv = k.shape[2]
    G = Hq // Hkv
    nb = S // Q_BLOCK
    qb = q.reshape(B, nb, Q_BLOCK, Hkv, G, d)

    def band(t):
        tb = t.reshape(B, nb, Q_BLOCK, Hkv, d)
        prev = jnp.pad(tb[:, :-1], ((0, 0), (1, 0), (0, 0), (0, 0), (0, 0)))
        return jnp.concatenate([prev, tb], axis=2)

    kb, vb = band(k), band(v)
    s = jnp.einsum('bnqkgd,bnskd->bnkgqs', qb, kb).astype(jnp.float32) * d ** -0.5
    qi = jnp.arange(Q_BLOCK)[:, None]
    sj = jnp.arange(2 * Q_BLOCK)[None, :]
    dist = qi + Q_BLOCK - sj
    in_win = (dist >= 0) & (dist < SWA_WINDOW)
    key_abs = jnp.arange(nb)[:, None, None] * Q_BLOCK + sj[None] - Q_BLOCK
    valid = in_win[None] & (key_abs >= 0)
    s = s + head_bias(tab, dist).reshape(Hkv, G, Q_BLOCK, 2 * Q_BLOCK)[None, None]
    s = jnp.where(valid[None, :, None, None], s, -jnp.inf)
    sink = sinks.astype(jnp.float32).reshape(1, 1, Hkv, G, 1, 1)
    m = jnp.maximum(jnp.max(s, axis=-1, keepdims=True), sink)
    e = jnp.exp(s - m)
    p = e / (jnp.sum(e, axis=-1, keepdims=True) + jnp.exp(sink - m))
    o = jnp.einsum('bnkgqs,bnskd->bnqkgd', p.astype(v.dtype), vb)
    return o.reshape(B, S, Hq * d)


def hierarchical_moe(h, w_rg, b_rg, w_re, b_re, w1, w3, w2):
    B, S, D = h.shape
    xf = h.reshape(-1, D)
    N = xf.shape[0]
    g_logits = (xf @ w_rg).astype(jnp.float32) + b_rg.astype(jnp.float32)
    g_prob = jax.nn.softmax(g_logits, axis=-1)
    grp = jnp.argmax(g_logits, axis=-1)
    p_grp = jnp.take_along_axis(g_prob, grp[:, None], axis=1)
    e_logits = ((xf @ w_re).astype(jnp.float32) + b_re.astype(jnp.float32)).reshape(N, N_GROUPS, EXPERTS_PER_GROUP)
    e_logits = jnp.take_along_axis(e_logits, grp[:, None, None], axis=1)[:, 0]
    top_val, top_idx = lax.top_k(e_logits, EXPERT_TOPK)
    gate = p_grp * jax.nn.softmax(top_val, axis=-1)
    expert = grp[:, None] * EXPERTS_PER_GROUP + top_idx

    A = N * EXPERT_TOPK
    eid = expert.reshape(-1)
    order = jnp.argsort(eid)
    eid_s = eid[order]
    tok_s = order // EXPERT_TOPK
    gate_s = gate.reshape(-1)[order]
    counts = jnp.bincount(eid, length=N_EXPERTS)
    padded = (counts + MOE_BLOCK - 1) // MOE_BLOCK * MOE_BLOCK
    pad_end = jnp.cumsum(padded)
    pad_start = pad_end - padded
    start = jnp.cumsum(counts) - counts
    dest = pad_start[eid_s] + jnp.arange(A) - start[eid_s]
    R = ((A + MOE_BLOCK - 1) // MOE_BLOCK + N_EXPERTS) * MOE_BLOCK
    nblk = R // MOE_BLOCK
    xbuf = jnp.zeros((R, D), xf.dtype).at[dest].set(xf[tok_s])
    blk_expert = jnp.minimum(jnp.searchsorted(pad_end, jnp.arange(nblk) * MOE_BLOCK, side='right'), N_EXPERTS - 1)

    def expert_block(args):
        xb, e = args
        return (jax.nn.silu(xb @ w1[e]) * (xb @ w3[e])) @ w2[e]

    ybuf = lax.map(expert_block, (xbuf.reshape(nblk, MOE_BLOCK, D), blk_expert)).reshape(R, D)
    y = ybuf[dest] * gate_s[:, None].astype(ybuf.dtype)
    out = jnp.zeros((N, D), h.dtype).at[tok_s].add(y)
    return out.reshape(B, S, D)


def setup_inputs(seed: int = 0) -> dict:
    key = jax.random.key(seed)
    ks = jax.random.split(key, 24)

    def nrm(k, shape, scale):
        return jax.random.normal(k, shape, jnp.float32) * scale

    return {
        "x": nrm(ks[0], (BATCH, SEQ, D_MODEL), 1.0),
        "rel_bias": nrm(ks[1], (REL_BUCKETS, N_BIAS_HEADS), 0.5),
        "g_mix": 1.0 + nrm(ks[2], (DEPTH, D_MODEL), 0.02),
        "w_in": nrm(ks[3], (DEPTH, D_MODEL, IN_COLS), D_MODEL ** -0.5),
        "diff_lq1": nrm(ks[4], (DEPTH, DIFF_QK_DIM), 0.1),
        "diff_lk1": nrm(ks[5], (DEPTH, DIFF_QK_DIM), 0.1),
        "diff_lq2": nrm(ks[6], (DEPTH, DIFF_QK_DIM), 0.1),
        "diff_lk2": nrm(ks[7], (DEPTH, DIFF_QK_DIM), 0.1),
        "diff_subln": 1.0 + nrm(ks[8], (DEPTH, DIFF_V_DIM), 0.02),
        "swa_sinks": nrm(ks[9], (DEPTH, SWA_Q_HEADS), 1.0),
        "w_gate": nrm(ks[10], (DEPTH, N_BRANCH, D_MODEL, D_MODEL), D_MODEL ** -0.5),
        "w_br": nrm(ks[11], (DEPTH, N_BRANCH, BRANCH_WIDTH, D_MODEL), BRANCH_WIDTH ** -0.5),
        "w_o": nrm(ks[12], (DEPTH, D_MODEL, D_MODEL), D_MODEL ** -0.5),
        "g_ffn": 1.0 + nrm(ks[13], (DEPTH, D_MODEL), 0.02),
        "w_route_group": nrm(ks[14], (DEPTH, D_MODEL, N_GROUPS), D_MODEL ** -0.5),
        "b_route_group": nrm(ks[15], (DEPTH, N_GROUPS), 0.01),
        "w_route_expert": nrm(ks[16], (DEPTH, D_MODEL, N_EXPERTS), D_MODEL ** -0.5),
        "b_route_expert": nrm(ks[17], (DEPTH, N_EXPERTS), 0.01),
        "w1": nrm(ks[18], (DEPTH, N_EXPERTS, D_MODEL, D_EXPERT), D_MODEL ** -0.5),
        "w3": nrm(ks[19], (DEPTH, N_EXPERTS, D_MODEL, D_EXPERT), D_MODEL ** -0.5),
        "w2": nrm(ks[20], (DEPTH, N_EXPERTS, D_EXPERT, D_MODEL), D_EXPERT ** -0.5),
        "g_final": 1.0 + nrm(ks[21], (D_MODEL,), 0.02),
    }


def reference(x, rel_bias, g_mix, w_in, diff_lq1, diff_lk1, diff_lq2, diff_lk2, diff_subln,
              swa_sinks, w_gate, w_br, w_o, g_ffn, w_route_group, b_route_group,
              w_route_expert, b_route_expert, w1, w3, w2, g_final):
    B, S, _ = x.shape
    tab = rel_bias.T
    tab_moba = tab[:MOBA_HEADS]
    tab_diff = tab[MOBA_HEADS:MOBA_HEADS + DIFF_HEADS]
    tab_swa = tab[MOBA_HEADS + DIFF_HEADS:]
    dq = DIFF_HEADS * DIFF_QK_DIM
    for l in range(DEPTH):
        h = rmsnorm(x, g_mix[l])
        proj = h @ w_in[l]
        pa, pb, pc, pd = jnp.split(proj, [MOBA_COLS, MOBA_COLS + DIFF_COLS,
                                          MOBA_COLS + DIFF_COLS + SB_COLS], axis=-1)
        qa, ka, va = [t.reshape(B, S, MOBA_HEADS, HEAD_DIM) for t in jnp.split(pa, 3, axis=-1)]
        o_a = moba_attention(qa, ka, va, tab_moba)
        q1, q2, k1, k2, vb = jnp.split(pb, [dq, 2 * dq, 3 * dq, 4 * dq], axis=-1)
        q1, q2, k1, k2 = [t.reshape(B, S, DIFF_HEADS, DIFF_QK_DIM) for t in (q1, q2, k1, k2)]
        vb = vb.reshape(B, S, DIFF_HEADS, DIFF_V_DIM)
        lam_init = 0.8 - 0.6 * math.exp(-0.3 * l)
        lam = (jnp.exp(jnp.sum(diff_lq1[l].astype(jnp.float32) * diff_lk1[l].astype(jnp.float32)))
               - jnp.exp(jnp.sum(diff_lq2[l].astype(jnp.float32) * diff_lk2[l].astype(jnp.float32)))
               + lam_init)
        o_b = diff_attention(q1, q2, k1, k2, vb, lam, lam_init, diff_subln[l], tab_diff)
        qc, kc, vc = [t.reshape(B, S, SB_HEADS, HEAD_DIM) for t in jnp.split(pc, 3, axis=-1)]
        o_c = stick_breaking_attention(qc, kc, vc)
        qd, kd, vd = jnp.split(pd, [SWA_Q_HEADS * HEAD_DIM, (SWA_Q_HEADS + SWA_KV_HEADS) * HEAD_DIM], axis=-1)
        o_d = sliding_window_attention(qd.reshape(B, S, SWA_Q_HEADS, HEAD_DIM),
                                       kd.reshape(B, S, SWA_KV_HEADS, HEAD_DIM),
                                       vd.reshape(B, S, SWA_KV_HEADS, HEAD_DIM),
                                       swa_sinks[l], tab_swa)
        branches = (o_a, o_b, o_c, o_d)
        merged = jax.nn.sigmoid(h @ w_gate[l, 0]) * (branches[0] @ w_br[l, 0])
        for i in range(1, N_BRANCH):
            merged = merged + jax.nn.sigmoid(h @ w_gate[l, i]) * (branches[i] @ w_br[l, i])
        x = x + merged @ w_o[l]
        x = x + hierarchical_moe(rmsnorm(x, g_ffn[l]), w_route_group[l], b_route_group[l],
                                 w_route_expert[l], b_route_expert[l], w1[l], w3[l], w2[l])
    return rmsnorm(x, g_final)
```

```python
import functools
import math

import numpy as np
import jax
import jax.numpy as jnp
from jax import lax
from jax.experimental import pallas as pl
from jax.experimental.pallas import tpu as pltpu

F32 = jnp.float32
BF16 = jnp.bfloat16

HEAD_DIM = 64
N_HEADS = 4
DIFF_QK = 32
SWA_KV_HEADS = 2
SWA_WINDOW = 128
MOBA_BLOCK = 256
MOBA_TOPK = 3
REL_BUCKETS = 32
REL_MAX_DIST = 128
N_GROUPS = 4
EXPERTS_PER_GROUP = 8
N_EXPERTS = N_GROUPS * EXPERTS_PER_GROUP
NORM_EPS = 1e-6

TB = 256
LANES = 128
ONES_ROWS = 16
NEG = -1e30
SB_EXIT = 104.0
MOE_ROWS = 256
VMEM_LIMIT = 56 * 1024 * 1024

QV_MOBA_Q, QV_MOBA_V, QV_DIFF_Q, QV_DIFF_V, QV_SB_Q, QV_SB_V, QV_SWA_Q = range(7)
QV_SWA_V_128 = 14
QV_ROWS = 7 * 256 + 128
KN_MOBA, KN_DIFF, KN_SB = range(3)
KN_SWA_128 = 6
KN_COLS = 3 * 256 + 128


def _cparams(n_grid):
    return pltpu.CompilerParams(dimension_semantics=("arbitrary",) * n_grid,
                                vmem_limit_bytes=VMEM_LIMIT)


def _rel_bucket_np(n):
    n = np.maximum(n, 0)
    max_exact = REL_BUCKETS // 2
    nf = np.maximum(n, 1).astype(np.float64)
    large = max_exact + (np.log(nf / max_exact) / math.log(REL_MAX_DIST / max_exact)
                         * (REL_BUCKETS - max_exact)).astype(np.int64)
    large = np.minimum(large, REL_BUCKETS - 1)
    return np.where(n < max_exact, n, large)


def _first_far_distance():
    d = np.arange(0, 4 * REL_MAX_DIST)
    b = _rel_bucket_np(d)
    return int(np.min(d[b == REL_BUCKETS - 1]))


def _causal_bias_tiles(tab):
    assert _first_far_distance() <= TB + 1
    s = np.arange(TB)[:, None]
    q = np.arange(TB)[None, :]
    tiles = []
    for d in range(2):
        n = d * TB + q - s
        idx = _rel_bucket_np(n)
        t = jnp.take(tab, jnp.asarray(idx.reshape(-1)), axis=1).reshape(tab.shape[0], TB, TB)
        t = t - tab[:, REL_BUCKETS - 1][:, None, None]
        tiles.append(jnp.where(jnp.asarray(n >= 0)[None], t, NEG))
    return jnp.stack(tiles, axis=1).astype(F32)


def _swa_bias_tiles(tab):
    q = np.arange(TB)[None, :]
    s = np.arange(TB)[:, None]
    n_cur = q - s
    sp = np.arange(SWA_WINDOW)[:, None]
    n_prev = q + SWA_WINDOW - sp
    out = []
    for n in (n_cur, n_prev):
        ok = (n >= 0) & (n < SWA_WINDOW)
        t = jnp.take(tab, jnp.asarray(_rel_bucket_np(n).reshape(-1)), axis=1).reshape((tab.shape[0],) + n.shape)
        out.append(jnp.where(jnp.asarray(ok)[None], t, NEG).astype(F32))
    return out


def _pad_rows(q, off, total):
    n, t = q.shape
    parts = []
    if off:
        parts.append(jnp.zeros((off, t), q.dtype))
    parts.append(q)
    if total - off - n:
        parts.append(jnp.zeros((total - off - n, t), q.dtype))
    return jnp.concatenate(parts, axis=0) if len(parts) > 1 else q


def _with_ones(v):
    return jnp.concatenate([v, jnp.ones((ONES_ROWS, v.shape[1]), v.dtype)], axis=0)


def _rms(x, g_row):
    ms = jnp.mean(x * x, axis=-1, keepdims=True)
    return x * lax.rsqrt(ms + NORM_EPS) * g_row


IN_T = 512
IN_CHUNK = 384


def _inproj_kernel(x_ref, g_ref, wn_ref, wt_ref, kn_ref, qvt_ref):
    h = _rms(x_ref[...], g_ref[...]).astype(BF16)
    kn_ref[...] = jnp.dot(h, wn_ref[...], preferred_element_type=F32).astype(BF16)
    for r0 in range(0, QV_ROWS, IN_CHUNK):
        pt = lax.dot_general(wt_ref[r0:r0 + IN_CHUNK, :], h, (((1,), (1,)), ((), ())),
                             preferred_element_type=F32)
        for s in range(IN_T // TB):
            qvt_ref[s, r0:r0 + IN_CHUNK, :] = pt[:, s * TB:(s + 1) * TB].astype(BF16)


def _inproj(x2, g, wn, wt):
    n, d = x2.shape
    return pl.pallas_call(
        _inproj_kernel,
        grid=(n // IN_T,),
        in_specs=[pl.BlockSpec((IN_T, d), lambda i: (i, 0)),
                  pl.BlockSpec((1, d), lambda i: (0, 0)),
                  pl.BlockSpec((d, KN_COLS), lambda i: (0, 0)),
                  pl.BlockSpec((QV_ROWS, d), lambda i: (0, 0))],
        out_specs=[pl.BlockSpec((IN_T, KN_COLS), lambda i: (i, 0)),
                   pl.BlockSpec((IN_T // TB, QV_ROWS, TB), lambda i: (i, 0, 0))],
        out_shape=[jax.ShapeDtypeStruct((n, KN_COLS), BF16),
                   jax.ShapeDtypeStruct((n // TB, QV_ROWS, TB), BF16)],
        compiler_params=_cparams(1),
        name="inproj",
    )(x2, g, wn, wt)


def _flash_step(c, kblk, vext, add, qpad_scr, m_scr, acc_scr):
    s = jnp.dot(kblk, qpad_scr[c], preferred_element_type=F32)
    if add is not None:
        s = s + add
    m_old = m_scr[c]
    m_new = jnp.maximum(m_old, jnp.max(s, axis=0, keepdims=True))
    p = jnp.exp(s - m_new).astype(BF16)
    alpha = jnp.exp(m_old - m_new)
    acc_scr[c] = acc_scr[c] * alpha + jnp.dot(vext, p, preferred_element_type=F32)
    m_scr[c] = m_new


def _flash_init(n_chain, m_scr, acc_scr):
    for c in range(n_chain):
        m_scr[c] = jnp.full(m_scr.shape[1:], NEG, F32)
        acc_scr[c] = jnp.zeros(acc_scr.shape[1:], F32)


def _diff_kernel(lam_init, qt_ref, k_ref, vt_ref, tile_ref, lq1_ref, lk1_ref, lq2_ref, lk2_ref,
                 subln_ref, o_ref, qpad_scr, m_scr, acc_scr):
    i = pl.program_id(1)
    n_chain = 2 * N_HEADS
    for h in range(N_HEADS):
        for mp in range(2):
            r0 = h * HEAD_DIM + mp * DIFF_QK
            qpad_scr[2 * h + mp] = _pad_rows(qt_ref[0, r0:r0 + DIFF_QK, :],
                                             (h % 2) * HEAD_DIM + mp * DIFF_QK, LANES)
    _flash_init(n_chain, m_scr, acc_scr)

    def block(j, tile_idx):
        row = pl.multiple_of(j * TB, TB)
        for h in range(N_HEADS):
            g = h // 2
            kblk = k_ref[pl.ds(row, TB), g * LANES:(g + 1) * LANES]
            vext = _with_ones(vt_ref[j, h * HEAD_DIM:(h + 1) * HEAD_DIM, :])
            add = None if tile_idx is None else tile_ref[h, tile_idx]
            for mp in range(2):
                _flash_step(2 * h + mp, kblk, vext, add, qpad_scr, m_scr, acc_scr)

    block(i, 0)

    @pl.when(i >= 1)
    def _():
        block(i - 1, 1)

    def far(j, carry):
        block(j, None)
        return carry

    lax.fori_loop(0, jnp.maximum(i - 1, 0), far, 0)

    lam = (jnp.exp(jnp.sum(lq1_ref[...] * lk1_ref[...], keepdims=True))
           - jnp.exp(jnp.sum(lq2_ref[...] * lk2_ref[...], keepdims=True)) + lam_init)
    for h in range(N_HEADS):
        a1 = acc_scr[2 * h]
        a2 = acc_scr[2 * h + 1]
        o1 = a1[:HEAD_DIM] / a1[HEAD_DIM:HEAD_DIM + 1]
        o2 = a2[:HEAD_DIM] / a2[HEAD_DIM:HEAD_DIM + 1]
        a = o1 - lam * o2
        ms = jnp.mean(a * a, axis=0, keepdims=True)
        y = a * lax.rsqrt(ms + NORM_EPS) * subln_ref[...] * (1.0 - lam_init)
        o_ref[0, h * HEAD_DIM:(h + 1) * HEAD_DIM, :] = y.astype(BF16)


def _attn_common_specs(nb, q_blk, k_blk, v_blk):
    return [pl.BlockSpec((1, TB, TB), lambda b, i: (b * nb + i, q_blk, 0)),
            pl.BlockSpec((nb * TB, TB), lambda b, i: (b, k_blk)),
            pl.BlockSpec((nb, TB, TB), lambda b, i: (b, v_blk, 0))]


def _small_spec(shape):
    return pl.BlockSpec(shape, lambda b, i: (0,) * len(shape))


def _diff_attention(kn, qvt, tiles, lq1, lk1, lq2, lk2, subln, lam_init, batch):
    nblk = qvt.shape[0]
    nb = nblk // batch
    n_chain = 2 * N_HEADS
    return pl.pallas_call(
        functools.partial(_diff_kernel, lam_init),
        grid=(batch, nb),
        in_specs=_attn_common_specs(nb, QV_DIFF_Q, KN_DIFF, QV_DIFF_V) + [
            _small_spec((N_HEADS, 2, TB, TB)),
            _small_spec((1, DIFF_QK)), _small_spec((1, DIFF_QK)),
            _small_spec((1, DIFF_QK)), _small_spec((1, DIFF_QK)),
            _small_spec((HEAD_DIM, 1))],
        out_specs=pl.BlockSpec((1, TB, TB), lambda b, i: (b * nb + i, 0, 0)),
        out_shape=jax.ShapeDtypeStruct((nblk, TB, TB), BF16),
        scratch_shapes=[pltpu.VMEM((n_chain, LANES, TB), BF16),
                        pltpu.VMEM((n_chain, 1, TB), F32),
                        pltpu.VMEM((n_chain, HEAD_DIM + ONES_ROWS, TB), F32)],
        compiler_params=_cparams(2),
        name="diff_attn",
    )(qvt, kn, qvt, tiles, lq1, lk1, lq2, lk2, subln)


def _moba_kernel(nb, nbp, qt_ref, k_ref, vt_ref, tile_ref, o_ref, qpad_scr, m_scr, acc_scr,
                 kmean_scr, sel_scr):
    i = pl.program_id(1)

    @pl.when(i == 0)
    def _():
        kmean_scr[...] = jnp.zeros(kmean_scr.shape, F32)
        for jb in range(nb):
            blk = k_ref[jb * TB:(jb + 1) * TB, :].astype(F32)
            kmean_scr[jb:jb + 1, :] = jnp.mean(blk, axis=0, keepdims=True)

    for h in range(N_HEADS):
        qpad_scr[h] = _pad_rows(qt_ref[0, h * HEAD_DIM:(h + 1) * HEAD_DIM, :], (h % 2) * HEAD_DIM, LANES)
    _flash_init(N_HEADS, m_scr, acc_scr)

    blk_id = lax.broadcasted_iota(jnp.int32, (nbp, TB), 0)
    for h in range(N_HEADS):
        g = h // 2
        km = kmean_scr[:, g * LANES:(g + 1) * LANES].astype(BF16)
        gate = jnp.dot(km, qpad_scr[h], preferred_element_type=F32)
        avail = blk_id < i
        sel = jnp.zeros((nbp, TB), jnp.bool_)
        for _ in range(MOBA_TOPK):
            gm = jnp.where(avail, gate, -jnp.inf)
            best = jnp.max(gm, axis=0, keepdims=True)
            is_best = avail & (gm == best)
            first = jnp.min(jnp.where(is_best, blk_id, nbp), axis=0, keepdims=True)
            pick = blk_id == first
            sel = sel | pick
            avail = avail & jnp.logical_not(pick)
        sel_scr[h] = jnp.where(sel, 0.0, NEG).astype(F32)

    def block(j, tile_idx, masked):
        row = pl.multiple_of(j * TB, TB)
        for h in range(N_HEADS):
            g = h // 2
            kblk = k_ref[pl.ds(row, TB), g * LANES:(g + 1) * LANES]
            vext = _with_ones(vt_ref[j, h * HEAD_DIM:(h + 1) * HEAD_DIM, :])
            add = None if tile_idx is None else tile_ref[h, tile_idx]
            if masked:
                cm = sel_scr[h, pl.ds(j, 1), :]
                add = cm if add is None else add + cm
            _flash_step(h, kblk, vext, add, qpad_scr, m_scr, acc_scr)

    block(i, 0, False)

    @pl.when(i >= 1)
    def _():
        block(i - 1, 1, True)

    def far(j, carry):
        block(j, None, True)
        return carry

    lax.fori_loop(0, jnp.maximum(i - 1, 0), far, 0)

    for h in range(N_HEADS):
        a = acc_scr[h]
        o_ref[0, h * HEAD_DIM:(h + 1) * HEAD_DIM, :] = (a[:HEAD_DIM] / a[HEAD_DIM:HEAD_DIM + 1]).astype(BF16)


def _moba_attention(kn, qvt, tiles, batch):
    nblk = qvt.shape[0]
    nb = nblk // batch
    nbp = max(8, -(-nb // 8) * 8)
    return pl.pallas_call(
        functools.partial(_moba_kernel, nb, nbp),
        grid=(batch, nb),
        in_specs=_attn_common_specs(nb, QV_MOBA_Q, KN_MOBA, QV_MOBA_V) + [
            _small_spec((N_HEADS, 2, TB, TB))],
        out_specs=pl.BlockSpec((1, TB, TB), lambda b, i: (b * nb + i, 0, 0)),
        out_shape=jax.ShapeDtypeStruct((nblk, TB, TB), BF16),
        scratch_shapes=[pltpu.VMEM((N_HEADS, LANES, TB), BF16),
                        pltpu.VMEM((N_HEADS, 1, TB), F32),
                        pltpu.VMEM((N_HEADS, HEAD_DIM + ONES_ROWS, TB), F32),
                        pltpu.VMEM((nbp, TB), F32),
                        pltpu.VMEM((N_HEADS, nbp, TB), F32)],
        compiler_params=_cparams(2),
        name="moba_attn",
    )(qvt, kn, qvt, tiles)


def _sb_kernel(qt_ref, k_ref, vt_ref, o_ref, qpad_scr, acc_scr, c_scr):
    i = pl.program_id(1)
    rows = lax.broadcasted_iota(jnp.int32, (TB, TB), 0)
    cols = lax.broadcasted_iota(jnp.int32, (TB, TB), 1)
    upper = (cols > rows).astype(BF16)
    strict = cols > rows

    def weights(h, j, first):
        g = h // 2
        row = pl.multiple_of(j * TB, TB)
        z = jnp.dot(k_ref[pl.ds(row, TB), g * LANES:(g + 1) * LANES], qpad_scr[h],
                    preferred_element_type=F32)
        softplus = jnp.maximum(z, 0.0) + jnp.log(1.0 + jnp.exp(-jnp.abs(z)))
        log_keep = -softplus
        if first:
            log_keep = jnp.where(strict, log_keep, 0.0)
        hi = log_keep.astype(BF16)
        lo = (log_keep - hi.astype(F32)).astype(BF16)
        later = (jnp.dot(upper, hi, preferred_element_type=F32)
                 + jnp.dot(upper, lo, preferred_element_type=F32))
        if not first:
            later = later + c_scr[h]
        w = jnp.exp(z + log_keep + later)
        if first:
            w = jnp.where(strict, w, 0.0)
        pv = jnp.dot(vt_ref[j, h * HEAD_DIM:(h + 1) * HEAD_DIM, :], w.astype(BF16),
                     preferred_element_type=F32)
        c_new = later[0:1, :] + log_keep[0:1, :]
        return pv, c_new

    for h in range(N_HEADS):
        qpad_scr[h] = _pad_rows(qt_ref[0, h * HEAD_DIM:(h + 1) * HEAD_DIM, :], (h % 2) * HEAD_DIM, LANES)

    for h in range(N_HEADS):
        pv, c_new = weights(h, i, True)
        acc_scr[h] = pv
        c_scr[h] = c_new

        def cond(carry):
            j, go = carry
            return jnp.logical_and(j >= 0, go)

        def body(carry, h=h):
            j, _ = carry
            pv, c_new = weights(h, j, False)
            acc_scr[h] = acc_scr[h] + pv
            c_scr[h] = c_new
            return j - 1, jnp.max(c_new) > -SB_EXIT

        lax.while_loop(cond, body, (i - 1, jnp.max(c_new) > -SB_EXIT))
        o_ref[0, h * HEAD_DIM:(h + 1) * HEAD_DIM, :] = acc_scr[h].astype(BF16)


def _sb_attention(kn, qvt, batch):
    nblk = qvt.shape[0]
    nb = nblk // batch
    return pl.pallas_call(
        _sb_kernel,
        grid=(batch, nb),
        in_specs=_attn_common_specs(nb, QV_SB_Q, KN_SB, QV_SB_V),
        out_specs=pl.BlockSpec((1, TB, TB), lambda b, i: (b * nb + i, 0, 0)),
        out_shape=jax.ShapeDtypeStruct((nblk, TB, TB), BF16),
        scratch_shapes=[pltpu.VMEM((N_HEADS, LANES, TB), BF16),
                        pltpu.VMEM((N_HEADS, HEAD_DIM, TB), F32),
                        pltpu.VMEM((N_HEADS, 1, TB), F32)],
        compiler_params=_cparams(2),
        name="sb_attn",
    )(qvt, kn, qvt)


def _swa_kernel(sink_ref, qt_ref, kc_ref, kp_ref, vc_ref, vp_ref, tc_ref, tp_ref, o_ref):
    i = pl.program_id(1)
    kc = kc_ref[...]
    kp = kp_ref[...]
    vc = vc_ref[0]
    vp = vp_ref[0][:, TB - SWA_WINDOW:]
    no_prev = jnp.where(i > 0, 0.0, NEG)
    group = N_HEADS // SWA_KV_HEADS
    for h in range(N_HEADS):
        kv = h // group
        qpad = _pad_rows(qt_ref[0, h * HEAD_DIM:(h + 1) * HEAD_DIM, :], kv * HEAD_DIM, LANES)
        s_c = jnp.dot(kc, qpad, preferred_element_type=F32) + tc_ref[h]
        s_p = jnp.dot(kp, qpad, preferred_element_type=F32) + tp_ref[h] + no_prev
        sink = sink_ref[h]
        m = jnp.maximum(jnp.maximum(jnp.max(s_c, axis=0, keepdims=True),
                                    jnp.max(s_p, axis=0, keepdims=True)), sink)
        p_c = jnp.exp(s_c - m).astype(BF16)
        p_p = jnp.exp(s_p - m).astype(BF16)
        o = (jnp.dot(_with_ones(vc[kv * HEAD_DIM:(kv + 1) * HEAD_DIM]), p_c, preferred_element_type=F32)
             + jnp.dot(_with_ones(vp[kv * HEAD_DIM:(kv + 1) * HEAD_DIM]), p_p, preferred_element_type=F32))
        denom = o[HEAD_DIM:HEAD_DIM + 1] + jnp.exp(sink - m)
        o_ref[0, h * HEAD_DIM:(h + 1) * HEAD_DIM, :] = (o[:HEAD_DIM] / denom).astype(BF16)


def _swa_attention(kn, qvt, tile_cur, tile_prev, sinks, batch):
    nblk = qvt.shape[0]
    nb = nblk // batch
    half = TB // SWA_WINDOW
    kv_rows = SWA_KV_HEADS * HEAD_DIM
    return pl.pallas_call(
        _swa_kernel,
        grid=(batch, nb),
        in_specs=[pl.BlockSpec(memory_space=pltpu.SMEM),
                  pl.BlockSpec((1, TB, TB), lambda b, i: (b * nb + i, QV_SWA_Q, 0)),
                  pl.BlockSpec((TB, kv_rows), lambda b, i: (b * nb + i, KN_SWA_128)),
                  pl.BlockSpec((SWA_WINDOW, kv_rows),
                               lambda b, i: (b * nb * half + jnp.maximum(half * i - 1, 0), KN_SWA_128)),
                  pl.BlockSpec((1, kv_rows, TB), lambda b, i: (b * nb + i, QV_SWA_V_128, 0)),
                  pl.BlockSpec((1, kv_rows, TB), lambda b, i: (b * nb + jnp.maximum(i - 1, 0), QV_SWA_V_128, 0)),
                  _small_spec((N_HEADS, TB, TB)),
                  _small_spec((N_HEADS, SWA_WINDOW, TB))],
        out_specs=pl.BlockSpec((1, TB, TB), lambda b, i: (b * nb + i, 0, 0)),
        out_shape=jax.ShapeDtypeStruct((nblk, TB, TB), BF16),
        compiler_params=_cparams(2),
        name="swa_attn",
    )(sinks, qvt, kn, kn, qvt, qvt, tile_cur, tile_prev)


MERGE_T = 512


def _merge_kernel(x_ref, g_ref, oa_ref, ob_ref, oc_ref, od_ref, wg_ref, wbr_ref, wo_ref, x1_ref):
    x = x_ref[...]
    h = _rms(x, g_ref[...]).astype(BF16)
    merged = None
    for bi, o_ref in enumerate((oa_ref, ob_ref, oc_ref, od_ref)):
        gate = jax.nn.sigmoid(jnp.dot(h, wg_ref[bi], preferred_element_type=F32))
        branch = jnp.concatenate(
            [lax.dot_general(o_ref[s], wbr_ref[bi], (((0,), (0,)), ((), ())), preferred_element_type=F32)
             for s in range(MERGE_T // TB)], axis=0)
        term = gate * branch
        merged = term if merged is None else merged + term
    x1_ref[...] = x + jnp.dot(merged.astype(BF16), wo_ref[...], preferred_element_type=F32)


def _merge(x2, g, o_a, o_b, o_c, o_d, wg, wbr, wo):
    n, d = x2.shape
    o_spec = pl.BlockSpec((MERGE_T // TB, TB, TB), lambda i: (i, 0, 0))
    return pl.pallas_call(
        _merge_kernel,
        grid=(n // MERGE_T,),
        in_specs=[pl.BlockSpec((MERGE_T, d), lambda i: (i, 0)),
                  pl.BlockSpec((1, d), lambda i: (0, 0)),
                  o_spec, o_spec, o_spec, o_spec,
                  pl.BlockSpec(wg.shape, lambda i: (0, 0, 0)),
                  pl.BlockSpec(wbr.shape, lambda i: (0, 0, 0)),
                  pl.BlockSpec(wo.shape, lambda i: (0, 0))],
        out_specs=pl.BlockSpec((MERGE_T, d), lambda i: (i, 0)),
        out_shape=jax.ShapeDtypeStruct((n, d), F32),
        compiler_params=_cparams(1),
        name="merge",
    )(x2, g, o_a, o_b, o_c, o_d, wg, wbr, wo)


ROUTER_T = 512
ROUTER_ROWS = 8 + N_EXPERTS


def _first_argmax_rows(v, n_rows):
    best = jnp.max(v, axis=0, keepdims=True)
    ids = lax.broadcasted_iota(jnp.int32, v.shape, 0)
    return best, jnp.min(jnp.where(v == best, ids, n_rows), axis=0, keepdims=True)


def _router_kernel(x_ref, g_ref, whi_ref, wlo_ref, b_ref, h2_ref, eid_ref, gate_ref, rank_ref, cnt_ref,
                   base_scr):
    i = pl.program_id(0)

    @pl.when(i == 0)
    def _():
        base_scr[...] = jnp.zeros(base_scr.shape, F32)

    h2 = _rms(x_ref[...], g_ref[...])
    h2_ref[...] = h2
    h_hi = h2.astype(BF16)
    h_lo = (h2 - h_hi.astype(F32)).astype(BF16)
    nt = (((1,), (1,)), ((), ()))
    logits = (lax.dot_general(whi_ref[...], h_hi, nt, preferred_element_type=F32)
              + lax.dot_general(whi_ref[...], h_lo, nt, preferred_element_type=F32)
              + lax.dot_general(wlo_ref[...], h_hi, nt, preferred_element_type=F32)
              + b_ref[...])
    gl = logits[0:8]
    gmax, grp = _first_argmax_rows(gl, 8)
    p_grp = 1.0 / jnp.sum(jnp.exp(gl - gmax), axis=0, keepdims=True)
    e_sel = jnp.zeros((EXPERTS_PER_GROUP, ROUTER_T), F32)
    for g in range(N_GROUPS):
        e_sel = jnp.where(grp == g, logits[8 + 8 * g:16 + 8 * g], e_sel)
    ids8 = lax.broadcasted_iota(jnp.int32, e_sel.shape, 0)
    v1, i1 = _first_argmax_rows(e_sel, EXPERTS_PER_GROUP)
    e_rest = jnp.where(ids8 == i1, -jnp.inf, e_sel)
    v2, i2 = _first_argmax_rows(e_rest, EXPERTS_PER_GROUP)
    r = jnp.exp(v2 - v1)
    s1 = 1.0 / (1.0 + r)
    gate_ref[0:1, :] = p_grp * s1
    gate_ref[1:2, :] = p_grp * (r * s1)
    e1 = grp * EXPERTS_PER_GROUP + i1
    e2 = grp * EXPERTS_PER_GROUP + i2
    eid_ref[0:1, :] = e1
    eid_ref[1:2, :] = e2

    ids_e = lax.broadcasted_iota(jnp.int32, (N_EXPERTS, ROUTER_T), 0)
    oh1 = ids_e == e1
    oh2 = ids_e == e2
    cnt = oh1.astype(F32) + oh2.astype(F32)
    tr = lax.broadcasted_iota(jnp.int32, (ROUTER_T, ROUTER_T), 0)
    tc = lax.broadcasted_iota(jnp.int32, (ROUTER_T, ROUTER_T), 1)
    before = (tr < tc).astype(BF16)
    prefix = jnp.dot(cnt.astype(BF16), before, preferred_element_type=F32) + base_scr[:, 0:1]
    rank_ref[0:1, :] = jnp.sum(jnp.where(oh1, prefix, 0.0), axis=0, keepdims=True).astype(jnp.int32)
    rank_ref[1:2, :] = jnp.sum(jnp.where(oh2, prefix, 0.0), axis=0, keepdims=True).astype(jnp.int32)
    base_scr[...] = base_scr[...] + jnp.sum(cnt, axis=1, keepdims=True)
    cnt_ref[...] = base_scr[...]


def _router(x1, g, w_hi, w_lo, bias):
    n, d = x1.shape
    row2 = lambda dt: jax.ShapeDtypeStruct((2, n), dt)
    spec2 = pl.BlockSpec((2, ROUTER_T), lambda i: (0, i))
    return pl.pallas_call(
        _router_kernel,
        grid=(n // ROUTER_T,),
        in_specs=[pl.BlockSpec((ROUTER_T, d), lambda i: (i, 0)),
                  pl.BlockSpec((1, d), lambda i: (0, 0)),
                  pl.BlockSpec((ROUTER_ROWS, d), lambda i: (0, 0)),
                  pl.BlockSpec((ROUTER_ROWS, d), lambda i: (0, 0)),
                  pl.BlockSpec((ROUTER_ROWS, 1), lambda i: (0, 0))],
        out_specs=[pl.BlockSpec((ROUTER_T, d), lambda i: (i, 0)), spec2, spec2, spec2,
                   pl.BlockSpec((N_EXPERTS, LANES), lambda i: (0, 0))],
        out_shape=[jax.ShapeDtypeStruct((n, d), F32), row2(jnp.int32), row2(F32), row2(jnp.int32),
                   jax.ShapeDtypeStruct((N_EXPERTS, LANES), F32)],
        scratch_shapes=[pltpu.VMEM((N_EXPERTS, LANES), F32)],
        compiler_params=_cparams(1),
        name="router",
    )(x1, g, w_hi, w_lo, bias)


ROW_T = 256


def _dispatch_kernel(dest_ref, h2_ref, xin_ref, xbuf_ref, sem):
    del xin_ref

    def row_copy(r, k):
        return pltpu.make_async_copy(h2_ref.at[pl.ds(r, 1)], xbuf_ref.at[pl.ds(dest_ref[k, r], 1)], sem)

    def issue(r, carry):
        row_copy(r, 0).start()
        row_copy(r, 1).start()
        return carry

    def drain(r, carry):
        row_copy(r, 0).wait()
        row_copy(r, 1).wait()
        return carry

    lax.fori_loop(0, ROW_T, issue, 0)
    lax.fori_loop(0, ROW_T, drain, 0)


def _dispatch(dest3, h2, xbuf_init):
    n, d = h2.shape
    return pl.pallas_call(
        _dispatch_kernel,
        grid=(n // ROW_T,),
        in_specs=[pl.BlockSpec((None, 2, ROW_T), lambda i: (i, 0, 0), memory_space=pltpu.SMEM),
                  pl.BlockSpec((ROW_T, d), lambda i: (i, 0)),
                  pl.BlockSpec(memory_space=pl.ANY)],
        out_specs=pl.BlockSpec(memory_space=pl.ANY),
        out_shape=jax.ShapeDtypeStruct(xbuf_init.shape, xbuf_init.dtype),
        scratch_shapes=[pltpu.SemaphoreType.DMA(())],
        input_output_aliases={2: 0},
        compiler_params=_cparams(1),
        name="dispatch",
    )(dest3, h2, xbuf_init)


def _expert_kernel(be_ref, nu_ref, x_ref, w1_ref, w3_ref, w2_ref, y_ref):
    del be_ref
    used = pl.program_id(0) < nu_ref[0]

    @pl.when(used)
    def _():
        xb = x_ref[...].astype(BF16)
        a = jnp.dot(xb, w1_ref[...], preferred_element_type=F32)
        b = jnp.dot(xb, w3_ref[...], preferred_element_type=F32)
        mid = (a * jax.nn.sigmoid(a) * b).astype(BF16)
        y_ref[...] = jnp.dot(mid, w2_ref[...], preferred_element_type=F32)

    @pl.when(jnp.logical_not(used))
    def _():
        y_ref[...] = jnp.zeros(y_ref.shape, F32)


def _experts(blk_expert, n_used, xbuf, w1, w3, w2):
    r, d = xbuf.shape
    de = w1.shape[-1]
    row_map = lambda i, be, nu: (jnp.minimum(i, nu[0] - 1), 0)
    grid_spec = pltpu.PrefetchScalarGridSpec(
        num_scalar_prefetch=2,
        grid=(r // MOE_ROWS,),
        in_specs=[pl.BlockSpec((MOE_ROWS, d), row_map),
                  pl.BlockSpec((None, d, de), lambda i, be, nu: (be[i], 0, 0)),
                  pl.BlockSpec((None, d, de), lambda i, be, nu: (be[i], 0, 0)),
                  pl.BlockSpec((None, de, d), lambda i, be, nu: (be[i], 0, 0))],
        out_specs=pl.BlockSpec((MOE_ROWS, d), lambda i, be, nu: (i, 0)),
    )
    return pl.pallas_call(
        _expert_kernel,
        grid_spec=grid_spec,
        out_shape=jax.ShapeDtypeStruct((r, d), F32),
        compiler_params=_cparams(1),
        name="experts",
    )(blk_expert, n_used, xbuf, w1, w3, w2)


def _combine_kernel(final, dest_ref, gate_ref, x1_ref, gf_ref, y_ref, out_ref, buf, sem):
    def row_copy(r, k):
        return pltpu.make_async_copy(y_ref.at[pl.ds(dest_ref[k, r], 1)], buf.at[k, pl.ds(r, 1)], sem)

    def issue(r, carry):
        row_copy(r, 0).start()
        row_copy(r, 1).start()
        return carry

    def drain(r, carry):
        row_copy(r, 0).wait()
        row_copy(r, 1).wait()
        return carry

    lax.fori_loop(0, ROW_T, issue, 0)
    lax.fori_loop(0, ROW_T, drain, 0)
    gate = gate_ref[...]
    out = x1_ref[...] + gate[:, 0:1] * buf[0] + gate[:, 1:2] * buf[1]
    if final:
        out = _rms(out, gf_ref[...])
    out_ref[...] = out


def _combine(dest3, gate_t, x1, g_final, y, final):
    n, d = x1.shape
    return pl.pallas_call(
        functools.partial(_combine_kernel, final),
        grid=(n // ROW_T,),
        in_specs=[pl.BlockSpec((None, 2, ROW_T), lambda i: (i, 0, 0), memory_space=pltpu.SMEM),
                  pl.BlockSpec((ROW_T, 2), lambda i: (i, 0)),
                  pl.BlockSpec((ROW_T, d), lambda i: (i, 0)),
                  pl.BlockSpec((1, d), lambda i: (0, 0)),
                  pl.BlockSpec(memory_space=pl.ANY)],
        out_specs=pl.BlockSpec((ROW_T, d), lambda i: (i, 0)),
        out_shape=jax.ShapeDtypeStruct((n, d), F32),
        scratch_shapes=[pltpu.VMEM((2, ROW_T, d), F32), pltpu.SemaphoreType.DMA(())],
        compiler_params=_cparams(1),
        name="combine",
    )(dest3, gate_t, x1, g_final, y)


def _projection_columns():
    hd, nh = HEAD_DIM, N_HEADS
    pa, pb, pc, pd = 0, 3 * nh * hd, 6 * nh * hd, 9 * nh * hd
    blk = nh * hd
    rng = lambda a, n: list(range(a, a + n))
    diff_q, diff_k = [], []
    for h in range(nh):
        diff_q += rng(pb + h * DIFF_QK, DIFF_QK) + rng(pb + nh * DIFF_QK + h * DIFF_QK, DIFF_QK)
        diff_k += rng(pb + 2 * nh * DIFF_QK + h * DIFF_QK, DIFF_QK) + rng(pb + 3 * nh * DIFF_QK + h * DIFF_QK, DIFF_QK)
    kv = SWA_KV_HEADS * hd
    idx_n = rng(pa + blk, blk) + diff_k + rng(pc + blk, blk) + rng(pd + blk, kv)
    idx_t = (rng(pa, blk) + rng(pa + 2 * blk, blk) + diff_q + rng(pb + 4 * nh * DIFF_QK, blk)
             + rng(pc, blk) + rng(pc + 2 * blk, blk) + rng(pd, blk) + rng(pd + blk + kv, kv))
    s64, s32 = hd ** -0.5, DIFF_QK ** -0.5
    scale_t = ([s64] * blk + [1.0] * blk + [s32] * blk + [1.0] * blk + [s64] * blk + [1.0] * blk
               + [s64] * blk + [1.0] * kv)
    assert len(idx_n) == KN_COLS and len(idx_t) == QV_ROWS == len(scale_t)
    return np.asarray(idx_n), np.asarray(idx_t), np.asarray(scale_t, np.float32)


def _router_weights(w_rg, b_rg, w_re, b_re):
    d = w_rg.shape[0]
    w = jnp.concatenate([w_rg.T, jnp.zeros((8 - N_GROUPS, d), F32), w_re.T], axis=0)
    b = jnp.concatenate([b_rg.astype(F32), jnp.full((8 - N_GROUPS,), NEG, F32), b_re.astype(F32)])[:, None]
    w_hi = w.astype(BF16)
    w_lo = (w - w_hi.astype(F32)).astype(BF16)
    return w_hi, w_lo, b


def _moe_plan(eid, rank, counts, n_rows_total):
    padded = (counts + MOE_ROWS - 1) // MOE_ROWS * MOE_ROWS
    pad_end = jnp.cumsum(padded)
    pad_start = pad_end - padded
    dest = jnp.take(pad_start, eid) + rank
    n_blk = n_rows_total // MOE_ROWS
    n_used = (pad_end[-1] // MOE_ROWS).astype(jnp.int32)
    blk = jnp.minimum(jnp.arange(n_blk, dtype=jnp.int32), n_used - 1) * MOE_ROWS
    blk_expert = jnp.minimum(jnp.searchsorted(pad_end, blk, side='right'), N_EXPERTS - 1).astype(jnp.int32)
    return dest.astype(jnp.int32), blk_expert, n_used.reshape(1)


def kernel(x, rel_bias, g_mix, w_in, diff_lq1, diff_lk1, diff_lq2, diff_lk2, diff_subln, swa_sinks,
           w_gate, w_br, w_o, g_ffn, w_route_group, b_route_group, w_route_expert, b_route_expert,
           w1, w3, w2, g_final):
    batch, seq, d = x.shape
    n = batch * seq
    depth = w_in.shape[0]
    assert seq % TB == 0 and n % MERGE_T == 0 and TB == MOBA_BLOCK
    tab = rel_bias.T.astype(F32)
    tiles_moba = _causal_bias_tiles(tab[:N_HEADS])
    tiles_diff = _causal_bias_tiles(tab[N_HEADS:2 * N_HEADS])
    tile_cur, tile_prev = _swa_bias_tiles(tab[2 * N_HEADS:])
    idx_n, idx_t, scale_t = _projection_columns()
    n_rows_total = n * 2 + N_EXPERTS * MOE_ROWS
    row = lambda v: v.astype(F32)[None, :]

    x2 = x.reshape(n, d)
    for l in range(depth):
        lam_init = 0.8 - 0.6 * math.exp(-0.3 * l)
        wn = w_in[l][:, idx_n].astype(BF16)
        wt = (w_in[l][:, idx_t] * scale_t[None, :]).T.astype(BF16)
        kn, qvt = _inproj(x2, row(g_mix[l]), wn, wt)
        o_a = _moba_attention(kn, qvt, tiles_moba, batch)
        o_b = _diff_attention(kn, qvt, tiles_diff, row(diff_lq1[l]), row(diff_lk1[l]), row(diff_lq2[l]),
                              row(diff_lk2[l]), diff_subln[l].astype(F32)[:, None], lam_init, batch)
        o_c = _sb_attention(kn, qvt, batch)
        o_d = _swa_attention(kn, qvt, tile_cur, tile_prev, swa_sinks[l].astype(F32), batch)
        x1 = _merge(x2, row(g_mix[l]), o_a, o_b, o_c, o_d, w_gate[l].astype(BF16), w_br[l].astype(BF16),
                    w_o[l].astype(BF16))
        w_hi, w_lo, r_bias = _router_weights(w_route_group[l], b_route_group[l], w_route_expert[l],
                                             b_route_expert[l])
        h2, eid, gate, rank, cnt = _router(x1, row(g_ffn[l]), w_hi, w_lo, r_bias)
        dest, blk_expert, n_used = _moe_plan(eid, rank, cnt[:, 0].astype(jnp.int32), n_rows_total)
        dest3 = dest.reshape(2, n // ROW_T, ROW_T).transpose(1, 0, 2)
        xbuf = _dispatch(dest3, h2, jnp.zeros((n_rows_total, d), F32))
        y = _experts(blk_expert, n_used, xbuf, w1[l].astype(BF16), w3[l].astype(BF16), w2[l].astype(BF16))
        x2 = _combine(dest3, gate.T, x1, row(g_final), y, l == depth - 1)
    return x2.reshape(batch, seq, d)
```

```python
import functools
import math

import numpy as np
import jax
import jax.numpy as jnp
from jax import lax
from jax.experimental import pallas as pl
from jax.experimental.pallas import tpu as pltpu

F32 = jnp.float32
BF16 = jnp.bfloat16

HEAD_DIM = 64
N_HEADS = 4
DIFF_QK = 32
SWA_KV_HEADS = 2
SWA_WINDOW = 128
MOBA_BLOCK = 256
MOBA_TOPK = 3
REL_BUCKETS = 32
REL_MAX_DIST = 128
N_GROUPS = 4
EXPERTS_PER_GROUP = 8
N_EXPERTS = N_GROUPS * EXPERTS_PER_GROUP
NORM_EPS = 1e-6

TB = 256
LANES = 128
ONES_ROWS = 16
NEG = -1e30
LOG2E = math.log2(math.e)
SB_EXIT = 104.0
MOE_ROWS = 256
VMEM_LIMIT = 56 * 1024 * 1024

QV_MOBA_Q, QV_MOBA_V, QV_DIFF_Q, QV_DIFF_V, QV_SB_Q, QV_SB_V, QV_SWA_Q = range(7)
QV_SWA_V_128 = 14
QV_ROWS = 7 * 256 + 128
KN_MOBA, KN_DIFF, KN_SB = range(3)
KN_SWA_128 = 6
KN_COLS = 3 * 256 + 128


def _cparams(n_grid):
    return pltpu.CompilerParams(dimension_semantics=("arbitrary",) * n_grid,
                                vmem_limit_bytes=VMEM_LIMIT)


def _rel_bucket_np(n):
    n = np.maximum(n, 0)
    max_exact = REL_BUCKETS // 2
    nf = np.maximum(n, 1).astype(np.float64)
    large = max_exact + (np.log(nf / max_exact) / math.log(REL_MAX_DIST / max_exact)
                         * (REL_BUCKETS - max_exact)).astype(np.int64)
    large = np.minimum(large, REL_BUCKETS - 1)
    return np.where(n < max_exact, n, large)


def _first_far_distance():
    d = np.arange(0, 4 * REL_MAX_DIST)
    b = _rel_bucket_np(d)
    return int(np.min(d[b == REL_BUCKETS - 1]))


def _toeplitz_bias(tab, rows, cols, base, valid_fn, shift_far, unit):
    length = rows + cols - 1
    off = np.concatenate([np.arange(0, cols), np.arange(cols - length, 0)])
    n = base + off
    onehot = np.zeros((REL_BUCKETS, length), np.float32)
    onehot[_rel_bucket_np(n), np.arange(length)] = 1.0
    vec = jnp.dot(tab, jnp.asarray(onehot), precision=lax.Precision.HIGHEST)
    if shift_far:
        vec = vec - tab[:, REL_BUCKETS - 1:]
    vec = jnp.where(jnp.asarray(valid_fn(n))[None, :], vec * unit, NEG).astype(F32)
    flat = jnp.tile(vec, (1, rows))[:, :rows * (length - 1)]
    return flat.reshape(tab.shape[0], rows, length - 1)[:, :, :cols]


def _causal_bias_tiles(tab):
    assert _first_far_distance() <= TB + 1
    tiles = [_toeplitz_bias(tab, TB, TB, d * TB, lambda n: n >= 0, True, LOG2E) for d in range(2)]
    return jnp.stack(tiles, axis=1)


def _swa_bias_tiles(tab):
    in_window = lambda n: (n >= 0) & (n < SWA_WINDOW)
    return (_toeplitz_bias(tab, TB, TB, 0, in_window, False, 1.0),
            _toeplitz_bias(tab, SWA_WINDOW, TB, SWA_WINDOW, in_window, False, 1.0))


def _pad_rows(q, off, total):
    n, t = q.shape
    parts = []
    if off:
        parts.append(jnp.zeros((off, t), q.dtype))
    parts.append(q)
    if total - off - n:
        parts.append(jnp.zeros((total - off - n, t), q.dtype))
    return jnp.concatenate(parts, axis=0) if len(parts) > 1 else q


def _with_ones(v):
    return jnp.concatenate([v, jnp.ones((ONES_ROWS, v.shape[1]), v.dtype)], axis=0)


def _rms(x, g_row):
    ms = jnp.mean(x * x, axis=-1, keepdims=True)
    return x * lax.rsqrt(ms + NORM_EPS) * g_row


IN_T = 512
IN_CHUNK = 384


def _inproj_kernel(x_ref, g_ref, wn_ref, wt_ref, kn_ref, qvt_ref):
    h = _rms(x_ref[...], g_ref[...]).astype(BF16)
    kn_ref[...] = jnp.dot(h, wn_ref[...], preferred_element_type=F32).astype(BF16)
    for r0 in range(0, QV_ROWS, IN_CHUNK):
        pt = lax.dot_general(wt_ref[r0:r0 + IN_CHUNK, :], h, (((1,), (1,)), ((), ())),
                             preferred_element_type=F32)
        for s in range(IN_T // TB):
            qvt_ref[s, r0:r0 + IN_CHUNK, :] = pt[:, s * TB:(s + 1) * TB].astype(BF16)


def _inproj(x2, g, wn, wt):
    n, d = x2.shape
    return pl.pallas_call(
        _inproj_kernel,
        grid=(n // IN_T,),
        in_specs=[pl.BlockSpec((IN_T, d), lambda i: (i, 0)),
                  pl.BlockSpec((1, d), lambda i: (0, 0)),
                  pl.BlockSpec((d, KN_COLS), lambda i: (0, 0)),
                  pl.BlockSpec((QV_ROWS, d), lambda i: (0, 0))],
        out_specs=[pl.BlockSpec((IN_T, KN_COLS), lambda i: (i, 0)),
                   pl.BlockSpec((IN_T // TB, QV_ROWS, TB), lambda i: (i, 0, 0))],
        out_shape=[jax.ShapeDtypeStruct((n, KN_COLS), BF16),
                   jax.ShapeDtypeStruct((n // TB, QV_ROWS, TB), BF16)],
        compiler_params=_cparams(1),
        name="inproj",
    )(x2, g, wn, wt)


class _FlashScratch:
    def __init__(self, n_chain, qpad, m, acc, s, mx, p):
        self.n_chain, self.qpad, self.m, self.acc, self.s, self.mx, self.p = n_chain, qpad, m, acc, s, mx, p

    @staticmethod
    def shapes(n_chain):
        return [pltpu.VMEM((n_chain, LANES, TB), BF16),
                pltpu.VMEM((n_chain, 1, TB), F32),
                pltpu.VMEM((n_chain, HEAD_DIM + ONES_ROWS, TB), F32),
                pltpu.VMEM((2 * n_chain, TB, TB), F32),
                pltpu.VMEM((2 * n_chain, 1, TB), F32),
                pltpu.VMEM((n_chain, TB, TB), BF16)]


def _flash_scores(sc, slot, chains):
    for c, kblk, add in chains:
        s = jnp.dot(kblk, sc.qpad[c], preferred_element_type=F32)
        if add is not None:
            s = s + add
        sc.s[slot * sc.n_chain + c] = s
        sc.mx[slot * sc.n_chain + c] = jnp.max(s, axis=0, keepdims=True)


def _flash_update(sc, slot, chains):
    alphas = []
    for c, _ in chains:
        m_old = sc.m[c]
        m_new = jnp.maximum(m_old, sc.mx[slot * sc.n_chain + c])
        sc.p[c] = jnp.exp2(sc.s[slot * sc.n_chain + c] - m_new).astype(BF16)
        alphas.append(jnp.exp2(m_old - m_new))
        sc.m[c] = m_new
    for (c, vext), alpha in zip(chains, alphas):
        sc.acc[c] = sc.acc[c] * alpha + jnp.dot(vext, sc.p[c], preferred_element_type=F32)


def _flash_causal(sc, i, score_chains, value_chains):
    for c in range(sc.n_chain):
        sc.m[c] = jnp.full(sc.m.shape[1:], NEG, F32)
        sc.acc[c] = jnp.zeros(sc.acc.shape[1:], F32)
    _flash_scores(sc, 0, score_chains(i, 0))
    _flash_update(sc, 0, value_chains(i))

    @pl.when(i >= 1)
    def _():
        _flash_scores(sc, 0, score_chains(i - 1, 1))
        _flash_update(sc, 0, value_chains(i - 1))

    n_far = jnp.maximum(i - 1, 0)

    @pl.when(n_far > 0)
    def _():
        _flash_scores(sc, 0, score_chains(0, None))

    def far_pair(t, carry):
        j = 2 * t
        _flash_scores(sc, 1, score_chains(j + 1, None))
        _flash_update(sc, 0, value_chains(j))
        _flash_scores(sc, 0, score_chains(jnp.minimum(j + 2, n_far - 1), None))
        _flash_update(sc, 1, value_chains(j + 1))
        return carry

    lax.fori_loop(0, n_far // 2, far_pair, 0)

    @pl.when(lax.rem(n_far, 2) == 1)
    def _():
        _flash_update(sc, 0, value_chains(n_far - 1))


def _diff_kernel(lam_init, qt_ref, k_ref, vt_ref, tile_ref, lq1_ref, lk1_ref, lq2_ref, lk2_ref,
                 subln_ref, o_ref, *scratch):
    i = pl.program_id(1)
    sc = _FlashScratch(2 * N_HEADS, *scratch)
    acc_scr = sc.acc
    for h in range(N_HEADS):
        for mp in range(2):
            r0 = h * HEAD_DIM + mp * DIFF_QK
            sc.qpad[2 * h + mp] = _pad_rows(qt_ref[0, r0:r0 + DIFF_QK, :],
                                            (h % 2) * HEAD_DIM + mp * DIFF_QK, LANES)

    def score_chains(j, tile_idx):
        row = pl.multiple_of(j * TB, TB)
        chains = []
        for h in range(N_HEADS):
            g = h // 2
            kblk = k_ref[pl.ds(row, TB), g * LANES:(g + 1) * LANES]
            add = None if tile_idx is None else tile_ref[h, tile_idx]
            chains += [(2 * h + mp, kblk, add) for mp in range(2)]
        return chains

    def value_chains(j):
        chains = []
        for h in range(N_HEADS):
            vext = _with_ones(vt_ref[j, h * HEAD_DIM:(h + 1) * HEAD_DIM, :])
            chains += [(2 * h + mp, vext) for mp in range(2)]
        return chains

    _flash_causal(sc, i, score_chains, value_chains)

    lam =(jnp.exp(jnp.sum(lq1_ref[...] * lk1_ref[...], keepdims=True))
           - jnp.exp(jnp.sum(lq2_ref[...] * lk2_ref[...], keepdims=True)) + lam_init)
    for h in range(N_HEADS):
        a1 = acc_scr[2 * h]
        a2 = acc_scr[2 * h + 1]
        o1 = a1[:HEAD_DIM] / a1[HEAD_DIM:HEAD_DIM + 1]
        o2 = a2[:HEAD_DIM] / a2[HEAD_DIM:HEAD_DIM + 1]
        a = o1 - lam * o2
        ms = jnp.mean(a * a, axis=0, keepdims=True)
        y = a * lax.rsqrt(ms + NORM_EPS) * subln_ref[...] * (1.0 - lam_init)
        o_ref[0, h * HEAD_DIM:(h + 1) * HEAD_DIM, :] = y.astype(BF16)


def _attn_common_specs(nb, q_blk, k_blk, v_blk):
    return [pl.BlockSpec((1, TB, TB), lambda b, i: (b * nb + i, q_blk, 0)),
            pl.BlockSpec((nb * TB, TB), lambda b, i: (b, k_blk)),
            pl.BlockSpec((nb, TB, TB), lambda b, i: (b, v_blk, 0))]


def _small_spec(shape):
    return pl.BlockSpec(shape, lambda b, i: (0,) * len(shape))


def _diff_attention(kn, qvt, tiles, lq1, lk1, lq2, lk2, subln, lam_init, batch):
    nblk = qvt.shape[0]
    nb = nblk // batch
    n_chain = 2 * N_HEADS
    return pl.pallas_call(
        functools.partial(_diff_kernel, lam_init),
        grid=(batch, nb),
        in_specs=_attn_common_specs(nb, QV_DIFF_Q, KN_DIFF, QV_DIFF_V) + [
            _small_spec((N_HEADS, 2, TB, TB)),
            _small_spec((1, DIFF_QK)), _small_spec((1, DIFF_QK)),
            _small_spec((1, DIFF_QK)), _small_spec((1, DIFF_QK)),
            _small_spec((HEAD_DIM, 1))],
        out_specs=pl.BlockSpec((1, TB, TB), lambda b, i: (b * nb + i, 0, 0)),
        out_shape=jax.ShapeDtypeStruct((nblk, TB, TB), BF16),
        scratch_shapes=_FlashScratch.shapes(n_chain),
        compiler_params=_cparams(2),
        name="diff_attn",
    )(qvt, kn, qvt, tiles, lq1, lk1, lq2, lk2, subln)


def _moba_kernel(nb, nbp, qt_ref, k_ref, vt_ref, tile_ref, o_ref, kmean_scr, sel_scr, *scratch):
    i = pl.program_id(1)
    sc = _FlashScratch(N_HEADS, *scratch)
    qpad_scr, acc_scr = sc.qpad, sc.acc

    @pl.when(i == 0)
    def _():
        kmean_scr[...] = jnp.zeros(kmean_scr.shape, F32)
        for jb in range(nb):
            blk = k_ref[jb * TB:(jb + 1) * TB, :].astype(F32)
            kmean_scr[jb:jb + 1, :] = jnp.mean(blk, axis=0, keepdims=True)

    for h in range(N_HEADS):
        qpad_scr[h] = _pad_rows(qt_ref[0, h * HEAD_DIM:(h + 1) * HEAD_DIM, :], (h % 2) * HEAD_DIM, LANES)

    blk_id = lax.broadcasted_iota(jnp.int32, (nbp, TB), 0)
    for h in range(N_HEADS):
        g = h // 2
        km = kmean_scr[:, g * LANES:(g + 1) * LANES].astype(BF16)
        gate = jnp.dot(km, qpad_scr[h], preferred_element_type=F32)
        avail = blk_id < i
        sel = jnp.zeros((nbp, TB), jnp.bool_)
        for _ in range(MOBA_TOPK):
            gm = jnp.where(avail, gate, -jnp.inf)
            best = jnp.max(gm, axis=0, keepdims=True)
            is_best = avail & (gm == best)
            first = jnp.min(jnp.where(is_best, blk_id, nbp), axis=0, keepdims=True)
            pick = blk_id == first
            sel = sel | pick
            avail = avail & jnp.logical_not(pick)
        sel_scr[h] = jnp.where(sel, 0.0, NEG).astype(F32)

    def score_chains(j, tile_idx):
        row = pl.multiple_of(j * TB, TB)
        chains = []
        for h in range(N_HEADS):
            g = h // 2
            kblk = k_ref[pl.ds(row, TB), g * LANES:(g + 1) * LANES]
            add = None if tile_idx is None else tile_ref[h, tile_idx]
            if tile_idx != 0:
                cm = sel_scr[h, pl.ds(j, 1), :]
                add = cm if add is None else add + cm
            chains.append((h, kblk, add))
        return chains

    def value_chains(j):
        return [(h, _with_ones(vt_ref[j, h * HEAD_DIM:(h + 1) * HEAD_DIM, :])) for h in range(N_HEADS)]

    _flash_causal(sc, i, score_chains, value_chains)

    for h in range(N_HEADS):
        a = acc_scr[h]
        o_ref[0, h * HEAD_DIM:(h + 1) * HEAD_DIM, :] = (a[:HEAD_DIM] / a[HEAD_DIM:HEAD_DIM + 1]).astype(BF16)


def _moba_attention(kn, qvt, tiles, batch):
    nblk = qvt.shape[0]
    nb = nblk // batch
    nbp = max(8, -(-nb // 8) * 8)
    return pl.pallas_call(
        functools.partial(_moba_kernel, nb, nbp),
        grid=(batch, nb),
        in_specs=_attn_common_specs(nb, QV_MOBA_Q, KN_MOBA, QV_MOBA_V) + [
            _small_spec((N_HEADS, 2, TB, TB))],
        out_specs=pl.BlockSpec((1, TB, TB), lambda b, i: (b * nb + i, 0, 0)),
        out_shape=jax.ShapeDtypeStruct((nblk, TB, TB), BF16),
        scratch_shapes=[pltpu.VMEM((nbp, TB), F32),
                        pltpu.VMEM((N_HEADS, nbp, TB), F32)] + _FlashScratch.shapes(N_HEADS),
        compiler_params=_cparams(2),
        name="moba_attn",
    )(qvt, kn, qvt, tiles)


def _sb_kernel(qt_ref, k_ref, vt_ref, o_ref, qpad_scr, acc_scr, c_scr):
    i = pl.program_id(1)
    rows = lax.broadcasted_iota(jnp.int32, (TB, TB), 0)
    cols = lax.broadcasted_iota(jnp.int32, (TB, TB), 1)
    upper = (cols > rows).astype(BF16)
    strict = cols > rows

    def weights(h, j, first):
        g = h // 2
        row = pl.multiple_of(j * TB, TB)
        z = jnp.dot(k_ref[pl.ds(row, TB), g * LANES:(g + 1) * LANES], qpad_scr[h],
                    preferred_element_type=F32)
        softplus = jnp.maximum(z, 0.0) + jnp.log(1.0 + jnp.exp(-jnp.abs(z)))
        log_keep = -softplus
        if first:
            log_keep = jnp.where(strict, log_keep, 0.0)
        hi = log_keep.astype(BF16)
        lo = (log_keep - hi.astype(F32)).astype(BF16)
        later = (jnp.dot(upper, hi, preferred_element_type=F32)
                 + jnp.dot(upper, lo, preferred_element_type=F32))
        if not first:
            later = later + c_scr[h]
        w = jnp.exp(z + log_keep + later)
        if first:
            w = jnp.where(strict, w, 0.0)
        pv = jnp.dot(vt_ref[j, h * HEAD_DIM:(h + 1) * HEAD_DIM, :], w.astype(BF16),
                     preferred_element_type=F32)
        c_new = later[0:1, :] + log_keep[0:1, :]
        return pv, c_new

    for h in range(N_HEADS):
        qpad_scr[h] = _pad_rows(qt_ref[0, h * HEAD_DIM:(h + 1) * HEAD_DIM, :], (h % 2) * HEAD_DIM, LANES)

    for h in range(N_HEADS):
        pv, c_new = weights(h, i, True)
        acc_scr[h] = pv
        c_scr[h] = c_new

        def cond(carry):
            j, go = carry
            return jnp.logical_and(j >= 0, go)

        def body(carry, h=h):
            j, _ = carry
            pv, c_new = weights(h, j, False)
            acc_scr[h] = acc_scr[h] + pv
            c_scr[h] = c_new
            return j - 1, jnp.max(c_new) > -SB_EXIT

        lax.while_loop(cond, body, (i - 1, jnp.max(c_new) > -SB_EXIT))
        o_ref[0, h * HEAD_DIM:(h + 1) * HEAD_DIM, :] = acc_scr[h].astype(BF16)


def _sb_attention(kn, qvt, batch):
    nblk = qvt.shape[0]
    nb = nblk // batch
    return pl.pallas_call(
        _sb_kernel,
        grid=(batch, nb),
        in_specs=_attn_common_specs(nb, QV_SB_Q, KN_SB, QV_SB_V),
        out_specs=pl.BlockSpec((1, TB, TB), lambda b, i: (b * nb + i, 0, 0)),
        out_shape=jax.ShapeDtypeStruct((nblk, TB, TB), BF16),
        scratch_shapes=[pltpu.VMEM((N_HEADS, LANES, TB), BF16),
                        pltpu.VMEM((N_HEADS, HEAD_DIM, TB), F32),
                        pltpu.VMEM((N_HEADS, 1, TB), F32)],
        compiler_params=_cparams(2),
        name="sb_attn",
    )(qvt, kn, qvt)


def _swa_kernel(sink_ref, qt_ref, kc_ref, kp_ref, vc_ref, vp_ref, tc_ref, tp_ref, o_ref):
    i = pl.program_id(1)
    kc = kc_ref[...]
    kp = kp_ref[...]
    vc = vc_ref[0]
    vp = vp_ref[0][:, TB - SWA_WINDOW:]
    no_prev = jnp.where(i > 0, 0.0, NEG)
    group = N_HEADS // SWA_KV_HEADS
    for h in range(N_HEADS):
        kv = h // group
        qpad = _pad_rows(qt_ref[0, h * HEAD_DIM:(h + 1) * HEAD_DIM, :], kv * HEAD_DIM, LANES)
        s_c = jnp.dot(kc, qpad, preferred_element_type=F32) + tc_ref[h]
        s_p = jnp.dot(kp, qpad, preferred_element_type=F32) + tp_ref[h] + no_prev
        sink = sink_ref[h]
        m = jnp.maximum(jnp.maximum(jnp.max(s_c, axis=0, keepdims=True),
                                    jnp.max(s_p, axis=0, keepdims=True)), sink)
        p_c = jnp.exp(s_c - m).astype(BF16)
        p_p = jnp.exp(s_p - m).astype(BF16)
        o = (jnp.dot(_with_ones(vc[kv * HEAD_DIM:(kv + 1) * HEAD_DIM]), p_c, preferred_element_type=F32)
             + jnp.dot(_with_ones(vp[kv * HEAD_DIM:(kv + 1) * HEAD_DIM]), p_p, preferred_element_type=F32))
        denom = o[HEAD_DIM:HEAD_DIM + 1] + jnp.exp(sink - m)
        o_ref[0, h * HEAD_DIM:(h + 1) * HEAD_DIM, :] = (o[:HEAD_DIM] / denom).astype(BF16)


def _swa_attention(kn, qvt, tile_cur, tile_prev, sinks, batch):
    nblk = qvt.shape[0]
    nb = nblk // batch
    half = TB // SWA_WINDOW
    kv_rows = SWA_KV_HEADS * HEAD_DIM
    return pl.pallas_call(
        _swa_kernel,
        grid=(batch, nb),
        in_specs=[pl.BlockSpec(memory_space=pltpu.SMEM),
                  pl.BlockSpec((1, TB, TB), lambda b, i: (b * nb + i, QV_SWA_Q, 0)),
                  pl.BlockSpec((TB, kv_rows), lambda b, i: (b * nb + i, KN_SWA_128)),
                  pl.BlockSpec((SWA_WINDOW, kv_rows),
                               lambda b, i: (b * nb * half + jnp.maximum(half * i - 1, 0), KN_SWA_128)),
                  pl.BlockSpec((1, kv_rows, TB), lambda b, i: (b * nb + i, QV_SWA_V_128, 0)),
                  pl.BlockSpec((1, kv_rows, TB), lambda b, i: (b * nb + jnp.maximum(i - 1, 0), QV_SWA_V_128, 0)),
                  _small_spec((N_HEADS, TB, TB)),
                  _small_spec((N_HEADS, SWA_WINDOW, TB))],
        out_specs=pl.BlockSpec((1, TB, TB), lambda b, i: (b * nb + i, 0, 0)),
        out_shape=jax.ShapeDtypeStruct((nblk, TB, TB), BF16),
        compiler_params=_cparams(2),
        name="swa_attn",
    )(sinks, qvt, kn, kn, qvt, qvt, tile_cur, tile_prev)


MERGE_T = 512


def _merge_kernel(x_ref, g_ref, oa_ref, ob_ref, oc_ref, od_ref, wg_ref, wbr_ref, wo_ref, x1_ref):
    x = x_ref[...]
    h = _rms(x, g_ref[...]).astype(BF16)
    merged = None
    for bi, o_ref in enumerate((oa_ref, ob_ref, oc_ref, od_ref)):
        gate = jax.nn.sigmoid(jnp.dot(h, wg_ref[bi], preferred_element_type=F32))
        branch = jnp.concatenate(
            [lax.dot_general(o_ref[s], wbr_ref[bi], (((0,), (0,)), ((), ())), preferred_element_type=F32)
             for s in range(MERGE_T // TB)], axis=0)
        term = gate * branch
        merged = term if merged is None else merged + term
    x1_ref[...] = x + jnp.dot(merged.astype(BF16), wo_ref[...], preferred_element_type=F32)


def _merge(x2, g, o_a, o_b, o_c, o_d, wg, wbr, wo):
    n, d = x2.shape
    o_spec = pl.BlockSpec((MERGE_T // TB, TB, TB), lambda i: (i, 0, 0))
    return pl.pallas_call(
        _merge_kernel,
        grid=(n // MERGE_T,),
        in_specs=[pl.BlockSpec((MERGE_T, d), lambda i: (i, 0)),
                  pl.BlockSpec((1, d), lambda i: (0, 0)),
                  o_spec, o_spec, o_spec, o_spec,
                  pl.BlockSpec(wg.shape, lambda i: (0, 0, 0)),
                  pl.BlockSpec(wbr.shape, lambda i: (0, 0, 0)),
                  pl.BlockSpec(wo.shape, lambda i: (0, 0))],
        out_specs=pl.BlockSpec((MERGE_T, d), lambda i: (i, 0)),
        out_shape=jax.ShapeDtypeStruct((n, d), F32),
        compiler_params=_cparams(1),
        name="merge",
    )(x2, g, o_a, o_b, o_c, o_d, wg, wbr, wo)


ROUTER_T = 512
ROUTER_ROWS = 8 + N_EXPERTS


def _first_argmax_rows(v, n_rows):
    best = jnp.max(v, axis=0, keepdims=True)
    ids = lax.broadcasted_iota(jnp.int32, v.shape, 0)
    return best, jnp.min(jnp.where(v == best, ids, n_rows), axis=0, keepdims=True)


def _router_kernel(x_ref, g_ref, whi_ref, wlo_ref, b_ref, h2_ref, eid_ref, gate_ref, rank_ref, cnt_ref,
                   base_scr):
    i = pl.program_id(0)

    @pl.when(i == 0)
    def _():
        base_scr[...] = jnp.zeros(base_scr.shape, F32)

    h2 = _rms(x_ref[...], g_ref[...])
    h2_ref[...] = h2
    h_hi = h2.astype(BF16)
    h_lo = (h2 - h_hi.astype(F32)).astype(BF16)
    nt = (((1,), (1,)), ((), ()))
    logits = (lax.dot_general(whi_ref[...], h_hi, nt, preferred_element_type=F32)
              + lax.dot_general(whi_ref[...], h_lo, nt, preferred_element_type=F32)
              + lax.dot_general(wlo_ref[...], h_hi, nt, preferred_element_type=F32)
              + b_ref[...])
    gl = logits[0:8]
    gmax, grp = _first_argmax_rows(gl, 8)
    p_grp = 1.0 / jnp.sum(jnp.exp(gl - gmax), axis=0, keepdims=True)
    e_sel = jnp.zeros((EXPERTS_PER_GROUP, ROUTER_T), F32)
    for g in range(N_GROUPS):
        e_sel = jnp.where(grp == g, logits[8 + 8 * g:16 + 8 * g], e_sel)
    ids8 = lax.broadcasted_iota(jnp.int32, e_sel.shape, 0)
    v1, i1 = _first_argmax_rows(e_sel, EXPERTS_PER_GROUP)
    e_rest = jnp.where(ids8 == i1, -jnp.inf, e_sel)
    v2, i2 = _first_argmax_rows(e_rest, EXPERTS_PER_GROUP)
    r = jnp.exp(v2 - v1)
    s1 = 1.0 / (1.0 + r)
    gate_ref[0:1, :] = p_grp * s1
    gate_ref[1:2, :] = p_grp * (r * s1)
    e1 = grp * EXPERTS_PER_GROUP + i1
    e2 = grp * EXPERTS_PER_GROUP + i2
    eid_ref[0:1, :] = e1
    eid_ref[1:2, :] = e2

    ids_e = lax.broadcasted_iota(jnp.int32, (N_EXPERTS, ROUTER_T), 0)
    oh1 = ids_e == e1
    oh2 = ids_e == e2
    cnt = oh1.astype(F32) + oh2.astype(F32)
    tr = lax.broadcasted_iota(jnp.int32, (ROUTER_T, ROUTER_T), 0)
    tc = lax.broadcasted_iota(jnp.int32, (ROUTER_T, ROUTER_T), 1)
    before = (tr < tc).astype(BF16)
    prefix = jnp.dot(cnt.astype(BF16), before, preferred_element_type=F32) + base_scr[:, 0:1]
    rank_ref[0:1, :] = jnp.sum(jnp.where(oh1, prefix, 0.0), axis=0, keepdims=True).astype(jnp.int32)
    rank_ref[1:2, :] = jnp.sum(jnp.where(oh2, prefix, 0.0), axis=0, keepdims=True).astype(jnp.int32)
    base_scr[...] = base_scr[...] + jnp.sum(cnt, axis=1, keepdims=True)
    cnt_ref[...] = base_scr[...]


def _router(x1, g, w_hi, w_lo, bias):
    n, d = x1.shape
    row2 = lambda dt: jax.ShapeDtypeStruct((2, n), dt)
    spec2 = pl.BlockSpec((2, ROUTER_T), lambda i: (0, i))
    return pl.pallas_call(
        _router_kernel,
        grid=(n // ROUTER_T,),
        in_specs=[pl.BlockSpec((ROUTER_T, d), lambda i: (i, 0)),
                  pl.BlockSpec((1, d), lambda i: (0, 0)),
                  pl.BlockSpec((ROUTER_ROWS, d), lambda i: (0, 0)),
                  pl.BlockSpec((ROUTER_ROWS, d), lambda i: (0, 0)),
                  pl.BlockSpec((ROUTER_ROWS, 1), lambda i: (0, 0))],
        out_specs=[pl.BlockSpec((ROUTER_T, d), lambda i: (i, 0)), spec2, spec2, spec2,
                   pl.BlockSpec((N_EXPERTS, LANES), lambda i: (0, 0))],
        out_shape=[jax.ShapeDtypeStruct((n, d), F32), row2(jnp.int32), row2(F32), row2(jnp.int32),
                   jax.ShapeDtypeStruct((N_EXPERTS, LANES), F32)],
        scratch_shapes=[pltpu.VMEM((N_EXPERTS, LANES), F32)],
        compiler_params=_cparams(1),
        name="router",
    )(x1, g, w_hi, w_lo, bias)


ROW_T = 256


def _dispatch_kernel(dest_ref, h2_ref, xin_ref, xbuf_ref, sem):
    del xin_ref

    def row_copy(r, k):
        return pltpu.make_async_copy(h2_ref.at[pl.ds(r, 1)], xbuf_ref.at[pl.ds(dest_ref[k, r], 1)], sem)

    def issue(r, carry):
        row_copy(r, 0).start()
        row_copy(r, 1).start()
        return carry

    def drain(r, carry):
        row_copy(r, 0).wait()
        row_copy(r, 1).wait()
        return carry

    lax.fori_loop(0, ROW_T, issue, 0)
    lax.fori_loop(0, ROW_T, drain, 0)


def _dispatch(dest3, h2, xbuf_init):
    n, d = h2.shape
    return pl.pallas_call(
        _dispatch_kernel,
        grid=(n // ROW_T,),
        in_specs=[pl.BlockSpec((None, 2, ROW_T), lambda i: (i, 0, 0), memory_space=pltpu.SMEM),
                  pl.BlockSpec((ROW_T, d), lambda i: (i, 0)),
                  pl.BlockSpec(memory_space=pl.ANY)],
        out_specs=pl.BlockSpec(memory_space=pl.ANY),
        out_shape=jax.ShapeDtypeStruct(xbuf_init.shape, xbuf_init.dtype),
        scratch_shapes=[pltpu.SemaphoreType.DMA(())],
        input_output_aliases={2: 0},
        compiler_params=_cparams(1),
        name="dispatch",
    )(dest3, h2, xbuf_init)


def _expert_kernel(be_ref, nu_ref, x_ref, w1_ref, w3_ref, w2_ref, y_ref):
    del be_ref
    used = pl.program_id(0) < nu_ref[0]

    @pl.when(used)
    def _():
        xb = x_ref[...].astype(BF16)
        a = jnp.dot(xb, w1_ref[...], preferred_element_type=F32)
        b = jnp.dot(xb, w3_ref[...], preferred_element_type=F32)
        mid = (a * jax.nn.sigmoid(a) * b).astype(BF16)
        y_ref[...] = jnp.dot(mid, w2_ref[...], preferred_element_type=F32)

    @pl.when(jnp.logical_not(used))
    def _():
        y_ref[...] = jnp.zeros(y_ref.shape, F32)


def _experts(blk_expert, n_used, xbuf, w1, w3, w2):
    r, d = xbuf.shape
    de = w1.shape[-1]
    row_map = lambda i, be, nu: (jnp.minimum(i, nu[0] - 1), 0)
    grid_spec = pltpu.PrefetchScalarGridSpec(
        num_scalar_prefetch=2,
        grid=(r // MOE_ROWS,),
        in_specs=[pl.BlockSpec((MOE_ROWS, d), row_map),
                  pl.BlockSpec((None, d, de), lambda i, be, nu: (be[i], 0, 0)),
                  pl.BlockSpec((None, d, de), lambda i, be, nu: (be[i], 0, 0)),
                  pl.BlockSpec((None, de, d), lambda i, be, nu: (be[i], 0, 0))],
        out_specs=pl.BlockSpec((MOE_ROWS, d), lambda i, be, nu: (i, 0)),
    )
    return pl.pallas_call(
        _expert_kernel,
        grid_spec=grid_spec,
        out_shape=jax.ShapeDtypeStruct((r, d), F32),
        compiler_params=_cparams(1),
        name="experts",
    )(blk_expert, n_used, xbuf, w1, w3, w2)


def _combine_kernel(final, dest_ref, gate_ref, x1_ref, gf_ref, y_ref, out_ref, buf, sem):
    def row_copy(r, k):
        return pltpu.make_async_copy(y_ref.at[pl.ds(dest_ref[k, r], 1)], buf.at[k, pl.ds(r, 1)], sem)

    def issue(r, carry):
        row_copy(r, 0).start()
        row_copy(r, 1).start()
        return carry

    def drain(r, carry):
        row_copy(r, 0).wait()
        row_copy(r, 1).wait()
        return carry

    lax.fori_loop(0, ROW_T, issue, 0)
    lax.fori_loop(0, ROW_T, drain, 0)
    gate = gate_ref[...]
    out = x1_ref[...] + gate[:, 0:1] * buf[0] + gate[:, 1:2] * buf[1]
    if final:
        out = _rms(out, gf_ref[...])
    out_ref[...] = out


def _combine(dest3, gate_t, x1, g_final, y, final):
    n, d = x1.shape
    return pl.pallas_call(
        functools.partial(_combine_kernel, final),
        grid=(n // ROW_T,),
        in_specs=[pl.BlockSpec((None, 2, ROW_T), lambda i: (i, 0, 0), memory_space=pltpu.SMEM),
                  pl.BlockSpec((ROW_T, 2), lambda i: (i, 0)),
                  pl.BlockSpec((ROW_T, d), lambda i: (i, 0)),
                  pl.BlockSpec((1, d), lambda i: (0, 0)),
                  pl.BlockSpec(memory_space=pl.ANY)],
        out_specs=pl.BlockSpec((ROW_T, d), lambda i: (i, 0)),
        out_shape=jax.ShapeDtypeStruct((n, d), F32),
        scratch_shapes=[pltpu.VMEM((2, ROW_T, d), F32), pltpu.SemaphoreType.DMA(())],
        compiler_params=_cparams(1),
        name="combine",
    )(dest3, gate_t, x1, g_final, y)


def _projection_weights(w):
    d = w.shape[0]
    blk = N_HEADS * HEAD_DIM
    kv = SWA_KV_HEADS * HEAD_DIM
    pa, pb, pc, pd = w[:, :3 * blk], w[:, 3 * blk:6 * blk], w[:, 6 * blk:9 * blk], w[:, 9 * blk:]
    half = N_HEADS * DIFF_QK

    def per_head(a, b):
        return jnp.stack([a.reshape(d, N_HEADS, DIFF_QK), b.reshape(d, N_HEADS, DIFF_QK)], axis=2).reshape(d, blk)

    s64, s32 = HEAD_DIM ** -0.5, DIFF_QK ** -0.5
    wn = jnp.concatenate([pa[:, blk:2 * blk], per_head(pb[:, 2 * half:3 * half], pb[:, 3 * half:4 * half]),
                          pc[:, blk:2 * blk], pd[:, blk:blk + kv]], axis=1)
    wt = jnp.concatenate([pa[:, :blk] * (s64 * LOG2E), pa[:, 2 * blk:],
                          per_head(pb[:, :half], pb[:, half:2 * half]) * (s32 * LOG2E), pb[:, 4 * half:],
                          pc[:, :blk] * s64, pc[:, 2 * blk:],
                          pd[:, :blk] * s64, pd[:, blk + kv:]], axis=1)
    assert wn.shape[1] == KN_COLS and wt.shape[1] == QV_ROWS
    return wn.astype(BF16), wt.T.astype(BF16)


def _router_weights(w_rg, b_rg, w_re, b_re):
    d = w_rg.shape[0]
    w = jnp.concatenate([w_rg.T, jnp.zeros((8 - N_GROUPS, d), F32), w_re.T], axis=0)
    b = jnp.concatenate([b_rg.astype(F32), jnp.full((8 - N_GROUPS,), NEG, F32), b_re.astype(F32)])[:, None]
    w_hi = w.astype(BF16)
    w_lo = (w - w_hi.astype(F32)).astype(BF16)
    return w_hi, w_lo, b


def _moe_plan(eid, rank, counts, n_rows_total):
    padded = (counts + MOE_ROWS - 1) // MOE_ROWS * MOE_ROWS
    pad_end = jnp.cumsum(padded)
    pad_start = pad_end - padded
    experts = jnp.arange(N_EXPERTS, dtype=jnp.int32)
    start_of = jnp.sum(jnp.where(eid[..., None] == experts, pad_start, 0), axis=-1)
    dest = start_of + rank
    n_blk = n_rows_total // MOE_ROWS
    n_used = (pad_end[-1] // MOE_ROWS).astype(jnp.int32)
    blk = jnp.minimum(jnp.arange(n_blk, dtype=jnp.int32), n_used - 1) * MOE_ROWS
    blk_expert = jnp.minimum(jnp.sum(pad_end[None, :] <= blk[:, None], axis=1), N_EXPERTS - 1).astype(jnp.int32)
    return dest.astype(jnp.int32), blk_expert, n_used.reshape(1)


def kernel(x, rel_bias, g_mix, w_in, diff_lq1, diff_lk1, diff_lq2, diff_lk2, diff_subln, swa_sinks,
           w_gate, w_br, w_o, g_ffn, w_route_group, b_route_group, w_route_expert, b_route_expert,
           w1, w3, w2, g_final):
    batch, seq, d = x.shape
    n = batch * seq
    depth = w_in.shape[0]
    assert seq % TB == 0 and n % MERGE_T == 0 and TB == MOBA_BLOCK
    tab = rel_bias.T.astype(F32)
    tiles_moba = _causal_bias_tiles(tab[:N_HEADS])
    tiles_diff = _causal_bias_tiles(tab[N_HEADS:2 * N_HEADS])
    tile_cur, tile_prev = _swa_bias_tiles(tab[2 * N_HEADS:])
    n_rows_total = n * 2 + N_EXPERTS * MOE_ROWS
    row = lambda v: v.astype(F32)[None, :]

    x2 = x.reshape(n, d)
    for l in range(depth):
        lam_init = 0.8 - 0.6 * math.exp(-0.3 * l)
        wn, wt = _projection_weights(w_in[l])
        kn, qvt = _inproj(x2, row(g_mix[l]), wn, wt)
        o_a = _moba_attention(kn, qvt, tiles_moba, batch)
        o_b = _diff_attention(kn, qvt, tiles_diff, row(diff_lq1[l]), row(diff_lk1[l]), row(diff_lq2[l]),
                              row(diff_lk2[l]), diff_subln[l].astype(F32)[:, None], lam_init, batch)
        o_c = _sb_attention(kn, qvt, batch)
        o_d = _swa_attention(kn, qvt, tile_cur, tile_prev, swa_sinks[l].astype(F32), batch)
        x1 = _merge(x2, row(g_mix[l]), o_a, o_b, o_c, o_d, w_gate[l].astype(BF16), w_br[l].astype(BF16),
                    w_o[l].astype(BF16))
        w_hi, w_lo, r_bias = _router_weights(w_route_group[l], b_route_group[l], w_route_expert[l],
                                             b_route_expert[l])
        h2, eid, gate, rank, cnt = _router(x1, row(g_ffn[l]), w_hi, w_lo, r_bias)
        dest, blk_expert, n_used = _moe_plan(eid, rank, cnt[:, 0].astype(jnp.int32), n_rows_total)
        dest3 = dest.reshape(2, n // ROW_T, ROW_T).transpose(1, 0, 2)
        xbuf = _dispatch(dest3, h2, jnp.zeros((n_rows_total, d), F32))
        y = _experts(blk_expert, n_used, xbuf, w1[l].astype(BF16), w3[l].astype(BF16), w2[l].astype(BF16))
        x2 = _combine(dest3, gate.T, x1, row(g_final), y, l == depth - 1)
    return x2.reshape(batch, seq, d)
```

```python
import functools
import math

import numpy as np
import jax
import jax.numpy as jnp
from jax import lax
from jax.experimental import pallas as pl
from jax.experimental.pallas import tpu as pltpu

F32 = jnp.float32
BF16 = jnp.bfloat16

HEAD_DIM = 64
N_HEADS = 4
DIFF_QK = 32
SWA_KV_HEADS = 2
SWA_WINDOW = 128
MOBA_BLOCK = 256
MOBA_TOPK = 3
REL_BUCKETS = 32
REL_MAX_DIST = 128
N_GROUPS = 4
EXPERTS_PER_GROUP = 8
N_EXPERTS = N_GROUPS * EXPERTS_PER_GROUP
NORM_EPS = 1e-6

TB = 256
LANES = 128
ONES_ROWS = 16
NEG = -1e30
LOG2E = math.log2(math.e)
SB_EXIT = 104.0
MOE_ROWS = 256
VMEM_LIMIT = 56 * 1024 * 1024

QV_MOBA_Q, QV_MOBA_V, QV_DIFF_Q, QV_DIFF_V, QV_SB_Q, QV_SB_V, QV_SWA_Q = range(7)
QV_SWA_V_128 = 14
QV_ROWS = 7 * 256 + 128
KN_MOBA, KN_DIFF, KN_SB = range(3)
KN_SWA_128 = 6
KN_COLS = 3 * 256 + 128


def _cparams(n_grid):
    return pltpu.CompilerParams(dimension_semantics=("arbitrary",) * n_grid,
                                vmem_limit_bytes=VMEM_LIMIT)


def _rel_bucket_np(n):
    n = np.maximum(n, 0)
    max_exact = REL_BUCKETS // 2
    nf = np.maximum(n, 1).astype(np.float64)
    large = max_exact + (np.log(nf / max_exact) / math.log(REL_MAX_DIST / max_exact)
                         * (REL_BUCKETS - max_exact)).astype(np.int64)
    large = np.minimum(large, REL_BUCKETS - 1)
    return np.where(n < max_exact, n, large)


def _first_far_distance():
    d = np.arange(0, 4 * REL_MAX_DIST)
    b = _rel_bucket_np(d)
    return int(np.min(d[b == REL_BUCKETS - 1]))


def _toeplitz_bias(tab, rows, cols, base, valid_fn, shift_far, unit):
    length = rows + cols - 1
    off = np.concatenate([np.arange(0, cols), np.arange(cols - length, 0)])
    n = base + off
    onehot = np.zeros((REL_BUCKETS, length), np.float32)
    onehot[_rel_bucket_np(n), np.arange(length)] = 1.0
    vec = jnp.dot(tab, jnp.asarray(onehot), precision=lax.Precision.HIGHEST)
    if shift_far:
        vec = vec - tab[:, REL_BUCKETS - 1:]
    vec = jnp.where(jnp.asarray(valid_fn(n))[None, :], vec * unit, NEG).astype(F32)
    flat = jnp.tile(vec, (1, rows))[:, :rows * (length - 1)]
    return flat.reshape(tab.shape[0], rows, length - 1)[:, :, :cols]


def _causal_bias_tiles(tab):
    assert _first_far_distance() <= TB + 1
    tiles = [_toeplitz_bias(tab, TB, TB, d * TB, lambda n: n >= 0, True, LOG2E) for d in range(2)]
    return jnp.stack(tiles, axis=1)


def _swa_bias_tiles(tab):
    in_window = lambda n: (n >= 0) & (n < SWA_WINDOW)
    return (_toeplitz_bias(tab, TB, TB, 0, in_window, False, 1.0),
            _toeplitz_bias(tab, SWA_WINDOW, TB, SWA_WINDOW, in_window, False, 1.0))


def _pad_rows(q, off, total):
    n, t = q.shape
    parts = []
    if off:
        parts.append(jnp.zeros((off, t), q.dtype))
    parts.append(q)
    if total - off - n:
        parts.append(jnp.zeros((total - off - n, t), q.dtype))
    return jnp.concatenate(parts, axis=0) if len(parts) > 1 else q


def _with_ones(v):
    return jnp.concatenate([v, jnp.ones((ONES_ROWS, v.shape[1]), v.dtype)], axis=0)


def _rms(x, g_row):
    ms = jnp.mean(x * x, axis=-1, keepdims=True)
    return x * lax.rsqrt(ms + NORM_EPS) * g_row


IN_T = 512
IN_CHUNK = 384


def _inproj_kernel(x_ref, g_ref, wn_ref, wt_ref, kn_ref, qvt_ref):
    h = _rms(x_ref[...], g_ref[...]).astype(BF16)
    kn_ref[...] = jnp.dot(h, wn_ref[...], preferred_element_type=F32).astype(BF16)
    for r0 in range(0, QV_ROWS, IN_CHUNK):
        pt = lax.dot_general(wt_ref[r0:r0 + IN_CHUNK, :], h, (((1,), (1,)), ((), ())),
                             preferred_element_type=F32)
        for s in range(IN_T // TB):
            qvt_ref[s, r0:r0 + IN_CHUNK, :] = pt[:, s * TB:(s + 1) * TB].astype(BF16)


def _inproj(x2, g, wn, wt):
    n, d = x2.shape
    return pl.pallas_call(
        _inproj_kernel,
        grid=(n // IN_T,),
        in_specs=[pl.BlockSpec((IN_T, d), lambda i: (i, 0)),
                  pl.BlockSpec((1, d), lambda i: (0, 0)),
                  pl.BlockSpec((d, KN_COLS), lambda i: (0, 0)),
                  pl.BlockSpec((QV_ROWS, d), lambda i: (0, 0))],
        out_specs=[pl.BlockSpec((IN_T, KN_COLS), lambda i: (i, 0)),
                   pl.BlockSpec((IN_T // TB, QV_ROWS, TB), lambda i: (i, 0, 0))],
        out_shape=[jax.ShapeDtypeStruct((n, KN_COLS), BF16),
                   jax.ShapeDtypeStruct((n // TB, QV_ROWS, TB), BF16)],
        compiler_params=_cparams(1),
        name="inproj",
    )(x2, g, wn, wt)


class _FlashScratch:
    def __init__(self, n_chain, qpad, m, acc, s, mx, p):
        self.n_chain, self.qpad, self.m, self.acc, self.s, self.mx, self.p = n_chain, qpad, m, acc, s, mx, p

    @staticmethod
    def shapes(n_chain):
        return [pltpu.VMEM((n_chain, LANES, TB), BF16),
                pltpu.VMEM((n_chain, 1, TB), F32),
                pltpu.VMEM((n_chain, HEAD_DIM + ONES_ROWS, TB), F32),
                pltpu.VMEM((2 * n_chain, TB, TB), F32),
                pltpu.VMEM((2 * n_chain, 1, TB), F32),
                pltpu.VMEM((2 * n_chain, TB, TB), BF16)]


def _flash_scores(sc, slot, chains):
    for c, kblk, add in chains:
        s = jnp.dot(kblk, sc.qpad[c], preferred_element_type=F32)
        if add is not None:
            s = s + add
        sc.s[slot * sc.n_chain + c] = s
        sc.mx[slot * sc.n_chain + c] = jnp.max(s, axis=0, keepdims=True)


def _flash_update(sc, slot, chains, next_scores=None):
    alphas = []
    for idx, (c, _) in enumerate(chains):
        if next_scores is not None:
            _flash_scores(sc, next_scores[0], next_scores[1][idx:idx + 1])
        m_old = sc.m[c]
        m_new = jnp.maximum(m_old, sc.mx[slot * sc.n_chain + c])
        sc.p[slot * sc.n_chain + c] = jnp.exp2((sc.s[slot * sc.n_chain + c] - m_new).astype(BF16))
        alphas.append(jnp.exp2(m_old - m_new))
        sc.m[c] = m_new
    for (c, vext), alpha in zip(chains, alphas):
        sc.acc[c] = sc.acc[c] * alpha + jnp.dot(vext, sc.p[slot * sc.n_chain + c],
                                                preferred_element_type=F32)


def _flash_causal(sc, i, score_chains, value_chains):
    for c in range(sc.n_chain):
        sc.m[c] = jnp.full(sc.m.shape[1:], NEG, F32)
        sc.acc[c] = jnp.zeros(sc.acc.shape[1:], F32)
    _flash_scores(sc, 0, score_chains(i, 0))
    _flash_update(sc, 0, value_chains(i))

    @pl.when(i >= 1)
    def _():
        _flash_scores(sc, 0, score_chains(i - 1, 1))
        _flash_update(sc, 0, value_chains(i - 1))

    n_far = jnp.maximum(i - 1, 0)

    @pl.when(n_far > 0)
    def _():
        _flash_scores(sc, 0, score_chains(0, None))

    def far_pair(t, carry):
        j = 2 * t
        _flash_update(sc, 0, value_chains(j), (1, score_chains(j + 1, None)))
        _flash_update(sc, 1, value_chains(j + 1), (0, score_chains(jnp.minimum(j + 2, n_far - 1), None)))
        return carry

    lax.fori_loop(0, n_far // 2, far_pair, 0)

    @pl.when(lax.rem(n_far, 2) == 1)
    def _():
        _flash_update(sc, 0, value_chains(n_far - 1))


def _diff_kernel(lam_init, qt_ref, k_ref, vt_ref, tile_ref, lq1_ref, lk1_ref, lq2_ref, lk2_ref,
                 subln_ref, o_ref, *scratch):
    i = pl.program_id(1)
    sc = _FlashScratch(2 * N_HEADS, *scratch)
    acc_scr = sc.acc
    for h in range(N_HEADS):
        for mp in range(2):
            r0 = h * HEAD_DIM + mp * DIFF_QK
            sc.qpad[2 * h + mp] = _pad_rows(qt_ref[0, r0:r0 + DIFF_QK, :],
                                            (h % 2) * HEAD_DIM + mp * DIFF_QK, LANES)

    def score_chains(j, tile_idx):
        row = pl.multiple_of(j * TB, TB)
        chains = []
        for h in range(N_HEADS):
            g = h // 2
            kblk = k_ref[pl.ds(row, TB), g * LANES:(g + 1) * LANES]
            add = None if tile_idx is None else tile_ref[h, tile_idx]
            chains += [(2 * h + mp, kblk, add) for mp in range(2)]
        return chains

    def value_chains(j):
        chains = []
        for h in range(N_HEADS):
            vext = _with_ones(vt_ref[j, h * HEAD_DIM:(h + 1) * HEAD_DIM, :])
            chains += [(2 * h + mp, vext) for mp in range(2)]
        return chains

    _flash_causal(sc, i, score_chains, value_chains)

    lam =(jnp.exp(jnp.sum(lq1_ref[...] * lk1_ref[...], keepdims=True))
           - jnp.exp(jnp.sum(lq2_ref[...] * lk2_ref[...], keepdims=True)) + lam_init)
    for h in range(N_HEADS):
        a1 = acc_scr[2 * h]
        a2 = acc_scr[2 * h + 1]
        o1 = a1[:HEAD_DIM] / a1[HEAD_DIM:HEAD_DIM + 1]
        o2 = a2[:HEAD_DIM] / a2[HEAD_DIM:HEAD_DIM + 1]
        a = o1 - lam * o2
        ms = jnp.mean(a * a, axis=0, keepdims=True)
        y = a * lax.rsqrt(ms + NORM_EPS) * subln_ref[...] * (1.0 - lam_init)
        o_ref[0, h * HEAD_DIM:(h + 1) * HEAD_DIM, :] = y.astype(BF16)


def _attn_common_specs(nb, q_blk, k_blk, v_blk):
    return [pl.BlockSpec((1, TB, TB), lambda b, i: (b * nb + i, q_blk, 0)),
            pl.BlockSpec((nb * TB, TB), lambda b, i: (b, k_blk)),
            pl.BlockSpec((nb, TB, TB), lambda b, i: (b, v_blk, 0))]


def _small_spec(shape):
    return pl.BlockSpec(shape, lambda b, i: (0,) * len(shape))


def _diff_attention(kn, qvt, tiles, lq1, lk1, lq2, lk2, subln, lam_init, batch):
    nblk = qvt.shape[0]
    nb = nblk // batch
    n_chain = 2 * N_HEADS
    return pl.pallas_call(
        functools.partial(_diff_kernel, lam_init),
        grid=(batch, nb),
        in_specs=_attn_common_specs(nb, QV_DIFF_Q, KN_DIFF, QV_DIFF_V) + [
            _small_spec((N_HEADS, 2, TB, TB)),
            _small_spec((1, DIFF_QK)), _small_spec((1, DIFF_QK)),
            _small_spec((1, DIFF_QK)), _small_spec((1, DIFF_QK)),
            _small_spec((HEAD_DIM, 1))],
        out_specs=pl.BlockSpec((1, TB, TB), lambda b, i: (b * nb + i, 0, 0)),
        out_shape=jax.ShapeDtypeStruct((nblk, TB, TB), BF16),
        scratch_shapes=_FlashScratch.shapes(n_chain),
        compiler_params=_cparams(2),
        name="diff_attn",
    )(qvt, kn, qvt, tiles, lq1, lk1, lq2, lk2, subln)


def _moba_kernel(nb, nbp, qt_ref, k_ref, vt_ref, tile_ref, o_ref, kmean_scr, sel_scr, *scratch):
    i = pl.program_id(1)
    sc = _FlashScratch(N_HEADS, *scratch)
    qpad_scr, acc_scr = sc.qpad, sc.acc

    @pl.when(i == 0)
    def _():
        kmean_scr[...] = jnp.zeros(kmean_scr.shape, F32)
        for jb in range(nb):
            blk = k_ref[jb * TB:(jb + 1) * TB, :].astype(F32)
            kmean_scr[jb:jb + 1, :] = jnp.mean(blk, axis=0, keepdims=True)

    for h in range(N_HEADS):
        qpad_scr[h] = _pad_rows(qt_ref[0, h * HEAD_DIM:(h + 1) * HEAD_DIM, :], (h % 2) * HEAD_DIM, LANES)

    blk_id = lax.broadcasted_iota(jnp.int32, (nbp, TB), 0)
    for h in range(N_HEADS):
        g = h // 2
        km = kmean_scr[:, g * LANES:(g + 1) * LANES].astype(BF16)
        gate = jnp.dot(km, qpad_scr[h], preferred_element_type=F32)
        avail = blk_id < i
        sel = jnp.zeros((nbp, TB), jnp.bool_)
        for _ in range(MOBA_TOPK):
            gm = jnp.where(avail, gate, -jnp.inf)
            best = jnp.max(gm, axis=0, keepdims=True)
            is_best = avail & (gm == best)
            first = jnp.min(jnp.where(is_best, blk_id, nbp), axis=0, keepdims=True)
            pick = blk_id == first
            sel = sel | pick
            avail = avail & jnp.logical_not(pick)
        sel_scr[h] = jnp.where(sel, 0.0, NEG).astype(F32)

    def score_chains(j, tile_idx):
        row = pl.multiple_of(j * TB, TB)
        chains = []
        for h in range(N_HEADS):
            g = h // 2
            kblk = k_ref[pl.ds(row, TB), g * LANES:(g + 1) * LANES]
            add = None if tile_idx is None else tile_ref[h, tile_idx]
            if tile_idx != 0:
                cm = sel_scr[h, pl.ds(j, 1), :]
                add = cm if add is None else add + cm
            chains.append((h, kblk, add))
        return chains

    def value_chains(j):
        return [(h, _with_ones(vt_ref[j, h * HEAD_DIM:(h + 1) * HEAD_DIM, :])) for h in range(N_HEADS)]

    _flash_causal(sc, i, score_chains, value_chains)

    for h in range(N_HEADS):
        a = acc_scr[h]
        o_ref[0, h * HEAD_DIM:(h + 1) * HEAD_DIM, :] = (a[:HEAD_DIM] / a[HEAD_DIM:HEAD_DIM + 1]).astype(BF16)


def _moba_attention(kn, qvt, tiles, batch):
    nblk = qvt.shape[0]
    nb = nblk // batch
    nbp = max(8, -(-nb // 8) * 8)
    return pl.pallas_call(
        functools.partial(_moba_kernel, nb, nbp),
        grid=(batch, nb),
        in_specs=_attn_common_specs(nb, QV_MOBA_Q, KN_MOBA, QV_MOBA_V) + [
            _small_spec((N_HEADS, 2, TB, TB))],
        out_specs=pl.BlockSpec((1, TB, TB), lambda b, i: (b * nb + i, 0, 0)),
        out_shape=jax.ShapeDtypeStruct((nblk, TB, TB), BF16),
        scratch_shapes=[pltpu.VMEM((nbp, TB), F32),
                        pltpu.VMEM((N_HEADS, nbp, TB), F32)] + _FlashScratch.shapes(N_HEADS),
        compiler_params=_cparams(2),
        name="moba_attn",
    )(qvt, kn, qvt, tiles)


def _sb_kernel(qt_ref, k_ref, vt_ref, o_ref, qpad_scr, acc_scr, c_scr):
    i = pl.program_id(1)
    rows = lax.broadcasted_iota(jnp.int32, (TB, TB), 0)
    cols = lax.broadcasted_iota(jnp.int32, (TB, TB), 1)
    upper = (cols > rows).astype(BF16)
    strict = cols > rows

    heads = range(N_HEADS)

    def step(j, first):
        row = pl.multiple_of(j * TB, TB)
        zs = [jnp.dot(k_ref[pl.ds(row, TB), (h // 2) * LANES:(h // 2 + 1) * LANES], qpad_scr[h],
                      preferred_element_type=F32) for h in heads]
        log_keeps = []
        for z in zs:
            lk = -(jnp.maximum(z, 0.0) + jnp.log(1.0 + jnp.exp(-jnp.abs(z))))
            log_keeps.append(jnp.where(strict, lk, 0.0) if first else lk)
        laters = []
        for h in heads:
            hi = log_keeps[h].astype(BF16)
            lo = (log_keeps[h] - hi.astype(F32)).astype(BF16)
            later = (jnp.dot(upper, hi, preferred_element_type=F32)
                     + jnp.dot(upper, lo, preferred_element_type=F32))
            laters.append(later if first else later + c_scr[h])
        c_max = None
        for h in heads:
            w = jnp.exp(zs[h] + log_keeps[h] + laters[h])
            if first:
                w = jnp.where(strict, w, 0.0)
            pv = jnp.dot(vt_ref[j, h * HEAD_DIM:(h + 1) * HEAD_DIM, :], w.astype(BF16),
                         preferred_element_type=F32)
            acc_scr[h] = pv if first else acc_scr[h] + pv
            c_new = laters[h][0:1, :] + log_keeps[h][0:1, :]
            c_scr[h] = c_new
            c_max = c_new if c_max is None else jnp.maximum(c_max, c_new)
        return jnp.max(c_max) > -SB_EXIT

    for h in heads:
        qpad_scr[h] = _pad_rows(qt_ref[0, h * HEAD_DIM:(h + 1) * HEAD_DIM, :], (h % 2) * HEAD_DIM, LANES)

    def cond(carry):
        j, go = carry
        return jnp.logical_and(j >= 0, go)

    def body(carry):
        j, _ = carry
        return j - 1, step(j, False)

    lax.while_loop(cond, body, (i - 1, step(i, True)))
    for h in heads:
        o_ref[0, h * HEAD_DIM:(h + 1) * HEAD_DIM, :] = acc_scr[h].astype(BF16)


def _sb_attention(kn, qvt, batch):
    nblk = qvt.shape[0]
    nb = nblk // batch
    return pl.pallas_call(
        _sb_kernel,
        grid=(batch, nb),
        in_specs=_attn_common_specs(nb, QV_SB_Q, KN_SB, QV_SB_V),
        out_specs=pl.BlockSpec((1, TB, TB), lambda b, i: (b * nb + i, 0, 0)),
        out_shape=jax.ShapeDtypeStruct((nblk, TB, TB), BF16),
        scratch_shapes=[pltpu.VMEM((N_HEADS, LANES, TB), BF16),
                        pltpu.VMEM((N_HEADS, HEAD_DIM, TB), F32),
                        pltpu.VMEM((N_HEADS, 1, TB), F32)],
        compiler_params=_cparams(2),
        name="sb_attn",
    )(qvt, kn, qvt)


def _swa_kernel(sink_ref, qt_ref, kc_ref, kp_ref, vc_ref, vp_ref, tc_ref, tp_ref, o_ref):
    i = pl.program_id(1)
    kc = kc_ref[...]
    kp = kp_ref[...]
    vc = vc_ref[0]
    vp = vp_ref[0][:, TB - SWA_WINDOW:]
    no_prev = jnp.where(i > 0, 0.0, NEG)
    group = N_HEADS // SWA_KV_HEADS
    for h in range(N_HEADS):
        kv = h // group
        qpad = _pad_rows(qt_ref[0, h * HEAD_DIM:(h + 1) * HEAD_DIM, :], kv * HEAD_DIM, LANES)
        s_c = jnp.dot(kc, qpad, preferred_element_type=F32) + tc_ref[h]
        s_p = jnp.dot(kp, qpad, preferred_element_type=F32) + tp_ref[h] + no_prev
        sink = sink_ref[h]
        m = jnp.maximum(jnp.maximum(jnp.max(s_c, axis=0, keepdims=True),
                                    jnp.max(s_p, axis=0, keepdims=True)), sink)
        p_c = jnp.exp(s_c - m).astype(BF16)
        p_p = jnp.exp(s_p - m).astype(BF16)
        o = (jnp.dot(_with_ones(vc[kv * HEAD_DIM:(kv + 1) * HEAD_DIM]), p_c, preferred_element_type=F32)
             + jnp.dot(_with_ones(vp[kv * HEAD_DIM:(kv + 1) * HEAD_DIM]), p_p, preferred_element_type=F32))
        denom = o[HEAD_DIM:HEAD_DIM + 1] + jnp.exp(sink - m)
        o_ref[0, h * HEAD_DIM:(h + 1) * HEAD_DIM, :] = (o[:HEAD_DIM] / denom).astype(BF16)


def _swa_attention(kn, qvt, tile_cur, tile_prev, sinks, batch):
    nblk = qvt.shape[0]
    nb = nblk // batch
    half = TB // SWA_WINDOW
    kv_rows = SWA_KV_HEADS * HEAD_DIM
    return pl.pallas_call(
        _swa_kernel,
        grid=(batch, nb),
        in_specs=[pl.BlockSpec(memory_space=pltpu.SMEM),
                  pl.BlockSpec((1, TB, TB), lambda b, i: (b * nb + i, QV_SWA_Q, 0)),
                  pl.BlockSpec((TB, kv_rows), lambda b, i: (b * nb + i, KN_SWA_128)),
                  pl.BlockSpec((SWA_WINDOW, kv_rows),
                               lambda b, i: (b * nb * half + jnp.maximum(half * i - 1, 0), KN_SWA_128)),
                  pl.BlockSpec((1, kv_rows, TB), lambda b, i: (b * nb + i, QV_SWA_V_128, 0)),
                  pl.BlockSpec((1, kv_rows, TB), lambda b, i: (b * nb + jnp.maximum(i - 1, 0), QV_SWA_V_128, 0)),
                  _small_spec((N_HEADS, TB, TB)),
                  _small_spec((N_HEADS, SWA_WINDOW, TB))],
        out_specs=pl.BlockSpec((1, TB, TB), lambda b, i: (b * nb + i, 0, 0)),
        out_shape=jax.ShapeDtypeStruct((nblk, TB, TB), BF16),
        compiler_params=_cparams(2),
        name="swa_attn",
    )(sinks, qvt, kn, kn, qvt, qvt, tile_cur, tile_prev)


MERGE_T = 512


def _merge_kernel(x_ref, g_ref, oa_ref, ob_ref, oc_ref, od_ref, wg_ref, wbr_ref, wo_ref, x1_ref):
    x = x_ref[...]
    h = _rms(x, g_ref[...]).astype(BF16)
    merged = None
    for bi, o_ref in enumerate((oa_ref, ob_ref, oc_ref, od_ref)):
        gate = jax.nn.sigmoid(jnp.dot(h, wg_ref[bi], preferred_element_type=F32))
        branch = jnp.concatenate(
            [lax.dot_general(o_ref[s], wbr_ref[bi], (((0,), (0,)), ((), ())), preferred_element_type=F32)
             for s in range(MERGE_T // TB)], axis=0)
        term = gate * branch
        merged = term if merged is None else merged + term
    x1_ref[...] = x + jnp.dot(merged.astype(BF16), wo_ref[...], preferred_element_type=F32)


def _merge(x2, g, o_a, o_b, o_c, o_d, wg, wbr, wo):
    n, d = x2.shape
    o_spec = pl.BlockSpec((MERGE_T // TB, TB, TB), lambda i: (i, 0, 0))
    return pl.pallas_call(
        _merge_kernel,
        grid=(n // MERGE_T,),
        in_specs=[pl.BlockSpec((MERGE_T, d), lambda i: (i, 0)),
                  pl.BlockSpec((1, d), lambda i: (0, 0)),
                  o_spec, o_spec, o_spec, o_spec,
                  pl.BlockSpec(wg.shape, lambda i: (0, 0, 0)),
                  pl.BlockSpec(wbr.shape, lambda i: (0, 0, 0)),
                  pl.BlockSpec(wo.shape, lambda i: (0, 0))],
        out_specs=pl.BlockSpec((MERGE_T, d), lambda i: (i, 0)),
        out_shape=jax.ShapeDtypeStruct((n, d), F32),
        compiler_params=_cparams(1),
        name="merge",
    )(x2, g, o_a, o_b, o_c, o_d, wg, wbr, wo)


ROUTER_T = 512
ROUTER_ROWS = 8 + N_EXPERTS


def _first_argmax_rows(v, n_rows):
    best = jnp.max(v, axis=0, keepdims=True)
    ids = lax.broadcasted_iota(jnp.int32, v.shape, 0)
    return best, jnp.min(jnp.where(v == best, ids, n_rows), axis=0, keepdims=True)


def _router_kernel(x_ref, g_ref, whi_ref, wlo_ref, b_ref, h2_ref, eid_ref, gate_ref, rank_ref, cnt_ref,
                   base_scr):
    i = pl.program_id(0)

    @pl.when(i == 0)
    def _():
        base_scr[...] = jnp.zeros(base_scr.shape, F32)

    h2 = _rms(x_ref[...], g_ref[...])
    h2_ref[...] = h2
    h_hi = h2.astype(BF16)
    h_lo = (h2 - h_hi.astype(F32)).astype(BF16)
    nt = (((1,), (1,)), ((), ()))
    logits = (lax.dot_general(whi_ref[...], h_hi, nt, preferred_element_type=F32)
              + lax.dot_general(whi_ref[...], h_lo, nt, preferred_element_type=F32)
              + lax.dot_general(wlo_ref[...], h_hi, nt, preferred_element_type=F32)
              + b_ref[...])
    gl = logits[0:8]
    gmax, grp = _first_argmax_rows(gl, 8)
    p_grp = 1.0 / jnp.sum(jnp.exp(gl - gmax), axis=0, keepdims=True)
    e_sel = jnp.zeros((EXPERTS_PER_GROUP, ROUTER_T), F32)
    for g in range(N_GROUPS):
        e_sel = jnp.where(grp == g, logits[8 + 8 * g:16 + 8 * g], e_sel)
    ids8 = lax.broadcasted_iota(jnp.int32, e_sel.shape, 0)
    v1, i1 = _first_argmax_rows(e_sel, EXPERTS_PER_GROUP)
    e_rest = jnp.where(ids8 == i1, -jnp.inf, e_sel)
    v2, i2 = _first_argmax_rows(e_rest, EXPERTS_PER_GROUP)
    r = jnp.exp(v2 - v1)
    s1 = 1.0 / (1.0 + r)
    gate_ref[0:1, :] = p_grp * s1
    gate_ref[1:2, :] = p_grp * (r * s1)
    e1 = grp * EXPERTS_PER_GROUP + i1
    e2 = grp * EXPERTS_PER_GROUP + i2
    eid_ref[0:1, :] = e1
    eid_ref[1:2, :] = e2

    ids_e = lax.broadcasted_iota(jnp.int32, (N_EXPERTS, ROUTER_T), 0)
    oh1 = ids_e == e1
    oh2 = ids_e == e2
    cnt = oh1.astype(F32) + oh2.astype(F32)
    tr = lax.broadcasted_iota(jnp.int32, (ROUTER_T, ROUTER_T), 0)
    tc = lax.broadcasted_iota(jnp.int32, (ROUTER_T, ROUTER_T), 1)
    before = (tr < tc).astype(BF16)
    prefix = jnp.dot(cnt.astype(BF16), before, preferred_element_type=F32) + base_scr[:, 0:1]
    rank_ref[0:1, :] = jnp.sum(jnp.where(oh1, prefix, 0.0), axis=0, keepdims=True).astype(jnp.int32)
    rank_ref[1:2, :] = jnp.sum(jnp.where(oh2, prefix, 0.0), axis=0, keepdims=True).astype(jnp.int32)
    base_scr[...] = base_scr[...] + jnp.sum(cnt, axis=1, keepdims=True)
    cnt_ref[...] = base_scr[...]


def _router(x1, g, w_hi, w_lo, bias):
    n, d = x1.shape
    row2 = lambda dt: jax.ShapeDtypeStruct((2, n), dt)
    spec2 = pl.BlockSpec((2, ROUTER_T), lambda i: (0, i))
    return pl.pallas_call(
        _router_kernel,
        grid=(n // ROUTER_T,),
        in_specs=[pl.BlockSpec((ROUTER_T, d), lambda i: (i, 0)),
                  pl.BlockSpec((1, d), lambda i: (0, 0)),
                  pl.BlockSpec((ROUTER_ROWS, d), lambda i: (0, 0)),
                  pl.BlockSpec((ROUTER_ROWS, d), lambda i: (0, 0)),
                  pl.BlockSpec((ROUTER_ROWS, 1), lambda i: (0, 0))],
        out_specs=[pl.BlockSpec((ROUTER_T, d), lambda i: (i, 0)), spec2, spec2, spec2,
                   pl.BlockSpec((N_EXPERTS, LANES), lambda i: (0, 0))],
        out_shape=[jax.ShapeDtypeStruct((n, d), F32), row2(jnp.int32), row2(F32), row2(jnp.int32),
                   jax.ShapeDtypeStruct((N_EXPERTS, LANES), F32)],
        scratch_shapes=[pltpu.VMEM((N_EXPERTS, LANES), F32)],
        compiler_params=_cparams(1),
        name="router",
    )(x1, g, w_hi, w_lo, bias)


ROW_T = 256


def _dispatch_kernel(dest_ref, h2_ref, xin_ref, xbuf_ref, sem):
    del xin_ref

    def row_copy(r, k):
        return pltpu.make_async_copy(h2_ref.at[pl.ds(r, 1)], xbuf_ref.at[pl.ds(dest_ref[k, r], 1)], sem)

    def issue(r, carry):
        row_copy(r, 0).start()
        row_copy(r, 1).start()
        return carry

    def drain(r, carry):
        row_copy(r, 0).wait()
        row_copy(r, 1).wait()
        return carry

    lax.fori_loop(0, ROW_T, issue, 0, unroll=8)
    lax.fori_loop(0, ROW_T, drain, 0, unroll=8)


def _dispatch(dest3, h2, xbuf_init):
    n, d = h2.shape
    return pl.pallas_call(
        _dispatch_kernel,
        grid=(n // ROW_T,),
        in_specs=[pl.BlockSpec((None, 2, ROW_T), lambda i: (i, 0, 0), memory_space=pltpu.SMEM),
                  pl.BlockSpec((ROW_T, d), lambda i: (i, 0)),
                  pl.BlockSpec(memory_space=pl.ANY)],
        out_specs=pl.BlockSpec(memory_space=pl.ANY),
        out_shape=jax.ShapeDtypeStruct(xbuf_init.shape, xbuf_init.dtype),
        scratch_shapes=[pltpu.SemaphoreType.DMA(())],
        input_output_aliases={2: 0},
        compiler_params=_cparams(1),
        name="dispatch",
    )(dest3, h2, xbuf_init)


def _expert_kernel(be_ref, nu_ref, x_ref, w1_ref, w3_ref, w2_ref, y_ref):
    del be_ref
    used = pl.program_id(0) < nu_ref[0]

    @pl.when(used)
    def _():
        xb = x_ref[...].astype(BF16)
        a = jnp.dot(xb, w1_ref[...], preferred_element_type=F32)
        b = jnp.dot(xb, w3_ref[...], preferred_element_type=F32)
        mid = (a * jax.nn.sigmoid(a) * b).astype(BF16)
        y_ref[...] = jnp.dot(mid, w2_ref[...], preferred_element_type=F32)

    @pl.when(jnp.logical_not(used))
    def _():
        y_ref[...] = jnp.zeros(y_ref.shape, F32)


def _experts(blk_expert, n_used, xbuf, w1, w3, w2):
    r, d = xbuf.shape
    de = w1.shape[-1]
    row_map = lambda i, be, nu: (jnp.minimum(i, nu[0] - 1), 0)
    grid_spec = pltpu.PrefetchScalarGridSpec(
        num_scalar_prefetch=2,
        grid=(r // MOE_ROWS,),
        in_specs=[pl.BlockSpec((MOE_ROWS, d), row_map),
                  pl.BlockSpec((None, d, de), lambda i, be, nu: (be[i], 0, 0)),
                  pl.BlockSpec((None, d, de), lambda i, be, nu: (be[i], 0, 0)),
                  pl.BlockSpec((None, de, d), lambda i, be, nu: (be[i], 0, 0))],
        out_specs=pl.BlockSpec((MOE_ROWS, d), lambda i, be, nu: (i, 0)),
    )
    return pl.pallas_call(
        _expert_kernel,
        grid_spec=grid_spec,
        out_shape=jax.ShapeDtypeStruct((r, d), F32),
        compiler_params=_cparams(1),
        name="experts",
    )(blk_expert, n_used, xbuf, w1, w3, w2)


def _combine_kernel(final, dest_ref, gate_ref, x1_ref, gf_ref, y_ref, out_ref, buf, sem):
    def row_copy(r, k):
        return pltpu.make_async_copy(y_ref.at[pl.ds(dest_ref[k, r], 1)], buf.at[k, pl.ds(r, 1)], sem)

    def issue(r, carry):
        row_copy(r, 0).start()
        row_copy(r, 1).start()
        return carry

    def drain(r, carry):
        row_copy(r, 0).wait()
        row_copy(r, 1).wait()
        return carry

    lax.fori_loop(0, ROW_T, issue, 0, unroll=8)
    lax.fori_loop(0, ROW_T, drain, 0, unroll=8)
    gate = gate_ref[...]
    out = x1_ref[...] + gate[:, 0:1] * buf[0] + gate[:, 1:2] * buf[1]
    if final:
        out = _rms(out, gf_ref[...])
    out_ref[...] = out


def _combine(dest3, gate_t, x1, g_final, y, final):
    n, d = x1.shape
    return pl.pallas_call(
        functools.partial(_combine_kernel, final),
        grid=(n // ROW_T,),
        in_specs=[pl.BlockSpec((None, 2, ROW_T), lambda i: (i, 0, 0), memory_space=pltpu.SMEM),
                  pl.BlockSpec((ROW_T, 2), lambda i: (i, 0)),
                  pl.BlockSpec((ROW_T, d), lambda i: (i, 0)),
                  pl.BlockSpec((1, d), lambda i: (0, 0)),
                  pl.BlockSpec(memory_space=pl.ANY)],
        out_specs=pl.BlockSpec((ROW_T, d), lambda i: (i, 0)),
        out_shape=jax.ShapeDtypeStruct((n, d), F32),
        scratch_shapes=[pltpu.VMEM((2, ROW_T, d), F32), pltpu.SemaphoreType.DMA(())],
        compiler_params=_cparams(1),
        name="combine",
    )(dest3, gate_t, x1, g_final, y)


def _projection_weights(w):
    d = w.shape[0]
    blk = N_HEADS * HEAD_DIM
    kv = SWA_KV_HEADS * HEAD_DIM
    pa, pb, pc, pd = w[:, :3 * blk], w[:, 3 * blk:6 * blk], w[:, 6 * blk:9 * blk], w[:, 9 * blk:]
    half = N_HEADS * DIFF_QK

    def per_head(a, b):
        return jnp.stack([a.reshape(d, N_HEADS, DIFF_QK), b.reshape(d, N_HEADS, DIFF_QK)], axis=2).reshape(d, blk)

    s64, s32 = HEAD_DIM ** -0.5, DIFF_QK ** -0.5
    wn = jnp.concatenate([pa[:, blk:2 * blk], per_head(pb[:, 2 * half:3 * half], pb[:, 3 * half:4 * half]),
                          pc[:, blk:2 * blk], pd[:, blk:blk + kv]], axis=1)
    wt = jnp.concatenate([pa[:, :blk] * (s64 * LOG2E), pa[:, 2 * blk:],
                          per_head(pb[:, :half], pb[:, half:2 * half]) * (s32 * LOG2E), pb[:, 4 * half:],
                          pc[:, :blk] * s64, pc[:, 2 * blk:],
                          pd[:, :blk] * s64, pd[:, blk + kv:]], axis=1)
    assert wn.shape[1] == KN_COLS and wt.shape[1] == QV_ROWS
    return wn.astype(BF16), wt.T.astype(BF16)


def _router_weights(w_rg, b_rg, w_re, b_re):
    d = w_rg.shape[0]
    w = jnp.concatenate([w_rg.T, jnp.zeros((8 - N_GROUPS, d), F32), w_re.T], axis=0)
    b = jnp.concatenate([b_rg.astype(F32), jnp.full((8 - N_GROUPS,), NEG, F32), b_re.astype(F32)])[:, None]
    w_hi = w.astype(BF16)
    w_lo = (w - w_hi.astype(F32)).astype(BF16)
    return w_hi, w_lo, b


def _moe_plan(eid, rank, counts, n_rows_total):
    padded = (counts + MOE_ROWS - 1) // MOE_ROWS * MOE_ROWS
    pad_end = jnp.cumsum(padded)
    pad_start = pad_end - padded
    experts = jnp.arange(N_EXPERTS, dtype=jnp.int32)
    start_of = jnp.sum(jnp.where(eid[..., None] == experts, pad_start, 0), axis=-1)
    dest = start_of + rank
    n_blk = n_rows_total // MOE_ROWS
    n_used = (pad_end[-1] // MOE_ROWS).astype(jnp.int32)
    blk = jnp.minimum(jnp.arange(n_blk, dtype=jnp.int32), n_used - 1) * MOE_ROWS
    blk_expert = jnp.minimum(jnp.sum(pad_end[None, :] <= blk[:, None], axis=1), N_EXPERTS - 1).astype(jnp.int32)
    return dest.astype(jnp.int32), blk_expert, n_used.reshape(1)


def kernel(x, rel_bias, g_mix, w_in, diff_lq1, diff_lk1, diff_lq2, diff_lk2, diff_subln, swa_sinks,
           w_gate, w_br, w_o, g_ffn, w_route_group, b_route_group, w_route_expert, b_route_expert,
           w1, w3, w2, g_final):
    batch, seq, d = x.shape
    n = batch * seq
    depth = w_in.shape[0]
    assert seq % TB == 0 and n % MERGE_T == 0 and TB == MOBA_BLOCK
    tab = rel_bias.T.astype(F32)
    tiles_moba = _causal_bias_tiles(tab[:N_HEADS])
    tiles_diff = _causal_bias_tiles(tab[N_HEADS:2 * N_HEADS])
    tile_cur, tile_prev = _swa_bias_tiles(tab[2 * N_HEADS:])
    n_rows_total = n * 2 + N_EXPERTS * MOE_ROWS
    row = lambda v: v.astype(F32)[None, :]

    x2 = x.reshape(n, d)
    for l in range(depth):
        lam_init = 0.8 - 0.6 * math.exp(-0.3 * l)
        wn, wt = _projection_weights(w_in[l])
        kn, qvt = _inproj(x2, row(g_mix[l]), wn, wt)
        o_a = _moba_attention(kn, qvt, tiles_moba, batch)
        o_b = _diff_attention(kn, qvt, tiles_diff, row(diff_lq1[l]), row(diff_lk1[l]), row(diff_lq2[l]),
                              row(diff_lk2[l]), diff_subln[l].astype(F32)[:, None], lam_init, batch)
        o_c = _sb_attention(kn, qvt, batch)
        o_d = _swa_attention(kn, qvt, tile_cur, tile_prev, swa_sinks[l].astype(F32), batch)
        x1 = _merge(x2, row(g_mix[l]), o_a, o_b, o_c, o_d, w_gate[l].astype(BF16), w_br[l].astype(BF16),
                    w_o[l].astype(BF16))
        w_hi, w_lo, r_bias = _router_weights(w_route_group[l], b_route_group[l], w_route_expert[l],
                                             b_route_expert[l])
        h2, eid, gate, rank, cnt = _router(x1, row(g_ffn[l]), w_hi, w_lo, r_bias)
        dest, blk_expert, n_used = _moe_plan(eid, rank, cnt[:, 0].astype(jnp.int32), n_rows_total)
        dest3 = dest.reshape(2, n // ROW_T, ROW_T).transpose(1, 0, 2)
        xbuf = _dispatch(dest3, h2, jnp.zeros((n_rows_total, d), F32))
        y = _experts(blk_expert, n_used, xbuf, w1[l].astype(BF16), w3[l].astype(BF16), w2[l].astype(BF16))
        x2 = _combine(dest3, gate.T, x1, row(g_final), y, l == depth - 1)
    return x2.reshape(batch, seq, d)
```

```python
import functools
import math

import numpy as np
import jax
import jax.numpy as jnp
from jax import lax
from jax.experimental import pallas as pl
from jax.experimental.pallas import tpu as pltpu

F32 = jnp.float32
BF16 = jnp.bfloat16

HEAD_DIM = 64
N_HEADS = 4
DIFF_QK = 32
SWA_KV_HEADS = 2
SWA_WINDOW = 128
MOBA_BLOCK = 256
MOBA_TOPK = 3
REL_BUCKETS = 32
REL_MAX_DIST = 128
N_GROUPS = 4
EXPERTS_PER_GROUP = 8
N_EXPERTS = N_GROUPS * EXPERTS_PER_GROUP
NORM_EPS = 1e-6

TB = 256
LANES = 128
ONES_ROWS = 16
NEG = -1e30
LOG2E = math.log2(math.e)
SB_EXIT = 104.0
MOE_ROWS = 256
VMEM_LIMIT = 56 * 1024 * 1024

QV_MOBA_Q, QV_MOBA_V, QV_DIFF_Q, QV_DIFF_V, QV_SB_Q, QV_SB_V, QV_SWA_Q = range(7)
QV_SWA_V_128 = 14
QV_ROWS = 7 * 256 + 128
KN_MOBA, KN_DIFF, KN_SB = range(3)
KN_SWA_128 = 6
KN_COLS = 3 * 256 + 128


def _cparams(n_grid):
    return pltpu.CompilerParams(dimension_semantics=("arbitrary",) * n_grid,
                                vmem_limit_bytes=VMEM_LIMIT)


def _rel_bucket_np(n):
    n = np.maximum(n, 0)
    max_exact = REL_BUCKETS // 2
    nf = np.maximum(n, 1).astype(np.float64)
    large = max_exact + (np.log(nf / max_exact) / math.log(REL_MAX_DIST / max_exact)
                         * (REL_BUCKETS - max_exact)).astype(np.int64)
    large = np.minimum(large, REL_BUCKETS - 1)
    return np.where(n < max_exact, n, large)


def _first_far_distance():
    d = np.arange(0, 4 * REL_MAX_DIST)
    b = _rel_bucket_np(d)
    return int(np.min(d[b == REL_BUCKETS - 1]))


def _toeplitz_bias(tab, rows, cols, base, valid_fn, shift_far, unit):
    length = rows + cols - 1
    off = np.concatenate([np.arange(0, cols), np.arange(cols - length, 0)])
    n = base + off
    onehot = np.zeros((REL_BUCKETS, length), np.float32)
    onehot[_rel_bucket_np(n), np.arange(length)] = 1.0
    vec = jnp.dot(tab, jnp.asarray(onehot), precision=lax.Precision.HIGHEST)
    if shift_far:
        vec = vec - tab[:, REL_BUCKETS - 1:]
    vec = jnp.where(jnp.asarray(valid_fn(n))[None, :], vec * unit, NEG).astype(F32)
    flat = jnp.tile(vec, (1, rows))[:, :rows * (length - 1)]
    return flat.reshape(tab.shape[0], rows, length - 1)[:, :, :cols]


def _causal_bias_tiles(tab):
    assert _first_far_distance() <= TB + 1
    tiles = [_toeplitz_bias(tab, TB, TB, d * TB, lambda n: n >= 0, True, LOG2E) for d in range(2)]
    return jnp.stack(tiles, axis=1)


def _swa_bias_tiles(tab):
    in_window = lambda n: (n >= 0) & (n < SWA_WINDOW)
    return (_toeplitz_bias(tab, TB, TB, 0, in_window, False, 1.0),
            _toeplitz_bias(tab, SWA_WINDOW, TB, SWA_WINDOW, in_window, False, 1.0))


def _pad_rows(q, off, total):
    n, t = q.shape
    parts = []
    if off:
        parts.append(jnp.zeros((off, t), q.dtype))
    parts.append(q)
    if total - off - n:
        parts.append(jnp.zeros((total - off - n, t), q.dtype))
    return jnp.concatenate(parts, axis=0) if len(parts) > 1 else q


def _with_ones(v):
    return jnp.concatenate([v, jnp.ones((ONES_ROWS, v.shape[1]), v.dtype)], axis=0)


def _rms(x, g_row):
    ms = jnp.mean(x * x, axis=-1, keepdims=True)
    return x * lax.rsqrt(ms + NORM_EPS) * g_row


IN_T = 512
IN_CHUNK = 384


def _inproj_kernel(x_ref, g_ref, wn_ref, wt_ref, kn_ref, qvt_ref):
    h = _rms(x_ref[...], g_ref[...]).astype(BF16)
    kn_ref[...] = jnp.dot(h, wn_ref[...], preferred_element_type=F32).astype(BF16)
    for r0 in range(0, QV_ROWS, IN_CHUNK):
        pt = lax.dot_general(wt_ref[r0:r0 + IN_CHUNK, :], h, (((1,), (1,)), ((), ())),
                             preferred_element_type=F32)
        for s in range(IN_T // TB):
            qvt_ref[s, r0:r0 + IN_CHUNK, :] = pt[:, s * TB:(s + 1) * TB].astype(BF16)


def _inproj(x2, g, wn, wt):
    n, d = x2.shape
    return pl.pallas_call(
        _inproj_kernel,
        grid=(n // IN_T,),
        in_specs=[pl.BlockSpec((IN_T, d), lambda i: (i, 0)),
                  pl.BlockSpec((1, d), lambda i: (0, 0)),
                  pl.BlockSpec((d, KN_COLS), lambda i: (0, 0)),
                  pl.BlockSpec((QV_ROWS, d), lambda i: (0, 0))],
        out_specs=[pl.BlockSpec((IN_T, KN_COLS), lambda i: (i, 0)),
                   pl.BlockSpec((IN_T // TB, QV_ROWS, TB), lambda i: (i, 0, 0))],
        out_shape=[jax.ShapeDtypeStruct((n, KN_COLS), BF16),
                   jax.ShapeDtypeStruct((n // TB, QV_ROWS, TB), BF16)],
        compiler_params=_cparams(1),
        name="inproj",
    )(x2, g, wn, wt)


FLASH_OVERFLOW = 100.0
FLASH_SKEW = 4


class _FlashScratch:
    def __init__(self, n_chain, qpad, m, acc, over, s):
        self.n_chain, self.qpad, self.m, self.acc, self.over, self.s = n_chain, qpad, m, acc, over, s

    @staticmethod
    def shapes(n_chain):
        return [pltpu.VMEM((n_chain, LANES, TB), BF16),
                pltpu.VMEM((n_chain, 1, TB), F32),
                pltpu.VMEM((n_chain, HEAD_DIM + ONES_ROWS, TB), F32),
                pltpu.VMEM((n_chain, 1, TB), F32),
                pltpu.VMEM((n_chain, TB, TB), F32)]


def _flash_two_pass(sc, chains):
    if len(chains) > sc.n_chain:
        for k in range(0, len(chains), sc.n_chain):
            _flash_two_pass(sc, chains[k:k + sc.n_chain])
        return
    block_max = []
    for c, kblk, add, _ in chains:
        s = jnp.dot(kblk, sc.qpad[c], preferred_element_type=F32)
        if add is not None:
            s = s + add
        sc.s[c] = s
        block_max.append(jnp.max(s, axis=0, keepdims=True))
    for (c, _, _, vext), mx in zip(chains, block_max):
        m_old = sc.m[c]
        m_new = jnp.maximum(m_old, mx)
        p = jnp.exp2(sc.s[c] - m_new).astype(BF16)
        sc.acc[c] = sc.acc[c] * jnp.exp2(m_old - m_new) + jnp.dot(vext, p, preferred_element_type=F32)
        sc.m[c] = m_new


def _flash_lagged(sc, chains):
    def finish(c, p, mx, ref, vext):
        pv = jnp.dot(vext, p, preferred_element_type=F32)
        ref_new = jnp.maximum(ref, mx)
        sc.acc[c] = (sc.acc[c] + pv) * jnp.exp2(ref - ref_new)
        sc.m[c] = ref_new
        sc.over[c] = jnp.maximum(sc.over[c], mx - ref)

    pending = []
    for c, kblk, add, vext in chains:
        s = jnp.dot(kblk, sc.qpad[c], preferred_element_type=F32)
        if add is not None:
            s = s + add
        if len(pending) >= min(FLASH_SKEW, sc.n_chain):
            finish(*pending.pop(0))
        ref = sc.m[c]
        pending.append((c, jnp.exp2(s - ref).astype(BF16), jnp.max(s, axis=0, keepdims=True), ref, vext))
    for item in pending:
        finish(*item)


def _flash_causal(sc, i, block_chains):
    def init():
        for c in range(sc.n_chain):
            sc.m[c] = jnp.full(sc.m.shape[1:], NEG, F32)
            sc.acc[c] = jnp.zeros(sc.acc.shape[1:], F32)
            sc.over[c] = jnp.full(sc.over.shape[1:], NEG, F32)

    def run(step):
        init()
        _flash_two_pass(sc, block_chains(i, 0))

        @pl.when(i >= 1)
        def _():
            step(sc, block_chains(i - 1, 1))

        n_far = jnp.maximum(i - 1, 0)

        def far_pair(t, carry):
            step(sc, block_chains(2 * t, None) + block_chains(2 * t + 1, None))
            return carry

        lax.fori_loop(0, n_far // 2, far_pair, 0)

        @pl.when(lax.rem(n_far, 2) == 1)
        def _():
            step(sc, block_chains(n_far - 1, None))

    run(_flash_lagged)
    worst = sc.over[0]
    for c in range(1, sc.n_chain):
        worst = jnp.maximum(worst, sc.over[c])

    @pl.when(jnp.max(worst) > FLASH_OVERFLOW)
    def _():
        run(_flash_two_pass)


def _diff_kernel(lam_init, qt_ref, k_ref, vt_ref, tile_ref, lq1_ref, lk1_ref, lq2_ref, lk2_ref,
                 subln_ref, o_ref, *scratch):
    i = pl.program_id(1)
    sc = _FlashScratch(2 * N_HEADS, *scratch)
    acc_scr = sc.acc
    for h in range(N_HEADS):
        for mp in range(2):
            r0 = h * HEAD_DIM + mp * DIFF_QK
            sc.qpad[2 * h + mp] = _pad_rows(qt_ref[0, r0:r0 + DIFF_QK, :],
                                            (h % 2) * HEAD_DIM + mp * DIFF_QK, LANES)

    def block_chains(j, tile_idx):
        row = pl.multiple_of(j * TB, TB)
        chains = []
        for h in range(N_HEADS):
            g = h // 2
            kblk = k_ref[pl.ds(row, TB), g * LANES:(g + 1) * LANES]
            add = None if tile_idx is None else tile_ref[h, tile_idx]
            vext = _with_ones(vt_ref[j, h * HEAD_DIM:(h + 1) * HEAD_DIM, :])
            chains += [(2 * h + mp, kblk, add, vext) for mp in range(2)]
        return chains

    _flash_causal(sc, i, block_chains)

    lam =(jnp.exp(jnp.sum(lq1_ref[...] * lk1_ref[...], keepdims=True))
           - jnp.exp(jnp.sum(lq2_ref[...] * lk2_ref[...], keepdims=True)) + lam_init)
    for h in range(N_HEADS):
        a1 = acc_scr[2 * h]
        a2 = acc_scr[2 * h + 1]
        o1 = a1[:HEAD_DIM] / a1[HEAD_DIM:HEAD_DIM + 1]
        o2 = a2[:HEAD_DIM] / a2[HEAD_DIM:HEAD_DIM + 1]
        a = o1 - lam * o2
        ms = jnp.mean(a * a, axis=0, keepdims=True)
        y = a * lax.rsqrt(ms + NORM_EPS) * subln_ref[...] * (1.0 - lam_init)
        o_ref[0, h * HEAD_DIM:(h + 1) * HEAD_DIM, :] = y.astype(BF16)


def _attn_common_specs(nb, q_blk, k_blk, v_blk):
    return [pl.BlockSpec((1, TB, TB), lambda b, i: (b * nb + i, q_blk, 0)),
            pl.BlockSpec((nb * TB, TB), lambda b, i: (b, k_blk)),
            pl.BlockSpec((nb, TB, TB), lambda b, i: (b, v_blk, 0))]


def _small_spec(shape):
    return pl.BlockSpec(shape, lambda b, i: (0,) * len(shape))


def _diff_attention(kn, qvt, tiles, lq1, lk1, lq2, lk2, subln, lam_init, batch):
    nblk = qvt.shape[0]
    nb = nblk // batch
    n_chain = 2 * N_HEADS
    return pl.pallas_call(
        functools.partial(_diff_kernel, lam_init),
        grid=(batch, nb),
        in_specs=_attn_common_specs(nb, QV_DIFF_Q, KN_DIFF, QV_DIFF_V) + [
            _small_spec((N_HEADS, 2, TB, TB)),
            _small_spec((1, DIFF_QK)), _small_spec((1, DIFF_QK)),
            _small_spec((1, DIFF_QK)), _small_spec((1, DIFF_QK)),
            _small_spec((HEAD_DIM, 1))],
        out_specs=pl.BlockSpec((1, TB, TB), lambda b, i: (b * nb + i, 0, 0)),
        out_shape=jax.ShapeDtypeStruct((nblk, TB, TB), BF16),
        scratch_shapes=_FlashScratch.shapes(n_chain),
        compiler_params=_cparams(2),
        name="diff_attn",
    )(qvt, kn, qvt, tiles, lq1, lk1, lq2, lk2, subln)


def _moba_kernel(nb, nbp, qt_ref, k_ref, vt_ref, tile_ref, o_ref, kmean_scr, sel_scr, *scratch):
    i = pl.program_id(1)
    sc = _FlashScratch(N_HEADS, *scratch)
    qpad_scr, acc_scr = sc.qpad, sc.acc

    @pl.when(i == 0)
    def _():
        kmean_scr[...] = jnp.zeros(kmean_scr.shape, F32)
        for jb in range(nb):
            blk = k_ref[jb * TB:(jb + 1) * TB, :].astype(F32)
            kmean_scr[jb:jb + 1, :] = jnp.mean(blk, axis=0, keepdims=True)

    for h in range(N_HEADS):
        qpad_scr[h] = _pad_rows(qt_ref[0, h * HEAD_DIM:(h + 1) * HEAD_DIM, :], (h % 2) * HEAD_DIM, LANES)

    blk_id = lax.broadcasted_iota(jnp.int32, (nbp, TB), 0)
    for h in range(N_HEADS):
        g = h // 2
        km = kmean_scr[:, g * LANES:(g + 1) * LANES].astype(BF16)
        gate = jnp.dot(km, qpad_scr[h], preferred_element_type=F32)
        avail = blk_id < i
        sel = jnp.zeros((nbp, TB), jnp.bool_)
        for _ in range(MOBA_TOPK):
            gm = jnp.where(avail, gate, -jnp.inf)
            best = jnp.max(gm, axis=0, keepdims=True)
            is_best = avail & (gm == best)
            first = jnp.min(jnp.where(is_best, blk_id, nbp), axis=0, keepdims=True)
            pick = blk_id == first
            sel = sel | pick
            avail = avail & jnp.logical_not(pick)
        sel_scr[h] = jnp.where(sel, 0.0, NEG).astype(F32)

    def block_chains(j, tile_idx):
        row = pl.multiple_of(j * TB, TB)
        chains = []
        for h in range(N_HEADS):
            g = h // 2
            kblk = k_ref[pl.ds(row, TB), g * LANES:(g + 1) * LANES]
            add = None if tile_idx is None else tile_ref[h, tile_idx]
            if tile_idx != 0:
                cm = sel_scr[h, pl.ds(j, 1), :]
                add = cm if add is None else add + cm
            chains.append((h, kblk, add, _with_ones(vt_ref[j, h * HEAD_DIM:(h + 1) * HEAD_DIM, :])))
        return chains

    _flash_causal(sc, i, block_chains)

    for h in range(N_HEADS):
        a = acc_scr[h]
        o_ref[0, h * HEAD_DIM:(h + 1) * HEAD_DIM, :] = (a[:HEAD_DIM] / a[HEAD_DIM:HEAD_DIM + 1]).astype(BF16)


def _moba_attention(kn, qvt, tiles, batch):
    nblk = qvt.shape[0]
    nb = nblk // batch
    nbp = max(8, -(-nb // 8) * 8)
    return pl.pallas_call(
        functools.partial(_moba_kernel, nb, nbp),
        grid=(batch, nb),
        in_specs=_attn_common_specs(nb, QV_MOBA_Q, KN_MOBA, QV_MOBA_V) + [
            _small_spec((N_HEADS, 2, TB, TB))],
        out_specs=pl.BlockSpec((1, TB, TB), lambda b, i: (b * nb + i, 0, 0)),
        out_shape=jax.ShapeDtypeStruct((nblk, TB, TB), BF16),
        scratch_shapes=[pltpu.VMEM((nbp, TB), F32),
                        pltpu.VMEM((N_HEADS, nbp, TB), F32)] + _FlashScratch.shapes(N_HEADS),
        compiler_params=_cparams(2),
        name="moba_attn",
    )(qvt, kn, qvt, tiles)


def _sb_kernel(qt_ref, k_ref, vt_ref, o_ref, qpad_scr, acc_scr, c_scr):
    i = pl.program_id(1)
    rows = lax.broadcasted_iota(jnp.int32, (TB, TB), 0)
    cols = lax.broadcasted_iota(jnp.int32, (TB, TB), 1)
    upper = (cols > rows).astype(BF16)
    strict = cols > rows

    heads = range(N_HEADS)

    def step(j, first):
        row = pl.multiple_of(j * TB, TB)
        zs = [jnp.dot(k_ref[pl.ds(row, TB), (h // 2) * LANES:(h // 2 + 1) * LANES], qpad_scr[h],
                      preferred_element_type=F32) for h in heads]
        log_keeps = []
        for z in zs:
            lk = -(jnp.maximum(z, 0.0) + jnp.log(1.0 + jnp.exp(-jnp.abs(z))))
            log_keeps.append(jnp.where(strict, lk, 0.0) if first else lk)
        laters = []
        for h in heads:
            hi = log_keeps[h].astype(BF16)
            lo = (log_keeps[h] - hi.astype(F32)).astype(BF16)
            later = (jnp.dot(upper, hi, preferred_element_type=F32)
                     + jnp.dot(upper, lo, preferred_element_type=F32))
            laters.append(later if first else later + c_scr[h])
        c_max = None
        for h in heads:
            w = jnp.exp(zs[h] + log_keeps[h] + laters[h])
            if first:
                w = jnp.where(strict, w, 0.0)
            pv = jnp.dot(vt_ref[j, h * HEAD_DIM:(h + 1) * HEAD_DIM, :], w.astype(BF16),
                         preferred_element_type=F32)
            acc_scr[h] = pv if first else acc_scr[h] + pv
            c_new = laters[h][0:1, :] + log_keeps[h][0:1, :]
            c_scr[h] = c_new
            c_max = c_new if c_max is None else jnp.maximum(c_max, c_new)
        return jnp.max(c_max) > -SB_EXIT

    for h in heads:
        qpad_scr[h] = _pad_rows(qt_ref[0, h * HEAD_DIM:(h + 1) * HEAD_DIM, :], (h % 2) * HEAD_DIM, LANES)

    def cond(carry):
        j, go = carry
        return jnp.logical_and(j >= 0, go)

    def body(carry):
        j, _ = carry
        return j - 1, step(j, False)

    lax.while_loop(cond, body, (i - 1, step(i, True)))
    for h in heads:
        o_ref[0, h * HEAD_DIM:(h + 1) * HEAD_DIM, :] = acc_scr[h].astype(BF16)


def _sb_attention(kn, qvt, batch):
    nblk = qvt.shape[0]
    nb = nblk // batch
    return pl.pallas_call(
        _sb_kernel,
        grid=(batch, nb),
        in_specs=_attn_common_specs(nb, QV_SB_Q, KN_SB, QV_SB_V),
        out_specs=pl.BlockSpec((1, TB, TB), lambda b, i: (b * nb + i, 0, 0)),
        out_shape=jax.ShapeDtypeStruct((nblk, TB, TB), BF16),
        scratch_shapes=[pltpu.VMEM((N_HEADS, LANES, TB), BF16),
                        pltpu.VMEM((N_HEADS, HEAD_DIM, TB), F32),
                        pltpu.VMEM((N_HEADS, 1, TB), F32)],
        compiler_params=_cparams(2),
        name="sb_attn",
    )(qvt, kn, qvt)


def _swa_kernel(sink_ref, qt_ref, kc_ref, kp_ref, vc_ref, vp_ref, tc_ref, tp_ref, o_ref):
    i = pl.program_id(1)
    kc = kc_ref[...]
    kp = kp_ref[...]
    vc = vc_ref[0]
    vp = vp_ref[0][:, TB - SWA_WINDOW:]
    no_prev = jnp.where(i > 0, 0.0, NEG)
    group = N_HEADS // SWA_KV_HEADS
    for h in range(N_HEADS):
        kv = h // group
        qpad = _pad_rows(qt_ref[0, h * HEAD_DIM:(h + 1) * HEAD_DIM, :], kv * HEAD_DIM, LANES)
        s_c = jnp.dot(kc, qpad, preferred_element_type=F32) + tc_ref[h]
        s_p = jnp.dot(kp, qpad, preferred_element_type=F32) + tp_ref[h] + no_prev
        sink = sink_ref[h]
        m = jnp.maximum(jnp.maximum(jnp.max(s_c, axis=0, keepdims=True),
                                    jnp.max(s_p, axis=0, keepdims=True)), sink)
        p_c = jnp.exp(s_c - m).astype(BF16)
        p_p = jnp.exp(s_p - m).astype(BF16)
        o = (jnp.dot(_with_ones(vc[kv * HEAD_DIM:(kv + 1) * HEAD_DIM]), p_c, preferred_element_type=F32)
             + jnp.dot(_with_ones(vp[kv * HEAD_DIM:(kv + 1) * HEAD_DIM]), p_p, preferred_element_type=F32))
        denom = o[HEAD_DIM:HEAD_DIM + 1] + jnp.exp(sink - m)
        o_ref[0, h * HEAD_DIM:(h + 1) * HEAD_DIM, :] = (o[:HEAD_DIM] / denom).astype(BF16)


def _swa_attention(kn, qvt, tile_cur, tile_prev, sinks, batch):
    nblk = qvt.shape[0]
    nb = nblk // batch
    half = TB // SWA_WINDOW
    kv_rows = SWA_KV_HEADS * HEAD_DIM
    return pl.pallas_call(
        _swa_kernel,
        grid=(batch, nb),
        in_specs=[pl.BlockSpec(memory_space=pltpu.SMEM),
                  pl.BlockSpec((1, TB, TB), lambda b, i: (b * nb + i, QV_SWA_Q, 0)),
                  pl.BlockSpec((TB, kv_rows), lambda b, i: (b * nb + i, KN_SWA_128)),
                  pl.BlockSpec((SWA_WINDOW, kv_rows),
                               lambda b, i: (b * nb * half + jnp.maximum(half * i - 1, 0), KN_SWA_128)),
                  pl.BlockSpec((1, kv_rows, TB), lambda b, i: (b * nb + i, QV_SWA_V_128, 0)),
                  pl.BlockSpec((1, kv_rows, TB), lambda b, i: (b * nb + jnp.maximum(i - 1, 0), QV_SWA_V_128, 0)),
                  _small_spec((N_HEADS, TB, TB)),
                  _small_spec((N_HEADS, SWA_WINDOW, TB))],
        out_specs=pl.BlockSpec((1, TB, TB), lambda b, i: (b * nb + i, 0, 0)),
        out_shape=jax.ShapeDtypeStruct((nblk, TB, TB), BF16),
        compiler_params=_cparams(2),
        name="swa_attn",
    )(sinks, qvt, kn, kn, qvt, qvt, tile_cur, tile_prev)


MERGE_T = 512


def _merge_kernel(x_ref, g_ref, oa_ref, ob_ref, oc_ref, od_ref, wg_ref, wbr_ref, wo_ref, x1_ref):
    x = x_ref[...]
    h = _rms(x, g_ref[...]).astype(BF16)
    merged = None
    for bi, o_ref in enumerate((oa_ref, ob_ref, oc_ref, od_ref)):
        gate = jax.nn.sigmoid(jnp.dot(h, wg_ref[bi], preferred_element_type=F32))
        branch = jnp.concatenate(
            [lax.dot_general(o_ref[s], wbr_ref[bi], (((0,), (0,)), ((), ())), preferred_element_type=F32)
             for s in range(MERGE_T // TB)], axis=0)
        term = gate * branch
        merged = term if merged is None else merged + term
    x1_ref[...] = x + jnp.dot(merged.astype(BF16), wo_ref[...], preferred_element_type=F32)


def _merge(x2, g, o_a, o_b, o_c, o_d, wg, wbr, wo):
    n, d = x2.shape
    o_spec = pl.BlockSpec((MERGE_T // TB, TB, TB), lambda i: (i, 0, 0))
    return pl.pallas_call(
        _merge_kernel,
        grid=(n // MERGE_T,),
        in_specs=[pl.BlockSpec((MERGE_T, d), lambda i: (i, 0)),
                  pl.BlockSpec((1, d), lambda i: (0, 0)),
                  o_spec, o_spec, o_spec, o_spec,
                  pl.BlockSpec(wg.shape, lambda i: (0, 0, 0)),
                  pl.BlockSpec(wbr.shape, lambda i: (0, 0, 0)),
                  pl.BlockSpec(wo.shape, lambda i: (0, 0))],
        out_specs=pl.BlockSpec((MERGE_T, d), lambda i: (i, 0)),
        out_shape=jax.ShapeDtypeStruct((n, d), F32),
        compiler_params=_cparams(1),
        name="merge",
    )(x2, g, o_a, o_b, o_c, o_d, wg, wbr, wo)


ROUTER_T = 512
ROUTER_ROWS = 8 + N_EXPERTS


def _first_argmax_rows(v, n_rows):
    best = jnp.max(v, axis=0, keepdims=True)
    ids = lax.broadcasted_iota(jnp.int32, v.shape, 0)
    return best, jnp.min(jnp.where(v == best, ids, n_rows), axis=0, keepdims=True)


def _router_kernel(x_ref, g_ref, whi_ref, wlo_ref, b_ref, h2_ref, eid_ref, gate_ref, rank_ref, cnt_ref,
                   base_scr):
    i = pl.program_id(0)

    @pl.when(i == 0)
    def _():
        base_scr[...] = jnp.zeros(base_scr.shape, F32)

    h2 = _rms(x_ref[...], g_ref[...])
    h2_ref[...] = h2
    h_hi = h2.astype(BF16)
    h_lo = (h2 - h_hi.astype(F32)).astype(BF16)
    nt = (((1,), (1,)), ((), ()))
    logits = (lax.dot_general(whi_ref[...], h_hi, nt, preferred_element_type=F32)
              + lax.dot_general(whi_ref[...], h_lo, nt, preferred_element_type=F32)
              + lax.dot_general(wlo_ref[...], h_hi, nt, preferred_element_type=F32)
              + b_ref[...])
    gl = logits[0:8]
    gmax, grp = _first_argmax_rows(gl, 8)
    p_grp = 1.0 / jnp.sum(jnp.exp(gl - gmax), axis=0, keepdims=True)
    e_sel = jnp.zeros((EXPERTS_PER_GROUP, ROUTER_T), F32)
    for g in range(N_GROUPS):
        e_sel = jnp.where(grp == g, logits[8 + 8 * g:16 + 8 * g], e_sel)
    ids8 = lax.broadcasted_iota(jnp.int32, e_sel.shape, 0)
    v1, i1 = _first_argmax_rows(e_sel, EXPERTS_PER_GROUP)
    e_rest = jnp.where(ids8 == i1, -jnp.inf, e_sel)
    v2, i2 = _first_argmax_rows(e_rest, EXPERTS_PER_GROUP)
    r = jnp.exp(v2 - v1)
    s1 = 1.0 / (1.0 + r)
    gate_ref[0:1, :] = p_grp * s1
    gate_ref[1:2, :] = p_grp * (r * s1)
    e1 = grp * EXPERTS_PER_GROUP + i1
    e2 = grp * EXPERTS_PER_GROUP + i2
    eid_ref[0:1, :] = e1
    eid_ref[1:2, :] = e2

    ids_e = lax.broadcasted_iota(jnp.int32, (N_EXPERTS, ROUTER_T), 0)
    oh1 = ids_e == e1
    oh2 = ids_e == e2
    cnt = oh1.astype(F32) + oh2.astype(F32)
    tr = lax.broadcasted_iota(jnp.int32, (ROUTER_T, ROUTER_T), 0)
    tc = lax.broadcasted_iota(jnp.int32, (ROUTER_T, ROUTER_T), 1)
    before = (tr < tc).astype(BF16)
    prefix = jnp.dot(cnt.astype(BF16), before, preferred_element_type=F32) + base_scr[:, 0:1]
    rank_ref[0:1, :] = jnp.sum(jnp.where(oh1, prefix, 0.0), axis=0, keepdims=True).astype(jnp.int32)
    rank_ref[1:2, :] = jnp.sum(jnp.where(oh2, prefix, 0.0), axis=0, keepdims=True).astype(jnp.int32)
    base_scr[...] = base_scr[...] + jnp.sum(cnt, axis=1, keepdims=True)
    cnt_ref[...] = base_scr[...]


def _router(x1, g, w_hi, w_lo, bias):
    n, d = x1.shape
    row2 = lambda dt: jax.ShapeDtypeStruct((2, n), dt)
    spec2 = pl.BlockSpec((2, ROUTER_T), lambda i: (0, i))
    return pl.pallas_call(
        _router_kernel,
        grid=(n // ROUTER_T,),
        in_specs=[pl.BlockSpec((ROUTER_T, d), lambda i: (i, 0)),
                  pl.BlockSpec((1, d), lambda i: (0, 0)),
                  pl.BlockSpec((ROUTER_ROWS, d), lambda i: (0, 0)),
                  pl.BlockSpec((ROUTER_ROWS, d), lambda i: (0, 0)),
                  pl.BlockSpec((ROUTER_ROWS, 1), lambda i: (0, 0))],
        out_specs=[pl.BlockSpec((ROUTER_T, d), lambda i: (i, 0)), spec2, spec2, spec2,
                   pl.BlockSpec((N_EXPERTS, LANES), lambda i: (0, 0))],
        out_shape=[jax.ShapeDtypeStruct((n, d), F32), row2(jnp.int32), row2(F32), row2(jnp.int32),
                   jax.ShapeDtypeStruct((N_EXPERTS, LANES), F32)],
        scratch_shapes=[pltpu.VMEM((N_EXPERTS, LANES), F32)],
        compiler_params=_cparams(1),
        name="router",
    )(x1, g, w_hi, w_lo, bias)


ROW_T = 256


def _dispatch_kernel(dest_ref, h2_ref, xin_ref, xbuf_ref, sem):
    del xin_ref

    def row_copy(r, k):
        return pltpu.make_async_copy(h2_ref.at[pl.ds(r, 1)], xbuf_ref.at[pl.ds(dest_ref[k, r], 1)], sem)

    def issue(r, carry):
        row_copy(r, 0).start()
        row_copy(r, 1).start()
        return carry

    def drain(r, carry):
        row_copy(r, 0).wait()
        row_copy(r, 1).wait()
        return carry

    lax.fori_loop(0, ROW_T, issue, 0, unroll=8)
    lax.fori_loop(0, ROW_T, drain, 0, unroll=8)


def _dispatch(dest3, h2, xbuf_init):
    n, d = h2.shape
    return pl.pallas_call(
        _dispatch_kernel,
        grid=(n // ROW_T,),
        in_specs=[pl.BlockSpec((None, 2, ROW_T), lambda i: (i, 0, 0), memory_space=pltpu.SMEM),
                  pl.BlockSpec((ROW_T, d), lambda i: (i, 0)),
                  pl.BlockSpec(memory_space=pl.ANY)],
        out_specs=pl.BlockSpec(memory_space=pl.ANY),
        out_shape=jax.ShapeDtypeStruct(xbuf_init.shape, xbuf_init.dtype),
        scratch_shapes=[pltpu.SemaphoreType.DMA(())],
        input_output_aliases={2: 0},
        compiler_params=_cparams(1),
        name="dispatch",
    )(dest3, h2, xbuf_init)


def _expert_kernel(be_ref, nu_ref, x_ref, w1_ref, w3_ref, w2_ref, y_ref):
    del be_ref
    used = pl.program_id(0) < nu_ref[0]

    @pl.when(used)
    def _():
        xb = x_ref[...].astype(BF16)
        a = jnp.dot(xb, w1_ref[...], preferred_element_type=F32)
        b = jnp.dot(xb, w3_ref[...], preferred_element_type=F32)
        mid = (a * jax.nn.sigmoid(a) * b).astype(BF16)
        y_ref[...] = jnp.dot(mid, w2_ref[...], preferred_element_type=F32)

    @pl.when(jnp.logical_not(used))
    def _():
        y_ref[...] = jnp.zeros(y_ref.shape, F32)


def _experts(blk_expert, n_used, xbuf, w1, w3, w2):
    r, d = xbuf.shape
    de = w1.shape[-1]
    row_map = lambda i, be, nu: (jnp.minimum(i, nu[0] - 1), 0)
    grid_spec = pltpu.PrefetchScalarGridSpec(
        num_scalar_prefetch=2,
        grid=(r // MOE_ROWS,),
        in_specs=[pl.BlockSpec((MOE_ROWS, d), row_map),
                  pl.BlockSpec((None, d, de), lambda i, be, nu: (be[i], 0, 0)),
                  pl.BlockSpec((None, d, de), lambda i, be, nu: (be[i], 0, 0)),
                  pl.BlockSpec((None, de, d), lambda i, be, nu: (be[i], 0, 0))],
        out_specs=pl.BlockSpec((MOE_ROWS, d), lambda i, be, nu: (i, 0)),
    )
    return pl.pallas_call(
        _expert_kernel,
        grid_spec=grid_spec,
        out_shape=jax.ShapeDtypeStruct((r, d), F32),
        compiler_params=_cparams(1),
        name="experts",
    )(blk_expert, n_used, xbuf, w1, w3, w2)


def _combine_kernel(final, dest_ref, gate_ref, x1_ref, gf_ref, y_ref, out_ref, buf, sem):
    def row_copy(r, k):
        return pltpu.make_async_copy(y_ref.at[pl.ds(dest_ref[k, r], 1)], buf.at[k, pl.ds(r, 1)], sem)

    def issue(r, carry):
        row_copy(r, 0).start()
        row_copy(r, 1).start()
        return carry

    def drain(r, carry):
        row_copy(r, 0).wait()
        row_copy(r, 1).wait()
        return carry

    lax.fori_loop(0, ROW_T, issue, 0, unroll=8)
    lax.fori_loop(0, ROW_T, drain, 0, unroll=8)
    gate = gate_ref[...]
    out = x1_ref[...] + gate[:, 0:1] * buf[0] + gate[:, 1:2] * buf[1]
    if final:
        out = _rms(out, gf_ref[...])
    out_ref[...] = out


def _combine(dest3, gate_t, x1, g_final, y, final):
    n, d = x1.shape
    return pl.pallas_call(
        functools.partial(_combine_kernel, final),
        grid=(n // ROW_T,),
        in_specs=[pl.BlockSpec((None, 2, ROW_T), lambda i: (i, 0, 0), memory_space=pltpu.SMEM),
                  pl.BlockSpec((ROW_T, 2), lambda i: (i, 0)),
                  pl.BlockSpec((ROW_T, d), lambda i: (i, 0)),
                  pl.BlockSpec((1, d), lambda i: (0, 0)),
                  pl.BlockSpec(memory_space=pl.ANY)],
        out_specs=pl.BlockSpec((ROW_T, d), lambda i: (i, 0)),
        out_shape=jax.ShapeDtypeStruct((n, d), F32),
        scratch_shapes=[pltpu.VMEM((2, ROW_T, d), F32), pltpu.SemaphoreType.DMA(())],
        compiler_params=_cparams(1),
        name="combine",
    )(dest3, gate_t, x1, g_final, y)


def _projection_weights(w):
    d = w.shape[0]
    blk = N_HEADS * HEAD_DIM
    kv = SWA_KV_HEADS * HEAD_DIM
    pa, pb, pc, pd = w[:, :3 * blk], w[:, 3 * blk:6 * blk], w[:, 6 * blk:9 * blk], w[:, 9 * blk:]
    half = N_HEADS * DIFF_QK

    def per_head(a, b):
        return jnp.stack([a.reshape(d, N_HEADS, DIFF_QK), b.reshape(d, N_HEADS, DIFF_QK)], axis=2).reshape(d, blk)

    s64, s32 = HEAD_DIM ** -0.5, DIFF_QK ** -0.5
    wn = jnp.concatenate([pa[:, blk:2 * blk], per_head(pb[:, 2 * half:3 * half], pb[:, 3 * half:4 * half]),
                          pc[:, blk:2 * blk], pd[:, blk:blk + kv]], axis=1)
    wt = jnp.concatenate([pa[:, :blk] * (s64 * LOG2E), pa[:, 2 * blk:],
                          per_head(pb[:, :half], pb[:, half:2 * half]) * (s32 * LOG2E), pb[:, 4 * half:],
                          pc[:, :blk] * s64, pc[:, 2 * blk:],
                          pd[:, :blk] * s64, pd[:, blk + kv:]], axis=1)
    assert wn.shape[1] == KN_COLS and wt.shape[1] == QV_ROWS
    return wn.astype(BF16), wt.T.astype(BF16)


def _router_weights(w_rg, b_rg, w_re, b_re):
    d = w_rg.shape[0]
    w = jnp.concatenate([w_rg.T, jnp.zeros((8 - N_GROUPS, d), F32), w_re.T], axis=0)
    b = jnp.concatenate([b_rg.astype(F32), jnp.full((8 - N_GROUPS,), NEG, F32), b_re.astype(F32)])[:, None]
    w_hi = w.astype(BF16)
    w_lo = (w - w_hi.astype(F32)).astype(BF16)
    return w_hi, w_lo, b


def _moe_plan(eid, rank, counts, n_rows_total):
    padded = (counts + MOE_ROWS - 1) // MOE_ROWS * MOE_ROWS
    pad_end = jnp.cumsum(padded)
    pad_start = pad_end - padded
    experts = jnp.arange(N_EXPERTS, dtype=jnp.int32)
    start_of = jnp.sum(jnp.where(eid[..., None] == experts, pad_start, 0), axis=-1)
    dest = start_of + rank
    n_blk = n_rows_total // MOE_ROWS
    n_used = (pad_end[-1] // MOE_ROWS).astype(jnp.int32)
    blk = jnp.minimum(jnp.arange(n_blk, dtype=jnp.int32), n_used - 1) * MOE_ROWS
    blk_expert = jnp.minimum(jnp.sum(pad_end[None, :] <= blk[:, None], axis=1), N_EXPERTS - 1).astype(jnp.int32)
    return dest.astype(jnp.int32), blk_expert, n_used.reshape(1)


def kernel(x, rel_bias, g_mix, w_in, diff_lq1, diff_lk1, diff_lq2, diff_lk2, diff_subln, swa_sinks,
           w_gate, w_br, w_o, g_ffn, w_route_group, b_route_group, w_route_expert, b_route_expert,
           w1, w3, w2, g_final):
    batch, seq, d = x.shape
    n = batch * seq
    depth = w_in.shape[0]
    assert seq % TB == 0 and n % MERGE_T == 0 and TB == MOBA_BLOCK
    tab = rel_bias.T.astype(F32)
    tiles_moba = _causal_bias_tiles(tab[:N_HEADS])
    tiles_diff = _causal_bias_tiles(tab[N_HEADS:2 * N_HEADS])
    tile_cur, tile_prev = _swa_bias_tiles(tab[2 * N_HEADS:])
    n_rows_total = n * 2 + N_EXPERTS * MOE_ROWS
    row = lambda v: v.astype(F32)[None, :]

    x2 = x.reshape(n, d)
    xbuf = jnp.zeros((n_rows_total, d), F32)
    for l in range(depth):
        lam_init = 0.8 - 0.6 * math.exp(-0.3 * l)
        wn, wt = _projection_weights(w_in[l])
        kn, qvt = _inproj(x2, row(g_mix[l]), wn, wt)
        o_a = _moba_attention(kn, qvt, tiles_moba, batch)
        o_b = _diff_attention(kn, qvt, tiles_diff, row(diff_lq1[l]), row(diff_lk1[l]), row(diff_lq2[l]),
                              row(diff_lk2[l]), diff_subln[l].astype(F32)[:, None], lam_init, batch)
        o_c = _sb_attention(kn, qvt, batch)
        o_d = _swa_attention(kn, qvt, tile_cur, tile_prev, swa_sinks[l].astype(F32), batch)
        x1 = _merge(x2, row(g_mix[l]), o_a, o_b, o_c, o_d, w_gate[l].astype(BF16), w_br[l].astype(BF16),
                    w_o[l].astype(BF16))
        w_hi, w_lo, r_bias = _router_weights(w_route_group[l], b_route_group[l], w_route_expert[l],
                                             b_route_expert[l])
        h2, eid, gate, rank, cnt = _router(x1, row(g_ffn[l]), w_hi, w_lo, r_bias)
        dest, blk_expert, n_used = _moe_plan(eid, rank, cnt[:, 0].astype(jnp.int32), n_rows_total)
        dest3 = dest.reshape(2, n // ROW_T, ROW_T).transpose(1, 0, 2)
        xbuf = _dispatch(dest3, h2, xbuf)
        y = _experts(blk_expert, n_used, xbuf, w1[l].astype(BF16), w3[l].astype(BF16), w2[l].astype(BF16))
        x2 = _combine(dest3, gate.T, x1, row(g_final), y, l == depth - 1)
    return x2.reshape(batch, seq, d)
```

```python
import functools
import math

import numpy as np
import jax
import jax.numpy as jnp
from jax import lax
from jax.experimental import pallas as pl
from jax.experimental.pallas import tpu as pltpu
from jax.experimental.pallas import tpu_sc as plsc

F32 = jnp.float32
BF16 = jnp.bfloat16

HEAD_DIM = 64
N_HEADS = 4
DIFF_QK = 32
SWA_KV_HEADS = 2
SWA_WINDOW = 128
MOBA_BLOCK = 256
MOBA_TOPK = 3
REL_BUCKETS = 32
REL_MAX_DIST = 128
N_GROUPS = 4
EXPERTS_PER_GROUP = 8
N_EXPERTS = N_GROUPS * EXPERTS_PER_GROUP
NORM_EPS = 1e-6

TB = 256
LANES = 128
ONES_ROWS = 16
NEG = -1e30
LOG2E = math.log2(math.e)
SB_EXIT = 104.0
MOE_ROWS = 256
VMEM_LIMIT = 56 * 1024 * 1024

QV_MOBA_Q, QV_MOBA_V, QV_DIFF_Q, QV_DIFF_V, QV_SB_Q, QV_SB_V, QV_SWA_Q = range(7)
QV_SWA_V_128 = 14
QV_ROWS = 7 * 256 + 128
KN_MOBA, KN_DIFF, KN_SB = range(3)
KN_SWA_128 = 6
KN_COLS = 3 * 256 + 128


def _cparams(n_grid):
    return pltpu.CompilerParams(dimension_semantics=("arbitrary",) * n_grid,
                                vmem_limit_bytes=VMEM_LIMIT)


def _rel_bucket_np(n):
    n = np.maximum(n, 0)
    max_exact = REL_BUCKETS // 2
    nf = np.maximum(n, 1).astype(np.float64)
    large = max_exact + (np.log(nf / max_exact) / math.log(REL_MAX_DIST / max_exact)
                         * (REL_BUCKETS - max_exact)).astype(np.int64)
    large = np.minimum(large, REL_BUCKETS - 1)
    return np.where(n < max_exact, n, large)


def _first_far_distance():
    d = np.arange(0, 4 * REL_MAX_DIST)
    b = _rel_bucket_np(d)
    return int(np.min(d[b == REL_BUCKETS - 1]))


def _toeplitz_bias(tab, rows, cols, base, valid_fn, shift_far, unit):
    length = rows + cols - 1
    off = np.concatenate([np.arange(0, cols), np.arange(cols - length, 0)])
    n = base + off
    onehot = np.zeros((REL_BUCKETS, length), np.float32)
    onehot[_rel_bucket_np(n), np.arange(length)] = 1.0
    vec = jnp.dot(tab, jnp.asarray(onehot), precision=lax.Precision.HIGHEST)
    if shift_far:
        vec = vec - tab[:, REL_BUCKETS - 1:]
    vec = jnp.where(jnp.asarray(valid_fn(n))[None, :], vec * unit, NEG).astype(F32)
    flat = jnp.tile(vec, (1, rows))[:, :rows * (length - 1)]
    return flat.reshape(tab.shape[0], rows, length - 1)[:, :, :cols]


def _causal_bias_tiles(tab):
    assert _first_far_distance() <= TB + 1
    tiles = [_toeplitz_bias(tab, TB, TB, d * TB, lambda n: n >= 0, True, LOG2E) for d in range(2)]
    return jnp.stack(tiles, axis=1)


def _swa_bias_tiles(tab):
    in_window = lambda n: (n >= 0) & (n < SWA_WINDOW)
    return (_toeplitz_bias(tab, TB, TB, 0, in_window, False, 1.0),
            _toeplitz_bias(tab, SWA_WINDOW, TB, SWA_WINDOW, in_window, False, 1.0))


def _pad_rows(q, off, total):
    n, t = q.shape
    parts = []
    if off:
        parts.append(jnp.zeros((off, t), q.dtype))
    parts.append(q)
    if total - off - n:
        parts.append(jnp.zeros((total - off - n, t), q.dtype))
    return jnp.concatenate(parts, axis=0) if len(parts) > 1 else q


def _with_ones(v):
    return jnp.concatenate([v, jnp.ones((ONES_ROWS, v.shape[1]), v.dtype)], axis=0)


def _rms(x, g_row):
    ms = jnp.mean(x * x, axis=-1, keepdims=True)
    return x * lax.rsqrt(ms + NORM_EPS) * g_row


IN_T = 512
IN_CHUNK = 384


def _inproj_kernel(x_ref, g_ref, wn_ref, wt_ref, kn_ref, qvt_ref):
    h = _rms(x_ref[...], g_ref[...]).astype(BF16)
    kn_ref[...] = jnp.dot(h, wn_ref[...], preferred_element_type=F32).astype(BF16)
    for r0 in range(0, QV_ROWS, IN_CHUNK):
        pt = lax.dot_general(wt_ref[r0:r0 + IN_CHUNK, :], h, (((1,), (1,)), ((), ())),
                             preferred_element_type=F32)
        for s in range(IN_T // TB):
            qvt_ref[s, r0:r0 + IN_CHUNK, :] = pt[:, s * TB:(s + 1) * TB].astype(BF16)


def _inproj(x2, g, wn, wt):
    n, d = x2.shape
    return pl.pallas_call(
        _inproj_kernel,
        grid=(n // IN_T,),
        in_specs=[pl.BlockSpec((IN_T, d), lambda i: (i, 0)),
                  pl.BlockSpec((1, d), lambda i: (0, 0)),
                  pl.BlockSpec((d, KN_COLS), lambda i: (0, 0)),
                  pl.BlockSpec((QV_ROWS, d), lambda i: (0, 0))],
        out_specs=[pl.BlockSpec((IN_T, KN_COLS), lambda i: (i, 0)),
                   pl.BlockSpec((IN_T // TB, QV_ROWS, TB), lambda i: (i, 0, 0))],
        out_shape=[jax.ShapeDtypeStruct((n, KN_COLS), BF16),
                   jax.ShapeDtypeStruct((n // TB, QV_ROWS, TB), BF16)],
        compiler_params=_cparams(1),
        name="inproj",
    )(x2, g, wn, wt)


FLASH_OVERFLOW = 100.0
FLASH_SKEW = 4


class _FlashScratch:
    def __init__(self, n_chain, qpad, m, acc, over, s):
        self.n_chain, self.qpad, self.m, self.acc, self.over, self.s = n_chain, qpad, m, acc, over, s

    @staticmethod
    def shapes(n_chain):
        return [pltpu.VMEM((n_chain, LANES, TB), BF16),
                pltpu.VMEM((n_chain, 1, TB), F32),
                pltpu.VMEM((n_chain, HEAD_DIM + ONES_ROWS, TB), F32),
                pltpu.VMEM((n_chain, 1, TB), F32),
                pltpu.VMEM((n_chain, TB, TB), F32)]


def _flash_two_pass(sc, chains):
    if len(chains) > sc.n_chain:
        for k in range(0, len(chains), sc.n_chain):
            _flash_two_pass(sc, chains[k:k + sc.n_chain])
        return
    block_max = []
    for c, kblk, add, _ in chains:
        s = jnp.dot(kblk, sc.qpad[c], preferred_element_type=F32)
        if add is not None:
            s = s + add
        sc.s[c] = s
        block_max.append(jnp.max(s, axis=0, keepdims=True))
    for (c, _, _, vext), mx in zip(chains, block_max):
        m_old = sc.m[c]
        m_new = jnp.maximum(m_old, mx)
        p = jnp.exp2(sc.s[c] - m_new).astype(BF16)
        sc.acc[c] = sc.acc[c] * jnp.exp2(m_old - m_new) + jnp.dot(vext, p, preferred_element_type=F32)
        sc.m[c] = m_new


def _flash_lagged(sc, chains):
    def finish(c, p, mx, ref, vext):
        pv = jnp.dot(vext, p, preferred_element_type=F32)
        ref_new = jnp.maximum(ref, mx)
        sc.acc[c] = (sc.acc[c] + pv) * jnp.exp2(ref - ref_new)
        sc.m[c] = ref_new
        sc.over[c] = jnp.maximum(sc.over[c], mx - ref)

    pending = []
    for c, kblk, add, vext in chains:
        s = jnp.dot(kblk, sc.qpad[c], preferred_element_type=F32)
        if add is not None:
            s = s + add
        if len(pending) >= min(FLASH_SKEW, sc.n_chain):
            finish(*pending.pop(0))
        ref = sc.m[c]
        pending.append((c, jnp.exp2(s - ref).astype(BF16), jnp.max(s, axis=0, keepdims=True), ref, vext))
    for item in pending:
        finish(*item)


def _flash_causal(sc, i, block_chains):
    def init():
        for c in range(sc.n_chain):
            sc.m[c] = jnp.full(sc.m.shape[1:], NEG, F32)
            sc.acc[c] = jnp.zeros(sc.acc.shape[1:], F32)
            sc.over[c] = jnp.full(sc.over.shape[1:], NEG, F32)

    def run(step):
        init()
        _flash_two_pass(sc, block_chains(i, 0))

        @pl.when(i >= 1)
        def _():
            step(sc, block_chains(i - 1, 1))

        n_far = jnp.maximum(i - 1, 0)

        def far_pair(t, carry):
            step(sc, block_chains(2 * t, None) + block_chains(2 * t + 1, None))
            return carry

        lax.fori_loop(0, n_far // 2, far_pair, 0)

        @pl.when(lax.rem(n_far, 2) == 1)
        def _():
            step(sc, block_chains(n_far - 1, None))

    run(_flash_lagged)
    worst = sc.over[0]
    for c in range(1, sc.n_chain):
        worst = jnp.maximum(worst, sc.over[c])

    @pl.when(jnp.max(worst) > FLASH_OVERFLOW)
    def _():
        run(_flash_two_pass)


def _diff_kernel(lam_init, qt_ref, k_ref, vt_ref, tile_ref, lq1_ref, lk1_ref, lq2_ref, lk2_ref,
                 subln_ref, o_ref, *scratch):
    i = pl.program_id(1)
    sc = _FlashScratch(2 * N_HEADS, *scratch)
    acc_scr = sc.acc
    for h in range(N_HEADS):
        for mp in range(2):
            r0 = h * HEAD_DIM + mp * DIFF_QK
            sc.qpad[2 * h + mp] = _pad_rows(qt_ref[0, r0:r0 + DIFF_QK, :],
                                            (h % 2) * HEAD_DIM + mp * DIFF_QK, LANES)

    def block_chains(j, tile_idx):
        row = pl.multiple_of(j * TB, TB)
        chains = []
        for h in range(N_HEADS):
            g = h // 2
            kblk = k_ref[pl.ds(row, TB), g * LANES:(g + 1) * LANES]
            add = None if tile_idx is None else tile_ref[h, tile_idx]
            vext = _with_ones(vt_ref[j, h * HEAD_DIM:(h + 1) * HEAD_DIM, :])
            chains += [(2 * h + mp, kblk, add, vext) for mp in range(2)]
        return chains

    _flash_causal(sc, i, block_chains)

    lam =(jnp.exp(jnp.sum(lq1_ref[...] * lk1_ref[...], keepdims=True))
           - jnp.exp(jnp.sum(lq2_ref[...] * lk2_ref[...], keepdims=True)) + lam_init)
    for h in range(N_HEADS):
        a1 = acc_scr[2 * h]
        a2 = acc_scr[2 * h + 1]
        o1 = a1[:HEAD_DIM] / a1[HEAD_DIM:HEAD_DIM + 1]
        o2 = a2[:HEAD_DIM] / a2[HEAD_DIM:HEAD_DIM + 1]
        a = o1 - lam * o2
        ms = jnp.mean(a * a, axis=0, keepdims=True)
        y = a * lax.rsqrt(ms + NORM_EPS) * subln_ref[...] * (1.0 - lam_init)
        o_ref[0, h * HEAD_DIM:(h + 1) * HEAD_DIM, :] = y.astype(BF16)


def _attn_common_specs(nb, q_blk, k_blk, v_blk):
    return [pl.BlockSpec((1, TB, TB), lambda b, i: (b * nb + i, q_blk, 0)),
            pl.BlockSpec((nb * TB, TB), lambda b, i: (b, k_blk)),
            pl.BlockSpec((nb, TB, TB), lambda b, i: (b, v_blk, 0))]


def _small_spec(shape):
    return pl.BlockSpec(shape, lambda b, i: (0,) * len(shape))


def _diff_attention(kn, qvt, tiles, lq1, lk1, lq2, lk2, subln, lam_init, batch):
    nblk = qvt.shape[0]
    nb = nblk // batch
    n_chain = 2 * N_HEADS
    return pl.pallas_call(
        functools.partial(_diff_kernel, lam_init),
        grid=(batch, nb),
        in_specs=_attn_common_specs(nb, QV_DIFF_Q, KN_DIFF, QV_DIFF_V) + [
            _small_spec((N_HEADS, 2, TB, TB)),
            _small_spec((1, DIFF_QK)), _small_spec((1, DIFF_QK)),
            _small_spec((1, DIFF_QK)), _small_spec((1, DIFF_QK)),
            _small_spec((HEAD_DIM, 1))],
        out_specs=pl.BlockSpec((1, TB, TB), lambda b, i: (b * nb + i, 0, 0)),
        out_shape=jax.ShapeDtypeStruct((nblk, TB, TB), BF16),
        scratch_shapes=_FlashScratch.shapes(n_chain),
        compiler_params=_cparams(2),
        name="diff_attn",
    )(qvt, kn, qvt, tiles, lq1, lk1, lq2, lk2, subln)


def _moba_kernel(nb, nbp, qt_ref, k_ref, vt_ref, tile_ref, o_ref, kmean_scr, sel_scr, *scratch):
    i = pl.program_id(1)
    sc = _FlashScratch(N_HEADS, *scratch)
    qpad_scr, acc_scr = sc.qpad, sc.acc

    @pl.when(i == 0)
    def _():
        kmean_scr[...] = jnp.zeros(kmean_scr.shape, F32)
        for jb in range(nb):
            blk = k_ref[jb * TB:(jb + 1) * TB, :].astype(F32)
            kmean_scr[jb:jb + 1, :] = jnp.mean(blk, axis=0, keepdims=True)

    for h in range(N_HEADS):
        qpad_scr[h] = _pad_rows(qt_ref[0, h * HEAD_DIM:(h + 1) * HEAD_DIM, :], (h % 2) * HEAD_DIM, LANES)

    blk_id = lax.broadcasted_iota(jnp.int32, (nbp, TB), 0)
    for h in range(N_HEADS):
        g = h // 2
        km = kmean_scr[:, g * LANES:(g + 1) * LANES].astype(BF16)
        gate = jnp.dot(km, qpad_scr[h], preferred_element_type=F32)
        avail = blk_id < i
        sel = jnp.zeros((nbp, TB), jnp.bool_)
        for _ in range(MOBA_TOPK):
            gm = jnp.where(avail, gate, -jnp.inf)
            best = jnp.max(gm, axis=0, keepdims=True)
            is_best = avail & (gm == best)
            first = jnp.min(jnp.where(is_best, blk_id, nbp), axis=0, keepdims=True)
            pick = blk_id == first
            sel = sel | pick
            avail = avail & jnp.logical_not(pick)
        sel_scr[h] = jnp.where(sel, 0.0, NEG).astype(F32)

    def block_chains(j, tile_idx):
        row = pl.multiple_of(j * TB, TB)
        chains = []
        for h in range(N_HEADS):
            g = h // 2
            kblk = k_ref[pl.ds(row, TB), g * LANES:(g + 1) * LANES]
            add = None if tile_idx is None else tile_ref[h, tile_idx]
            if tile_idx != 0:
                cm = sel_scr[h, pl.ds(j, 1), :]
                add = cm if add is None else add + cm
            chains.append((h, kblk, add, _with_ones(vt_ref[j, h * HEAD_DIM:(h + 1) * HEAD_DIM, :])))
        return chains

    _flash_causal(sc, i, block_chains)

    for h in range(N_HEADS):
        a = acc_scr[h]
        o_ref[0, h * HEAD_DIM:(h + 1) * HEAD_DIM, :] = (a[:HEAD_DIM] / a[HEAD_DIM:HEAD_DIM + 1]).astype(BF16)


def _moba_attention(kn, qvt, tiles, batch):
    nblk = qvt.shape[0]
    nb = nblk // batch
    nbp = max(8, -(-nb // 8) * 8)
    return pl.pallas_call(
        functools.partial(_moba_kernel, nb, nbp),
        grid=(batch, nb),
        in_specs=_attn_common_specs(nb, QV_MOBA_Q, KN_MOBA, QV_MOBA_V) + [
            _small_spec((N_HEADS, 2, TB, TB))],
        out_specs=pl.BlockSpec((1, TB, TB), lambda b, i: (b * nb + i, 0, 0)),
        out_shape=jax.ShapeDtypeStruct((nblk, TB, TB), BF16),
        scratch_shapes=[pltpu.VMEM((nbp, TB), F32),
                        pltpu.VMEM((N_HEADS, nbp, TB), F32)] + _FlashScratch.shapes(N_HEADS),
        compiler_params=_cparams(2),
        name="moba_attn",
    )(qvt, kn, qvt, tiles)


def _sb_kernel(qt_ref, k_ref, vt_ref, o_ref, qpad_scr, acc_scr, c_scr):
    i = pl.program_id(1)
    rows = lax.broadcasted_iota(jnp.int32, (TB, TB), 0)
    cols = lax.broadcasted_iota(jnp.int32, (TB, TB), 1)
    upper = (cols > rows).astype(BF16)
    strict = cols > rows

    heads = range(N_HEADS)

    def step(j, first):
        row = pl.multiple_of(j * TB, TB)
        zs = [jnp.dot(k_ref[pl.ds(row, TB), (h // 2) * LANES:(h // 2 + 1) * LANES], qpad_scr[h],
                      preferred_element_type=F32) for h in heads]
        log_keeps = []
        for z in zs:
            lk = -(jnp.maximum(z, 0.0) + jnp.log(1.0 + jnp.exp(-jnp.abs(z))))
            log_keeps.append(jnp.where(strict, lk, 0.0) if first else lk)
        laters = []
        for h in heads:
            hi = log_keeps[h].astype(BF16)
            lo = (log_keeps[h] - hi.astype(F32)).astype(BF16)
            later = (jnp.dot(upper, hi, preferred_element_type=F32)
                     + jnp.dot(upper, lo, preferred_element_type=F32))
            laters.append(later if first else later + c_scr[h])
        c_max = None
        for h in heads:
            w = jnp.exp(zs[h] + log_keeps[h] + laters[h])
            if first:
                w = jnp.where(strict, w, 0.0)
            pv = jnp.dot(vt_ref[j, h * HEAD_DIM:(h + 1) * HEAD_DIM, :], w.astype(BF16),
                         preferred_element_type=F32)
            acc_scr[h] = pv if first else acc_scr[h] + pv
            c_new = laters[h][0:1, :] + log_keeps[h][0:1, :]
            c_scr[h] = c_new
            c_max = c_new if c_max is None else jnp.maximum(c_max, c_new)
        return jnp.max(c_max) > -SB_EXIT

    for h in heads:
        qpad_scr[h] = _pad_rows(qt_ref[0, h * HEAD_DIM:(h + 1) * HEAD_DIM, :], (h % 2) * HEAD_DIM, LANES)

    def cond(carry):
        j, go = carry
        return jnp.logical_and(j >= 0, go)

    def body(carry):
        j, _ = carry
        return j - 1, step(j, False)

    lax.while_loop(cond, body, (i - 1, step(i, True)))
    for h in heads:
        o_ref[0, h * HEAD_DIM:(h + 1) * HEAD_DIM, :] = acc_scr[h].astype(BF16)


def _sb_attention(kn, qvt, batch):
    nblk = qvt.shape[0]
    nb = nblk // batch
    return pl.pallas_call(
        _sb_kernel,
        grid=(batch, nb),
        in_specs=_attn_common_specs(nb, QV_SB_Q, KN_SB, QV_SB_V),
        out_specs=pl.BlockSpec((1, TB, TB), lambda b, i: (b * nb + i, 0, 0)),
        out_shape=jax.ShapeDtypeStruct((nblk, TB, TB), BF16),
        scratch_shapes=[pltpu.VMEM((N_HEADS, LANES, TB), BF16),
                        pltpu.VMEM((N_HEADS, HEAD_DIM, TB), F32),
                        pltpu.VMEM((N_HEADS, 1, TB), F32)],
        compiler_params=_cparams(2),
        name="sb_attn",
    )(qvt, kn, qvt)


def _swa_kernel(sink_ref, qt_ref, kc_ref, kp_ref, vc_ref, vp_ref, tc_ref, tp_ref, o_ref):
    i = pl.program_id(1)
    kc = kc_ref[...]
    kp = kp_ref[...]
    vc = vc_ref[0]
    vp = vp_ref[0][:, TB - SWA_WINDOW:]
    no_prev = jnp.where(i > 0, 0.0, NEG)
    group = N_HEADS // SWA_KV_HEADS
    for h in range(N_HEADS):
        kv = h // group
        qpad = _pad_rows(qt_ref[0, h * HEAD_DIM:(h + 1) * HEAD_DIM, :], kv * HEAD_DIM, LANES)
        s_c = jnp.dot(kc, qpad, preferred_element_type=F32) + tc_ref[h]
        s_p = jnp.dot(kp, qpad, preferred_element_type=F32) + tp_ref[h] + no_prev
        sink = sink_ref[h]
        m = jnp.maximum(jnp.maximum(jnp.max(s_c, axis=0, keepdims=True),
                                    jnp.max(s_p, axis=0, keepdims=True)), sink)
        p_c = jnp.exp(s_c - m).astype(BF16)
        p_p = jnp.exp(s_p - m).astype(BF16)
        o = (jnp.dot(_with_ones(vc[kv * HEAD_DIM:(kv + 1) * HEAD_DIM]), p_c, preferred_element_type=F32)
             + jnp.dot(_with_ones(vp[kv * HEAD_DIM:(kv + 1) * HEAD_DIM]), p_p, preferred_element_type=F32))
        denom = o[HEAD_DIM:HEAD_DIM + 1] + jnp.exp(sink - m)
        o_ref[0, h * HEAD_DIM:(h + 1) * HEAD_DIM, :] = (o[:HEAD_DIM] / denom).astype(BF16)


def _swa_attention(kn, qvt, tile_cur, tile_prev, sinks, batch):
    nblk = qvt.shape[0]
    nb = nblk // batch
    half = TB // SWA_WINDOW
    kv_rows = SWA_KV_HEADS * HEAD_DIM
    return pl.pallas_call(
        _swa_kernel,
        grid=(batch, nb),
        in_specs=[pl.BlockSpec(memory_space=pltpu.SMEM),
                  pl.BlockSpec((1, TB, TB), lambda b, i: (b * nb + i, QV_SWA_Q, 0)),
                  pl.BlockSpec((TB, kv_rows), lambda b, i: (b * nb + i, KN_SWA_128)),
                  pl.BlockSpec((SWA_WINDOW, kv_rows),
                               lambda b, i: (b * nb * half + jnp.maximum(half * i - 1, 0), KN_SWA_128)),
                  pl.BlockSpec((1, kv_rows, TB), lambda b, i: (b * nb + i, QV_SWA_V_128, 0)),
                  pl.BlockSpec((1, kv_rows, TB), lambda b, i: (b * nb + jnp.maximum(i - 1, 0), QV_SWA_V_128, 0)),
                  _small_spec((N_HEADS, TB, TB)),
                  _small_spec((N_HEADS, SWA_WINDOW, TB))],
        out_specs=pl.BlockSpec((1, TB, TB), lambda b, i: (b * nb + i, 0, 0)),
        out_shape=jax.ShapeDtypeStruct((nblk, TB, TB), BF16),
        compiler_params=_cparams(2),
        name="swa_attn",
    )(sinks, qvt, kn, kn, qvt, qvt, tile_cur, tile_prev)


MERGE_T = 512


def _merge_kernel(x_ref, g_ref, oa_ref, ob_ref, oc_ref, od_ref, wg_ref, wbr_ref, wo_ref, x1_ref):
    x = x_ref[...]
    h = _rms(x, g_ref[...]).astype(BF16)
    merged = None
    for bi, o_ref in enumerate((oa_ref, ob_ref, oc_ref, od_ref)):
        gate = jax.nn.sigmoid(jnp.dot(h, wg_ref[bi], preferred_element_type=F32))
        branch = jnp.concatenate(
            [lax.dot_general(o_ref[s], wbr_ref[bi], (((0,), (0,)), ((), ())), preferred_element_type=F32)
             for s in range(MERGE_T // TB)], axis=0)
        term = gate * branch
        merged = term if merged is None else merged + term
    x1_ref[...] = x + jnp.dot(merged.astype(BF16), wo_ref[...], preferred_element_type=F32)


def _merge(x2, g, o_a, o_b, o_c, o_d, wg, wbr, wo):
    n, d = x2.shape
    o_spec = pl.BlockSpec((MERGE_T // TB, TB, TB), lambda i: (i, 0, 0))
    return pl.pallas_call(
        _merge_kernel,
        grid=(n // MERGE_T,),
        in_specs=[pl.BlockSpec((MERGE_T, d), lambda i: (i, 0)),
                  pl.BlockSpec((1, d), lambda i: (0, 0)),
                  o_spec, o_spec, o_spec, o_spec,
                  pl.BlockSpec(wg.shape, lambda i: (0, 0, 0)),
                  pl.BlockSpec(wbr.shape, lambda i: (0, 0, 0)),
                  pl.BlockSpec(wo.shape, lambda i: (0, 0))],
        out_specs=pl.BlockSpec((MERGE_T, d), lambda i: (i, 0)),
        out_shape=jax.ShapeDtypeStruct((n, d), F32),
        compiler_params=_cparams(1),
        name="merge",
    )(x2, g, o_a, o_b, o_c, o_d, wg, wbr, wo)


ROUTER_T = 512
ROUTER_ROWS = 8 + N_EXPERTS


def _first_argmax_rows(v, n_rows):
    best = jnp.max(v, axis=0, keepdims=True)
    ids = lax.broadcasted_iota(jnp.int32, v.shape, 0)
    return best, jnp.min(jnp.where(v == best, ids, n_rows), axis=0, keepdims=True)


def _router_kernel(x_ref, g_ref, whi_ref, wlo_ref, b_ref, h2_ref, eid_ref, gate_ref, rank_ref, cnt_ref,
                   base_scr):
    i = pl.program_id(0)

    @pl.when(i == 0)
    def _():
        base_scr[...] = jnp.zeros(base_scr.shape, F32)

    h2 = _rms(x_ref[...], g_ref[...])
    h2_ref[...] = h2
    h_hi = h2.astype(BF16)
    h_lo = (h2 - h_hi.astype(F32)).astype(BF16)
    nt = (((1,), (1,)), ((), ()))
    logits = (lax.dot_general(whi_ref[...], h_hi, nt, preferred_element_type=F32)
              + lax.dot_general(whi_ref[...], h_lo, nt, preferred_element_type=F32)
              + lax.dot_general(wlo_ref[...], h_hi, nt, preferred_element_type=F32)
              + b_ref[...])
    gl = logits[0:8]
    gmax, grp = _first_argmax_rows(gl, 8)
    p_grp = 1.0 / jnp.sum(jnp.exp(gl - gmax), axis=0, keepdims=True)
    e_sel = jnp.zeros((EXPERTS_PER_GROUP, ROUTER_T), F32)
    for g in range(N_GROUPS):
        e_sel = jnp.where(grp == g, logits[8 + 8 * g:16 + 8 * g], e_sel)
    ids8 = lax.broadcasted_iota(jnp.int32, e_sel.shape, 0)
    v1, i1 = _first_argmax_rows(e_sel, EXPERTS_PER_GROUP)
    e_rest = jnp.where(ids8 == i1, -jnp.inf, e_sel)
    v2, i2 = _first_argmax_rows(e_rest, EXPERTS_PER_GROUP)
    r = jnp.exp(v2 - v1)
    s1 = 1.0 / (1.0 + r)
    gate_ref[0:1, :] = p_grp * s1
    gate_ref[1:2, :] = p_grp * (r * s1)
    e1 = grp * EXPERTS_PER_GROUP + i1
    e2 = grp * EXPERTS_PER_GROUP + i2
    eid_ref[0:1, :] = e1
    eid_ref[1:2, :] = e2

    ids_e = lax.broadcasted_iota(jnp.int32, (N_EXPERTS, ROUTER_T), 0)
    oh1 = ids_e == e1
    oh2 = ids_e == e2
    cnt = oh1.astype(F32) + oh2.astype(F32)
    tr = lax.broadcasted_iota(jnp.int32, (ROUTER_T, ROUTER_T), 0)
    tc = lax.broadcasted_iota(jnp.int32, (ROUTER_T, ROUTER_T), 1)
    before = (tr < tc).astype(BF16)
    prefix = jnp.dot(cnt.astype(BF16), before, preferred_element_type=F32) + base_scr[:, 0:1]
    rank_ref[0:1, :] = jnp.sum(jnp.where(oh1, prefix, 0.0), axis=0, keepdims=True).astype(jnp.int32)
    rank_ref[1:2, :] = jnp.sum(jnp.where(oh2, prefix, 0.0), axis=0, keepdims=True).astype(jnp.int32)
    base_scr[...] = base_scr[...] + jnp.sum(cnt, axis=1, keepdims=True)
    cnt_ref[...] = base_scr[...]


def _router(x1, g, w_hi, w_lo, bias):
    n, d = x1.shape
    row2 = lambda dt: jax.ShapeDtypeStruct((2, n), dt)
    spec2 = pl.BlockSpec((2, ROUTER_T), lambda i: (0, i))
    return pl.pallas_call(
        _router_kernel,
        grid=(n // ROUTER_T,),
        in_specs=[pl.BlockSpec((ROUTER_T, d), lambda i: (i, 0)),
                  pl.BlockSpec((1, d), lambda i: (0, 0)),
                  pl.BlockSpec((ROUTER_ROWS, d), lambda i: (0, 0)),
                  pl.BlockSpec((ROUTER_ROWS, d), lambda i: (0, 0)),
                  pl.BlockSpec((ROUTER_ROWS, 1), lambda i: (0, 0))],
        out_specs=[pl.BlockSpec((ROUTER_T, d), lambda i: (i, 0)), spec2, spec2, spec2,
                   pl.BlockSpec((N_EXPERTS, LANES), lambda i: (0, 0))],
        out_shape=[jax.ShapeDtypeStruct((n, d), F32), row2(jnp.int32), row2(F32), row2(jnp.int32),
                   jax.ShapeDtypeStruct((N_EXPERTS, LANES), F32)],
        scratch_shapes=[pltpu.VMEM((N_EXPERTS, LANES), F32)],
        compiler_params=_cparams(1),
        name="router",
    )(x1, g, w_hi, w_lo, bias)


ROW_T = 256


def _dispatch_kernel(dest_ref, h2_ref, xin_ref, xbuf_ref, sem):
    del xin_ref

    def row_copy(r, k):
        return pltpu.make_async_copy(h2_ref.at[pl.ds(r, 1)], xbuf_ref.at[pl.ds(dest_ref[k, r], 1)], sem)

    def issue(r, carry):
        row_copy(r, 0).start()
        row_copy(r, 1).start()
        return carry

    def drain(r, carry):
        row_copy(r, 0).wait()
        row_copy(r, 1).wait()
        return carry

    lax.fori_loop(0, ROW_T, issue, 0, unroll=8)
    lax.fori_loop(0, ROW_T, drain, 0, unroll=8)


def _dispatch(dest3, h2, xbuf_init):
    n, d = h2.shape
    return pl.pallas_call(
        _dispatch_kernel,
        grid=(n // ROW_T,),
        in_specs=[pl.BlockSpec((None, 2, ROW_T), lambda i: (i, 0, 0), memory_space=pltpu.SMEM),
                  pl.BlockSpec((ROW_T, d), lambda i: (i, 0)),
                  pl.BlockSpec(memory_space=pl.ANY)],
        out_specs=pl.BlockSpec(memory_space=pl.ANY),
        out_shape=jax.ShapeDtypeStruct(xbuf_init.shape, xbuf_init.dtype),
        scratch_shapes=[pltpu.SemaphoreType.DMA(())],
        input_output_aliases={2: 0},
        compiler_params=_cparams(1),
        name="dispatch",
    )(dest3, h2, xbuf_init)


def _expert_kernel(be_ref, nu_ref, x_ref, w1_ref, w3_ref, w2_ref, y_ref):
    del be_ref
    used = pl.program_id(0) < nu_ref[0]

    @pl.when(used)
    def _():
        xb = x_ref[...].astype(BF16)
        a = jnp.dot(xb, w1_ref[...], preferred_element_type=F32)
        b = jnp.dot(xb, w3_ref[...], preferred_element_type=F32)
        mid = (a * jax.nn.sigmoid(a) * b).astype(BF16)
        y_ref[...] = jnp.dot(mid, w2_ref[...], preferred_element_type=F32)

    @pl.when(jnp.logical_not(used))
    def _():
        y_ref[...] = jnp.zeros(y_ref.shape, F32)


def _experts(blk_expert, n_used, xbuf, w1, w3, w2):
    r, d = xbuf.shape
    de = w1.shape[-1]
    row_map = lambda i, be, nu: (jnp.minimum(i, nu[0] - 1), 0)
    grid_spec = pltpu.PrefetchScalarGridSpec(
        num_scalar_prefetch=2,
        grid=(r // MOE_ROWS,),
        in_specs=[pl.BlockSpec((MOE_ROWS, d), row_map),
                  pl.BlockSpec((None, d, de), lambda i, be, nu: (be[i], 0, 0)),
                  pl.BlockSpec((None, d, de), lambda i, be, nu: (be[i], 0, 0)),
                  pl.BlockSpec((None, de, d), lambda i, be, nu: (be[i], 0, 0))],
        out_specs=pl.BlockSpec((MOE_ROWS, d), lambda i, be, nu: (i, 0)),
    )
    return pl.pallas_call(
        _expert_kernel,
        grid_spec=grid_spec,
        out_shape=jax.ShapeDtypeStruct((r, d), F32),
        compiler_params=_cparams(1),
        name="experts",
    )(blk_expert, n_used, xbuf, w1, w3, w2)


def _combine_kernel(final, dest_ref, gate_ref, x1_ref, gf_ref, y_ref, out_ref, buf, sem):
    def row_copy(r, k):
        return pltpu.make_async_copy(y_ref.at[pl.ds(dest_ref[k, r], 1)], buf.at[k, pl.ds(r, 1)], sem)

    def issue(r, carry):
        row_copy(r, 0).start()
        row_copy(r, 1).start()
        return carry

    def drain(r, carry):
        row_copy(r, 0).wait()
        row_copy(r, 1).wait()
        return carry

    lax.fori_loop(0, ROW_T, issue, 0, unroll=8)
    lax.fori_loop(0, ROW_T, drain, 0, unroll=8)
    gate = gate_ref[...]
    out = x1_ref[...] + gate[:, 0:1] * buf[0] + gate[:, 1:2] * buf[1]
    if final:
        out = _rms(out, gf_ref[...])
    out_ref[...] = out


def _combine(dest3, gate_t, x1, g_final, y, final):
    n, d = x1.shape
    return pl.pallas_call(
        functools.partial(_combine_kernel, final),
        grid=(n // ROW_T,),
        in_specs=[pl.BlockSpec((None, 2, ROW_T), lambda i: (i, 0, 0), memory_space=pltpu.SMEM),
                  pl.BlockSpec((ROW_T, 2), lambda i: (i, 0)),
                  pl.BlockSpec((ROW_T, d), lambda i: (i, 0)),
                  pl.BlockSpec((1, d), lambda i: (0, 0)),
                  pl.BlockSpec(memory_space=pl.ANY)],
        out_specs=pl.BlockSpec((ROW_T, d), lambda i: (i, 0)),
        out_shape=jax.ShapeDtypeStruct((n, d), F32),
        scratch_shapes=[pltpu.VMEM((2, ROW_T, d), F32), pltpu.SemaphoreType.DMA(())],
        compiler_params=_cparams(1),
        name="combine",
    )(dest3, gate_t, x1, g_final, y)


SC_CORES = 2
SC_SUBCORES = 16
SC_ROWS = 32


def _sc_gather_rows(table, idx):
    n_idx = idx.shape[0]
    d = table.shape[1]
    n_workers = SC_CORES * SC_SUBCORES
    per_worker = n_idx // n_workers
    n_chunk = per_worker // SC_ROWS
    assert per_worker * n_workers == n_idx and n_chunk * SC_ROWS == per_worker and n_chunk % 2 == 0
    mesh = plsc.VectorSubcoreMesh(core_axis_name="c", subcore_axis_name="s",
                                  num_cores=SC_CORES, num_subcores=SC_SUBCORES)

    def body(table_hbm, idx_hbm, out_hbm, idx_v, rows_v, gsem, wsem):
        worker = lax.axis_index("s") * SC_CORES + lax.axis_index("c")
        base = worker * per_worker
        pltpu.sync_copy(idx_hbm.at[pl.ds(base, per_worker)], idx_v)

        def gather(c, b):
            return pltpu.make_async_copy(table_hbm.at[idx_v.at[pl.ds(c * SC_ROWS, SC_ROWS)]],
                                         rows_v.at[b], gsem.at[b])

        def put(c, b):
            return pltpu.make_async_copy(rows_v.at[b], out_hbm.at[pl.ds(base + c * SC_ROWS, SC_ROWS)],
                                         wsem.at[b])

        gather(0, 0).start()

        @pl.loop(0, n_chunk, step=2)
        def _(c0):
            for b in range(2):
                c = c0 + b
                gather(c, b).wait()

                @pl.when(c + 1 < n_chunk)
                def _():
                    @pl.when(c >= 1)
                    def _():
                        put(c - 1, 1 - b).wait()

                    gather(c + 1, 1 - b).start()

                put(c, b).start()

        put(n_chunk - 2, 0).wait()
        put(n_chunk - 1, 1).wait()

    return pl.kernel(
        body,
        out_type=jax.ShapeDtypeStruct((n_idx, d), table.dtype),
        mesh=mesh,
        scratch_types=[pltpu.VMEM((per_worker,), jnp.int32),
                       pltpu.VMEM((2, SC_ROWS, d), table.dtype),
                       pltpu.SemaphoreType.DMA((2,)),
                       pltpu.SemaphoreType.DMA((2,))],
        name="sc_gather_rows",
    )(table, idx)


def _combine_dense_kernel(final, gate_ref, x1_ref, gf_ref, y0_ref, y1_ref, out_ref):
    gate = gate_ref[...]
    out = x1_ref[...] + gate[:, 0:1] * y0_ref[...] + gate[:, 1:2] * y1_ref[...]
    if final:
        out = _rms(out, gf_ref[...])
    out_ref[...] = out


def _combine_dense(gate_t, x1, g_final, yg, final):
    n, d = x1.shape
    nb = n // MERGE_T
    return pl.pallas_call(
        functools.partial(_combine_dense_kernel, final),
        grid=(nb,),
        in_specs=[pl.BlockSpec((MERGE_T, 2), lambda i: (i, 0)),
                  pl.BlockSpec((MERGE_T, d), lambda i: (i, 0)),
                  pl.BlockSpec((1, d), lambda i: (0, 0)),
                  pl.BlockSpec((MERGE_T, d), lambda i: (i, 0)),
                  pl.BlockSpec((MERGE_T, d), lambda i: (i + nb, 0))],
        out_specs=pl.BlockSpec((MERGE_T, d), lambda i: (i, 0)),
        out_shape=jax.ShapeDtypeStruct((n, d), F32),
        compiler_params=_cparams(1),
        name="combine_dense",
    )(gate_t, x1, g_final, yg, yg)


def _projection_weights(w):
    d = w.shape[0]
    blk = N_HEADS * HEAD_DIM
    kv = SWA_KV_HEADS * HEAD_DIM
    pa, pb, pc, pd = w[:, :3 * blk], w[:, 3 * blk:6 * blk], w[:, 6 * blk:9 * blk], w[:, 9 * blk:]
    half = N_HEADS * DIFF_QK

    def per_head(a, b):
        return jnp.stack([a.reshape(d, N_HEADS, DIFF_QK), b.reshape(d, N_HEADS, DIFF_QK)], axis=2).reshape(d, blk)

    s64, s32 = HEAD_DIM ** -0.5, DIFF_QK ** -0.5
    wn = jnp.concatenate([pa[:, blk:2 * blk], per_head(pb[:, 2 * half:3 * half], pb[:, 3 * half:4 * half]),
                          pc[:, blk:2 * blk], pd[:, blk:blk + kv]], axis=1)
    wt = jnp.concatenate([pa[:, :blk] * (s64 * LOG2E), pa[:, 2 * blk:],
                          per_head(pb[:, :half], pb[:, half:2 * half]) * (s32 * LOG2E), pb[:, 4 * half:],
                          pc[:, :blk] * s64, pc[:, 2 * blk:],
                          pd[:, :blk] * s64, pd[:, blk + kv:]], axis=1)
    assert wn.shape[1] == KN_COLS and wt.shape[1] == QV_ROWS
    return wn.astype(BF16), wt.T.astype(BF16)


def _router_weights(w_rg, b_rg, w_re, b_re):
    d = w_rg.shape[0]
    w = jnp.concatenate([w_rg.T, jnp.zeros((8 - N_GROUPS, d), F32), w_re.T], axis=0)
    b = jnp.concatenate([b_rg.astype(F32), jnp.full((8 - N_GROUPS,), NEG, F32), b_re.astype(F32)])[:, None]
    w_hi = w.astype(BF16)
    w_lo = (w - w_hi.astype(F32)).astype(BF16)
    return w_hi, w_lo, b


def _moe_plan(eid, rank, counts, n_rows_total):
    padded = (counts + MOE_ROWS - 1) // MOE_ROWS * MOE_ROWS
    pad_end = jnp.cumsum(padded)
    pad_start = pad_end - padded
    experts = jnp.arange(N_EXPERTS, dtype=jnp.int32)
    start_of = jnp.sum(jnp.where(eid[..., None] == experts, pad_start, 0), axis=-1)
    dest = start_of + rank
    n_blk = n_rows_total // MOE_ROWS
    n_used = (pad_end[-1] // MOE_ROWS).astype(jnp.int32)
    blk = jnp.minimum(jnp.arange(n_blk, dtype=jnp.int32), n_used - 1) * MOE_ROWS
    blk_expert = jnp.minimum(jnp.sum(pad_end[None, :] <= blk[:, None], axis=1), N_EXPERTS - 1).astype(jnp.int32)
    return dest.astype(jnp.int32), blk_expert, n_used.reshape(1)


def kernel(x, rel_bias, g_mix, w_in, diff_lq1, diff_lk1, diff_lq2, diff_lk2, diff_subln, swa_sinks,
           w_gate, w_br, w_o, g_ffn, w_route_group, b_route_group, w_route_expert, b_route_expert,
           w1, w3, w2, g_final):
    batch, seq, d = x.shape
    n = batch * seq
    depth = w_in.shape[0]
    assert seq % TB == 0 and n % MERGE_T == 0 and TB == MOBA_BLOCK
    tab = rel_bias.T.astype(F32)
    tiles_moba = _causal_bias_tiles(tab[:N_HEADS])
    tiles_diff = _causal_bias_tiles(tab[N_HEADS:2 * N_HEADS])
    tile_cur, tile_prev = _swa_bias_tiles(tab[2 * N_HEADS:])
    n_rows_total = n * 2 + N_EXPERTS * MOE_ROWS
    row = lambda v: v.astype(F32)[None, :]

    x2 = x.reshape(n, d)
    tokens = jnp.arange(n, dtype=jnp.int32)
    for l in range(depth):
        lam_init = 0.8 - 0.6 * math.exp(-0.3 * l)
        wn, wt = _projection_weights(w_in[l])
        kn, qvt = _inproj(x2, row(g_mix[l]), wn, wt)
        o_a = _moba_attention(kn, qvt, tiles_moba, batch)
        o_b = _diff_attention(kn, qvt, tiles_diff, row(diff_lq1[l]), row(diff_lk1[l]), row(diff_lq2[l]),
                              row(diff_lk2[l]), diff_subln[l].astype(F32)[:, None], lam_init, batch)
        o_c = _sb_attention(kn, qvt, batch)
        o_d = _swa_attention(kn, qvt, tile_cur, tile_prev, swa_sinks[l].astype(F32), batch)
        x1 = _merge(x2, row(g_mix[l]), o_a, o_b, o_c, o_d, w_gate[l].astype(BF16), w_br[l].astype(BF16),
                    w_o[l].astype(BF16))
        w_hi, w_lo, r_bias = _router_weights(w_route_group[l], b_route_group[l], w_route_expert[l],
                                             b_route_expert[l])
        h2, eid, gate, rank, cnt = _router(x1, row(g_ffn[l]), w_hi, w_lo, r_bias)
        dest, blk_expert, n_used = _moe_plan(eid, rank, cnt[:, 0].astype(jnp.int32), n_rows_total)
        src = jnp.zeros((n_rows_total,), jnp.int32).at[dest.reshape(-1)].set(
            jnp.concatenate([tokens, tokens]), unique_indices=True)
        xbuf = _sc_gather_rows(h2, src)
        y = _experts(blk_expert, n_used, xbuf, w1[l].astype(BF16), w3[l].astype(BF16), w2[l].astype(BF16))
        yg = _sc_gather_rows(y, dest.reshape(-1))
        x2 = _combine_dense(gate.T, x1, row(g_final), yg, l == depth - 1)
    return x2.reshape(batch, seq, d)
```

```python
import functools
import math

import numpy as np
import jax
import jax.numpy as jnp
from jax import lax
from jax.experimental import pallas as pl
from jax.experimental.pallas import tpu as pltpu
from jax.experimental.pallas import tpu_sc as plsc

F32 = jnp.float32
BF16 = jnp.bfloat16

HEAD_DIM = 64
N_HEADS = 4
DIFF_QK = 32
SWA_KV_HEADS = 2
SWA_WINDOW = 128
MOBA_BLOCK = 256
MOBA_TOPK = 3
REL_BUCKETS = 32
REL_MAX_DIST = 128
N_GROUPS = 4
EXPERTS_PER_GROUP = 8
N_EXPERTS = N_GROUPS * EXPERTS_PER_GROUP
NORM_EPS = 1e-6

TB = 256
LANES = 128
ONES_ROWS = 16
NEG = -1e30
LOG2E = math.log2(math.e)
SB_EXIT = 104.0
MOE_ROWS = 256
VMEM_LIMIT = 56 * 1024 * 1024

QV_MOBA_Q, QV_MOBA_V, QV_DIFF_Q, QV_DIFF_V, QV_SB_Q, QV_SB_V, QV_SWA_Q = range(7)
QV_SWA_V_128 = 14
QV_ROWS = 7 * 256 + 128
KN_MOBA, KN_DIFF, KN_SB = range(3)
KN_SWA_128 = 6
KN_COLS = 3 * 256 + 128


def _cparams(n_grid):
    return pltpu.CompilerParams(dimension_semantics=("arbitrary",) * n_grid,
                                vmem_limit_bytes=VMEM_LIMIT)


def _rel_bucket_np(n):
    n = np.maximum(n, 0)
    max_exact = REL_BUCKETS // 2
    nf = np.maximum(n, 1).astype(np.float64)
    large = max_exact + (np.log(nf / max_exact) / math.log(REL_MAX_DIST / max_exact)
                         * (REL_BUCKETS - max_exact)).astype(np.int64)
    large = np.minimum(large, REL_BUCKETS - 1)
    return np.where(n < max_exact, n, large)


def _first_far_distance():
    d = np.arange(0, 4 * REL_MAX_DIST)
    b = _rel_bucket_np(d)
    return int(np.min(d[b == REL_BUCKETS - 1]))


def _toeplitz_bias(tab, rows, cols, base, valid_fn, shift_far, unit):
    length = rows + cols - 1
    off = np.concatenate([np.arange(0, cols), np.arange(cols - length, 0)])
    n = base + off
    onehot = np.zeros((REL_BUCKETS, length), np.float32)
    onehot[_rel_bucket_np(n), np.arange(length)] = 1.0
    vec = jnp.dot(tab, jnp.asarray(onehot), precision=lax.Precision.HIGHEST)
    if shift_far:
        vec = vec - tab[:, REL_BUCKETS - 1:]
    vec = jnp.where(jnp.asarray(valid_fn(n))[None, :], vec * unit, NEG).astype(F32)
    flat = jnp.tile(vec, (1, rows))[:, :rows * (length - 1)]
    return flat.reshape(tab.shape[0], rows, length - 1)[:, :, :cols]


def _causal_bias_tiles(tab):
    assert _first_far_distance() <= TB + 1
    tiles = [_toeplitz_bias(tab, TB, TB, d * TB, lambda n: n >= 0, True, LOG2E) for d in range(2)]
    return jnp.stack(tiles, axis=1)


def _swa_bias_tiles(tab):
    in_window = lambda n: (n >= 0) & (n < SWA_WINDOW)
    return (_toeplitz_bias(tab, TB, TB, 0, in_window, False, 1.0),
            _toeplitz_bias(tab, SWA_WINDOW, TB, SWA_WINDOW, in_window, False, 1.0))


def _pad_rows(q, off, total):
    n, t = q.shape
    parts = []
    if off:
        parts.append(jnp.zeros((off, t), q.dtype))
    parts.append(q)
    if total - off - n:
        parts.append(jnp.zeros((total - off - n, t), q.dtype))
    return jnp.concatenate(parts, axis=0) if len(parts) > 1 else q


def _with_ones(v):
    return jnp.concatenate([v, jnp.ones((ONES_ROWS, v.shape[1]), v.dtype)], axis=0)


def _rms(x, g_row):
    ms = jnp.mean(x * x, axis=-1, keepdims=True)
    return x * lax.rsqrt(ms + NORM_EPS) * g_row


IN_T = 512
IN_CHUNK = 384


def _inproj_kernel(x_ref, g_ref, wn_ref, wt_ref, kn_ref, qvt_ref):
    h = _rms(x_ref[...], g_ref[...]).astype(BF16)
    kn_ref[...] = jnp.dot(h, wn_ref[...], preferred_element_type=F32).astype(BF16)
    for r0 in range(0, QV_ROWS, IN_CHUNK):
        pt = lax.dot_general(wt_ref[r0:r0 + IN_CHUNK, :], h, (((1,), (1,)), ((), ())),
                             preferred_element_type=F32)
        for s in range(IN_T // TB):
            qvt_ref[s, r0:r0 + IN_CHUNK, :] = pt[:, s * TB:(s + 1) * TB].astype(BF16)


def _inproj(x2, g, wn, wt):
    n, d = x2.shape
    return pl.pallas_call(
        _inproj_kernel,
        grid=(n // IN_T,),
        in_specs=[pl.BlockSpec((IN_T, d), lambda i: (i, 0)),
                  pl.BlockSpec((1, d), lambda i: (0, 0)),
                  pl.BlockSpec((d, KN_COLS), lambda i: (0, 0)),
                  pl.BlockSpec((QV_ROWS, d), lambda i: (0, 0))],
        out_specs=[pl.BlockSpec((IN_T, KN_COLS), lambda i: (i, 0)),
                   pl.BlockSpec((IN_T // TB, QV_ROWS, TB), lambda i: (i, 0, 0))],
        out_shape=[jax.ShapeDtypeStruct((n, KN_COLS), BF16),
                   jax.ShapeDtypeStruct((n // TB, QV_ROWS, TB), BF16)],
        compiler_params=_cparams(1),
        name="inproj",
    )(x2, g, wn, wt)


FLASH_OVERFLOW = 100.0
FLASH_SKEW = 4


class _FlashScratch:
    def __init__(self, n_chain, qpad, m, acc, over, s):
        self.n_chain, self.qpad, self.m, self.acc, self.over, self.s = n_chain, qpad, m, acc, over, s

    @staticmethod
    def shapes(n_chain):
        return [pltpu.VMEM((n_chain, LANES, TB), BF16),
                pltpu.VMEM((n_chain, 1, TB), F32),
                pltpu.VMEM((n_chain, HEAD_DIM + ONES_ROWS, TB), F32),
                pltpu.VMEM((n_chain, 1, TB), F32),
                pltpu.VMEM((n_chain, TB, TB), F32)]


def _flash_two_pass(sc, chains):
    if len(chains) > sc.n_chain:
        for k in range(0, len(chains), sc.n_chain):
            _flash_two_pass(sc, chains[k:k + sc.n_chain])
        return
    block_max = []
    for c, kblk, add, _ in chains:
        s = jnp.dot(kblk, sc.qpad[c], preferred_element_type=F32)
        if add is not None:
            s = s + add
        sc.s[c] = s
        block_max.append(jnp.max(s, axis=0, keepdims=True))
    for (c, _, _, vext), mx in zip(chains, block_max):
        m_old = sc.m[c]
        m_new = jnp.maximum(m_old, mx)
        p = jnp.exp2(sc.s[c] - m_new).astype(BF16)
        sc.acc[c] = sc.acc[c] * jnp.exp2(m_old - m_new) + jnp.dot(vext, p, preferred_element_type=F32)
        sc.m[c] = m_new


def _flash_lagged(sc, chains):
    def finish(c, p, mx, ref, vext):
        pv = jnp.dot(vext, p, preferred_element_type=F32)
        ref_new = jnp.maximum(ref, mx)
        sc.acc[c] = (sc.acc[c] + pv) * jnp.exp2(ref - ref_new)
        sc.m[c] = ref_new
        sc.over[c] = jnp.maximum(sc.over[c], mx - ref)

    pending = []
    for c, kblk, add, vext in chains:
        s = jnp.dot(kblk, sc.qpad[c], preferred_element_type=F32)
        if add is not None:
            s = s + add
        if len(pending) >= min(FLASH_SKEW, sc.n_chain):
            finish(*pending.pop(0))
        ref = sc.m[c]
        pending.append((c, jnp.exp2(s - ref).astype(BF16), jnp.max(s, axis=0, keepdims=True), ref, vext))
    for item in pending:
        finish(*item)


def _flash_causal(sc, i, block_chains):
    def init():
        for c in range(sc.n_chain):
            sc.m[c] = jnp.full(sc.m.shape[1:], NEG, F32)
            sc.acc[c] = jnp.zeros(sc.acc.shape[1:], F32)
            sc.over[c] = jnp.full(sc.over.shape[1:], NEG, F32)

    def run(step):
        init()
        _flash_two_pass(sc, block_chains(i, 0))

        @pl.when(i >= 1)
        def _():
            step(sc, block_chains(i - 1, 1))

        n_far = jnp.maximum(i - 1, 0)

        def far_pair(t, carry):
            step(sc, block_chains(2 * t, None) + block_chains(2 * t + 1, None))
            return carry

        lax.fori_loop(0, n_far // 2, far_pair, 0)

        @pl.when(lax.rem(n_far, 2) == 1)
        def _():
            step(sc, block_chains(n_far - 1, None))

    run(_flash_lagged)
    worst = sc.over[0]
    for c in range(1, sc.n_chain):
        worst = jnp.maximum(worst, sc.over[c])

    @pl.when(jnp.max(worst) > FLASH_OVERFLOW)
    def _():
        run(_flash_two_pass)


def _diff_kernel(lam_init, qt_ref, k_ref, vt_ref, tile_ref, lq1_ref, lk1_ref, lq2_ref, lk2_ref,
                 subln_ref, o_ref, *scratch):
    i = pl.program_id(1)
    sc = _FlashScratch(2 * N_HEADS, *scratch)
    acc_scr = sc.acc
    for h in range(N_HEADS):
        for mp in range(2):
            r0 = h * HEAD_DIM + mp * DIFF_QK
            sc.qpad[2 * h + mp] = _pad_rows(qt_ref[0, r0:r0 + DIFF_QK, :],
                                            (h % 2) * HEAD_DIM + mp * DIFF_QK, LANES)

    def block_chains(j, tile_idx):
        row = pl.multiple_of(j * TB, TB)
        chains = []
        for h in range(N_HEADS):
            g = h // 2
            kblk = k_ref[pl.ds(row, TB), g * LANES:(g + 1) * LANES]
            add = None if tile_idx is None else tile_ref[h, tile_idx]
            vext = _with_ones(vt_ref[j, h * HEAD_DIM:(h + 1) * HEAD_DIM, :])
            chains += [(2 * h + mp, kblk, add, vext) for mp in range(2)]
        return chains

    _flash_causal(sc, i, block_chains)

    lam =(jnp.exp(jnp.sum(lq1_ref[...] * lk1_ref[...], keepdims=True))
           - jnp.exp(jnp.sum(lq2_ref[...] * lk2_ref[...], keepdims=True)) + lam_init)
    for h in range(N_HEADS):
        a1 = acc_scr[2 * h]
        a2 = acc_scr[2 * h + 1]
        o1 = a1[:HEAD_DIM] / a1[HEAD_DIM:HEAD_DIM + 1]
        o2 = a2[:HEAD_DIM] / a2[HEAD_DIM:HEAD_DIM + 1]
        a = o1 - lam * o2
        ms = jnp.mean(a * a, axis=0, keepdims=True)
        y = a * lax.rsqrt(ms + NORM_EPS) * subln_ref[...] * (1.0 - lam_init)
        o_ref[0, h * HEAD_DIM:(h + 1) * HEAD_DIM, :] = y.astype(BF16)


def _attn_common_specs(nb, q_blk, k_blk, v_blk):
    return [pl.BlockSpec((1, TB, TB), lambda b, i: (b * nb + i, q_blk, 0)),
            pl.BlockSpec((nb * TB, TB), lambda b, i: (b, k_blk)),
            pl.BlockSpec((nb, TB, TB), lambda b, i: (b, v_blk, 0))]


def _small_spec(shape):
    return pl.BlockSpec(shape, lambda b, i: (0,) * len(shape))


def _diff_attention(kn, qvt, tiles, lq1, lk1, lq2, lk2, subln, lam_init, batch):
    nblk = qvt.shape[0]
    nb = nblk // batch
    n_chain = 2 * N_HEADS
    return pl.pallas_call(
        functools.partial(_diff_kernel, lam_init),
        grid=(batch, nb),
        in_specs=_attn_common_specs(nb, QV_DIFF_Q, KN_DIFF, QV_DIFF_V) + [
            _small_spec((N_HEADS, 2, TB, TB)),
            _small_spec((1, DIFF_QK)), _small_spec((1, DIFF_QK)),
            _small_spec((1, DIFF_QK)), _small_spec((1, DIFF_QK)),
            _small_spec((HEAD_DIM, 1))],
        out_specs=pl.BlockSpec((1, TB, TB), lambda b, i: (b * nb + i, 0, 0)),
        out_shape=jax.ShapeDtypeStruct((nblk, TB, TB), BF16),
        scratch_shapes=_FlashScratch.shapes(n_chain),
        compiler_params=_cparams(2),
        name="diff_attn",
    )(qvt, kn, qvt, tiles, lq1, lk1, lq2, lk2, subln)


def _moba_kernel(nb, nbp, qt_ref, k_ref, vt_ref, tile_ref, o_ref, kmean_scr, sel_scr, *scratch):
    i = pl.program_id(1)
    sc = _FlashScratch(N_HEADS, *scratch)
    qpad_scr, acc_scr = sc.qpad, sc.acc

    @pl.when(i == 0)
    def _():
        kmean_scr[...] = jnp.zeros(kmean_scr.shape, F32)
        for jb in range(nb):
            blk = k_ref[jb * TB:(jb + 1) * TB, :].astype(F32)
            kmean_scr[jb:jb + 1, :] = jnp.mean(blk, axis=0, keepdims=True)

    for h in range(N_HEADS):
        qpad_scr[h] = _pad_rows(qt_ref[0, h * HEAD_DIM:(h + 1) * HEAD_DIM, :], (h % 2) * HEAD_DIM, LANES)

    blk_id = lax.broadcasted_iota(jnp.int32, (nbp, TB), 0)
    for h in range(N_HEADS):
        g = h // 2
        km = kmean_scr[:, g * LANES:(g + 1) * LANES].astype(BF16)
        gate = jnp.dot(km, qpad_scr[h], preferred_element_type=F32)
        avail = blk_id < i
        sel = jnp.zeros((nbp, TB), jnp.bool_)
        for _ in range(MOBA_TOPK):
            gm = jnp.where(avail, gate, -jnp.inf)
            best = jnp.max(gm, axis=0, keepdims=True)
            is_best = avail & (gm == best)
            first = jnp.min(jnp.where(is_best, blk_id, nbp), axis=0, keepdims=True)
            pick = blk_id == first
            sel = sel | pick
            avail = avail & jnp.logical_not(pick)
        sel_scr[h] = jnp.where(sel, 0.0, NEG).astype(F32)

    def block_chains(j, tile_idx):
        row = pl.multiple_of(j * TB, TB)
        chains = []
        for h in range(N_HEADS):
            g = h // 2
            kblk = k_ref[pl.ds(row, TB), g * LANES:(g + 1) * LANES]
            add = None if tile_idx is None else tile_ref[h, tile_idx]
            if tile_idx != 0:
                cm = sel_scr[h, pl.ds(j, 1), :]
                add = cm if add is None else add + cm
            chains.append((h, kblk, add, _with_ones(vt_ref[j, h * HEAD_DIM:(h + 1) * HEAD_DIM, :])))
        return chains

    _flash_causal(sc, i, block_chains)

    for h in range(N_HEADS):
        a = acc_scr[h]
        o_ref[0, h * HEAD_DIM:(h + 1) * HEAD_DIM, :] = (a[:HEAD_DIM] / a[HEAD_DIM:HEAD_DIM + 1]).astype(BF16)


def _moba_attention(kn, qvt, tiles, batch):
    nblk = qvt.shape[0]
    nb = nblk // batch
    nbp = max(8, -(-nb // 8) * 8)
    return pl.pallas_call(
        functools.partial(_moba_kernel, nb, nbp),
        grid=(batch, nb),
        in_specs=_attn_common_specs(nb, QV_MOBA_Q, KN_MOBA, QV_MOBA_V) + [
            _small_spec((N_HEADS, 2, TB, TB))],
        out_specs=pl.BlockSpec((1, TB, TB), lambda b, i: (b * nb + i, 0, 0)),
        out_shape=jax.ShapeDtypeStruct((nblk, TB, TB), BF16),
        scratch_shapes=[pltpu.VMEM((nbp, TB), F32),
                        pltpu.VMEM((N_HEADS, nbp, TB), F32)] + _FlashScratch.shapes(N_HEADS),
        compiler_params=_cparams(2),
        name="moba_attn",
    )(qvt, kn, qvt, tiles)


def _sb_kernel(qt_ref, k_ref, vt_ref, o_ref, qpad_scr, acc_scr, c_scr):
    i = pl.program_id(1)
    rows = lax.broadcasted_iota(jnp.int32, (TB, TB), 0)
    cols = lax.broadcasted_iota(jnp.int32, (TB, TB), 1)
    upper = (cols > rows).astype(BF16)
    strict = cols > rows

    heads = range(N_HEADS)

    def step(j, first):
        row = pl.multiple_of(j * TB, TB)
        zs = [jnp.dot(k_ref[pl.ds(row, TB), (h // 2) * LANES:(h // 2 + 1) * LANES], qpad_scr[h],
                      preferred_element_type=F32) for h in heads]
        log_keeps = []
        for z in zs:
            lk = -(jnp.maximum(z, 0.0) + jnp.log(1.0 + jnp.exp(-jnp.abs(z))))
            log_keeps.append(jnp.where(strict, lk, 0.0) if first else lk)
        laters = []
        for h in heads:
            hi = log_keeps[h].astype(BF16)
            lo = (log_keeps[h] - hi.astype(F32)).astype(BF16)
            later = (jnp.dot(upper, hi, preferred_element_type=F32)
                     + jnp.dot(upper, lo, preferred_element_type=F32))
            laters.append(later if first else later + c_scr[h])
        c_max = None
        for h in heads:
            w = jnp.exp(zs[h] + log_keeps[h] + laters[h])
            if first:
                w = jnp.where(strict, w, 0.0)
            pv = jnp.dot(vt_ref[j, h * HEAD_DIM:(h + 1) * HEAD_DIM, :], w.astype(BF16),
                         preferred_element_type=F32)
            acc_scr[h] = pv if first else acc_scr[h] + pv
            c_new = laters[h][0:1, :] + log_keeps[h][0:1, :]
            c_scr[h] = c_new
            c_max = c_new if c_max is None else jnp.maximum(c_max, c_new)
        return jnp.max(c_max) > -SB_EXIT

    for h in heads:
        qpad_scr[h] = _pad_rows(qt_ref[0, h * HEAD_DIM:(h + 1) * HEAD_DIM, :], (h % 2) * HEAD_DIM, LANES)

    def cond(carry):
        j, go = carry
        return jnp.logical_and(j >= 0, go)

    def body(carry):
        j, _ = carry
        return j - 1, step(j, False)

    lax.while_loop(cond, body, (i - 1, step(i, True)))
    for h in heads:
        o_ref[0, h * HEAD_DIM:(h + 1) * HEAD_DIM, :] = acc_scr[h].astype(BF16)


def _sb_attention(kn, qvt, batch):
    nblk = qvt.shape[0]
    nb = nblk // batch
    return pl.pallas_call(
        _sb_kernel,
        grid=(batch, nb),
        in_specs=_attn_common_specs(nb, QV_SB_Q, KN_SB, QV_SB_V),
        out_specs=pl.BlockSpec((1, TB, TB), lambda b, i: (b * nb + i, 0, 0)),
        out_shape=jax.ShapeDtypeStruct((nblk, TB, TB), BF16),
        scratch_shapes=[pltpu.VMEM((N_HEADS, LANES, TB), BF16),
                        pltpu.VMEM((N_HEADS, HEAD_DIM, TB), F32),
                        pltpu.VMEM((N_HEADS, 1, TB), F32)],
        compiler_params=_cparams(2),
        name="sb_attn",
    )(qvt, kn, qvt)


def _swa_kernel(sink_ref, qt_ref, kc_ref, kp_ref, vc_ref, vp_ref, tc_ref, tp_ref, o_ref):
    i = pl.program_id(1)
    kc = kc_ref[...]
    kp = kp_ref[...]
    vc = vc_ref[0]
    vp = vp_ref[0][:, TB - SWA_WINDOW:]
    no_prev = jnp.where(i > 0, 0.0, NEG)
    group = N_HEADS // SWA_KV_HEADS
    for h in range(N_HEADS):
        kv = h // group
        qpad = _pad_rows(qt_ref[0, h * HEAD_DIM:(h + 1) * HEAD_DIM, :], kv * HEAD_DIM, LANES)
        s_c = jnp.dot(kc, qpad, preferred_element_type=F32) + tc_ref[h]
        s_p = jnp.dot(kp, qpad, preferred_element_type=F32) + tp_ref[h] + no_prev
        sink = sink_ref[h]
        m = jnp.maximum(jnp.maximum(jnp.max(s_c, axis=0, keepdims=True),
                                    jnp.max(s_p, axis=0, keepdims=True)), sink)
        p_c = jnp.exp(s_c - m).astype(BF16)
        p_p = jnp.exp(s_p - m).astype(BF16)
        o = (jnp.dot(_with_ones(vc[kv * HEAD_DIM:(kv + 1) * HEAD_DIM]), p_c, preferred_element_type=F32)
             + jnp.dot(_with_ones(vp[kv * HEAD_DIM:(kv + 1) * HEAD_DIM]), p_p, preferred_element_type=F32))
        denom = o[HEAD_DIM:HEAD_DIM + 1] + jnp.exp(sink - m)
        o_ref[0, h * HEAD_DIM:(h + 1) * HEAD_DIM, :] = (o[:HEAD_DIM] / denom).astype(BF16)


def _swa_attention(kn, qvt, tile_cur, tile_prev, sinks, batch):
    nblk = qvt.shape[0]
    nb = nblk // batch
    half = TB // SWA_WINDOW
    kv_rows = SWA_KV_HEADS * HEAD_DIM
    return pl.pallas_call(
        _swa_kernel,
        grid=(batch, nb),
        in_specs=[pl.BlockSpec(memory_space=pltpu.SMEM),
                  pl.BlockSpec((1, TB, TB), lambda b, i: (b * nb + i, QV_SWA_Q, 0)),
                  pl.BlockSpec((TB, kv_rows), lambda b, i: (b * nb + i, KN_SWA_128)),
                  pl.BlockSpec((SWA_WINDOW, kv_rows),
                               lambda b, i: (b * nb * half + jnp.maximum(half * i - 1, 0), KN_SWA_128)),
                  pl.BlockSpec((1, kv_rows, TB), lambda b, i: (b * nb + i, QV_SWA_V_128, 0)),
                  pl.BlockSpec((1, kv_rows, TB), lambda b, i: (b * nb + jnp.maximum(i - 1, 0), QV_SWA_V_128, 0)),
                  _small_spec((N_HEADS, TB, TB)),
                  _small_spec((N_HEADS, SWA_WINDOW, TB))],
        out_specs=pl.BlockSpec((1, TB, TB), lambda b, i: (b * nb + i, 0, 0)),
        out_shape=jax.ShapeDtypeStruct((nblk, TB, TB), BF16),
        compiler_params=_cparams(2),
        name="swa_attn",
    )(sinks, qvt, kn, kn, qvt, qvt, tile_cur, tile_prev)


MERGE_T = 512


def _merge_kernel(x_ref, g_ref, oa_ref, ob_ref, oc_ref, od_ref, wg_ref, wbr_ref, wo_ref, x1_ref):
    x = x_ref[...]
    h = _rms(x, g_ref[...]).astype(BF16)
    merged = None
    for bi, o_ref in enumerate((oa_ref, ob_ref, oc_ref, od_ref)):
        gate = jax.nn.sigmoid(jnp.dot(h, wg_ref[bi], preferred_element_type=F32))
        branch = jnp.concatenate(
            [lax.dot_general(o_ref[s], wbr_ref[bi], (((0,), (0,)), ((), ())), preferred_element_type=F32)
             for s in range(MERGE_T // TB)], axis=0)
        term = gate * branch
        merged = term if merged is None else merged + term
    x1_ref[...] = x + jnp.dot(merged.astype(BF16), wo_ref[...], preferred_element_type=F32)


def _merge(x2, g, o_a, o_b, o_c, o_d, wg, wbr, wo):
    n, d = x2.shape
    o_spec = pl.BlockSpec((MERGE_T // TB, TB, TB), lambda i: (i, 0, 0))
    return pl.pallas_call(
        _merge_kernel,
        grid=(n // MERGE_T,),
        in_specs=[pl.BlockSpec((MERGE_T, d), lambda i: (i, 0)),
                  pl.BlockSpec((1, d), lambda i: (0, 0)),
                  o_spec, o_spec, o_spec, o_spec,
                  pl.BlockSpec(wg.shape, lambda i: (0, 0, 0)),
                  pl.BlockSpec(wbr.shape, lambda i: (0, 0, 0)),
                  pl.BlockSpec(wo.shape, lambda i: (0, 0))],
        out_specs=pl.BlockSpec((MERGE_T, d), lambda i: (i, 0)),
        out_shape=jax.ShapeDtypeStruct((n, d), F32),
        compiler_params=_cparams(1),
        name="merge",
    )(x2, g, o_a, o_b, o_c, o_d, wg, wbr, wo)


ROUTER_T = 512
ROUTER_ROWS = 8 + N_EXPERTS


def _first_argmax_rows(v, n_rows):
    best = jnp.max(v, axis=0, keepdims=True)
    ids = lax.broadcasted_iota(jnp.int32, v.shape, 0)
    return best, jnp.min(jnp.where(v == best, ids, n_rows), axis=0, keepdims=True)


def _router_kernel(x_ref, g_ref, whi_ref, wlo_ref, b_ref, h2_ref, eid_ref, gate_ref, rank_ref, cnt_ref,
                   base_scr):
    i = pl.program_id(0)

    @pl.when(i == 0)
    def _():
        base_scr[...] = jnp.zeros(base_scr.shape, F32)

    h2 = _rms(x_ref[...], g_ref[...])
    h2_ref[...] = h2
    h_hi = h2.astype(BF16)
    h_lo = (h2 - h_hi.astype(F32)).astype(BF16)
    nt = (((1,), (1,)), ((), ()))
    logits = (lax.dot_general(whi_ref[...], h_hi, nt, preferred_element_type=F32)
              + lax.dot_general(whi_ref[...], h_lo, nt, preferred_element_type=F32)
              + lax.dot_general(wlo_ref[...], h_hi, nt, preferred_element_type=F32)
              + b_ref[...])
    gl = logits[0:8]
    gmax, grp = _first_argmax_rows(gl, 8)
    p_grp = 1.0 / jnp.sum(jnp.exp(gl - gmax), axis=0, keepdims=True)
    e_sel = jnp.zeros((EXPERTS_PER_GROUP, ROUTER_T), F32)
    for g in range(N_GROUPS):
        e_sel = jnp.where(grp == g, logits[8 + 8 * g:16 + 8 * g], e_sel)
    ids8 = lax.broadcasted_iota(jnp.int32, e_sel.shape, 0)
    v1, i1 = _first_argmax_rows(e_sel, EXPERTS_PER_GROUP)
    e_rest = jnp.where(ids8 == i1, -jnp.inf, e_sel)
    v2, i2 = _first_argmax_rows(e_rest, EXPERTS_PER_GROUP)
    r = jnp.exp(v2 - v1)
    s1 = 1.0 / (1.0 + r)
    gate_ref[0:1, :] = p_grp * s1
    gate_ref[1:2, :] = p_grp * (r * s1)
    e1 = grp * EXPERTS_PER_GROUP + i1
    e2 = grp * EXPERTS_PER_GROUP + i2
    eid_ref[0:1, :] = e1
    eid_ref[1:2, :] = e2

    ids_e = lax.broadcasted_iota(jnp.int32, (N_EXPERTS, ROUTER_T), 0)
    oh1 = ids_e == e1
    oh2 = ids_e == e2
    cnt = oh1.astype(F32) + oh2.astype(F32)
    tr = lax.broadcasted_iota(jnp.int32, (ROUTER_T, ROUTER_T), 0)
    tc = lax.broadcasted_iota(jnp.int32, (ROUTER_T, ROUTER_T), 1)
    before = (tr < tc).astype(BF16)
    prefix = jnp.dot(cnt.astype(BF16), before, preferred_element_type=F32) + base_scr[:, 0:1]
    rank_ref[0:1, :] = jnp.sum(jnp.where(oh1, prefix, 0.0), axis=0, keepdims=True).astype(jnp.int32)
    rank_ref[1:2, :] = jnp.sum(jnp.where(oh2, prefix, 0.0), axis=0, keepdims=True).astype(jnp.int32)
    base_scr[...] = base_scr[...] + jnp.sum(cnt, axis=1, keepdims=True)
    cnt_ref[...] = base_scr[...]


def _router(x1, g, w_hi, w_lo, bias):
    n, d = x1.shape
    row2 = lambda dt: jax.ShapeDtypeStruct((2, n), dt)
    spec2 = pl.BlockSpec((2, ROUTER_T), lambda i: (0, i))
    return pl.pallas_call(
        _router_kernel,
        grid=(n // ROUTER_T,),
        in_specs=[pl.BlockSpec((ROUTER_T, d), lambda i: (i, 0)),
                  pl.BlockSpec((1, d), lambda i: (0, 0)),
                  pl.BlockSpec((ROUTER_ROWS, d), lambda i: (0, 0)),
                  pl.BlockSpec((ROUTER_ROWS, d), lambda i: (0, 0)),
                  pl.BlockSpec((ROUTER_ROWS, 1), lambda i: (0, 0))],
        out_specs=[pl.BlockSpec((ROUTER_T, d), lambda i: (i, 0)), spec2, spec2, spec2,
                   pl.BlockSpec((N_EXPERTS, LANES), lambda i: (0, 0))],
        out_shape=[jax.ShapeDtypeStruct((n, d), F32), row2(jnp.int32), row2(F32), row2(jnp.int32),
                   jax.ShapeDtypeStruct((N_EXPERTS, LANES), F32)],
        scratch_shapes=[pltpu.VMEM((N_EXPERTS, LANES), F32)],
        compiler_params=_cparams(1),
        name="router",
    )(x1, g, w_hi, w_lo, bias)


ROW_T = 256


def _dispatch_kernel(dest_ref, h2_ref, xin_ref, xbuf_ref, sem):
    del xin_ref

    def row_copy(r, k):
        return pltpu.make_async_copy(h2_ref.at[pl.ds(r, 1)], xbuf_ref.at[pl.ds(dest_ref[k, r], 1)], sem)

    def issue(r, carry):
        row_copy(r, 0).start()
        row_copy(r, 1).start()
        return carry

    def drain(r, carry):
        row_copy(r, 0).wait()
        row_copy(r, 1).wait()
        return carry

    lax.fori_loop(0, ROW_T, issue, 0, unroll=8)
    lax.fori_loop(0, ROW_T, drain, 0, unroll=8)


def _dispatch(dest3, h2, xbuf_init):
    n, d = h2.shape
    return pl.pallas_call(
        _dispatch_kernel,
        grid=(n // ROW_T,),
        in_specs=[pl.BlockSpec((None, 2, ROW_T), lambda i: (i, 0, 0), memory_space=pltpu.SMEM),
                  pl.BlockSpec((ROW_T, d), lambda i: (i, 0)),
                  pl.BlockSpec(memory_space=pl.ANY)],
        out_specs=pl.BlockSpec(memory_space=pl.ANY),
        out_shape=jax.ShapeDtypeStruct(xbuf_init.shape, xbuf_init.dtype),
        scratch_shapes=[pltpu.SemaphoreType.DMA(())],
        input_output_aliases={2: 0},
        compiler_params=_cparams(1),
        name="dispatch",
    )(dest3, h2, xbuf_init)


def _expert_kernel(be_ref, nu_ref, nv_ref, x_ref, w1_ref, w3_ref, w2_ref, y_ref):
    del be_ref
    used = pl.program_id(0) < nu_ref[0]

    @pl.when(used)
    def _():
        live = lax.broadcasted_iota(jnp.int32, x_ref.shape, 0) < nv_ref[pl.program_id(0)]
        xb = jnp.where(live, x_ref[...], 0.0).astype(BF16)
        a = jnp.dot(xb, w1_ref[...], preferred_element_type=F32)
        b = jnp.dot(xb, w3_ref[...], preferred_element_type=F32)
        mid = (a * jax.nn.sigmoid(a) * b).astype(BF16)
        y_ref[...] = jnp.dot(mid, w2_ref[...], preferred_element_type=F32)

    @pl.when(jnp.logical_not(used))
    def _():
        y_ref[...] = jnp.zeros(y_ref.shape, F32)


def _experts(blk_expert, n_used, n_valid, xbuf, w1, w3, w2):
    r, d = xbuf.shape
    de = w1.shape[-1]
    row_map = lambda i, be, nu, nv: (jnp.minimum(i, nu[0] - 1), 0)
    grid_spec = pltpu.PrefetchScalarGridSpec(
        num_scalar_prefetch=3,
        grid=(r // MOE_ROWS,),
        in_specs=[pl.BlockSpec((MOE_ROWS, d), row_map),
                  pl.BlockSpec((None, d, de), lambda i, be, nu, nv: (be[i], 0, 0)),
                  pl.BlockSpec((None, d, de), lambda i, be, nu, nv: (be[i], 0, 0)),
                  pl.BlockSpec((None, de, d), lambda i, be, nu, nv: (be[i], 0, 0))],
        out_specs=pl.BlockSpec((MOE_ROWS, d), lambda i, be, nu, nv: (i, 0)),
    )
    return pl.pallas_call(
        _expert_kernel,
        grid_spec=grid_spec,
        out_shape=jax.ShapeDtypeStruct((r, d), F32),
        compiler_params=_cparams(1),
        name="experts",
    )(blk_expert, n_used, n_valid, xbuf, w1, w3, w2)


def _combine_kernel(final, dest_ref, gate_ref, x1_ref, gf_ref, y_ref, out_ref, buf, sem):
    def row_copy(r, k):
        return pltpu.make_async_copy(y_ref.at[pl.ds(dest_ref[k, r], 1)], buf.at[k, pl.ds(r, 1)], sem)

    def issue(r, carry):
        row_copy(r, 0).start()
        row_copy(r, 1).start()
        return carry

    def drain(r, carry):
        row_copy(r, 0).wait()
        row_copy(r, 1).wait()
        return carry

    lax.fori_loop(0, ROW_T, issue, 0, unroll=8)
    lax.fori_loop(0, ROW_T, drain, 0, unroll=8)
    gate = gate_ref[...]
    out = x1_ref[...] + gate[:, 0:1] * buf[0] + gate[:, 1:2] * buf[1]
    if final:
        out = _rms(out, gf_ref[...])
    out_ref[...] = out


def _combine(dest3, gate_t, x1, g_final, y, final):
    n, d = x1.shape
    return pl.pallas_call(
        functools.partial(_combine_kernel, final),
        grid=(n // ROW_T,),
        in_specs=[pl.BlockSpec((None, 2, ROW_T), lambda i: (i, 0, 0), memory_space=pltpu.SMEM),
                  pl.BlockSpec((ROW_T, 2), lambda i: (i, 0)),
                  pl.BlockSpec((ROW_T, d), lambda i: (i, 0)),
                  pl.BlockSpec((1, d), lambda i: (0, 0)),
                  pl.BlockSpec(memory_space=pl.ANY)],
        out_specs=pl.BlockSpec((ROW_T, d), lambda i: (i, 0)),
        out_shape=jax.ShapeDtypeStruct((n, d), F32),
        scratch_shapes=[pltpu.VMEM((2, ROW_T, d), F32), pltpu.SemaphoreType.DMA(())],
        compiler_params=_cparams(1),
        name="combine",
    )(dest3, gate_t, x1, g_final, y)


SC_CORES = 2
SC_SUBCORES = 16
SC_ROWS = 32


def _sc_gather_rows(table, idx):
    n_idx = idx.shape[0]
    d = table.shape[1]
    n_workers = SC_CORES * SC_SUBCORES
    per_worker = n_idx // n_workers
    n_chunk = per_worker // SC_ROWS
    assert per_worker * n_workers == n_idx and n_chunk * SC_ROWS == per_worker and n_chunk % 2 == 0
    mesh = plsc.VectorSubcoreMesh(core_axis_name="c", subcore_axis_name="s",
                                  num_cores=SC_CORES, num_subcores=SC_SUBCORES)

    def body(table_hbm, idx_hbm, out_hbm, idx_v, rows_v, gsem, wsem):
        worker = lax.axis_index("s") * SC_CORES + lax.axis_index("c")
        base = worker * per_worker
        pltpu.sync_copy(idx_hbm.at[pl.ds(base, per_worker)], idx_v)

        def gather(c, b):
            return pltpu.make_async_copy(table_hbm.at[idx_v.at[pl.ds(c * SC_ROWS, SC_ROWS)]],
                                         rows_v.at[b], gsem.at[b])

        def put(c, b):
            return pltpu.make_async_copy(rows_v.at[b], out_hbm.at[pl.ds(base + c * SC_ROWS, SC_ROWS)],
                                         wsem.at[b])

        gather(0, 0).start()

        @pl.loop(0, n_chunk, step=2)
        def _(c0):
            for b in range(2):
                c = c0 + b
                gather(c, b).wait()

                @pl.when(c + 1 < n_chunk)
                def _():
                    @pl.when(c >= 1)
                    def _():
                        put(c - 1, 1 - b).wait()

                    gather(c + 1, 1 - b).start()

                put(c, b).start()

        put(n_chunk - 2, 0).wait()
        put(n_chunk - 1, 1).wait()

    return pl.kernel(
        body,
        out_type=jax.ShapeDtypeStruct((n_idx, d), table.dtype),
        mesh=mesh,
        scratch_types=[pltpu.VMEM((per_worker,), jnp.int32),
                       pltpu.VMEM((2, SC_ROWS, d), table.dtype),
                       pltpu.SemaphoreType.DMA((2,)),
                       pltpu.SemaphoreType.DMA((2,))],
        name="sc_gather_rows",
    )(table, idx)


def _sc_scatter_rows(rows, dest, n_out):
    n, d = rows.shape
    n_workers = SC_CORES * SC_SUBCORES
    per_worker = n // n_workers
    n_chunk = per_worker // SC_ROWS
    assert per_worker * n_workers == n and n_chunk * SC_ROWS == per_worker and n_chunk % 2 == 0
    mesh = plsc.VectorSubcoreMesh(core_axis_name="c", subcore_axis_name="s",
                                  num_cores=SC_CORES, num_subcores=SC_SUBCORES)
    dest3 = dest.reshape(2, n // SC_ROWS, SC_ROWS)

    def body(rows_hbm, idx_hbm, out_hbm, idx_v, rows_v, lsem, ssem):
        worker = lax.axis_index("s") * SC_CORES + lax.axis_index("c")
        for k in range(2):
            pltpu.sync_copy(idx_hbm.at[k, pl.ds(worker * n_chunk, n_chunk)], idx_v.at[k])

        def load(c, b):
            return pltpu.make_async_copy(rows_hbm.at[pl.ds(worker * per_worker + c * SC_ROWS, SC_ROWS)],
                                         rows_v.at[b], lsem.at[b])

        def scatter(c, b, k):
            return pltpu.make_async_copy(rows_v.at[b], out_hbm.at[idx_v.at[k, c]], ssem.at[b])

        load(0, 0).start()

        @pl.loop(0, n_chunk, step=2)
        def _(c0):
            for b in range(2):
                c = c0 + b
                load(c, b).wait()

                @pl.when(c + 1 < n_chunk)
                def _():
                    @pl.when(c >= 1)
                    def _():
                        scatter(c - 1, 1 - b, 0).wait()
                        scatter(c - 1, 1 - b, 1).wait()

                    load(c + 1, 1 - b).start()

                scatter(c, b, 0).start()
                scatter(c, b, 1).start()

        for c, b in ((n_chunk - 2, 0), (n_chunk - 1, 1)):
            scatter(c, b, 0).wait()
            scatter(c, b, 1).wait()

    return pl.kernel(
        body,
        out_type=jax.ShapeDtypeStruct((n_out, d), rows.dtype),
        mesh=mesh,
        scratch_types=[pltpu.VMEM((2, n_chunk, SC_ROWS), jnp.int32),
                       pltpu.VMEM((2, SC_ROWS, d), rows.dtype),
                       pltpu.SemaphoreType.DMA((2,)),
                       pltpu.SemaphoreType.DMA((2,))],
        name="sc_scatter_rows",
    )(rows, dest3)


def _combine_dense_kernel(final, gate_ref, x1_ref, gf_ref, y0_ref, y1_ref, out_ref):
    gate = gate_ref[...]
    out = x1_ref[...] + gate[:, 0:1] * y0_ref[...] + gate[:, 1:2] * y1_ref[...]
    if final:
        out = _rms(out, gf_ref[...])
    out_ref[...] = out


def _combine_dense(gate_t, x1, g_final, yg, final):
    n, d = x1.shape
    nb = n // MERGE_T
    return pl.pallas_call(
        functools.partial(_combine_dense_kernel, final),
        grid=(nb,),
        in_specs=[pl.BlockSpec((MERGE_T, 2), lambda i: (i, 0)),
                  pl.BlockSpec((MERGE_T, d), lambda i: (i, 0)),
                  pl.BlockSpec((1, d), lambda i: (0, 0)),
                  pl.BlockSpec((MERGE_T, d), lambda i: (i, 0)),
                  pl.BlockSpec((MERGE_T, d), lambda i: (i + nb, 0))],
        out_specs=pl.BlockSpec((MERGE_T, d), lambda i: (i, 0)),
        out_shape=jax.ShapeDtypeStruct((n, d), F32),
        compiler_params=_cparams(1),
        name="combine_dense",
    )(gate_t, x1, g_final, yg, yg)


def _projection_weights(w):
    d = w.shape[0]
    blk = N_HEADS * HEAD_DIM
    kv = SWA_KV_HEADS * HEAD_DIM
    pa, pb, pc, pd = w[:, :3 * blk], w[:, 3 * blk:6 * blk], w[:, 6 * blk:9 * blk], w[:, 9 * blk:]
    half = N_HEADS * DIFF_QK

    def per_head(a, b):
        return jnp.stack([a.reshape(d, N_HEADS, DIFF_QK), b.reshape(d, N_HEADS, DIFF_QK)], axis=2).reshape(d, blk)

    s64, s32 = HEAD_DIM ** -0.5, DIFF_QK ** -0.5
    wn = jnp.concatenate([pa[:, blk:2 * blk], per_head(pb[:, 2 * half:3 * half], pb[:, 3 * half:4 * half]),
                          pc[:, blk:2 * blk], pd[:, blk:blk + kv]], axis=1)
    wt = jnp.concatenate([pa[:, :blk] * (s64 * LOG2E), pa[:, 2 * blk:],
                          per_head(pb[:, :half], pb[:, half:2 * half]) * (s32 * LOG2E), pb[:, 4 * half:],
                          pc[:, :blk] * s64, pc[:, 2 * blk:],
                          pd[:, :blk] * s64, pd[:, blk + kv:]], axis=1)
    assert wn.shape[1] == KN_COLS and wt.shape[1] == QV_ROWS
    return wn.astype(BF16), wt.T.astype(BF16)


def _router_weights(w_rg, b_rg, w_re, b_re):
    d = w_rg.shape[0]
    w = jnp.concatenate([w_rg.T, jnp.zeros((8 - N_GROUPS, d), F32), w_re.T], axis=0)
    b = jnp.concatenate([b_rg.astype(F32), jnp.full((8 - N_GROUPS,), NEG, F32), b_re.astype(F32)])[:, None]
    w_hi = w.astype(BF16)
    w_lo = (w - w_hi.astype(F32)).astype(BF16)
    return w_hi, w_lo, b


def _moe_plan(eid, rank, counts, n_rows_total):
    padded = (counts + MOE_ROWS - 1) // MOE_ROWS * MOE_ROWS
    pad_end = jnp.cumsum(padded)
    pad_start = pad_end - padded
    experts = jnp.arange(N_EXPERTS, dtype=jnp.int32)
    start_of = jnp.sum(jnp.where(eid[..., None] == experts, pad_start, 0), axis=-1)
    dest = start_of + rank
    n_blk = n_rows_total // MOE_ROWS
    n_used = (pad_end[-1] // MOE_ROWS).astype(jnp.int32)
    blk = jnp.minimum(jnp.arange(n_blk, dtype=jnp.int32), n_used - 1) * MOE_ROWS
    blk_expert = jnp.minimum(jnp.sum(pad_end[None, :] <= blk[:, None], axis=1), N_EXPERTS - 1).astype(jnp.int32)
    live_end = jnp.sum(jnp.where(blk_expert[:, None] == experts, pad_start + counts, 0), axis=-1)
    n_valid = jnp.clip(live_end - blk, 0, MOE_ROWS).astype(jnp.int32)
    return dest.astype(jnp.int32), blk_expert, n_used.reshape(1), n_valid


def kernel(x, rel_bias, g_mix, w_in, diff_lq1, diff_lk1, diff_lq2, diff_lk2, diff_subln, swa_sinks,
           w_gate, w_br, w_o, g_ffn, w_route_group, b_route_group, w_route_expert, b_route_expert,
           w1, w3, w2, g_final):
    batch, seq, d = x.shape
    n = batch * seq
    depth = w_in.shape[0]
    assert seq % TB == 0 and n % MERGE_T == 0 and TB == MOBA_BLOCK
    tab = rel_bias.T.astype(F32)
    tiles_moba = _causal_bias_tiles(tab[:N_HEADS])
    tiles_diff = _causal_bias_tiles(tab[N_HEADS:2 * N_HEADS])
    tile_cur, tile_prev = _swa_bias_tiles(tab[2 * N_HEADS:])
    n_rows_total = n * 2 + N_EXPERTS * MOE_ROWS
    row = lambda v: v.astype(F32)[None, :]

    x2 = x.reshape(n, d)
    for l in range(depth):
        lam_init = 0.8 - 0.6 * math.exp(-0.3 * l)
        wn, wt = _projection_weights(w_in[l])
        kn, qvt = _inproj(x2, row(g_mix[l]), wn, wt)
        o_a = _moba_attention(kn, qvt, tiles_moba, batch)
        o_b = _diff_attention(kn, qvt, tiles_diff, row(diff_lq1[l]), row(diff_lk1[l]), row(diff_lq2[l]),
                              row(diff_lk2[l]), diff_subln[l].astype(F32)[:, None], lam_init, batch)
        o_c = _sb_attention(kn, qvt, batch)
        o_d = _swa_attention(kn, qvt, tile_cur, tile_prev, swa_sinks[l].astype(F32), batch)
        x1 = _merge(x2, row(g_mix[l]), o_a, o_b, o_c, o_d, w_gate[l].astype(BF16), w_br[l].astype(BF16),
                    w_o[l].astype(BF16))
        w_hi, w_lo, r_bias = _router_weights(w_route_group[l], b_route_group[l], w_route_expert[l],
                                             b_route_expert[l])
        h2, eid, gate, rank, cnt = _router(x1, row(g_ffn[l]), w_hi, w_lo, r_bias)
        dest, blk_expert, n_used, n_valid = _moe_plan(eid, rank, cnt[:, 0].astype(jnp.int32), n_rows_total)
        xbuf = _sc_scatter_rows(h2, dest, n_rows_total)
        y = _experts(blk_expert, n_used, n_valid, xbuf, w1[l].astype(BF16), w3[l].astype(BF16),
                     w2[l].astype(BF16))
        yg = _sc_gather_rows(y, dest.reshape(-1))
        x2 = _combine_dense(gate.T, x1, row(g_final), yg, l == depth - 1)
    return x2.reshape(batch, seq, d)
```

```python
import functools
import math

import numpy as np
import jax
import jax.numpy as jnp
from jax import lax
from jax.experimental import pallas as pl
from jax.experimental.pallas import tpu as pltpu
from jax.experimental.pallas import tpu_sc as plsc

F32 = jnp.float32
BF16 = jnp.bfloat16

HEAD_DIM = 64
N_HEADS = 4
DIFF_QK = 32
SWA_KV_HEADS = 2
SWA_WINDOW = 128
MOBA_BLOCK = 256
MOBA_TOPK = 3
REL_BUCKETS = 32
REL_MAX_DIST = 128
N_GROUPS = 4
EXPERTS_PER_GROUP = 8
N_EXPERTS = N_GROUPS * EXPERTS_PER_GROUP
NORM_EPS = 1e-6

TB = 256
LANES = 128
ONES_ROWS = 16
NEG = -1e30
LOG2E = math.log2(math.e)
SB_EXIT = 104.0
MOE_ROWS = 512
VMEM_LIMIT = 56 * 1024 * 1024

QV_MOBA_Q, QV_MOBA_V, QV_DIFF_Q, QV_DIFF_V, QV_SB_Q, QV_SB_V, QV_SWA_Q = range(7)
QV_SWA_V_128 = 14
QV_ROWS = 7 * 256 + 128
KN_MOBA, KN_DIFF, KN_SB = range(3)
KN_SWA_128 = 6
KN_COLS = 3 * 256 + 128


def _cparams(n_grid):
    return pltpu.CompilerParams(dimension_semantics=("arbitrary",) * n_grid,
                                vmem_limit_bytes=VMEM_LIMIT)


def _rel_bucket_np(n):
    n = np.maximum(n, 0)
    max_exact = REL_BUCKETS // 2
    nf = np.maximum(n, 1).astype(np.float64)
    large = max_exact + (np.log(nf / max_exact) / math.log(REL_MAX_DIST / max_exact)
                         * (REL_BUCKETS - max_exact)).astype(np.int64)
    large = np.minimum(large, REL_BUCKETS - 1)
    return np.where(n < max_exact, n, large)


def _first_far_distance():
    d = np.arange(0, 4 * REL_MAX_DIST)
    b = _rel_bucket_np(d)
    return int(np.min(d[b == REL_BUCKETS - 1]))


def _toeplitz_bias(tab, rows, cols, base, valid_fn, shift_far, unit):
    length = rows + cols - 1
    off = np.concatenate([np.arange(0, cols), np.arange(cols - length, 0)])
    n = base + off
    onehot = np.zeros((REL_BUCKETS, length), np.float32)
    onehot[_rel_bucket_np(n), np.arange(length)] = 1.0
    vec = jnp.dot(tab, jnp.asarray(onehot), precision=lax.Precision.HIGHEST)
    if shift_far:
        vec = vec - tab[:, REL_BUCKETS - 1:]
    vec = jnp.where(jnp.asarray(valid_fn(n))[None, :], vec * unit, NEG).astype(F32)
    flat = jnp.tile(vec, (1, rows))[:, :rows * (length - 1)]
    return flat.reshape(tab.shape[0], rows, length - 1)[:, :, :cols]


def _causal_bias_tiles(tab):
    assert _first_far_distance() <= TB + 1
    tiles = [_toeplitz_bias(tab, TB, TB, d * TB, lambda n: n >= 0, True, LOG2E) for d in range(2)]
    return jnp.stack(tiles, axis=1)


def _swa_bias_tiles(tab):
    in_window = lambda n: (n >= 0) & (n < SWA_WINDOW)
    return (_toeplitz_bias(tab, TB, TB, 0, in_window, False, 1.0),
            _toeplitz_bias(tab, SWA_WINDOW, TB, SWA_WINDOW, in_window, False, 1.0))


def _pad_rows(q, off, total):
    n, t = q.shape
    parts = []
    if off:
        parts.append(jnp.zeros((off, t), q.dtype))
    parts.append(q)
    if total - off - n:
        parts.append(jnp.zeros((total - off - n, t), q.dtype))
    return jnp.concatenate(parts, axis=0) if len(parts) > 1 else q


def _with_ones(v):
    return jnp.concatenate([v, jnp.ones((ONES_ROWS, v.shape[1]), v.dtype)], axis=0)


def _pack_bf16_pairs(x):
    w = x.shape[1] // 2
    lo = lax.bitcast_convert_type(x[:, :w].astype(BF16).astype(F32), jnp.uint32)
    hi = lax.bitcast_convert_type(x[:, w:].astype(BF16).astype(F32), jnp.uint32)
    return hi | (lo >> 16)


def _unpack_bf16_pairs(u):
    lo = lax.bitcast_convert_type(u << 16, F32)
    hi = lax.bitcast_convert_type(u & jnp.uint32(0xFFFF0000), F32)
    return jnp.concatenate([lo, hi], axis=1)


def _rms(x, g_row):
    ms = jnp.mean(x * x, axis=-1, keepdims=True)
    return x * lax.rsqrt(ms + NORM_EPS) * g_row


IN_T = 512
IN_CHUNK = 384


def _inproj_kernel(x_ref, g_ref, wn_ref, wt_ref, kn_ref, qvt_ref):
    h = _rms(x_ref[...], g_ref[...]).astype(BF16)
    kn_ref[...] = jnp.dot(h, wn_ref[...], preferred_element_type=F32).astype(BF16)
    for r0 in range(0, QV_ROWS, IN_CHUNK):
        pt = lax.dot_general(wt_ref[r0:r0 + IN_CHUNK, :], h, (((1,), (1,)), ((), ())),
                             preferred_element_type=F32)
        for s in range(IN_T // TB):
            qvt_ref[s, r0:r0 + IN_CHUNK, :] = pt[:, s * TB:(s + 1) * TB].astype(BF16)


def _inproj(x2, g, wn, wt):
    n, d = x2.shape
    return pl.pallas_call(
        _inproj_kernel,
        grid=(n // IN_T,),
        in_specs=[pl.BlockSpec((IN_T, d), lambda i: (i, 0)),
                  pl.BlockSpec((1, d), lambda i: (0, 0)),
                  pl.BlockSpec((d, KN_COLS), lambda i: (0, 0)),
                  pl.BlockSpec((QV_ROWS, d), lambda i: (0, 0))],
        out_specs=[pl.BlockSpec((IN_T, KN_COLS), lambda i: (i, 0)),
                   pl.BlockSpec((IN_T // TB, QV_ROWS, TB), lambda i: (i, 0, 0))],
        out_shape=[jax.ShapeDtypeStruct((n, KN_COLS), BF16),
                   jax.ShapeDtypeStruct((n // TB, QV_ROWS, TB), BF16)],
        compiler_params=_cparams(1),
        name="inproj",
    )(x2, g, wn, wt)


FLASH_OVERFLOW = 100.0
FLASH_SKEW = 4


class _FlashScratch:
    def __init__(self, n_chain, qpad, m, acc, over, s):
        self.n_chain, self.qpad, self.m, self.acc, self.over, self.s = n_chain, qpad, m, acc, over, s

    @staticmethod
    def shapes(n_chain):
        return [pltpu.VMEM((n_chain, LANES, TB), BF16),
                pltpu.VMEM((n_chain, 1, TB), F32),
                pltpu.VMEM((n_chain, HEAD_DIM + ONES_ROWS, TB), F32),
                pltpu.VMEM((n_chain, 1, TB), F32),
                pltpu.VMEM((n_chain, TB, TB), F32)]


def _flash_two_pass(sc, chains):
    if len(chains) > sc.n_chain:
        for k in range(0, len(chains), sc.n_chain):
            _flash_two_pass(sc, chains[k:k + sc.n_chain])
        return
    block_max = []
    for c, kblk, add, _, colmask in chains:
        s = jnp.dot(kblk, sc.qpad[c], preferred_element_type=F32)
        if add is not None:
            s = s + add
        if colmask is not None:
            s = s + colmask
        sc.s[c] = s
        block_max.append(jnp.max(s, axis=0, keepdims=True))
    for (c, _, _, vext, _), mx in zip(chains, block_max):
        m_old = sc.m[c]
        m_new = jnp.maximum(m_old, mx)
        p = jnp.exp2(sc.s[c] - m_new).astype(BF16)
        sc.acc[c] = sc.acc[c] * jnp.exp2(m_old - m_new) + jnp.dot(vext, p, preferred_element_type=F32)
        sc.m[c] = m_new


def _flash_lagged(sc, chains):
    def finish(c, p, mx, ref, vext):
        pv = jnp.dot(vext, p, preferred_element_type=F32)
        ref_new = jnp.maximum(ref, mx)
        sc.acc[c] = (sc.acc[c] + pv) * jnp.exp2(ref - ref_new)
        sc.m[c] = ref_new
        sc.over[c] = jnp.maximum(sc.over[c], mx - ref)

    pending = []
    for c, kblk, add, vext, colmask in chains:
        s = jnp.dot(kblk, sc.qpad[c], preferred_element_type=F32)
        if add is not None:
            s = s + add
        if len(pending) >= min(FLASH_SKEW, sc.n_chain):
            finish(*pending.pop(0))
        ref = sc.m[c]
        mx = jnp.max(s, axis=0, keepdims=True)
        if colmask is None:
            p = jnp.exp2(s - ref)
        else:
            p = jnp.exp2(s - (ref - colmask))
            mx = mx + colmask
        pending.append((c, p.astype(BF16), mx, ref, vext))
    for item in pending:
        finish(*item)


def _flash_causal(sc, i, block_chains):
    def init():
        for c in range(sc.n_chain):
            sc.m[c] = jnp.full(sc.m.shape[1:], NEG, F32)
            sc.acc[c] = jnp.zeros(sc.acc.shape[1:], F32)
            sc.over[c] = jnp.full(sc.over.shape[1:], NEG, F32)

    def run(step):
        init()
        _flash_two_pass(sc, block_chains(i, 0))

        @pl.when(i >= 1)
        def _():
            step(sc, block_chains(i - 1, 1))

        n_far = jnp.maximum(i - 1, 0)

        def far_pair(t, carry):
            step(sc, block_chains(2 * t, None) + block_chains(2 * t + 1, None))
            return carry

        lax.fori_loop(0, n_far // 2, far_pair, 0)

        @pl.when(lax.rem(n_far, 2) == 1)
        def _():
            step(sc, block_chains(n_far - 1, None))

    run(_flash_lagged)
    worst = sc.over[0]
    for c in range(1, sc.n_chain):
        worst = jnp.maximum(worst, sc.over[c])

    @pl.when(jnp.max(worst) > FLASH_OVERFLOW)
    def _():
        run(_flash_two_pass)


def _diff_kernel(lam_init, qt_ref, k_ref, vt_ref, tile_ref, lq1_ref, lk1_ref, lq2_ref, lk2_ref,
                 subln_ref, o_ref, *scratch):
    i = pl.program_id(1)
    sc = _FlashScratch(2 * N_HEADS, *scratch)
    acc_scr = sc.acc
    for h in range(N_HEADS):
        for mp in range(2):
            r0 = h * HEAD_DIM + mp * DIFF_QK
            sc.qpad[2 * h + mp] = _pad_rows(qt_ref[0, r0:r0 + DIFF_QK, :],
                                            (h % 2) * HEAD_DIM + mp * DIFF_QK, LANES)

    def block_chains(j, tile_idx):
        row = pl.multiple_of(j * TB, TB)
        chains = []
        for h in range(N_HEADS):
            g = h // 2
            kblk = k_ref[pl.ds(row, TB), g * LANES:(g + 1) * LANES]
            add = None if tile_idx is None else tile_ref[h, tile_idx]
            vext = _with_ones(vt_ref[j, h * HEAD_DIM:(h + 1) * HEAD_DIM, :])
            chains += [(2 * h + mp, kblk, add, vext, None) for mp in range(2)]
        return chains

    _flash_causal(sc, i, block_chains)

    lam =(jnp.exp(jnp.sum(lq1_ref[...] * lk1_ref[...], keepdims=True))
           - jnp.exp(jnp.sum(lq2_ref[...] * lk2_ref[...], keepdims=True)) + lam_init)
    for h in range(N_HEADS):
        a1 = acc_scr[2 * h]
        a2 = acc_scr[2 * h + 1]
        o1 = a1[:HEAD_DIM] / a1[HEAD_DIM:HEAD_DIM + 1]
        o2 = a2[:HEAD_DIM] / a2[HEAD_DIM:HEAD_DIM + 1]
        a = o1 - lam * o2
        ms = jnp.mean(a * a, axis=0, keepdims=True)
        y = a * lax.rsqrt(ms + NORM_EPS) * subln_ref[...] * (1.0 - lam_init)
        o_ref[0, h * HEAD_DIM:(h + 1) * HEAD_DIM, :] = y.astype(BF16)


def _attn_common_specs(nb, q_blk, k_blk, v_blk):
    return [pl.BlockSpec((1, TB, TB), lambda b, i: (b * nb + i, q_blk, 0)),
            pl.BlockSpec((nb * TB, TB), lambda b, i: (b, k_blk)),
            pl.BlockSpec((nb, TB, TB), lambda b, i: (b, v_blk, 0))]


def _small_spec(shape):
    return pl.BlockSpec(shape, lambda b, i: (0,) * len(shape))


def _diff_attention(kn, qvt, tiles, lq1, lk1, lq2, lk2, subln, lam_init, batch):
    nblk = qvt.shape[0]
    nb = nblk // batch
    n_chain = 2 * N_HEADS
    return pl.pallas_call(
        functools.partial(_diff_kernel, lam_init),
        grid=(batch, nb),
        in_specs=_attn_common_specs(nb, QV_DIFF_Q, KN_DIFF, QV_DIFF_V) + [
            _small_spec((N_HEADS, 2, TB, TB)),
            _small_spec((1, DIFF_QK)), _small_spec((1, DIFF_QK)),
            _small_spec((1, DIFF_QK)), _small_spec((1, DIFF_QK)),
            _small_spec((HEAD_DIM, 1))],
        out_specs=pl.BlockSpec((1, TB, TB), lambda b, i: (b * nb + i, 0, 0)),
        out_shape=jax.ShapeDtypeStruct((nblk, TB, TB), BF16),
        scratch_shapes=_FlashScratch.shapes(n_chain),
        compiler_params=_cparams(2),
        name="diff_attn",
    )(qvt, kn, qvt, tiles, lq1, lk1, lq2, lk2, subln)


def _moba_kernel(nb, nbp, qt_ref, k_ref, vt_ref, tile_ref, o_ref, kmean_scr, sel_scr, *scratch):
    i = pl.program_id(1)
    sc = _FlashScratch(N_HEADS, *scratch)
    qpad_scr, acc_scr = sc.qpad, sc.acc

    @pl.when(i == 0)
    def _():
        kmean_scr[...] = jnp.zeros(kmean_scr.shape, F32)
        for jb in range(nb):
            blk = k_ref[jb * TB:(jb + 1) * TB, :].astype(F32)
            kmean_scr[jb:jb + 1, :] = jnp.mean(blk, axis=0, keepdims=True)

    for h in range(N_HEADS):
        qpad_scr[h] = _pad_rows(qt_ref[0, h * HEAD_DIM:(h + 1) * HEAD_DIM, :], (h % 2) * HEAD_DIM, LANES)

    blk_id = lax.broadcasted_iota(jnp.int32, (nbp, TB), 0)
    for h in range(N_HEADS):
        g = h // 2
        km = kmean_scr[:, g * LANES:(g + 1) * LANES].astype(BF16)
        gate = jnp.dot(km, qpad_scr[h], preferred_element_type=F32)
        avail = blk_id < i
        sel = jnp.zeros((nbp, TB), jnp.bool_)
        for _ in range(MOBA_TOPK):
            gm = jnp.where(avail, gate, -jnp.inf)
            best = jnp.max(gm, axis=0, keepdims=True)
            is_best = avail & (gm == best)
            first = jnp.min(jnp.where(is_best, blk_id, nbp), axis=0, keepdims=True)
            pick = blk_id == first
            sel = sel | pick
            avail = avail & jnp.logical_not(pick)
        sel_scr[h] = jnp.where(sel, 0.0, NEG).astype(F32)

    def block_chains(j, tile_idx):
        row = pl.multiple_of(j * TB, TB)
        chains = []
        for h in range(N_HEADS):
            g = h // 2
            kblk = k_ref[pl.ds(row, TB), g * LANES:(g + 1) * LANES]
            add = None if tile_idx is None else tile_ref[h, tile_idx]
            colmask = None if tile_idx == 0 else sel_scr[h, pl.ds(j, 1), :]
            chains.append((h, kblk, add, _with_ones(vt_ref[j, h * HEAD_DIM:(h + 1) * HEAD_DIM, :]), colmask))
        return chains

    _flash_causal(sc, i, block_chains)

    for h in range(N_HEADS):
        a = acc_scr[h]
        o_ref[0, h * HEAD_DIM:(h + 1) * HEAD_DIM, :] = (a[:HEAD_DIM] / a[HEAD_DIM:HEAD_DIM + 1]).astype(BF16)


def _moba_attention(kn, qvt, tiles, batch):
    nblk = qvt.shape[0]
    nb = nblk // batch
    nbp = max(8, -(-nb // 8) * 8)
    return pl.pallas_call(
        functools.partial(_moba_kernel, nb, nbp),
        grid=(batch, nb),
        in_specs=_attn_common_specs(nb, QV_MOBA_Q, KN_MOBA, QV_MOBA_V) + [
            _small_spec((N_HEADS, 2, TB, TB))],
        out_specs=pl.BlockSpec((1, TB, TB), lambda b, i: (b * nb + i, 0, 0)),
        out_shape=jax.ShapeDtypeStruct((nblk, TB, TB), BF16),
        scratch_shapes=[pltpu.VMEM((nbp, TB), F32),
                        pltpu.VMEM((N_HEADS, nbp, TB), F32)] + _FlashScratch.shapes(N_HEADS),
        compiler_params=_cparams(2),
        name="moba_attn",
    )(qvt, kn, qvt, tiles)


def _sb_kernel(qt_ref, k_ref, vt_ref, o_ref, qpad_scr, acc_scr, c_scr):
    i = pl.program_id(1)
    rows = lax.broadcasted_iota(jnp.int32, (TB, TB), 0)
    cols = lax.broadcasted_iota(jnp.int32, (TB, TB), 1)
    upper = (cols > rows).astype(BF16)
    strict = cols > rows

    heads = range(N_HEADS)

    def step(j, first):
        row = pl.multiple_of(j * TB, TB)
        zs = [jnp.dot(k_ref[pl.ds(row, TB), (h // 2) * LANES:(h // 2 + 1) * LANES], qpad_scr[h],
                      preferred_element_type=F32) for h in heads]
        log_keeps = []
        for z in zs:
            lk = -(jnp.maximum(z, 0.0) + jnp.log(1.0 + jnp.exp(-jnp.abs(z))))
            log_keeps.append(jnp.where(strict, lk, 0.0) if first else lk)
        laters = []
        for h in heads:
            hi = log_keeps[h].astype(BF16)
            lo = (log_keeps[h] - hi.astype(F32)).astype(BF16)
            later = (jnp.dot(upper, hi, preferred_element_type=F32)
                     + jnp.dot(upper, lo, preferred_element_type=F32))
            laters.append(later if first else later + c_scr[h])
        c_max = None
        for h in heads:
            w = jnp.exp(zs[h] + log_keeps[h] + laters[h])
            if first:
                w = jnp.where(strict, w, 0.0)
            pv = jnp.dot(vt_ref[j, h * HEAD_DIM:(h + 1) * HEAD_DIM, :], w.astype(BF16),
                         preferred_element_type=F32)
            acc_scr[h] = pv if first else acc_scr[h] + pv
            c_new = laters[h][0:1, :] + log_keeps[h][0:1, :]
            c_scr[h] = c_new
            c_max = c_new if c_max is None else jnp.maximum(c_max, c_new)
        return jnp.max(c_max) > -SB_EXIT

    for h in heads:
        qpad_scr[h] = _pad_rows(qt_ref[0, h * HEAD_DIM:(h + 1) * HEAD_DIM, :], (h % 2) * HEAD_DIM, LANES)

    def cond(carry):
        j, go = carry
        return jnp.logical_and(j >= 0, go)

    def body(carry):
        j, _ = carry
        return j - 1, step(j, False)

    lax.while_loop(cond, body, (i - 1, step(i, True)))
    for h in heads:
        o_ref[0, h * HEAD_DIM:(h + 1) * HEAD_DIM, :] = acc_scr[h].astype(BF16)


def _sb_attention(kn, qvt, batch):
    nblk = qvt.shape[0]
    nb = nblk // batch
    return pl.pallas_call(
        _sb_kernel,
        grid=(batch, nb),
        in_specs=_attn_common_specs(nb, QV_SB_Q, KN_SB, QV_SB_V),
        out_specs=pl.BlockSpec((1, TB, TB), lambda b, i: (b * nb + i, 0, 0)),
        out_shape=jax.ShapeDtypeStruct((nblk, TB, TB), BF16),
        scratch_shapes=[pltpu.VMEM((N_HEADS, LANES, TB), BF16),
                        pltpu.VMEM((N_HEADS, HEAD_DIM, TB), F32),
                        pltpu.VMEM((N_HEADS, 1, TB), F32)],
        compiler_params=_cparams(2),
        name="sb_attn",
    )(qvt, kn, qvt)


def _swa_kernel(sink_ref, qt_ref, kc_ref, kp_ref, vc_ref, vp_ref, tc_ref, tp_ref, o_ref):
    i = pl.program_id(1)
    kc = kc_ref[...]
    kp = kp_ref[...]
    vc = vc_ref[0]
    vp = vp_ref[0][:, TB - SWA_WINDOW:]
    no_prev = jnp.where(i > 0, 0.0, NEG)
    group = N_HEADS // SWA_KV_HEADS
    for h in range(N_HEADS):
        kv = h // group
        qpad = _pad_rows(qt_ref[0, h * HEAD_DIM:(h + 1) * HEAD_DIM, :], kv * HEAD_DIM, LANES)
        s_c = jnp.dot(kc, qpad, preferred_element_type=F32) + tc_ref[h]
        s_p = jnp.dot(kp, qpad, preferred_element_type=F32) + tp_ref[h] + no_prev
        sink = sink_ref[h]
        m = jnp.maximum(jnp.maximum(jnp.max(s_c, axis=0, keepdims=True),
                                    jnp.max(s_p, axis=0, keepdims=True)), sink)
        p_c = jnp.exp(s_c - m).astype(BF16)
        p_p = jnp.exp(s_p - m).astype(BF16)
        o = (jnp.dot(_with_ones(vc[kv * HEAD_DIM:(kv + 1) * HEAD_DIM]), p_c, preferred_element_type=F32)
             + jnp.dot(_with_ones(vp[kv * HEAD_DIM:(kv + 1) * HEAD_DIM]), p_p, preferred_element_type=F32))
        denom = o[HEAD_DIM:HEAD_DIM + 1] + jnp.exp(sink - m)
        o_ref[0, h * HEAD_DIM:(h + 1) * HEAD_DIM, :] = (o[:HEAD_DIM] / denom).astype(BF16)


def _swa_attention(kn, qvt, tile_cur, tile_prev, sinks, batch):
    nblk = qvt.shape[0]
    nb = nblk // batch
    half = TB // SWA_WINDOW
    kv_rows = SWA_KV_HEADS * HEAD_DIM
    return pl.pallas_call(
        _swa_kernel,
        grid=(batch, nb),
        in_specs=[pl.BlockSpec(memory_space=pltpu.SMEM),
                  pl.BlockSpec((1, TB, TB), lambda b, i: (b * nb + i, QV_SWA_Q, 0)),
                  pl.BlockSpec((TB, kv_rows), lambda b, i: (b * nb + i, KN_SWA_128)),
                  pl.BlockSpec((SWA_WINDOW, kv_rows),
                               lambda b, i: (b * nb * half + jnp.maximum(half * i - 1, 0), KN_SWA_128)),
                  pl.BlockSpec((1, kv_rows, TB), lambda b, i: (b * nb + i, QV_SWA_V_128, 0)),
                  pl.BlockSpec((1, kv_rows, TB), lambda b, i: (b * nb + jnp.maximum(i - 1, 0), QV_SWA_V_128, 0)),
                  _small_spec((N_HEADS, TB, TB)),
                  _small_spec((N_HEADS, SWA_WINDOW, TB))],
        out_specs=pl.BlockSpec((1, TB, TB), lambda b, i: (b * nb + i, 0, 0)),
        out_shape=jax.ShapeDtypeStruct((nblk, TB, TB), BF16),
        compiler_params=_cparams(2),
        name="swa_attn",
    )(sinks, qvt, kn, kn, qvt, qvt, tile_cur, tile_prev)


MERGE_T = 512


def _merge_kernel(x_ref, g_ref, oa_ref, ob_ref, oc_ref, od_ref, wg_ref, wbr_ref, wo_ref, x1_ref):
    x = x_ref[...]
    h = _rms(x, g_ref[...]).astype(BF16)
    merged = None
    for bi, o_ref in enumerate((oa_ref, ob_ref, oc_ref, od_ref)):
        gate = jax.nn.sigmoid(jnp.dot(h, wg_ref[bi], preferred_element_type=F32))
        branch = jnp.concatenate(
            [lax.dot_general(o_ref[s], wbr_ref[bi], (((0,), (0,)), ((), ())), preferred_element_type=F32)
             for s in range(MERGE_T // TB)], axis=0)
        term = gate * branch
        merged = term if merged is None else merged + term
    x1_ref[...] = x + jnp.dot(merged.astype(BF16), wo_ref[...], preferred_element_type=F32)


def _merge(x2, g, o_a, o_b, o_c, o_d, wg, wbr, wo):
    n, d = x2.shape
    o_spec = pl.BlockSpec((MERGE_T // TB, TB, TB), lambda i: (i, 0, 0))
    return pl.pallas_call(
        _merge_kernel,
        grid=(n // MERGE_T,),
        in_specs=[pl.BlockSpec((MERGE_T, d), lambda i: (i, 0)),
                  pl.BlockSpec((1, d), lambda i: (0, 0)),
                  o_spec, o_spec, o_spec, o_spec,
                  pl.BlockSpec(wg.shape, lambda i: (0, 0, 0)),
                  pl.BlockSpec(wbr.shape, lambda i: (0, 0, 0)),
                  pl.BlockSpec(wo.shape, lambda i: (0, 0))],
        out_specs=pl.BlockSpec((MERGE_T, d), lambda i: (i, 0)),
        out_shape=jax.ShapeDtypeStruct((n, d), F32),
        compiler_params=_cparams(1),
        name="merge",
    )(x2, g, o_a, o_b, o_c, o_d, wg, wbr, wo)


ROUTER_T = 512
ROUTER_ROWS = 8 + N_EXPERTS


def _first_argmax_rows(v, n_rows):
    best = jnp.max(v, axis=0, keepdims=True)
    ids = lax.broadcasted_iota(jnp.int32, v.shape, 0)
    return best, jnp.min(jnp.where(v == best, ids, n_rows), axis=0, keepdims=True)


def _router_kernel(x_ref, g_ref, whi_ref, wlo_ref, b_ref, h2_ref, eid_ref, gate_ref, rank_ref, cnt_ref,
                   base_scr):
    i = pl.program_id(0)

    @pl.when(i == 0)
    def _():
        base_scr[...] = jnp.zeros(base_scr.shape, F32)

    h2 = _rms(x_ref[...], g_ref[...])
    h2_ref[...] = _pack_bf16_pairs(h2)
    h_hi = h2.astype(BF16)
    h_lo = (h2 - h_hi.astype(F32)).astype(BF16)
    nt = (((1,), (1,)), ((), ()))
    logits = (lax.dot_general(whi_ref[...], h_hi, nt, preferred_element_type=F32)
              + lax.dot_general(whi_ref[...], h_lo, nt, preferred_element_type=F32)
              + lax.dot_general(wlo_ref[...], h_hi, nt, preferred_element_type=F32)
              + b_ref[...])
    gl = logits[0:8]
    gmax, grp = _first_argmax_rows(gl, 8)
    p_grp = 1.0 / jnp.sum(jnp.exp(gl - gmax), axis=0, keepdims=True)
    e_sel = jnp.zeros((EXPERTS_PER_GROUP, ROUTER_T), F32)
    for g in range(N_GROUPS):
        e_sel = jnp.where(grp == g, logits[8 + 8 * g:16 + 8 * g], e_sel)
    ids8 = lax.broadcasted_iota(jnp.int32, e_sel.shape, 0)
    v1, i1 = _first_argmax_rows(e_sel, EXPERTS_PER_GROUP)
    e_rest = jnp.where(ids8 == i1, -jnp.inf, e_sel)
    v2, i2 = _first_argmax_rows(e_rest, EXPERTS_PER_GROUP)
    r = jnp.exp(v2 - v1)
    s1 = 1.0 / (1.0 + r)
    gate_ref[0:1, :] = p_grp * s1
    gate_ref[1:2, :] = p_grp * (r * s1)
    e1 = grp * EXPERTS_PER_GROUP + i1
    e2 = grp * EXPERTS_PER_GROUP + i2
    eid_ref[0:1, :] = e1
    eid_ref[1:2, :] = e2

    ids_e = lax.broadcasted_iota(jnp.int32, (N_EXPERTS, ROUTER_T), 0)
    oh1 = ids_e == e1
    oh2 = ids_e == e2
    cnt = oh1.astype(F32) + oh2.astype(F32)
    tr = lax.broadcasted_iota(jnp.int32, (ROUTER_T, ROUTER_T), 0)
    tc = lax.broadcasted_iota(jnp.int32, (ROUTER_T, ROUTER_T), 1)
    before = (tr < tc).astype(BF16)
    prefix = jnp.dot(cnt.astype(BF16), before, preferred_element_type=F32) + base_scr[:, 0:1]
    rank_ref[0:1, :] = jnp.sum(jnp.where(oh1, prefix, 0.0), axis=0, keepdims=True).astype(jnp.int32)
    rank_ref[1:2, :] = jnp.sum(jnp.where(oh2, prefix, 0.0), axis=0, keepdims=True).astype(jnp.int32)
    base_scr[...] = base_scr[...] + jnp.sum(cnt, axis=1, keepdims=True)
    cnt_ref[...] = base_scr[...]


def _router(x1, g, w_hi, w_lo, bias):
    n, d = x1.shape
    row2 = lambda dt: jax.ShapeDtypeStruct((2, n), dt)
    spec2 = pl.BlockSpec((2, ROUTER_T), lambda i: (0, i))
    return pl.pallas_call(
        _router_kernel,
        grid=(n // ROUTER_T,),
        in_specs=[pl.BlockSpec((ROUTER_T, d), lambda i: (i, 0)),
                  pl.BlockSpec((1, d), lambda i: (0, 0)),
                  pl.BlockSpec((ROUTER_ROWS, d), lambda i: (0, 0)),
                  pl.BlockSpec((ROUTER_ROWS, d), lambda i: (0, 0)),
                  pl.BlockSpec((ROUTER_ROWS, 1), lambda i: (0, 0))],
        out_specs=[pl.BlockSpec((ROUTER_T, d // 2), lambda i: (i, 0)), spec2, spec2, spec2,
                   pl.BlockSpec((N_EXPERTS, LANES), lambda i: (0, 0))],
        out_shape=[jax.ShapeDtypeStruct((n, d // 2), jnp.uint32), row2(jnp.int32), row2(F32), row2(jnp.int32),
                   jax.ShapeDtypeStruct((N_EXPERTS, LANES), F32)],
        scratch_shapes=[pltpu.VMEM((N_EXPERTS, LANES), F32)],
        compiler_params=_cparams(1),
        name="router",
    )(x1, g, w_hi, w_lo, bias)


ROW_T = 256


def _dispatch_kernel(dest_ref, h2_ref, xin_ref, xbuf_ref, sem):
    del xin_ref

    def row_copy(r, k):
        return pltpu.make_async_copy(h2_ref.at[pl.ds(r, 1)], xbuf_ref.at[pl.ds(dest_ref[k, r], 1)], sem)

    def issue(r, carry):
        row_copy(r, 0).start()
        row_copy(r, 1).start()
        return carry

    def drain(r, carry):
        row_copy(r, 0).wait()
        row_copy(r, 1).wait()
        return carry

    lax.fori_loop(0, ROW_T, issue, 0, unroll=8)
    lax.fori_loop(0, ROW_T, drain, 0, unroll=8)


def _dispatch(dest3, h2, xbuf_init):
    n, d = h2.shape
    return pl.pallas_call(
        _dispatch_kernel,
        grid=(n // ROW_T,),
        in_specs=[pl.BlockSpec((None, 2, ROW_T), lambda i: (i, 0, 0), memory_space=pltpu.SMEM),
                  pl.BlockSpec((ROW_T, d), lambda i: (i, 0)),
                  pl.BlockSpec(memory_space=pl.ANY)],
        out_specs=pl.BlockSpec(memory_space=pl.ANY),
        out_shape=jax.ShapeDtypeStruct(xbuf_init.shape, xbuf_init.dtype),
        scratch_shapes=[pltpu.SemaphoreType.DMA(())],
        input_output_aliases={2: 0},
        compiler_params=_cparams(1),
        name="dispatch",
    )(dest3, h2, xbuf_init)


def _expert_kernel(be_ref, nu_ref, nv_ref, x_ref, w1_ref, w3_ref, w2_ref, y_ref):
    del be_ref
    used = pl.program_id(0) < nu_ref[0]

    @pl.when(used)
    def _():
        live = lax.broadcasted_iota(jnp.int32, x_ref.shape, 0) < nv_ref[pl.program_id(0)]
        xb = _unpack_bf16_pairs(jnp.where(live, x_ref[...], jnp.uint32(0))).astype(BF16)
        a = jnp.dot(xb, w1_ref[...].astype(BF16), preferred_element_type=F32)
        b = jnp.dot(xb, w3_ref[...].astype(BF16), preferred_element_type=F32)
        mid = (a * jax.nn.sigmoid(a) * b).astype(BF16)
        y_ref[...] = _pack_bf16_pairs(jnp.dot(mid, w2_ref[...].astype(BF16), preferred_element_type=F32))

    @pl.when(jnp.logical_not(used))
    def _():
        y_ref[...] = jnp.zeros(y_ref.shape, jnp.uint32)


def _experts(blk_expert, n_used, n_valid, xbuf, w1, w3, w2, layer):
    r = xbuf.shape[0]
    d = w1.shape[-2]
    de = w1.shape[-1]
    assert xbuf.shape[1] * 2 == d
    row_map = lambda i, be, nu, nv: (jnp.minimum(i, nu[0] - 1), 0)
    grid_spec = pltpu.PrefetchScalarGridSpec(
        num_scalar_prefetch=3,
        grid=(r // MOE_ROWS,),
        in_specs=[pl.BlockSpec((MOE_ROWS, d // 2), row_map),
                  pl.BlockSpec((None, None, d, de), lambda i, be, nu, nv: (layer, be[i], 0, 0)),
                  pl.BlockSpec((None, None, d, de), lambda i, be, nu, nv: (layer, be[i], 0, 0)),
                  pl.BlockSpec((None, None, de, d), lambda i, be, nu, nv: (layer, be[i], 0, 0))],
        out_specs=pl.BlockSpec((MOE_ROWS, d // 2), lambda i, be, nu, nv: (i, 0)),
    )
    return pl.pallas_call(
        _expert_kernel,
        grid_spec=grid_spec,
        out_shape=jax.ShapeDtypeStruct((r, d // 2), jnp.uint32),
        compiler_params=_cparams(1),
        name="experts",
    )(blk_expert, n_used, n_valid, xbuf, w1, w3, w2)


def _combine_kernel(final, dest_ref, gate_ref, x1_ref, gf_ref, y_ref, out_ref, buf, sem):
    def row_copy(r, k):
        return pltpu.make_async_copy(y_ref.at[pl.ds(dest_ref[k, r], 1)], buf.at[k, pl.ds(r, 1)], sem)

    def issue(r, carry):
        row_copy(r, 0).start()
        row_copy(r, 1).start()
        return carry

    def drain(r, carry):
        row_copy(r, 0).wait()
        row_copy(r, 1).wait()
        return carry

    lax.fori_loop(0, ROW_T, issue, 0, unroll=8)
    lax.fori_loop(0, ROW_T, drain, 0, unroll=8)
    gate = gate_ref[...]
    out = x1_ref[...] + gate[:, 0:1] * buf[0] + gate[:, 1:2] * buf[1]
    if final:
        out = _rms(out, gf_ref[...])
    out_ref[...] = out


def _combine(dest3, gate_t, x1, g_final, y, final):
    n, d = x1.shape
    return pl.pallas_call(
        functools.partial(_combine_kernel, final),
        grid=(n // ROW_T,),
        in_specs=[pl.BlockSpec((None, 2, ROW_T), lambda i: (i, 0, 0), memory_space=pltpu.SMEM),
                  pl.BlockSpec((ROW_T, 2), lambda i: (i, 0)),
                  pl.BlockSpec((ROW_T, d), lambda i: (i, 0)),
                  pl.BlockSpec((1, d), lambda i: (0, 0)),
                  pl.BlockSpec(memory_space=pl.ANY)],
        out_specs=pl.BlockSpec((ROW_T, d), lambda i: (i, 0)),
        out_shape=jax.ShapeDtypeStruct((n, d), F32),
        scratch_shapes=[pltpu.VMEM((2, ROW_T, d), F32), pltpu.SemaphoreType.DMA(())],
        compiler_params=_cparams(1),
        name="combine",
    )(dest3, gate_t, x1, g_final, y)


SC_CORES = 2
SC_SUBCORES = 16
SC_ROWS = 64


def _sc_gather_rows(table, idx):
    n_idx = idx.shape[0]
    d = table.shape[1]
    n_workers = SC_CORES * SC_SUBCORES
    per_worker = n_idx // n_workers
    n_chunk = per_worker // SC_ROWS
    assert per_worker * n_workers == n_idx and n_chunk * SC_ROWS == per_worker and n_chunk % 2 == 0
    mesh = plsc.VectorSubcoreMesh(core_axis_name="c", subcore_axis_name="s",
                                  num_cores=SC_CORES, num_subcores=SC_SUBCORES)

    def body(table_hbm, idx_hbm, out_hbm, idx_v, rows_v, gsem, wsem):
        worker = lax.axis_index("s") * SC_CORES + lax.axis_index("c")
        base = worker * per_worker
        pltpu.sync_copy(idx_hbm.at[pl.ds(base, per_worker)], idx_v)

        def gather(c, b):
            return pltpu.make_async_copy(table_hbm.at[idx_v.at[pl.ds(c * SC_ROWS, SC_ROWS)]],
                                         rows_v.at[b], gsem.at[b])

        def put(c, b):
            return pltpu.make_async_copy(rows_v.at[b], out_hbm.at[pl.ds(base + c * SC_ROWS, SC_ROWS)],
                                         wsem.at[b])

        gather(0, 0).start()

        @pl.loop(0, n_chunk, step=2)
        def _(c0):
            for b in range(2):
                c = c0 + b
                gather(c, b).wait()

                @pl.when(c + 1 < n_chunk)
                def _():
                    @pl.when(c >= 1)
                    def _():
                        put(c - 1, 1 - b).wait()

                    gather(c + 1, 1 - b).start()

                put(c, b).start()

        put(n_chunk - 2, 0).wait()
        put(n_chunk - 1, 1).wait()

    return pl.kernel(
        body,
        out_type=jax.ShapeDtypeStruct((n_idx, d), table.dtype),
        mesh=mesh,
        scratch_types=[pltpu.VMEM((per_worker,), jnp.int32),
                       pltpu.VMEM((2, SC_ROWS, d), table.dtype),
                       pltpu.SemaphoreType.DMA((2,)),
                       pltpu.SemaphoreType.DMA((2,))],
        name="sc_gather_rows",
    )(table, idx)


def _sc_scatter_rows(rows, dest, n_out):
    n, d = rows.shape
    n_workers = SC_CORES * SC_SUBCORES
    per_worker = n // n_workers
    n_chunk = per_worker // SC_ROWS
    assert per_worker * n_workers == n and n_chunk * SC_ROWS == per_worker and n_chunk % 2 == 0
    mesh = plsc.VectorSubcoreMesh(core_axis_name="c", subcore_axis_name="s",
                                  num_cores=SC_CORES, num_subcores=SC_SUBCORES)
    dest3 = dest.reshape(2, n // SC_ROWS, SC_ROWS)

    def body(rows_hbm, idx_hbm, out_hbm, idx_v, rows_v, lsem, ssem):
        worker = lax.axis_index("s") * SC_CORES + lax.axis_index("c")
        for k in range(2):
            pltpu.sync_copy(idx_hbm.at[k, pl.ds(worker * n_chunk, n_chunk)], idx_v.at[k])

        def load(c, b):
            return pltpu.make_async_copy(rows_hbm.at[pl.ds(worker * per_worker + c * SC_ROWS, SC_ROWS)],
                                         rows_v.at[b], lsem.at[b])

        def scatter(c, b, k):
            return pltpu.make_async_copy(rows_v.at[b], out_hbm.at[idx_v.at[k, c]], ssem.at[b])

        load(0, 0).start()

        @pl.loop(0, n_chunk, step=2)
        def _(c0):
            for b in range(2):
                c = c0 + b
                load(c, b).wait()

                @pl.when(c + 1 < n_chunk)
                def _():
                    @pl.when(c >= 1)
                    def _():
                        scatter(c - 1, 1 - b, 0).wait()
                        scatter(c - 1, 1 - b, 1).wait()

                    load(c + 1, 1 - b).start()

                scatter(c, b, 0).start()
                scatter(c, b, 1).start()

        for c, b in ((n_chunk - 2, 0), (n_chunk - 1, 1)):
            scatter(c, b, 0).wait()
            scatter(c, b, 1).wait()

    return pl.kernel(
        body,
        out_type=jax.ShapeDtypeStruct((n_out, d), rows.dtype),
        mesh=mesh,
        scratch_types=[pltpu.VMEM((2, n_chunk, SC_ROWS), jnp.int32),
                       pltpu.VMEM((2, SC_ROWS, d), rows.dtype),
                       pltpu.SemaphoreType.DMA((2,)),
                       pltpu.SemaphoreType.DMA((2,))],
        name="sc_scatter_rows",
    )(rows, dest3)


def _combine_dense_kernel(final, gate_ref, x1_ref, gf_ref, y0_ref, y1_ref, out_ref):
    gate = gate_ref[...]
    out = (x1_ref[...] + gate[:, 0:1] * _unpack_bf16_pairs(y0_ref[...])
           + gate[:, 1:2] * _unpack_bf16_pairs(y1_ref[...]))
    if final:
        out = _rms(out, gf_ref[...])
    out_ref[...] = out


def _combine_dense(gate_t, x1, g_final, yg, final):
    n, d = x1.shape
    nb = n // MERGE_T
    return pl.pallas_call(
        functools.partial(_combine_dense_kernel, final),
        grid=(nb,),
        in_specs=[pl.BlockSpec((MERGE_T, 2), lambda i: (i, 0)),
                  pl.BlockSpec((MERGE_T, d), lambda i: (i, 0)),
                  pl.BlockSpec((1, d), lambda i: (0, 0)),
                  pl.BlockSpec((MERGE_T, d // 2), lambda i: (i, 0)),
                  pl.BlockSpec((MERGE_T, d // 2), lambda i: (i + nb, 0))],
        out_specs=pl.BlockSpec((MERGE_T, d), lambda i: (i, 0)),
        out_shape=jax.ShapeDtypeStruct((n, d), F32),
        compiler_params=_cparams(1),
        name="combine_dense",
    )(gate_t, x1, g_final, yg, yg)


def _projection_weights(w):
    d = w.shape[0]
    blk = N_HEADS * HEAD_DIM
    kv = SWA_KV_HEADS * HEAD_DIM
    pa, pb, pc, pd = w[:, :3 * blk], w[:, 3 * blk:6 * blk], w[:, 6 * blk:9 * blk], w[:, 9 * blk:]
    half = N_HEADS * DIFF_QK

    def per_head(a, b):
        return jnp.stack([a.reshape(d, N_HEADS, DIFF_QK), b.reshape(d, N_HEADS, DIFF_QK)], axis=2).reshape(d, blk)

    s64, s32 = HEAD_DIM ** -0.5, DIFF_QK ** -0.5
    wn = jnp.concatenate([pa[:, blk:2 * blk], per_head(pb[:, 2 * half:3 * half], pb[:, 3 * half:4 * half]),
                          pc[:, blk:2 * blk], pd[:, blk:blk + kv]], axis=1)
    wt = jnp.concatenate([pa[:, :blk] * (s64 * LOG2E), pa[:, 2 * blk:],
                          per_head(pb[:, :half], pb[:, half:2 * half]) * (s32 * LOG2E), pb[:, 4 * half:],
                          pc[:, :blk] * s64, pc[:, 2 * blk:],
                          pd[:, :blk] * s64, pd[:, blk + kv:]], axis=1)
    assert wn.shape[1] == KN_COLS and wt.shape[1] == QV_ROWS
    return wn.astype(BF16), wt.T.astype(BF16)


def _router_weights(w_rg, b_rg, w_re, b_re):
    d = w_rg.shape[0]
    w = jnp.concatenate([w_rg.T, jnp.zeros((8 - N_GROUPS, d), F32), w_re.T], axis=0)
    b = jnp.concatenate([b_rg.astype(F32), jnp.full((8 - N_GROUPS,), NEG, F32), b_re.astype(F32)])[:, None]
    w_hi = w.astype(BF16)
    w_lo = (w - w_hi.astype(F32)).astype(BF16)
    return w_hi, w_lo, b


def _moe_plan(eid, rank, counts, n_rows_total):
    padded = (counts + MOE_ROWS - 1) // MOE_ROWS * MOE_ROWS
    pad_end = jnp.cumsum(padded)
    pad_start = pad_end - padded
    experts = jnp.arange(N_EXPERTS, dtype=jnp.int32)
    start_of = jnp.sum(jnp.where(eid[..., None] == experts, pad_start, 0), axis=-1)
    dest = start_of + rank
    n_blk = n_rows_total // MOE_ROWS
    n_used = (pad_end[-1] // MOE_ROWS).astype(jnp.int32)
    blk = jnp.minimum(jnp.arange(n_blk, dtype=jnp.int32), n_used - 1) * MOE_ROWS
    blk_expert = jnp.minimum(jnp.sum(pad_end[None, :] <= blk[:, None], axis=1), N_EXPERTS - 1).astype(jnp.int32)
    live_end = jnp.sum(jnp.where(blk_expert[:, None] == experts, pad_start + counts, 0), axis=-1)
    n_valid = jnp.clip(live_end - blk, 0, MOE_ROWS).astype(jnp.int32)
    return dest.astype(jnp.int32), blk_expert, n_used.reshape(1), n_valid


def kernel(x, rel_bias, g_mix, w_in, diff_lq1, diff_lk1, diff_lq2, diff_lk2, diff_subln, swa_sinks,
           w_gate, w_br, w_o, g_ffn, w_route_group, b_route_group, w_route_expert, b_route_expert,
           w1, w3, w2, g_final):
    batch, seq, d = x.shape
    n = batch * seq
    depth = w_in.shape[0]
    assert seq % TB == 0 and n % MERGE_T == 0 and TB == MOBA_BLOCK
    tab = rel_bias.T.astype(F32)
    tiles_moba = _causal_bias_tiles(tab[:N_HEADS])
    tiles_diff = _causal_bias_tiles(tab[N_HEADS:2 * N_HEADS])
    tile_cur, tile_prev = _swa_bias_tiles(tab[2 * N_HEADS:])
    n_rows_total = n * 2 + N_EXPERTS * MOE_ROWS
    row = lambda v: v.astype(F32)[None, :]

    x2 = x.reshape(n, d)
    for l in range(depth):
        lam_init = 0.8 - 0.6 * math.exp(-0.3 * l)
        wn, wt = _projection_weights(w_in[l])
        kn, qvt = _inproj(x2, row(g_mix[l]), wn, wt)
        o_a = _moba_attention(kn, qvt, tiles_moba, batch)
        o_b = _diff_attention(kn, qvt, tiles_diff, row(diff_lq1[l]), row(diff_lk1[l]), row(diff_lq2[l]),
                              row(diff_lk2[l]), diff_subln[l].astype(F32)[:, None], lam_init, batch)
        o_c = _sb_attention(kn, qvt, batch)
        o_d = _swa_attention(kn, qvt, tile_cur, tile_prev, swa_sinks[l].astype(F32), batch)
        x1 = _merge(x2, row(g_mix[l]), o_a, o_b, o_c, o_d, w_gate[l].astype(BF16), w_br[l].astype(BF16),
                    w_o[l].astype(BF16))
        w_hi, w_lo, r_bias = _router_weights(w_route_group[l], b_route_group[l], w_route_expert[l],
                                             b_route_expert[l])
        h2, eid, gate, rank, cnt = _router(x1, row(g_ffn[l]), w_hi, w_lo, r_bias)
        dest, blk_expert, n_used, n_valid = _moe_plan(eid, rank, cnt[:, 0].astype(jnp.int32), n_rows_total)
        xbuf = _sc_scatter_rows(h2, dest, n_rows_total)
        y = _experts(blk_expert, n_used, n_valid, xbuf, w1, w3, w2, l)
        yg = _sc_gather_rows(y, dest.reshape(-1))
        x2 = _combine_dense(gate.T, x1, row(g_final), yg, l == depth - 1)
    return x2.reshape(batch, seq, d)
```

```python
import functools
import math

import numpy as np
import jax
import jax.numpy as jnp
from jax import lax
from jax.experimental import pallas as pl
from jax.experimental.pallas import tpu as pltpu
from jax.experimental.pallas import tpu_sc as plsc

F32 = jnp.float32
BF16 = jnp.bfloat16

HEAD_DIM = 64
N_HEADS = 4
DIFF_QK = 32
SWA_KV_HEADS = 2
SWA_WINDOW = 128
MOBA_BLOCK = 256
MOBA_TOPK = 3
REL_BUCKETS = 32
REL_MAX_DIST = 128
N_GROUPS = 4
EXPERTS_PER_GROUP = 8
N_EXPERTS = N_GROUPS * EXPERTS_PER_GROUP
NORM_EPS = 1e-6

TB = 256
LANES = 128
ONES_ROWS = 16
NEG = -1e30
LOG2E = math.log2(math.e)
SB_EXIT = 104.0
MOE_ROWS = 512
VMEM_LIMIT = 56 * 1024 * 1024

QV_MOBA_Q, QV_MOBA_V, QV_DIFF_Q, QV_DIFF_V, QV_SB_Q, QV_SB_V, QV_SWA_Q = range(7)
QV_SWA_V_128 = 14
QV_ROWS = 7 * 256 + 128
KN_MOBA, KN_DIFF, KN_SB = range(3)
KN_SWA_128 = 6
KN_COLS = 3 * 256 + 128


def _cparams(n_grid):
    return pltpu.CompilerParams(dimension_semantics=("arbitrary",) * n_grid,
                                vmem_limit_bytes=VMEM_LIMIT)


def _rel_bucket_np(n):
    n = np.maximum(n, 0)
    max_exact = REL_BUCKETS // 2
    nf = np.maximum(n, 1).astype(np.float64)
    large = max_exact + (np.log(nf / max_exact) / math.log(REL_MAX_DIST / max_exact)
                         * (REL_BUCKETS - max_exact)).astype(np.int64)
    large = np.minimum(large, REL_BUCKETS - 1)
    return np.where(n < max_exact, n, large)


def _first_far_distance():
    d = np.arange(0, 4 * REL_MAX_DIST)
    b = _rel_bucket_np(d)
    return int(np.min(d[b == REL_BUCKETS - 1]))


def _toeplitz_bias(tab, rows, cols, base, valid_fn, shift_far, unit):
    length = rows + cols - 1
    off = np.concatenate([np.arange(0, cols), np.arange(cols - length, 0)])
    n = base + off
    onehot = np.zeros((REL_BUCKETS, length), np.float32)
    onehot[_rel_bucket_np(n), np.arange(length)] = 1.0
    vec = jnp.dot(tab, jnp.asarray(onehot), precision=lax.Precision.HIGHEST)
    if shift_far:
        vec = vec - tab[:, REL_BUCKETS - 1:]
    vec = jnp.where(jnp.asarray(valid_fn(n))[None, :], vec * unit, NEG).astype(F32)
    flat = jnp.tile(vec, (1, rows))[:, :rows * (length - 1)]
    return flat.reshape(tab.shape[0], rows, length - 1)[:, :, :cols]


def _causal_bias_tiles(tab):
    assert _first_far_distance() <= TB + 1
    tiles = [_toeplitz_bias(tab, TB, TB, d * TB, lambda n: n >= 0, True, LOG2E) for d in range(2)]
    return jnp.stack(tiles, axis=1)


def _swa_bias_tiles(tab):
    in_window = lambda n: (n >= 0) & (n < SWA_WINDOW)
    return (_toeplitz_bias(tab, TB, TB, 0, in_window, False, 1.0),
            _toeplitz_bias(tab, SWA_WINDOW, TB, SWA_WINDOW, in_window, False, 1.0))


def _pad_rows(q, off, total):
    n, t = q.shape
    parts = []
    if off:
        parts.append(jnp.zeros((off, t), q.dtype))
    parts.append(q)
    if total - off - n:
        parts.append(jnp.zeros((total - off - n, t), q.dtype))
    return jnp.concatenate(parts, axis=0) if len(parts) > 1 else q


def _with_ones(v):
    return jnp.concatenate([v, jnp.ones((ONES_ROWS, v.shape[1]), v.dtype)], axis=0)


def _pack_bf16_pairs(x):
    w = x.shape[1] // 2
    lo = lax.bitcast_convert_type(x[:, :w].astype(BF16).astype(F32), jnp.uint32)
    hi = lax.bitcast_convert_type(x[:, w:].astype(BF16).astype(F32), jnp.uint32)
    return hi | (lo >> 16)


def _unpack_bf16_pairs(u):
    lo = lax.bitcast_convert_type(u << 16, F32)
    hi = lax.bitcast_convert_type(u & jnp.uint32(0xFFFF0000), F32)
    return jnp.concatenate([lo, hi], axis=1)


def _rms(x, g_row):
    ms = jnp.mean(x * x, axis=-1, keepdims=True)
    return x * lax.rsqrt(ms + NORM_EPS) * g_row


IN_T = 512
IN_CHUNK = 384


def _inproj_kernel(x_ref, g_ref, wn_ref, wt_ref, kn_ref, qvt_ref):
    h = _rms(x_ref[...], g_ref[...]).astype(BF16)
    kn_ref[...] = jnp.dot(h, wn_ref[...], preferred_element_type=F32).astype(BF16)
    for r0 in range(0, QV_ROWS, IN_CHUNK):
        pt = lax.dot_general(wt_ref[r0:r0 + IN_CHUNK, :], h, (((1,), (1,)), ((), ())),
                             preferred_element_type=F32)
        for s in range(IN_T // TB):
            qvt_ref[s, r0:r0 + IN_CHUNK, :] = pt[:, s * TB:(s + 1) * TB].astype(BF16)


def _inproj(x2, g, wn, wt):
    n, d = x2.shape
    return pl.pallas_call(
        _inproj_kernel,
        grid=(n // IN_T,),
        in_specs=[pl.BlockSpec((IN_T, d), lambda i: (i, 0)),
                  pl.BlockSpec((1, d), lambda i: (0, 0)),
                  pl.BlockSpec((d, KN_COLS), lambda i: (0, 0)),
                  pl.BlockSpec((QV_ROWS, d), lambda i: (0, 0))],
        out_specs=[pl.BlockSpec((IN_T, KN_COLS), lambda i: (i, 0)),
                   pl.BlockSpec((IN_T // TB, QV_ROWS, TB), lambda i: (i, 0, 0))],
        out_shape=[jax.ShapeDtypeStruct((n, KN_COLS), BF16),
                   jax.ShapeDtypeStruct((n // TB, QV_ROWS, TB), BF16)],
        compiler_params=_cparams(1),
        name="inproj",
    )(x2, g, wn, wt)


FLASH_OVERFLOW = 100.0
FLASH_SKEW = 4


class _FlashScratch:
    def __init__(self, n_chain, qpad, m, acc, over, s):
        self.n_chain, self.qpad, self.m, self.acc, self.over, self.s = n_chain, qpad, m, acc, over, s

    @staticmethod
    def shapes(n_chain):
        return [pltpu.VMEM((n_chain, LANES, TB), BF16),
                pltpu.VMEM((n_chain, 1, TB), F32),
                pltpu.VMEM((n_chain, HEAD_DIM + ONES_ROWS, TB), F32),
                pltpu.VMEM((n_chain, 1, TB), F32),
                pltpu.VMEM((n_chain, TB, TB), F32)]


def _flash_two_pass(sc, chains):
    if len(chains) > sc.n_chain:
        for k in range(0, len(chains), sc.n_chain):
            _flash_two_pass(sc, chains[k:k + sc.n_chain])
        return
    block_max = []
    for c, kblk, add, _, colmask in chains:
        s = jnp.dot(kblk, sc.qpad[c], preferred_element_type=F32)
        if add is not None:
            s = s + add
        if colmask is not None:
            s = s + colmask
        sc.s[c] = s
        block_max.append(jnp.max(s, axis=0, keepdims=True))
    for (c, _, _, vext, _), mx in zip(chains, block_max):
        m_old = sc.m[c]
        m_new = jnp.maximum(m_old, mx)
        p = jnp.exp2(sc.s[c] - m_new).astype(BF16)
        sc.acc[c] = sc.acc[c] * jnp.exp2(m_old - m_new) + jnp.dot(vext, p, preferred_element_type=F32)
        sc.m[c] = m_new


def _flash_lagged(sc, chains):
    def finish(c, p, mx, ref, vext):
        pv = jnp.dot(vext, p, preferred_element_type=F32)
        ref_new = jnp.maximum(ref, mx)
        sc.acc[c] = (sc.acc[c] + pv) * jnp.exp2(ref - ref_new)
        sc.m[c] = ref_new
        sc.over[c] = jnp.maximum(sc.over[c], mx - ref)

    pending = []
    for c, kblk, add, vext, colmask in chains:
        s = jnp.dot(kblk, sc.qpad[c], preferred_element_type=F32)
        if add is not None:
            s = s + add
        if len(pending) >= min(FLASH_SKEW, sc.n_chain):
            finish(*pending.pop(0))
        ref = sc.m[c]
        mx = jnp.max(s, axis=0, keepdims=True)
        if colmask is None:
            p = jnp.exp2(s - ref)
        else:
            p = jnp.exp2(s - (ref - colmask))
            mx = mx + colmask
        pending.append((c, p.astype(BF16), mx, ref, vext))
    for item in pending:
        finish(*item)


def _flash_causal(sc, i, block_chains, far_group):
    assert far_group & (far_group - 1) == 0

    def init():
        for c in range(sc.n_chain):
            sc.m[c] = jnp.full(sc.m.shape[1:], NEG, F32)
            sc.acc[c] = jnp.zeros(sc.acc.shape[1:], F32)
            sc.over[c] = jnp.full(sc.over.shape[1:], NEG, F32)

    def run(step):
        init()
        _flash_two_pass(sc, block_chains(i, 0))

        @pl.when(i >= 1)
        def _():
            step(sc, block_chains(i - 1, 1))

        n_far = jnp.maximum(i - 1, 0)

        def far_blocks(j, count):
            chains = []
            for k in range(count):
                chains += block_chains(j + k, None)
            step(sc, chains)

        if step is _flash_two_pass:
            lax.fori_loop(0, n_far, lambda j, carry: (far_blocks(j, 1), carry)[1], 0)
            return

        def far_group_step(t, carry):
            far_blocks(far_group * t, far_group)
            return carry

        n_group = n_far // far_group
        lax.fori_loop(0, n_group, far_group_step, 0)
        done = n_group * far_group
        size = far_group // 2
        while size >= 1:
            has_piece = (n_far & size) != 0

            @pl.when(has_piece)
            def _(done=done, size=size):
                far_blocks(done, size)

            done = done + jnp.where(has_piece, size, 0)
            size //= 2

    run(_flash_lagged)
    worst = sc.over[0]
    for c in range(1, sc.n_chain):
        worst = jnp.maximum(worst, sc.over[c])

    @pl.when(jnp.max(worst) > FLASH_OVERFLOW)
    def _():
        run(_flash_two_pass)


def _diff_kernel(lam_init, qt_ref, k_ref, vt_ref, tile_ref, lq1_ref, lk1_ref, lq2_ref, lk2_ref,
                 subln_ref, o_ref, *scratch):
    i = pl.program_id(1)
    sc = _FlashScratch(2 * N_HEADS, *scratch)
    acc_scr = sc.acc
    for h in range(N_HEADS):
        for mp in range(2):
            r0 = h * HEAD_DIM + mp * DIFF_QK
            sc.qpad[2 * h + mp] = _pad_rows(qt_ref[0, r0:r0 + DIFF_QK, :],
                                            (h % 2) * HEAD_DIM + mp * DIFF_QK, LANES)

    def block_chains(j, tile_idx):
        row = pl.multiple_of(j * TB, TB)
        chains = []
        for h in range(N_HEADS):
            g = h // 2
            kblk = k_ref[pl.ds(row, TB), g * LANES:(g + 1) * LANES]
            add = None if tile_idx is None else tile_ref[h, tile_idx]
            vext = _with_ones(vt_ref[j, h * HEAD_DIM:(h + 1) * HEAD_DIM, :])
            chains += [(2 * h + mp, kblk, add, vext, None) for mp in range(2)]
        return chains

    _flash_causal(sc, i, block_chains, far_group=4)

    lam =(jnp.exp(jnp.sum(lq1_ref[...] * lk1_ref[...], keepdims=True))
           - jnp.exp(jnp.sum(lq2_ref[...] * lk2_ref[...], keepdims=True)) + lam_init)
    for h in range(N_HEADS):
        a1 = acc_scr[2 * h]
        a2 = acc_scr[2 * h + 1]
        o1 = a1[:HEAD_DIM] / a1[HEAD_DIM:HEAD_DIM + 1]
        o2 = a2[:HEAD_DIM] / a2[HEAD_DIM:HEAD_DIM + 1]
        a = o1 - lam * o2
        ms = jnp.mean(a * a, axis=0, keepdims=True)
        y = a * lax.rsqrt(ms + NORM_EPS) * subln_ref[...] * (1.0 - lam_init)
        o_ref[0, h * HEAD_DIM:(h + 1) * HEAD_DIM, :] = y.astype(BF16)


def _attn_common_specs(nb, q_blk, k_blk, v_blk):
    return [pl.BlockSpec((1, TB, TB), lambda b, i: (b * nb + i, q_blk, 0)),
            pl.BlockSpec((nb * TB, TB), lambda b, i: (b, k_blk)),
            pl.BlockSpec((nb, TB, TB), lambda b, i: (b, v_blk, 0))]


def _small_spec(shape):
    return pl.BlockSpec(shape, lambda b, i: (0,) * len(shape))


def _diff_attention(kn, qvt, tiles, lq1, lk1, lq2, lk2, subln, lam_init, batch):
    nblk = qvt.shape[0]
    nb = nblk // batch
    n_chain = 2 * N_HEADS
    return pl.pallas_call(
        functools.partial(_diff_kernel, lam_init),
        grid=(batch, nb),
        in_specs=_attn_common_specs(nb, QV_DIFF_Q, KN_DIFF, QV_DIFF_V) + [
            _small_spec((N_HEADS, 2, TB, TB)),
            _small_spec((1, DIFF_QK)), _small_spec((1, DIFF_QK)),
            _small_spec((1, DIFF_QK)), _small_spec((1, DIFF_QK)),
            _small_spec((HEAD_DIM, 1))],
        out_specs=pl.BlockSpec((1, TB, TB), lambda b, i: (b * nb + i, 0, 0)),
        out_shape=jax.ShapeDtypeStruct((nblk, TB, TB), BF16),
        scratch_shapes=_FlashScratch.shapes(n_chain),
        compiler_params=_cparams(2),
        name="diff_attn",
    )(qvt, kn, qvt, tiles, lq1, lk1, lq2, lk2, subln)


def _moba_kernel(nb, nbp, qt_ref, k_ref, vt_ref, tile_ref, o_ref, kmean_scr, sel_scr, *scratch):
    i = pl.program_id(1)
    sc = _FlashScratch(N_HEADS, *scratch)
    qpad_scr, acc_scr = sc.qpad, sc.acc

    @pl.when(i == 0)
    def _():
        kmean_scr[...] = jnp.zeros(kmean_scr.shape, F32)
        for jb in range(nb):
            blk = k_ref[jb * TB:(jb + 1) * TB, :].astype(F32)
            kmean_scr[jb:jb + 1, :] = jnp.mean(blk, axis=0, keepdims=True)

    for h in range(N_HEADS):
        qpad_scr[h] = _pad_rows(qt_ref[0, h * HEAD_DIM:(h + 1) * HEAD_DIM, :], (h % 2) * HEAD_DIM, LANES)

    blk_id = lax.broadcasted_iota(jnp.int32, (nbp, TB), 0)
    for h in range(N_HEADS):
        g = h // 2
        km = kmean_scr[:, g * LANES:(g + 1) * LANES].astype(BF16)
        gate = jnp.dot(km, qpad_scr[h], preferred_element_type=F32)
        avail = blk_id < i
        sel = jnp.zeros((nbp, TB), jnp.bool_)
        for _ in range(MOBA_TOPK):
            gm = jnp.where(avail, gate, -jnp.inf)
            best = jnp.max(gm, axis=0, keepdims=True)
            is_best = avail & (gm == best)
            first = jnp.min(jnp.where(is_best, blk_id, nbp), axis=0, keepdims=True)
            pick = blk_id == first
            sel = sel | pick
            avail = avail & jnp.logical_not(pick)
        sel_scr[h] = jnp.where(sel, 0.0, NEG).astype(F32)

    def block_chains(j, tile_idx):
        row = pl.multiple_of(j * TB, TB)
        chains = []
        for h in range(N_HEADS):
            g = h // 2
            kblk = k_ref[pl.ds(row, TB), g * LANES:(g + 1) * LANES]
            add = None if tile_idx is None else tile_ref[h, tile_idx]
            colmask = None if tile_idx == 0 else sel_scr[h, pl.ds(j, 1), :]
            chains.append((h, kblk, add, _with_ones(vt_ref[j, h * HEAD_DIM:(h + 1) * HEAD_DIM, :]), colmask))
        return chains

    _flash_causal(sc, i, block_chains, far_group=8)

    for h in range(N_HEADS):
        a = acc_scr[h]
        o_ref[0, h * HEAD_DIM:(h + 1) * HEAD_DIM, :] = (a[:HEAD_DIM] / a[HEAD_DIM:HEAD_DIM + 1]).astype(BF16)


def _moba_attention(kn, qvt, tiles, batch):
    nblk = qvt.shape[0]
    nb = nblk // batch
    nbp = max(8, -(-nb // 8) * 8)
    return pl.pallas_call(
        functools.partial(_moba_kernel, nb, nbp),
        grid=(batch, nb),
        in_specs=_attn_common_specs(nb, QV_MOBA_Q, KN_MOBA, QV_MOBA_V) + [
            _small_spec((N_HEADS, 2, TB, TB))],
        out_specs=pl.BlockSpec((1, TB, TB), lambda b, i: (b * nb + i, 0, 0)),
        out_shape=jax.ShapeDtypeStruct((nblk, TB, TB), BF16),
        scratch_shapes=[pltpu.VMEM((nbp, TB), F32),
                        pltpu.VMEM((N_HEADS, nbp, TB), F32)] + _FlashScratch.shapes(N_HEADS),
        compiler_params=_cparams(2),
        name="moba_attn",
    )(qvt, kn, qvt, tiles)


def _sb_kernel(qt_ref, k_ref, vt_ref, o_ref, qpad_scr, acc_scr, c_scr):
    i = pl.program_id(1)
    rows = lax.broadcasted_iota(jnp.int32, (TB, TB), 0)
    cols = lax.broadcasted_iota(jnp.int32, (TB, TB), 1)
    upper = (cols > rows).astype(BF16)
    strict = cols > rows

    heads = range(N_HEADS)

    def step(j, first):
        row = pl.multiple_of(j * TB, TB)
        zs = [jnp.dot(k_ref[pl.ds(row, TB), (h // 2) * LANES:(h // 2 + 1) * LANES], qpad_scr[h],
                      preferred_element_type=F32) for h in heads]
        log_keeps = []
        for z in zs:
            lk = -(jnp.maximum(z, 0.0) + jnp.log2(1.0 + jnp.exp2(-jnp.abs(z))))
            log_keeps.append(jnp.where(strict, lk, 0.0) if first else lk)
        laters = []
        for h in heads:
            hi = log_keeps[h].astype(BF16)
            lo = (log_keeps[h] - hi.astype(F32)).astype(BF16)
            later = (jnp.dot(upper, hi, preferred_element_type=F32)
                     + jnp.dot(upper, lo, preferred_element_type=F32))
            laters.append(later if first else later + c_scr[h])
        c_max = None
        for h in heads:
            w = jnp.exp2(zs[h] + log_keeps[h] + laters[h])
            if first:
                w = jnp.where(strict, w, 0.0)
            pv = jnp.dot(vt_ref[j, h * HEAD_DIM:(h + 1) * HEAD_DIM, :], w.astype(BF16),
                         preferred_element_type=F32)
            acc_scr[h] = pv if first else acc_scr[h] + pv
            c_new = laters[h][0:1, :] + log_keeps[h][0:1, :]
            c_scr[h] = c_new
            c_max = c_new if c_max is None else jnp.maximum(c_max, c_new)
        return jnp.max(c_max) > -SB_EXIT * LOG2E

    for h in heads:
        qpad_scr[h] = _pad_rows(qt_ref[0, h * HEAD_DIM:(h + 1) * HEAD_DIM, :], (h % 2) * HEAD_DIM, LANES)

    def cond(carry):
        j, go = carry
        return jnp.logical_and(j >= 0, go)

    def body(carry):
        j, _ = carry
        return j - 1, step(j, False)

    lax.while_loop(cond, body, (i - 1, step(i, True)))
    for h in heads:
        o_ref[0, h * HEAD_DIM:(h + 1) * HEAD_DIM, :] = acc_scr[h].astype(BF16)


def _sb_attention(kn, qvt, batch):
    nblk = qvt.shape[0]
    nb = nblk // batch
    return pl.pallas_call(
        _sb_kernel,
        grid=(batch, nb),
        in_specs=_attn_common_specs(nb, QV_SB_Q, KN_SB, QV_SB_V),
        out_specs=pl.BlockSpec((1, TB, TB), lambda b, i: (b * nb + i, 0, 0)),
        out_shape=jax.ShapeDtypeStruct((nblk, TB, TB), BF16),
        scratch_shapes=[pltpu.VMEM((N_HEADS, LANES, TB), BF16),
                        pltpu.VMEM((N_HEADS, HEAD_DIM, TB), F32),
                        pltpu.VMEM((N_HEADS, 1, TB), F32)],
        compiler_params=_cparams(2),
        name="sb_attn",
    )(qvt, kn, qvt)


def _swa_kernel(sink_ref, qt_ref, kc_ref, kp_ref, vc_ref, vp_ref, tc_ref, tp_ref, o_ref):
    i = pl.program_id(1)
    kc = kc_ref[...]
    kp = kp_ref[...]
    vc = vc_ref[0]
    vp = vp_ref[0][:, TB - SWA_WINDOW:]
    no_prev = jnp.where(i > 0, 0.0, NEG)
    group = N_HEADS // SWA_KV_HEADS
    for h in range(N_HEADS):
        kv = h // group
        qpad = _pad_rows(qt_ref[0, h * HEAD_DIM:(h + 1) * HEAD_DIM, :], kv * HEAD_DIM, LANES)
        s_c = jnp.dot(kc, qpad, preferred_element_type=F32) + tc_ref[h]
        s_p = jnp.dot(kp, qpad, preferred_element_type=F32) + tp_ref[h] + no_prev
        sink = sink_ref[h]
        m = jnp.maximum(jnp.maximum(jnp.max(s_c, axis=0, keepdims=True),
                                    jnp.max(s_p, axis=0, keepdims=True)), sink)
        p_c = jnp.exp(s_c - m).astype(BF16)
        p_p = jnp.exp(s_p - m).astype(BF16)
        o = (jnp.dot(_with_ones(vc[kv * HEAD_DIM:(kv + 1) * HEAD_DIM]), p_c, preferred_element_type=F32)
             + jnp.dot(_with_ones(vp[kv * HEAD_DIM:(kv + 1) * HEAD_DIM]), p_p, preferred_element_type=F32))
        denom = o[HEAD_DIM:HEAD_DIM + 1] + jnp.exp(sink - m)
        o_ref[0, h * HEAD_DIM:(h + 1) * HEAD_DIM, :] = (o[:HEAD_DIM] / denom).astype(BF16)


def _swa_attention(kn, qvt, tile_cur, tile_prev, sinks, batch):
    nblk = qvt.shape[0]
    nb = nblk // batch
    half = TB // SWA_WINDOW
    kv_rows = SWA_KV_HEADS * HEAD_DIM
    return pl.pallas_call(
        _swa_kernel,
        grid=(batch, nb),
        in_specs=[pl.BlockSpec(memory_space=pltpu.SMEM),
                  pl.BlockSpec((1, TB, TB), lambda b, i: (b * nb + i, QV_SWA_Q, 0)),
                  pl.BlockSpec((TB, kv_rows), lambda b, i: (b * nb + i, KN_SWA_128)),
                  pl.BlockSpec((SWA_WINDOW, kv_rows),
                               lambda b, i: (b * nb * half + jnp.maximum(half * i - 1, 0), KN_SWA_128)),
                  pl.BlockSpec((1, kv_rows, TB), lambda b, i: (b * nb + i, QV_SWA_V_128, 0)),
                  pl.BlockSpec((1, kv_rows, TB), lambda b, i: (b * nb + jnp.maximum(i - 1, 0), QV_SWA_V_128, 0)),
                  _small_spec((N_HEADS, TB, TB)),
                  _small_spec((N_HEADS, SWA_WINDOW, TB))],
        out_specs=pl.BlockSpec((1, TB, TB), lambda b, i: (b * nb + i, 0, 0)),
        out_shape=jax.ShapeDtypeStruct((nblk, TB, TB), BF16),
        compiler_params=_cparams(2),
        name="swa_attn",
    )(sinks, qvt, kn, kn, qvt, qvt, tile_cur, tile_prev)


MERGE_T = 512


def _merge_kernel(x_ref, g_ref, oa_ref, ob_ref, oc_ref, od_ref, wg_ref, wbr_ref, wo_ref, x1_ref):
    x = x_ref[...]
    h = _rms(x, g_ref[...]).astype(BF16)
    merged = None
    for bi, o_ref in enumerate((oa_ref, ob_ref, oc_ref, od_ref)):
        gate = jax.nn.sigmoid(jnp.dot(h, wg_ref[bi], preferred_element_type=F32))
        branch = jnp.concatenate(
            [lax.dot_general(o_ref[s], wbr_ref[bi], (((0,), (0,)), ((), ())), preferred_element_type=F32)
             for s in range(MERGE_T // TB)], axis=0)
        term = gate * branch
        merged = term if merged is None else merged + term
    x1_ref[...] = x + jnp.dot(merged.astype(BF16), wo_ref[...], preferred_element_type=F32)


def _merge(x2, g, o_a, o_b, o_c, o_d, wg, wbr, wo):
    n, d = x2.shape
    o_spec = pl.BlockSpec((MERGE_T // TB, TB, TB), lambda i: (i, 0, 0))
    return pl.pallas_call(
        _merge_kernel,
        grid=(n // MERGE_T,),
        in_specs=[pl.BlockSpec((MERGE_T, d), lambda i: (i, 0)),
                  pl.BlockSpec((1, d), lambda i: (0, 0)),
                  o_spec, o_spec, o_spec, o_spec,
                  pl.BlockSpec(wg.shape, lambda i: (0, 0, 0)),
                  pl.BlockSpec(wbr.shape, lambda i: (0, 0, 0)),
                  pl.BlockSpec(wo.shape, lambda i: (0, 0))],
        out_specs=pl.BlockSpec((MERGE_T, d), lambda i: (i, 0)),
        out_shape=jax.ShapeDtypeStruct((n, d), F32),
        compiler_params=_cparams(1),
        name="merge",
    )(x2, g, o_a, o_b, o_c, o_d, wg, wbr, wo)


ROUTER_T = 512
ROUTER_ROWS = 8 + N_EXPERTS


def _first_argmax_rows(v, n_rows):
    best = jnp.max(v, axis=0, keepdims=True)
    ids = lax.broadcasted_iota(jnp.int32, v.shape, 0)
    return best, jnp.min(jnp.where(v == best, ids, n_rows), axis=0, keepdims=True)


def _router_kernel(x_ref, g_ref, whi_ref, wlo_ref, b_ref, h2_ref, eid_ref, gate_ref, rank_ref, cnt_ref,
                   base_scr):
    i = pl.program_id(0)

    @pl.when(i == 0)
    def _():
        base_scr[...] = jnp.zeros(base_scr.shape, F32)

    h2 = _rms(x_ref[...], g_ref[...])
    h2_ref[...] = _pack_bf16_pairs(h2)
    h_hi = h2.astype(BF16)
    h_lo = (h2 - h_hi.astype(F32)).astype(BF16)
    nt = (((1,), (1,)), ((), ()))
    logits = (lax.dot_general(whi_ref[...], h_hi, nt, preferred_element_type=F32)
              + lax.dot_general(whi_ref[...], h_lo, nt, preferred_element_type=F32)
              + lax.dot_general(wlo_ref[...], h_hi, nt, preferred_element_type=F32)
              + b_ref[...])
    gl = logits[0:8]
    gmax, grp = _first_argmax_rows(gl, 8)
    p_grp = 1.0 / jnp.sum(jnp.exp(gl - gmax), axis=0, keepdims=True)
    e_sel = jnp.zeros((EXPERTS_PER_GROUP, ROUTER_T), F32)
    for g in range(N_GROUPS):
        e_sel = jnp.where(grp == g, logits[8 + 8 * g:16 + 8 * g], e_sel)
    ids8 = lax.broadcasted_iota(jnp.int32, e_sel.shape, 0)
    v1, i1 = _first_argmax_rows(e_sel, EXPERTS_PER_GROUP)
    e_rest = jnp.where(ids8 == i1, -jnp.inf, e_sel)
    v2, i2 = _first_argmax_rows(e_rest, EXPERTS_PER_GROUP)
    r = jnp.exp(v2 - v1)
    s1 = 1.0 / (1.0 + r)
    gate_ref[0:1, :] = p_grp * s1
    gate_ref[1:2, :] = p_grp * (r * s1)
    e1 = grp * EXPERTS_PER_GROUP + i1
    e2 = grp * EXPERTS_PER_GROUP + i2
    eid_ref[0:1, :] = e1
    eid_ref[1:2, :] = e2

    ids_e = lax.broadcasted_iota(jnp.int32, (N_EXPERTS, ROUTER_T), 0)
    oh1 = ids_e == e1
    oh2 = ids_e == e2
    cnt = oh1.astype(F32) + oh2.astype(F32)
    tr = lax.broadcasted_iota(jnp.int32, (ROUTER_T, ROUTER_T), 0)
    tc = lax.broadcasted_iota(jnp.int32, (ROUTER_T, ROUTER_T), 1)
    before = (tr < tc).astype(BF16)
    prefix = jnp.dot(cnt.astype(BF16), before, preferred_element_type=F32) + base_scr[:, 0:1]
    rank_ref[0:1, :] = jnp.sum(jnp.where(oh1, prefix, 0.0), axis=0, keepdims=True).astype(jnp.int32)
    rank_ref[1:2, :] = jnp.sum(jnp.where(oh2, prefix, 0.0), axis=0, keepdims=True).astype(jnp.int32)
    base_scr[...] = base_scr[...] + jnp.sum(cnt, axis=1, keepdims=True)
    cnt_ref[...] = base_scr[...]


def _router(x1, g, w_hi, w_lo, bias):
    n, d = x1.shape
    row2 = lambda dt: jax.ShapeDtypeStruct((2, n), dt)
    spec2 = pl.BlockSpec((2, ROUTER_T), lambda i: (0, i))
    return pl.pallas_call(
        _router_kernel,
        grid=(n // ROUTER_T,),
        in_specs=[pl.BlockSpec((ROUTER_T, d), lambda i: (i, 0)),
                  pl.BlockSpec((1, d), lambda i: (0, 0)),
                  pl.BlockSpec((ROUTER_ROWS, d), lambda i: (0, 0)),
                  pl.BlockSpec((ROUTER_ROWS, d), lambda i: (0, 0)),
                  pl.BlockSpec((ROUTER_ROWS, 1), lambda i: (0, 0))],
        out_specs=[pl.BlockSpec((ROUTER_T, d // 2), lambda i: (i, 0)), spec2, spec2, spec2,
                   pl.BlockSpec((N_EXPERTS, LANES), lambda i: (0, 0))],
        out_shape=[jax.ShapeDtypeStruct((n, d // 2), jnp.uint32), row2(jnp.int32), row2(F32), row2(jnp.int32),
                   jax.ShapeDtypeStruct((N_EXPERTS, LANES), F32)],
        scratch_shapes=[pltpu.VMEM((N_EXPERTS, LANES), F32)],
        compiler_params=_cparams(1),
        name="router",
    )(x1, g, w_hi, w_lo, bias)


ROW_T = 256


def _dispatch_kernel(dest_ref, h2_ref, xin_ref, xbuf_ref, sem):
    del xin_ref

    def row_copy(r, k):
        return pltpu.make_async_copy(h2_ref.at[pl.ds(r, 1)], xbuf_ref.at[pl.ds(dest_ref[k, r], 1)], sem)

    def issue(r, carry):
        row_copy(r, 0).start()
        row_copy(r, 1).start()
        return carry

    def drain(r, carry):
        row_copy(r, 0).wait()
        row_copy(r, 1).wait()
        return carry

    lax.fori_loop(0, ROW_T, issue, 0, unroll=8)
    lax.fori_loop(0, ROW_T, drain, 0, unroll=8)


def _dispatch(dest3, h2, xbuf_init):
    n, d = h2.shape
    return pl.pallas_call(
        _dispatch_kernel,
        grid=(n // ROW_T,),
        in_specs=[pl.BlockSpec((None, 2, ROW_T), lambda i: (i, 0, 0), memory_space=pltpu.SMEM),
                  pl.BlockSpec((ROW_T, d), lambda i: (i, 0)),
                  pl.BlockSpec(memory_space=pl.ANY)],
        out_specs=pl.BlockSpec(memory_space=pl.ANY),
        out_shape=jax.ShapeDtypeStruct(xbuf_init.shape, xbuf_init.dtype),
        scratch_shapes=[pltpu.SemaphoreType.DMA(())],
        input_output_aliases={2: 0},
        compiler_params=_cparams(1),
        name="dispatch",
    )(dest3, h2, xbuf_init)


def _expert_kernel(be_ref, nu_ref, nv_ref, x_ref, w1_ref, w3_ref, w2_ref, y_ref):
    del be_ref
    used = pl.program_id(0) < nu_ref[0]

    @pl.when(used)
    def _():
        live = lax.broadcasted_iota(jnp.int32, x_ref.shape, 0) < nv_ref[pl.program_id(0)]
        xb = _unpack_bf16_pairs(jnp.where(live, x_ref[...], jnp.uint32(0))).astype(BF16)
        a = jnp.dot(xb, w1_ref[...].astype(BF16), preferred_element_type=F32)
        b = jnp.dot(xb, w3_ref[...].astype(BF16), preferred_element_type=F32)
        mid = (a * jax.nn.sigmoid(a) * b).astype(BF16)
        y_ref[...] = _pack_bf16_pairs(jnp.dot(mid, w2_ref[...].astype(BF16), preferred_element_type=F32))

    @pl.when(jnp.logical_not(used))
    def _():
        y_ref[...] = jnp.zeros(y_ref.shape, jnp.uint32)


def _experts(blk_expert, n_used, n_valid, xbuf, w1, w3, w2, layer):
    r = xbuf.shape[0]
    d = w1.shape[-2]
    de = w1.shape[-1]
    assert xbuf.shape[1] * 2 == d
    row_map = lambda i, be, nu, nv: (jnp.minimum(i, nu[0] - 1), 0)
    grid_spec = pltpu.PrefetchScalarGridSpec(
        num_scalar_prefetch=3,
        grid=(r // MOE_ROWS,),
        in_specs=[pl.BlockSpec((MOE_ROWS, d // 2), row_map),
                  pl.BlockSpec((None, None, d, de), lambda i, be, nu, nv: (layer, be[i], 0, 0)),
                  pl.BlockSpec((None, None, d, de), lambda i, be, nu, nv: (layer, be[i], 0, 0)),
                  pl.BlockSpec((None, None, de, d), lambda i, be, nu, nv: (layer, be[i], 0, 0))],
        out_specs=pl.BlockSpec((MOE_ROWS, d // 2), lambda i, be, nu, nv: (i, 0)),
    )
    return pl.pallas_call(
        _expert_kernel,
        grid_spec=grid_spec,
        out_shape=jax.ShapeDtypeStruct((r, d // 2), jnp.uint32),
        compiler_params=_cparams(1),
        name="experts",
    )(blk_expert, n_used, n_valid, xbuf, w1, w3, w2)


def _combine_kernel(final, dest_ref, gate_ref, x1_ref, gf_ref, y_ref, out_ref, buf, sem):
    def row_copy(r, k):
        return pltpu.make_async_copy(y_ref.at[pl.ds(dest_ref[k, r], 1)], buf.at[k, pl.ds(r, 1)], sem)

    def issue(r, carry):
        row_copy(r, 0).start()
        row_copy(r, 1).start()
        return carry

    def drain(r, carry):
        row_copy(r, 0).wait()
        row_copy(r, 1).wait()
        return carry

    lax.fori_loop(0, ROW_T, issue, 0, unroll=8)
    lax.fori_loop(0, ROW_T, drain, 0, unroll=8)
    gate = gate_ref[...]
    out = x1_ref[...] + gate[:, 0:1] * buf[0] + gate[:, 1:2] * buf[1]
    if final:
        out = _rms(out, gf_ref[...])
    out_ref[...] = out


def _combine(dest3, gate_t, x1, g_final, y, final):
    n, d = x1.shape
    return pl.pallas_call(
        functools.partial(_combine_kernel, final),
        grid=(n // ROW_T,),
        in_specs=[pl.BlockSpec((None, 2, ROW_T), lambda i: (i, 0, 0), memory_space=pltpu.SMEM),
                  pl.BlockSpec((ROW_T, 2), lambda i: (i, 0)),
                  pl.BlockSpec((ROW_T, d), lambda i: (i, 0)),
                  pl.BlockSpec((1, d), lambda i: (0, 0)),
                  pl.BlockSpec(memory_space=pl.ANY)],
        out_specs=pl.BlockSpec((ROW_T, d), lambda i: (i, 0)),
        out_shape=jax.ShapeDtypeStruct((n, d), F32),
        scratch_shapes=[pltpu.VMEM((2, ROW_T, d), F32), pltpu.SemaphoreType.DMA(())],
        compiler_params=_cparams(1),
        name="combine",
    )(dest3, gate_t, x1, g_final, y)


SC_CORES = 2
SC_SUBCORES = 16
SC_ROWS = 64


def _sc_gather_rows(table, idx):
    n_idx = idx.shape[0]
    d = table.shape[1]
    n_workers = SC_CORES * SC_SUBCORES
    per_worker = n_idx // n_workers
    n_chunk = per_worker // SC_ROWS
    assert per_worker * n_workers == n_idx and n_chunk * SC_ROWS == per_worker and n_chunk % 2 == 0
    mesh = plsc.VectorSubcoreMesh(core_axis_name="c", subcore_axis_name="s",
                                  num_cores=SC_CORES, num_subcores=SC_SUBCORES)

    def body(table_hbm, idx_hbm, out_hbm, idx_v, rows_v, gsem, wsem):
        worker = lax.axis_index("s") * SC_CORES + lax.axis_index("c")
        base = worker * per_worker
        pltpu.sync_copy(idx_hbm.at[pl.ds(base, per_worker)], idx_v)

        def gather(c, b):
            return pltpu.make_async_copy(table_hbm.at[idx_v.at[pl.ds(c * SC_ROWS, SC_ROWS)]],
                                         rows_v.at[b], gsem.at[b])

        def put(c, b):
            return pltpu.make_async_copy(rows_v.at[b], out_hbm.at[pl.ds(base + c * SC_ROWS, SC_ROWS)],
                                         wsem.at[b])

        gather(0, 0).start()

        @pl.loop(0, n_chunk, step=2)
        def _(c0):
            for b in range(2):
                c = c0 + b
                gather(c, b).wait()

                @pl.when(c + 1 < n_chunk)
                def _():
                    @pl.when(c >= 1)
                    def _():
                        put(c - 1, 1 - b).wait()

                    gather(c + 1, 1 - b).start()

                put(c, b).start()

        put(n_chunk - 2, 0).wait()
        put(n_chunk - 1, 1).wait()

    return pl.kernel(
        body,
        out_type=jax.ShapeDtypeStruct((n_idx, d), table.dtype),
        mesh=mesh,
        scratch_types=[pltpu.VMEM((per_worker,), jnp.int32),
                       pltpu.VMEM((2, SC_ROWS, d), table.dtype),
                       pltpu.SemaphoreType.DMA((2,)),
                       pltpu.SemaphoreType.DMA((2,))],
        name="sc_gather_rows",
    )(table, idx)


def _sc_scatter_rows(rows, dest, n_out):
    n, d = rows.shape
    n_workers = SC_CORES * SC_SUBCORES
    per_worker = n // n_workers
    n_chunk = per_worker // SC_ROWS
    assert per_worker * n_workers == n and n_chunk * SC_ROWS == per_worker and n_chunk % 2 == 0
    mesh = plsc.VectorSubcoreMesh(core_axis_name="c", subcore_axis_name="s",
                                  num_cores=SC_CORES, num_subcores=SC_SUBCORES)
    dest3 = dest.reshape(2, n // SC_ROWS, SC_ROWS)

    def body(rows_hbm, idx_hbm, out_hbm, idx_v, rows_v, lsem, ssem):
        worker = lax.axis_index("s") * SC_CORES + lax.axis_index("c")
        for k in range(2):
            pltpu.sync_copy(idx_hbm.at[k, pl.ds(worker * n_chunk, n_chunk)], idx_v.at[k])

        def load(c, b):
            return pltpu.make_async_copy(rows_hbm.at[pl.ds(worker * per_worker + c * SC_ROWS, SC_ROWS)],
                                         rows_v.at[b], lsem.at[b])

        def scatter(c, b, k):
            return pltpu.make_async_copy(rows_v.at[b], out_hbm.at[idx_v.at[k, c]], ssem.at[b])

        load(0, 0).start()

        @pl.loop(0, n_chunk, step=2)
        def _(c0):
            for b in range(2):
                c = c0 + b
                load(c, b).wait()

                @pl.when(c + 1 < n_chunk)
                def _():
                    @pl.when(c >= 1)
                    def _():
                        scatter(c - 1, 1 - b, 0).wait()
                        scatter(c - 1, 1 - b, 1).wait()

                    load(c + 1, 1 - b).start()

                scatter(c, b, 0).start()
                scatter(c, b, 1).start()

        for c, b in ((n_chunk - 2, 0), (n_chunk - 1, 1)):
            scatter(c, b, 0).wait()
            scatter(c, b, 1).wait()

    return pl.kernel(
        body,
        out_type=jax.ShapeDtypeStruct((n_out, d), rows.dtype),
        mesh=mesh,
        scratch_types=[pltpu.VMEM((2, n_chunk, SC_ROWS), jnp.int32),
                       pltpu.VMEM((2, SC_ROWS, d), rows.dtype),
                       pltpu.SemaphoreType.DMA((2,)),
                       pltpu.SemaphoreType.DMA((2,))],
        name="sc_scatter_rows",
    )(rows, dest3)


def _combine_dense_kernel(final, gate_ref, x1_ref, gf_ref, y0_ref, y1_ref, out_ref):
    gate = gate_ref[...]
    out = (x1_ref[...] + gate[:, 0:1] * _unpack_bf16_pairs(y0_ref[...])
           + gate[:, 1:2] * _unpack_bf16_pairs(y1_ref[...]))
    if final:
        out = _rms(out, gf_ref[...])
    out_ref[...] = out


def _combine_dense(gate_t, x1, g_final, yg, final):
    n, d = x1.shape
    nb = n // MERGE_T
    return pl.pallas_call(
        functools.partial(_combine_dense_kernel, final),
        grid=(nb,),
        in_specs=[pl.BlockSpec((MERGE_T, 2), lambda i: (i, 0)),
                  pl.BlockSpec((MERGE_T, d), lambda i: (i, 0)),
                  pl.BlockSpec((1, d), lambda i: (0, 0)),
                  pl.BlockSpec((MERGE_T, d // 2), lambda i: (i, 0)),
                  pl.BlockSpec((MERGE_T, d // 2), lambda i: (i + nb, 0))],
        out_specs=pl.BlockSpec((MERGE_T, d), lambda i: (i, 0)),
        out_shape=jax.ShapeDtypeStruct((n, d), F32),
        compiler_params=_cparams(1),
        name="combine_dense",
    )(gate_t, x1, g_final, yg, yg)


def _projection_weights(w):
    d = w.shape[0]
    blk = N_HEADS * HEAD_DIM
    kv = SWA_KV_HEADS * HEAD_DIM
    pa, pb, pc, pd = w[:, :3 * blk], w[:, 3 * blk:6 * blk], w[:, 6 * blk:9 * blk], w[:, 9 * blk:]
    half = N_HEADS * DIFF_QK

    def per_head(a, b):
        return jnp.stack([a.reshape(d, N_HEADS, DIFF_QK), b.reshape(d, N_HEADS, DIFF_QK)], axis=2).reshape(d, blk)

    s64, s32 = HEAD_DIM ** -0.5, DIFF_QK ** -0.5
    wn = jnp.concatenate([pa[:, blk:2 * blk], per_head(pb[:, 2 * half:3 * half], pb[:, 3 * half:4 * half]),
                          pc[:, blk:2 * blk], pd[:, blk:blk + kv]], axis=1)
    wt = jnp.concatenate([pa[:, :blk] * (s64 * LOG2E), pa[:, 2 * blk:],
                          per_head(pb[:, :half], pb[:, half:2 * half]) * (s32 * LOG2E), pb[:, 4 * half:],
                          pc[:, :blk] * (s64 * LOG2E), pc[:, 2 * blk:],
                          pd[:, :blk] * s64, pd[:, blk + kv:]], axis=1)
    assert wn.shape[1] == KN_COLS and wt.shape[1] == QV_ROWS
    return wn.astype(BF16), wt.T.astype(BF16)


def _router_weights(w_rg, b_rg, w_re, b_re):
    d = w_rg.shape[0]
    w = jnp.concatenate([w_rg.T, jnp.zeros((8 - N_GROUPS, d), F32), w_re.T], axis=0)
    b = jnp.concatenate([b_rg.astype(F32), jnp.full((8 - N_GROUPS,), NEG, F32), b_re.astype(F32)])[:, None]
    w_hi = w.astype(BF16)
    w_lo = (w - w_hi.astype(F32)).astype(BF16)
    return w_hi, w_lo, b


def _moe_plan(eid, rank, counts, n_rows_total):
    padded = (counts + MOE_ROWS - 1) // MOE_ROWS * MOE_ROWS
    pad_end = jnp.cumsum(padded)
    pad_start = pad_end - padded
    experts = jnp.arange(N_EXPERTS, dtype=jnp.int32)
    start_of = jnp.sum(jnp.where(eid[..., None] == experts, pad_start, 0), axis=-1)
    dest = start_of + rank
    n_blk = n_rows_total // MOE_ROWS
    n_used = (pad_end[-1] // MOE_ROWS).astype(jnp.int32)
    blk = jnp.minimum(jnp.arange(n_blk, dtype=jnp.int32), n_used - 1) * MOE_ROWS
    blk_expert = jnp.minimum(jnp.sum(pad_end[None, :] <= blk[:, None], axis=1), N_EXPERTS - 1).astype(jnp.int32)
    live_end = jnp.sum(jnp.where(blk_expert[:, None] == experts, pad_start + counts, 0), axis=-1)
    n_valid = jnp.clip(live_end - blk, 0, MOE_ROWS).astype(jnp.int32)
    return dest.astype(jnp.int32), blk_expert, n_used.reshape(1), n_valid


def kernel(x, rel_bias, g_mix, w_in, diff_lq1, diff_lk1, diff_lq2, diff_lk2, diff_subln, swa_sinks,
           w_gate, w_br, w_o, g_ffn, w_route_group, b_route_group, w_route_expert, b_route_expert,
           w1, w3, w2, g_final):
    batch, seq, d = x.shape
    n = batch * seq
    depth = w_in.shape[0]
    assert seq % TB == 0 and n % MERGE_T == 0 and TB == MOBA_BLOCK
    tab = rel_bias.T.astype(F32)
    tiles_moba = _causal_bias_tiles(tab[:N_HEADS])
    tiles_diff = _causal_bias_tiles(tab[N_HEADS:2 * N_HEADS])
    tile_cur, tile_prev = _swa_bias_tiles(tab[2 * N_HEADS:])
    n_rows_total = n * 2 + N_EXPERTS * MOE_ROWS
    row = lambda v: v.astype(F32)[None, :]

    x2 = x.reshape(n, d)
    for l in range(depth):
        lam_init = 0.8 - 0.6 * math.exp(-0.3 * l)
        wn, wt = _projection_weights(w_in[l])
        kn, qvt = _inproj(x2, row(g_mix[l]), wn, wt)
        o_a = _moba_attention(kn, qvt, tiles_moba, batch)
        o_b = _diff_attention(kn, qvt, tiles_diff, row(diff_lq1[l]), row(diff_lk1[l]), row(diff_lq2[l]),
                              row(diff_lk2[l]), diff_subln[l].astype(F32)[:, None], lam_init, batch)
        o_c = _sb_attention(kn, qvt, batch)
        o_d = _swa_attention(kn, qvt, tile_cur, tile_prev, swa_sinks[l].astype(F32), batch)
        x1 = _merge(x2, row(g_mix[l]), o_a, o_b, o_c, o_d, w_gate[l].astype(BF16), w_br[l].astype(BF16),
                    w_o[l].astype(BF16))
        w_hi, w_lo, r_bias = _router_weights(w_route_group[l], b_route_group[l], w_route_expert[l],
                                             b_route_expert[l])
        h2, eid, gate, rank, cnt = _router(x1, row(g_ffn[l]), w_hi, w_lo, r_bias)
        dest, blk_expert, n_used, n_valid = _moe_plan(eid, rank, cnt[:, 0].astype(jnp.int32), n_rows_total)
        xbuf = _sc_scatter_rows(h2, dest, n_rows_total)
        y = _experts(blk_expert, n_used, n_valid, xbuf, w1, w3, w2, l)
        yg = _sc_gather_rows(y, dest.reshape(-1))
        x2 = _combine_dense(gate.T, x1, row(g_final), yg, l == depth - 1)
    return x2.reshape(batch, seq, d)
```

```python
import functools
import math

import numpy as np
import jax
import jax.numpy as jnp
from jax import lax
from jax.experimental import pallas as pl
from jax.experimental.pallas import tpu as pltpu
from jax.experimental.pallas import tpu_sc as plsc

F32 = jnp.float32
BF16 = jnp.bfloat16

HEAD_DIM = 64
N_HEADS = 4
DIFF_QK = 32
SWA_KV_HEADS = 2
SWA_WINDOW = 128
MOBA_BLOCK = 256
MOBA_TOPK = 3
REL_BUCKETS = 32
REL_MAX_DIST = 128
N_GROUPS = 4
EXPERTS_PER_GROUP = 8
N_EXPERTS = N_GROUPS * EXPERTS_PER_GROUP
NORM_EPS = 1e-6

TB = 256
LANES = 128
ONES_ROWS = 16
NEG = -1e30
LOG2E = math.log2(math.e)
SB_EXIT = 104.0
MOE_ROWS = 512
VMEM_LIMIT = 56 * 1024 * 1024

QV_MOBA_Q, QV_MOBA_V, QV_DIFF_Q, QV_DIFF_V, QV_SB_Q, QV_SB_V, QV_SWA_Q = range(7)
QV_SWA_V_128 = 14
QV_ROWS = 7 * 256 + 128
KN_MOBA, KN_DIFF, KN_SB = range(3)
KN_SWA_128 = 6
KN_COLS = 3 * 256 + 128


def _cparams(n_grid):
    return pltpu.CompilerParams(dimension_semantics=("arbitrary",) * n_grid,
                                vmem_limit_bytes=VMEM_LIMIT)


def _rel_bucket_np(n):
    n = np.maximum(n, 0)
    max_exact = REL_BUCKETS // 2
    nf = np.maximum(n, 1).astype(np.float64)
    large = max_exact + (np.log(nf / max_exact) / math.log(REL_MAX_DIST / max_exact)
                         * (REL_BUCKETS - max_exact)).astype(np.int64)
    large = np.minimum(large, REL_BUCKETS - 1)
    return np.where(n < max_exact, n, large)


def _first_far_distance():
    d = np.arange(0, 4 * REL_MAX_DIST)
    b = _rel_bucket_np(d)
    return int(np.min(d[b == REL_BUCKETS - 1]))


def _toeplitz_bias(tab, rows, cols, base, valid_fn, shift_far, unit):
    length = rows + cols - 1
    off = np.concatenate([np.arange(0, cols), np.arange(cols - length, 0)])
    n = base + off
    onehot = np.zeros((REL_BUCKETS, length), np.float32)
    onehot[_rel_bucket_np(n), np.arange(length)] = 1.0
    vec = jnp.dot(tab, jnp.asarray(onehot), precision=lax.Precision.HIGHEST)
    if shift_far:
        vec = vec - tab[:, REL_BUCKETS - 1:]
    vec = jnp.where(jnp.asarray(valid_fn(n))[None, :], vec * unit, NEG).astype(F32)
    flat = jnp.tile(vec, (1, rows))[:, :rows * (length - 1)]
    return flat.reshape(tab.shape[0], rows, length - 1)[:, :, :cols]


def _causal_bias_tiles(tab):
    assert _first_far_distance() <= TB + 1
    tiles = [_toeplitz_bias(tab, TB, TB, d * TB, lambda n: n >= 0, True, LOG2E) for d in range(2)]
    return jnp.stack(tiles, axis=1)


def _swa_bias_tiles(tab):
    in_window = lambda n: (n >= 0) & (n < SWA_WINDOW)
    return (_toeplitz_bias(tab, TB, TB, 0, in_window, False, 1.0),
            _toeplitz_bias(tab, SWA_WINDOW, TB, SWA_WINDOW, in_window, False, 1.0))


def _pad_rows(q, off, total):
    n, t = q.shape
    parts = []
    if off:
        parts.append(jnp.zeros((off, t), q.dtype))
    parts.append(q)
    if total - off - n:
        parts.append(jnp.zeros((total - off - n, t), q.dtype))
    return jnp.concatenate(parts, axis=0) if len(parts) > 1 else q


def _with_ones(v):
    return jnp.concatenate([v, jnp.ones((ONES_ROWS, v.shape[1]), v.dtype)], axis=0)


def _pack_bf16_pairs(x):
    w = x.shape[1] // 2
    lo = lax.bitcast_convert_type(x[:, :w].astype(BF16).astype(F32), jnp.uint32)
    hi = lax.bitcast_convert_type(x[:, w:].astype(BF16).astype(F32), jnp.uint32)
    return hi | (lo >> 16)


def _unpack_bf16_pairs(u):
    lo = lax.bitcast_convert_type(u << 16, F32)
    hi = lax.bitcast_convert_type(u & jnp.uint32(0xFFFF0000), F32)
    return jnp.concatenate([lo, hi], axis=1)


def _rms(x, g_row):
    ms = jnp.mean(x * x, axis=-1, keepdims=True)
    return x * lax.rsqrt(ms + NORM_EPS) * g_row


IN_T = 512
IN_CHUNK = 384


def _inproj_kernel(x_ref, g_ref, wn_ref, wt_ref, kn_ref, qvt_ref):
    h = _rms(x_ref[...], g_ref[...]).astype(BF16)
    kn_ref[...] = jnp.dot(h, wn_ref[...], preferred_element_type=F32).astype(BF16)
    for r0 in range(0, QV_ROWS, IN_CHUNK):
        pt = lax.dot_general(wt_ref[r0:r0 + IN_CHUNK, :], h, (((1,), (1,)), ((), ())),
                             preferred_element_type=F32)
        for s in range(IN_T // TB):
            qvt_ref[s, r0:r0 + IN_CHUNK, :] = pt[:, s * TB:(s + 1) * TB].astype(BF16)


def _inproj(x2, g, wn, wt):
    n, d = x2.shape
    return pl.pallas_call(
        _inproj_kernel,
        grid=(n // IN_T,),
        in_specs=[pl.BlockSpec((IN_T, d), lambda i: (i, 0)),
                  pl.BlockSpec((1, d), lambda i: (0, 0)),
                  pl.BlockSpec((d, KN_COLS), lambda i: (0, 0)),
                  pl.BlockSpec((QV_ROWS, d), lambda i: (0, 0))],
        out_specs=[pl.BlockSpec((IN_T, KN_COLS), lambda i: (i, 0)),
                   pl.BlockSpec((IN_T // TB, QV_ROWS, TB), lambda i: (i, 0, 0))],
        out_shape=[jax.ShapeDtypeStruct((n, KN_COLS), BF16),
                   jax.ShapeDtypeStruct((n // TB, QV_ROWS, TB), BF16)],
        compiler_params=_cparams(1),
        name="inproj",
    )(x2, g, wn, wt)


FLASH_OVERFLOW = 100.0
FLASH_SKEW = 4


class _FlashScratch:
    def __init__(self, n_chain, qpad, m, acc, over, s):
        self.n_chain, self.qpad, self.m, self.acc, self.over, self.s = n_chain, qpad, m, acc, over, s

    @staticmethod
    def shapes(n_chain):
        return [pltpu.VMEM((n_chain, LANES, TB), BF16),
                pltpu.VMEM((n_chain, 1, TB), F32),
                pltpu.VMEM((n_chain, HEAD_DIM + ONES_ROWS, TB), F32),
                pltpu.VMEM((n_chain, 1, TB), F32),
                pltpu.VMEM((n_chain, TB, TB), F32)]


def _flash_two_pass(sc, chains):
    if len(chains) > sc.n_chain:
        for k in range(0, len(chains), sc.n_chain):
            _flash_two_pass(sc, chains[k:k + sc.n_chain])
        return
    block_max = []
    for c, kblk, add, _, colmask in chains:
        s = jnp.dot(kblk, sc.qpad[c], preferred_element_type=F32)
        if add is not None:
            s = s + add
        if colmask is not None:
            s = s + colmask
        sc.s[c] = s
        block_max.append(jnp.max(s, axis=0, keepdims=True))
    for (c, _, _, vext, _), mx in zip(chains, block_max):
        m_old = sc.m[c]
        m_new = jnp.maximum(m_old, mx)
        p = jnp.exp2(sc.s[c] - m_new).astype(BF16)
        sc.acc[c] = sc.acc[c] * jnp.exp2(m_old - m_new) + jnp.dot(vext, p, preferred_element_type=F32)
        sc.m[c] = m_new


def _flash_lagged(sc, chains):
    def finish(c, p, mx, ref, vext):
        pv = jnp.dot(vext, p, preferred_element_type=F32)
        ref_new = jnp.maximum(ref, mx)
        sc.acc[c] = (sc.acc[c] + pv) * jnp.exp2(ref - ref_new)
        sc.m[c] = ref_new
        sc.over[c] = jnp.maximum(sc.over[c], mx - ref)

    pending = []
    for c, kblk, add, vext, colmask in chains:
        s = jnp.dot(kblk, sc.qpad[c], preferred_element_type=F32)
        if add is not None:
            s = s + add
        if len(pending) >= min(FLASH_SKEW, sc.n_chain):
            finish(*pending.pop(0))
        ref = sc.m[c]
        mx = jnp.max(s, axis=0, keepdims=True)
        if colmask is None:
            p = jnp.exp2(s - ref)
        else:
            p = jnp.exp2(s - (ref - colmask))
            mx = mx + colmask
        pending.append((c, p.astype(BF16), mx, ref, vext))
    for item in pending:
        finish(*item)


def _flash_causal(sc, i, block_chains, far_group):
    assert far_group & (far_group - 1) == 0

    def init():
        for c in range(sc.n_chain):
            sc.m[c] = jnp.full(sc.m.shape[1:], NEG, F32)
            sc.acc[c] = jnp.zeros(sc.acc.shape[1:], F32)
            sc.over[c] = jnp.full(sc.over.shape[1:], NEG, F32)

    def run(step):
        init()
        _flash_two_pass(sc, block_chains(i, 0))

        @pl.when(i >= 1)
        def _():
            step(sc, block_chains(i - 1, 1))

        n_far = jnp.maximum(i - 1, 0)

        def far_blocks(j, count):
            chains = []
            for k in range(count):
                chains += block_chains(j + k, None)
            step(sc, chains)

        if step is _flash_two_pass:
            lax.fori_loop(0, n_far, lambda j, carry: (far_blocks(j, 1), carry)[1], 0)
            return

        def far_group_step(t, carry):
            far_blocks(far_group * t, far_group)
            return carry

        n_group = n_far // far_group
        lax.fori_loop(0, n_group, far_group_step, 0)
        done = n_group * far_group
        size = far_group // 2
        while size >= 1:
            has_piece = (n_far & size) != 0

            @pl.when(has_piece)
            def _(done=done, size=size):
                far_blocks(done, size)

            done = done + jnp.where(has_piece, size, 0)
            size //= 2

    run(_flash_lagged)
    worst = sc.over[0]
    for c in range(1, sc.n_chain):
        worst = jnp.maximum(worst, sc.over[c])

    @pl.when(jnp.max(worst) > FLASH_OVERFLOW)
    def _():
        run(_flash_two_pass)


def _diff_kernel(lam_init, qt_ref, k_ref, vt_ref, tile_ref, lq1_ref, lk1_ref, lq2_ref, lk2_ref,
                 subln_ref, o_ref, *scratch):
    i = pl.program_id(1)
    sc = _FlashScratch(2 * N_HEADS, *scratch)
    acc_scr = sc.acc
    for h in range(N_HEADS):
        for mp in range(2):
            r0 = h * HEAD_DIM + mp * DIFF_QK
            sc.qpad[2 * h + mp] = _pad_rows(qt_ref[0, r0:r0 + DIFF_QK, :],
                                            (h % 2) * HEAD_DIM + mp * DIFF_QK, LANES)

    def block_chains(j, tile_idx):
        row = pl.multiple_of(j * TB, TB)
        chains = []
        for h in range(N_HEADS):
            g = h // 2
            kblk = k_ref[pl.ds(row, TB), g * LANES:(g + 1) * LANES]
            add = None if tile_idx is None else tile_ref[h, tile_idx]
            vext = _with_ones(vt_ref[j, h * HEAD_DIM:(h + 1) * HEAD_DIM, :])
            chains += [(2 * h + mp, kblk, add, vext, None) for mp in range(2)]
        return chains

    _flash_causal(sc, i, block_chains, far_group=8)

    lam =(jnp.exp(jnp.sum(lq1_ref[...] * lk1_ref[...], keepdims=True))
           - jnp.exp(jnp.sum(lq2_ref[...] * lk2_ref[...], keepdims=True)) + lam_init)
    for h in range(N_HEADS):
        a1 = acc_scr[2 * h]
        a2 = acc_scr[2 * h + 1]
        o1 = a1[:HEAD_DIM] / a1[HEAD_DIM:HEAD_DIM + 1]
        o2 = a2[:HEAD_DIM] / a2[HEAD_DIM:HEAD_DIM + 1]
        a = o1 - lam * o2
        ms = jnp.mean(a * a, axis=0, keepdims=True)
        y = a * lax.rsqrt(ms + NORM_EPS) * subln_ref[...] * (1.0 - lam_init)
        o_ref[0, h * HEAD_DIM:(h + 1) * HEAD_DIM, :] = y.astype(BF16)


def _attn_common_specs(nb, q_blk, k_blk, v_blk):
    return [pl.BlockSpec((1, TB, TB), lambda b, i: (b * nb + i, q_blk, 0)),
            pl.BlockSpec((nb * TB, TB), lambda b, i: (b, k_blk)),
            pl.BlockSpec((nb, TB, TB), lambda b, i: (b, v_blk, 0))]


def _small_spec(shape):
    return pl.BlockSpec(shape, lambda b, i: (0,) * len(shape))


def _diff_attention(kn, qvt, tiles, lq1, lk1, lq2, lk2, subln, lam_init, batch):
    nblk = qvt.shape[0]
    nb = nblk // batch
    n_chain = 2 * N_HEADS
    return pl.pallas_call(
        functools.partial(_diff_kernel, lam_init),
        grid=(batch, nb),
        in_specs=_attn_common_specs(nb, QV_DIFF_Q, KN_DIFF, QV_DIFF_V) + [
            _small_spec((N_HEADS, 2, TB, TB)),
            _small_spec((1, DIFF_QK)), _small_spec((1, DIFF_QK)),
            _small_spec((1, DIFF_QK)), _small_spec((1, DIFF_QK)),
            _small_spec((HEAD_DIM, 1))],
        out_specs=pl.BlockSpec((1, TB, TB), lambda b, i: (b * nb + i, 0, 0)),
        out_shape=jax.ShapeDtypeStruct((nblk, TB, TB), BF16),
        scratch_shapes=_FlashScratch.shapes(n_chain),
        compiler_params=_cparams(2),
        name="diff_attn",
    )(qvt, kn, qvt, tiles, lq1, lk1, lq2, lk2, subln)


def _moba_kernel(nb, nbp, qt_ref, k_ref, vt_ref, tile_ref, o_ref, kmean_scr, sel_scr, *scratch):
    i = pl.program_id(1)
    sc = _FlashScratch(N_HEADS, *scratch)
    qpad_scr, acc_scr = sc.qpad, sc.acc

    @pl.when(i == 0)
    def _():
        kmean_scr[...] = jnp.zeros(kmean_scr.shape, F32)
        for jb in range(nb):
            blk = k_ref[jb * TB:(jb + 1) * TB, :].astype(F32)
            kmean_scr[jb:jb + 1, :] = jnp.mean(blk, axis=0, keepdims=True)

    for h in range(N_HEADS):
        qpad_scr[h] = _pad_rows(qt_ref[0, h * HEAD_DIM:(h + 1) * HEAD_DIM, :], (h % 2) * HEAD_DIM, LANES)

    blk_id = lax.broadcasted_iota(jnp.int32, (nbp, TB), 0)
    for h in range(N_HEADS):
        g = h // 2
        km = kmean_scr[:, g * LANES:(g + 1) * LANES].astype(BF16)
        gate = jnp.dot(km, qpad_scr[h], preferred_element_type=F32)
        avail = blk_id < i
        sel = jnp.zeros((nbp, TB), jnp.bool_)
        for _ in range(MOBA_TOPK):
            gm = jnp.where(avail, gate, -jnp.inf)
            best = jnp.max(gm, axis=0, keepdims=True)
            is_best = avail & (gm == best)
            first = jnp.min(jnp.where(is_best, blk_id, nbp), axis=0, keepdims=True)
            pick = blk_id == first
            sel = sel | pick
            avail = avail & jnp.logical_not(pick)
        sel_scr[h] = jnp.where(sel, 0.0, NEG).astype(F32)

    def block_chains(j, tile_idx):
        row = pl.multiple_of(j * TB, TB)
        chains = []
        for h in range(N_HEADS):
            g = h // 2
            kblk = k_ref[pl.ds(row, TB), g * LANES:(g + 1) * LANES]
            add = None if tile_idx is None else tile_ref[h, tile_idx]
            colmask = None if tile_idx == 0 else sel_scr[h, pl.ds(j, 1), :]
            chains.append((h, kblk, add, _with_ones(vt_ref[j, h * HEAD_DIM:(h + 1) * HEAD_DIM, :]), colmask))
        return chains

    _flash_causal(sc, i, block_chains, far_group=16)

    for h in range(N_HEADS):
        a = acc_scr[h]
        o_ref[0, h * HEAD_DIM:(h + 1) * HEAD_DIM, :] = (a[:HEAD_DIM] / a[HEAD_DIM:HEAD_DIM + 1]).astype(BF16)


def _moba_attention(kn, qvt, tiles, batch):
    nblk = qvt.shape[0]
    nb = nblk // batch
    nbp = max(8, -(-nb // 8) * 8)
    return pl.pallas_call(
        functools.partial(_moba_kernel, nb, nbp),
        grid=(batch, nb),
        in_specs=_attn_common_specs(nb, QV_MOBA_Q, KN_MOBA, QV_MOBA_V) + [
            _small_spec((N_HEADS, 2, TB, TB))],
        out_specs=pl.BlockSpec((1, TB, TB), lambda b, i: (b * nb + i, 0, 0)),
        out_shape=jax.ShapeDtypeStruct((nblk, TB, TB), BF16),
        scratch_shapes=[pltpu.VMEM((nbp, TB), F32),
                        pltpu.VMEM((N_HEADS, nbp, TB), F32)] + _FlashScratch.shapes(N_HEADS),
        compiler_params=_cparams(2),
        name="moba_attn",
    )(qvt, kn, qvt, tiles)


def _sb_kernel(qt_ref, k_ref, vt_ref, o_ref, qpad_scr, acc_scr, c_scr):
    i = pl.program_id(1)
    rows = lax.broadcasted_iota(jnp.int32, (TB, TB), 0)
    cols = lax.broadcasted_iota(jnp.int32, (TB, TB), 1)
    upper = (cols > rows).astype(BF16)
    strict = cols > rows

    heads = range(N_HEADS)

    def step(j, first):
        row = pl.multiple_of(j * TB, TB)
        zs = [jnp.dot(k_ref[pl.ds(row, TB), (h // 2) * LANES:(h // 2 + 1) * LANES], qpad_scr[h],
                      preferred_element_type=F32) for h in heads]
        log_keeps = []
        for z in zs:
            lk = -(jnp.maximum(z, 0.0) + jnp.log2(1.0 + jnp.exp2(-jnp.abs(z))))
            log_keeps.append(jnp.where(strict, lk, 0.0) if first else lk)
        laters = []
        for h in heads:
            hi = log_keeps[h].astype(BF16)
            lo = (log_keeps[h] - hi.astype(F32)).astype(BF16)
            later = (jnp.dot(upper, hi, preferred_element_type=F32)
                     + jnp.dot(upper, lo, preferred_element_type=F32))
            laters.append(later if first else later + c_scr[h])
        c_max = None
        for h in heads:
            w = jnp.exp2(zs[h] + log_keeps[h] + laters[h])
            if first:
                w = jnp.where(strict, w, 0.0)
            pv = jnp.dot(vt_ref[j, h * HEAD_DIM:(h + 1) * HEAD_DIM, :], w.astype(BF16),
                         preferred_element_type=F32)
            acc_scr[h] = pv if first else acc_scr[h] + pv
            c_new = laters[h][0:1, :] + log_keeps[h][0:1, :]
            c_scr[h] = c_new
            c_max = c_new if c_max is None else jnp.maximum(c_max, c_new)
        return jnp.max(c_max) > -SB_EXIT * LOG2E

    for h in heads:
        qpad_scr[h] = _pad_rows(qt_ref[0, h * HEAD_DIM:(h + 1) * HEAD_DIM, :], (h % 2) * HEAD_DIM, LANES)

    def cond(carry):
        j, go = carry
        return jnp.logical_and(j >= 0, go)

    def body(carry):
        j, _ = carry
        return j - 1, step(j, False)

    lax.while_loop(cond, body, (i - 1, step(i, True)))
    for h in heads:
        o_ref[0, h * HEAD_DIM:(h + 1) * HEAD_DIM, :] = acc_scr[h].astype(BF16)


def _sb_attention(kn, qvt, batch):
    nblk = qvt.shape[0]
    nb = nblk // batch
    return pl.pallas_call(
        _sb_kernel,
        grid=(batch, nb),
        in_specs=_attn_common_specs(nb, QV_SB_Q, KN_SB, QV_SB_V),
        out_specs=pl.BlockSpec((1, TB, TB), lambda b, i: (b * nb + i, 0, 0)),
        out_shape=jax.ShapeDtypeStruct((nblk, TB, TB), BF16),
        scratch_shapes=[pltpu.VMEM((N_HEADS, LANES, TB), BF16),
                        pltpu.VMEM((N_HEADS, HEAD_DIM, TB), F32),
                        pltpu.VMEM((N_HEADS, 1, TB), F32)],
        compiler_params=_cparams(2),
        name="sb_attn",
    )(qvt, kn, qvt)


def _swa_kernel(sink_ref, qt_ref, kc_ref, kp_ref, vc_ref, vp_ref, tc_ref, tp_ref, o_ref):
    i = pl.program_id(1)
    kc = kc_ref[...]
    kp = kp_ref[...]
    vc = vc_ref[0]
    vp = vp_ref[0][:, TB - SWA_WINDOW:]
    no_prev = jnp.where(i > 0, 0.0, NEG)
    group = N_HEADS // SWA_KV_HEADS
    heads = range(N_HEADS)
    qpads = [_pad_rows(qt_ref[0, h * HEAD_DIM:(h + 1) * HEAD_DIM, :], (h // group) * HEAD_DIM, LANES)
             for h in heads]
    s_cur = [jnp.dot(kc, qpads[h], preferred_element_type=F32) + tc_ref[h] for h in heads]
    s_prev = [jnp.dot(kp, qpads[h], preferred_element_type=F32) + tp_ref[h] + no_prev for h in heads]
    ms = [jnp.maximum(jnp.maximum(jnp.max(s_cur[h], axis=0, keepdims=True),
                                  jnp.max(s_prev[h], axis=0, keepdims=True)), sink_ref[h]) for h in heads]
    p_cur = [jnp.exp(s_cur[h] - ms[h]).astype(BF16) for h in heads]
    p_prev = [jnp.exp(s_prev[h] - ms[h]).astype(BF16) for h in heads]
    for h in heads:
        kv = h // group
        o = (jnp.dot(_with_ones(vc[kv * HEAD_DIM:(kv + 1) * HEAD_DIM]), p_cur[h], preferred_element_type=F32)
             + jnp.dot(_with_ones(vp[kv * HEAD_DIM:(kv + 1) * HEAD_DIM]), p_prev[h],
                       preferred_element_type=F32))
        denom = o[HEAD_DIM:HEAD_DIM + 1] + jnp.exp(sink_ref[h] - ms[h])
        o_ref[0, h * HEAD_DIM:(h + 1) * HEAD_DIM, :] = (o[:HEAD_DIM] / denom).astype(BF16)


def _swa_attention(kn, qvt, tile_cur, tile_prev, sinks, batch):
    nblk = qvt.shape[0]
    nb = nblk // batch
    half = TB // SWA_WINDOW
    kv_rows = SWA_KV_HEADS * HEAD_DIM
    return pl.pallas_call(
        _swa_kernel,
        grid=(batch, nb),
        in_specs=[pl.BlockSpec(memory_space=pltpu.SMEM),
                  pl.BlockSpec((1, TB, TB), lambda b, i: (b * nb + i, QV_SWA_Q, 0)),
                  pl.BlockSpec((TB, kv_rows), lambda b, i: (b * nb + i, KN_SWA_128)),
                  pl.BlockSpec((SWA_WINDOW, kv_rows),
                               lambda b, i: (b * nb * half + jnp.maximum(half * i - 1, 0), KN_SWA_128)),
                  pl.BlockSpec((1, kv_rows, TB), lambda b, i: (b * nb + i, QV_SWA_V_128, 0)),
                  pl.BlockSpec((1, kv_rows, TB), lambda b, i: (b * nb + jnp.maximum(i - 1, 0), QV_SWA_V_128, 0)),
                  _small_spec((N_HEADS, TB, TB)),
                  _small_spec((N_HEADS, SWA_WINDOW, TB))],
        out_specs=pl.BlockSpec((1, TB, TB), lambda b, i: (b * nb + i, 0, 0)),
        out_shape=jax.ShapeDtypeStruct((nblk, TB, TB), BF16),
        compiler_params=_cparams(2),
        name="swa_attn",
    )(sinks, qvt, kn, kn, qvt, qvt, tile_cur, tile_prev)


MERGE_T = 512


def _merge_kernel(x_ref, g_ref, oa_ref, ob_ref, oc_ref, od_ref, wg_ref, wbr_ref, wo_ref, x1_ref):
    x = x_ref[...]
    h = _rms(x, g_ref[...]).astype(BF16)
    merged = None
    for bi, o_ref in enumerate((oa_ref, ob_ref, oc_ref, od_ref)):
        gate = jax.nn.sigmoid(jnp.dot(h, wg_ref[bi], preferred_element_type=F32))
        branch = jnp.concatenate(
            [lax.dot_general(o_ref[s], wbr_ref[bi], (((0,), (0,)), ((), ())), preferred_element_type=F32)
             for s in range(MERGE_T // TB)], axis=0)
        term = gate * branch
        merged = term if merged is None else merged + term
    x1_ref[...] = x + jnp.dot(merged.astype(BF16), wo_ref[...], preferred_element_type=F32)


def _merge(x2, g, o_a, o_b, o_c, o_d, wg, wbr, wo):
    n, d = x2.shape
    o_spec = pl.BlockSpec((MERGE_T // TB, TB, TB), lambda i: (i, 0, 0))
    return pl.pallas_call(
        _merge_kernel,
        grid=(n // MERGE_T,),
        in_specs=[pl.BlockSpec((MERGE_T, d), lambda i: (i, 0)),
                  pl.BlockSpec((1, d), lambda i: (0, 0)),
                  o_spec, o_spec, o_spec, o_spec,
                  pl.BlockSpec(wg.shape, lambda i: (0, 0, 0)),
                  pl.BlockSpec(wbr.shape, lambda i: (0, 0, 0)),
                  pl.BlockSpec(wo.shape, lambda i: (0, 0))],
        out_specs=pl.BlockSpec((MERGE_T, d), lambda i: (i, 0)),
        out_shape=jax.ShapeDtypeStruct((n, d), F32),
        compiler_params=_cparams(1),
        name="merge",
    )(x2, g, o_a, o_b, o_c, o_d, wg, wbr, wo)


ROUTER_T = 512
ROUTER_ROWS = 8 + N_EXPERTS


def _first_argmax_rows(v, n_rows):
    best = jnp.max(v, axis=0, keepdims=True)
    ids = lax.broadcasted_iota(jnp.int32, v.shape, 0)
    return best, jnp.min(jnp.where(v == best, ids, n_rows), axis=0, keepdims=True)


def _router_kernel(x_ref, g_ref, whi_ref, wlo_ref, b_ref, h2_ref, eid_ref, gate_ref, rank_ref, cnt_ref,
                   base_scr):
    i = pl.program_id(0)

    @pl.when(i == 0)
    def _():
        base_scr[...] = jnp.zeros(base_scr.shape, F32)

    h2 = _rms(x_ref[...], g_ref[...])
    h2_ref[...] = _pack_bf16_pairs(h2)
    h_hi = h2.astype(BF16)
    h_lo = (h2 - h_hi.astype(F32)).astype(BF16)
    nt = (((1,), (1,)), ((), ()))
    logits = (lax.dot_general(whi_ref[...], h_hi, nt, preferred_element_type=F32)
              + lax.dot_general(whi_ref[...], h_lo, nt, preferred_element_type=F32)
              + lax.dot_general(wlo_ref[...], h_hi, nt, preferred_element_type=F32)
              + b_ref[...])
    gl = logits[0:8]
    gmax, grp = _first_argmax_rows(gl, 8)
    p_grp = 1.0 / jnp.sum(jnp.exp(gl - gmax), axis=0, keepdims=True)
    e_sel = jnp.zeros((EXPERTS_PER_GROUP, ROUTER_T), F32)
    for g in range(N_GROUPS):
        e_sel = jnp.where(grp == g, logits[8 + 8 * g:16 + 8 * g], e_sel)
    ids8 = lax.broadcasted_iota(jnp.int32, e_sel.shape, 0)
    v1, i1 = _first_argmax_rows(e_sel, EXPERTS_PER_GROUP)
    e_rest = jnp.where(ids8 == i1, -jnp.inf, e_sel)
    v2, i2 = _first_argmax_rows(e_rest, EXPERTS_PER_GROUP)
    r = jnp.exp(v2 - v1)
    s1 = 1.0 / (1.0 + r)
    gate_ref[0:1, :] = p_grp * s1
    gate_ref[1:2, :] = p_grp * (r * s1)
    e1 = grp * EXPERTS_PER_GROUP + i1
    e2 = grp * EXPERTS_PER_GROUP + i2
    eid_ref[0:1, :] = e1
    eid_ref[1:2, :] = e2

    ids_e = lax.broadcasted_iota(jnp.int32, (N_EXPERTS, ROUTER_T), 0)
    oh1 = ids_e == e1
    oh2 = ids_e == e2
    cnt = oh1.astype(F32) + oh2.astype(F32)
    tr = lax.broadcasted_iota(jnp.int32, (ROUTER_T, ROUTER_T), 0)
    tc = lax.broadcasted_iota(jnp.int32, (ROUTER_T, ROUTER_T), 1)
    before = (tr < tc).astype(BF16)
    prefix = jnp.dot(cnt.astype(BF16), before, preferred_element_type=F32) + base_scr[:, 0:1]
    rank_ref[0:1, :] = jnp.sum(jnp.where(oh1, prefix, 0.0), axis=0, keepdims=True).astype(jnp.int32)
    rank_ref[1:2, :] = jnp.sum(jnp.where(oh2, prefix, 0.0), axis=0, keepdims=True).astype(jnp.int32)
    base_scr[...] = base_scr[...] + jnp.sum(cnt, axis=1, keepdims=True)
    cnt_ref[...] = base_scr[...]


def _router(x1, g, w_hi, w_lo, bias):
    n, d = x1.shape
    row2 = lambda dt: jax.ShapeDtypeStruct((2, n), dt)
    spec2 = pl.BlockSpec((2, ROUTER_T), lambda i: (0, i))
    return pl.pallas_call(
        _router_kernel,
        grid=(n // ROUTER_T,),
        in_specs=[pl.BlockSpec((ROUTER_T, d), lambda i: (i, 0)),
                  pl.BlockSpec((1, d), lambda i: (0, 0)),
                  pl.BlockSpec((ROUTER_ROWS, d), lambda i: (0, 0)),
                  pl.BlockSpec((ROUTER_ROWS, d), lambda i: (0, 0)),
                  pl.BlockSpec((ROUTER_ROWS, 1), lambda i: (0, 0))],
        out_specs=[pl.BlockSpec((ROUTER_T, d // 2), lambda i: (i, 0)), spec2, spec2, spec2,
                   pl.BlockSpec((N_EXPERTS, LANES), lambda i: (0, 0))],
        out_shape=[jax.ShapeDtypeStruct((n, d // 2), jnp.uint32), row2(jnp.int32), row2(F32), row2(jnp.int32),
                   jax.ShapeDtypeStruct((N_EXPERTS, LANES), F32)],
        scratch_shapes=[pltpu.VMEM((N_EXPERTS, LANES), F32)],
        compiler_params=_cparams(1),
        name="router",
    )(x1, g, w_hi, w_lo, bias)


ROW_T = 256


def _dispatch_kernel(dest_ref, h2_ref, xin_ref, xbuf_ref, sem):
    del xin_ref

    def row_copy(r, k):
        return pltpu.make_async_copy(h2_ref.at[pl.ds(r, 1)], xbuf_ref.at[pl.ds(dest_ref[k, r], 1)], sem)

    def issue(r, carry):
        row_copy(r, 0).start()
        row_copy(r, 1).start()
        return carry

    def drain(r, carry):
        row_copy(r, 0).wait()
        row_copy(r, 1).wait()
        return carry

    lax.fori_loop(0, ROW_T, issue, 0, unroll=8)
    lax.fori_loop(0, ROW_T, drain, 0, unroll=8)


def _dispatch(dest3, h2, xbuf_init):
    n, d = h2.shape
    return pl.pallas_call(
        _dispatch_kernel,
        grid=(n // ROW_T,),
        in_specs=[pl.BlockSpec((None, 2, ROW_T), lambda i: (i, 0, 0), memory_space=pltpu.SMEM),
                  pl.BlockSpec((ROW_T, d), lambda i: (i, 0)),
                  pl.BlockSpec(memory_space=pl.ANY)],
        out_specs=pl.BlockSpec(memory_space=pl.ANY),
        out_shape=jax.ShapeDtypeStruct(xbuf_init.shape, xbuf_init.dtype),
        scratch_shapes=[pltpu.SemaphoreType.DMA(())],
        input_output_aliases={2: 0},
        compiler_params=_cparams(1),
        name="dispatch",
    )(dest3, h2, xbuf_init)


def _expert_kernel(be_ref, nu_ref, nv_ref, x_ref, w1_ref, w3_ref, w2_ref, y_ref):
    del be_ref
    used = pl.program_id(0) < nu_ref[0]

    @pl.when(used)
    def _():
        live = lax.broadcasted_iota(jnp.int32, x_ref.shape, 0) < nv_ref[pl.program_id(0)]
        xb = _unpack_bf16_pairs(jnp.where(live, x_ref[...], jnp.uint32(0))).astype(BF16)
        a = jnp.dot(xb, w1_ref[...].astype(BF16), preferred_element_type=F32)
        b = jnp.dot(xb, w3_ref[...].astype(BF16), preferred_element_type=F32)
        mid = (a * jax.nn.sigmoid(a) * b).astype(BF16)
        y_ref[...] = _pack_bf16_pairs(jnp.dot(mid, w2_ref[...].astype(BF16), preferred_element_type=F32))

    @pl.when(jnp.logical_not(used))
    def _():
        y_ref[...] = jnp.zeros(y_ref.shape, jnp.uint32)


def _experts(blk_expert, n_used, n_valid, xbuf, w1, w3, w2, layer):
    r = xbuf.shape[0]
    d = w1.shape[-2]
    de = w1.shape[-1]
    assert xbuf.shape[1] * 2 == d
    row_map = lambda i, be, nu, nv: (jnp.minimum(i, nu[0] - 1), 0)
    grid_spec = pltpu.PrefetchScalarGridSpec(
        num_scalar_prefetch=3,
        grid=(r // MOE_ROWS,),
        in_specs=[pl.BlockSpec((MOE_ROWS, d // 2), row_map),
                  pl.BlockSpec((None, None, d, de), lambda i, be, nu, nv: (layer, be[i], 0, 0)),
                  pl.BlockSpec((None, None, d, de), lambda i, be, nu, nv: (layer, be[i], 0, 0)),
                  pl.BlockSpec((None, None, de, d), lambda i, be, nu, nv: (layer, be[i], 0, 0))],
        out_specs=pl.BlockSpec((MOE_ROWS, d // 2), lambda i, be, nu, nv: (i, 0)),
    )
    return pl.pallas_call(
        _expert_kernel,
        grid_spec=grid_spec,
        out_shape=jax.ShapeDtypeStruct((r, d // 2), jnp.uint32),
        compiler_params=_cparams(1),
        name="experts",
    )(blk_expert, n_used, n_valid, xbuf, w1, w3, w2)


def _combine_kernel(final, dest_ref, gate_ref, x1_ref, gf_ref, y_ref, out_ref, buf, sem):
    def row_copy(r, k):
        return pltpu.make_async_copy(y_ref.at[pl.ds(dest_ref[k, r], 1)], buf.at[k, pl.ds(r, 1)], sem)

    def issue(r, carry):
        row_copy(r, 0).start()
        row_copy(r, 1).start()
        return carry

    def drain(r, carry):
        row_copy(r, 0).wait()
        row_copy(r, 1).wait()
        return carry

    lax.fori_loop(0, ROW_T, issue, 0, unroll=8)
    lax.fori_loop(0, ROW_T, drain, 0, unroll=8)
    gate = gate_ref[...]
    out = x1_ref[...] + gate[:, 0:1] * buf[0] + gate[:, 1:2] * buf[1]
    if final:
        out = _rms(out, gf_ref[...])
    out_ref[...] = out


def _combine(dest3, gate_t, x1, g_final, y, final):
    n, d = x1.shape
    return pl.pallas_call(
        functools.partial(_combine_kernel, final),
        grid=(n // ROW_T,),
        in_specs=[pl.BlockSpec((None, 2, ROW_T), lambda i: (i, 0, 0), memory_space=pltpu.SMEM),
                  pl.BlockSpec((ROW_T, 2), lambda i: (i, 0)),
                  pl.BlockSpec((ROW_T, d), lambda i: (i, 0)),
                  pl.BlockSpec((1, d), lambda i: (0, 0)),
                  pl.BlockSpec(memory_space=pl.ANY)],
        out_specs=pl.BlockSpec((ROW_T, d), lambda i: (i, 0)),
        out_shape=jax.ShapeDtypeStruct((n, d), F32),
        scratch_shapes=[pltpu.VMEM((2, ROW_T, d), F32), pltpu.SemaphoreType.DMA(())],
        compiler_params=_cparams(1),
        name="combine",
    )(dest3, gate_t, x1, g_final, y)


SC_CORES = 2
SC_SUBCORES = 16
SC_ROWS = 64


def _sc_gather_rows(table, idx):
    n_idx = idx.shape[0]
    d = table.shape[1]
    n_workers = SC_CORES * SC_SUBCORES
    per_worker = n_idx // n_workers
    n_chunk = per_worker // SC_ROWS
    assert per_worker * n_workers == n_idx and n_chunk * SC_ROWS == per_worker and n_chunk % 2 == 0
    mesh = plsc.VectorSubcoreMesh(core_axis_name="c", subcore_axis_name="s",
                                  num_cores=SC_CORES, num_subcores=SC_SUBCORES)

    def body(table_hbm, idx_hbm, out_hbm, idx_v, rows_v, gsem, wsem):
        worker = lax.axis_index("s") * SC_CORES + lax.axis_index("c")
        base = worker * per_worker
        pltpu.sync_copy(idx_hbm.at[pl.ds(base, per_worker)], idx_v)

        def gather(c, b):
            return pltpu.make_async_copy(table_hbm.at[idx_v.at[pl.ds(c * SC_ROWS, SC_ROWS)]],
                                         rows_v.at[b], gsem.at[b])

        def put(c, b):
            return pltpu.make_async_copy(rows_v.at[b], out_hbm.at[pl.ds(base + c * SC_ROWS, SC_ROWS)],
                                         wsem.at[b])

        gather(0, 0).start()

        @pl.loop(0, n_chunk, step=2)
        def _(c0):
            for b in range(2):
                c = c0 + b
                gather(c, b).wait()

                @pl.when(c + 1 < n_chunk)
                def _():
                    @pl.when(c >= 1)
                    def _():
                        put(c - 1, 1 - b).wait()

                    gather(c + 1, 1 - b).start()

                put(c, b).start()

        put(n_chunk - 2, 0).wait()
        put(n_chunk - 1, 1).wait()

    return pl.kernel(
        body,
        out_type=jax.ShapeDtypeStruct((n_idx, d), table.dtype),
        mesh=mesh,
        scratch_types=[pltpu.VMEM((per_worker,), jnp.int32),
                       pltpu.VMEM((2, SC_ROWS, d), table.dtype),
                       pltpu.SemaphoreType.DMA((2,)),
                       pltpu.SemaphoreType.DMA((2,))],
        name="sc_gather_rows",
    )(table, idx)


def _sc_scatter_rows(rows, dest, n_out):
    n, d = rows.shape
    n_workers = SC_CORES * SC_SUBCORES
    per_worker = n // n_workers
    n_chunk = per_worker // SC_ROWS
    assert per_worker * n_workers == n and n_chunk * SC_ROWS == per_worker and n_chunk % 2 == 0
    mesh = plsc.VectorSubcoreMesh(core_axis_name="c", subcore_axis_name="s",
                                  num_cores=SC_CORES, num_subcores=SC_SUBCORES)
    dest3 = dest.reshape(2, n // SC_ROWS, SC_ROWS)

    def body(rows_hbm, idx_hbm, out_hbm, idx_v, rows_v, lsem, ssem):
        worker = lax.axis_index("s") * SC_CORES + lax.axis_index("c")
        for k in range(2):
            pltpu.sync_copy(idx_hbm.at[k, pl.ds(worker * n_chunk, n_chunk)], idx_v.at[k])

        def load(c, b):
            return pltpu.make_async_copy(rows_hbm.at[pl.ds(worker * per_worker + c * SC_ROWS, SC_ROWS)],
                                         rows_v.at[b], lsem.at[b])

        def scatter(c, b, k):
            return pltpu.make_async_copy(rows_v.at[b], out_hbm.at[idx_v.at[k, c]], ssem.at[b])

        load(0, 0).start()

        @pl.loop(0, n_chunk, step=2)
        def _(c0):
            for b in range(2):
                c = c0 + b
                load(c, b).wait()

                @pl.when(c + 1 < n_chunk)
                def _():
                    @pl.when(c >= 1)
                    def _():
                        scatter(c - 1, 1 - b, 0).wait()
                        scatter(c - 1, 1 - b, 1).wait()

                    load(c + 1, 1 - b).start()

                scatter(c, b, 0).start()
                scatter(c, b, 1).start()

        for c, b in ((n_chunk - 2, 0), (n_chunk - 1, 1)):
            scatter(c, b, 0).wait()
            scatter(c, b, 1).wait()

    return pl.kernel(
        body,
        out_type=jax.ShapeDtypeStruct((n_out, d), rows.dtype),
        mesh=mesh,
        scratch_types=[pltpu.VMEM((2, n_chunk, SC_ROWS), jnp.int32),
                       pltpu.VMEM((2, SC_ROWS, d), rows.dtype),
                       pltpu.SemaphoreType.DMA((2,)),
                       pltpu.SemaphoreType.DMA((2,))],
        name="sc_scatter_rows",
    )(rows, dest3)


def _combine_dense_kernel(final, gate_ref, x1_ref, gf_ref, y0_ref, y1_ref, out_ref):
    gate = gate_ref[...]
    out = (x1_ref[...] + gate[:, 0:1] * _unpack_bf16_pairs(y0_ref[...])
           + gate[:, 1:2] * _unpack_bf16_pairs(y1_ref[...]))
    if final:
        out = _rms(out, gf_ref[...])
    out_ref[...] = out


def _combine_dense(gate_t, x1, g_final, yg, final):
    n, d = x1.shape
    nb = n // MERGE_T
    return pl.pallas_call(
        functools.partial(_combine_dense_kernel, final),
        grid=(nb,),
        in_specs=[pl.BlockSpec((MERGE_T, 2), lambda i: (i, 0)),
                  pl.BlockSpec((MERGE_T, d), lambda i: (i, 0)),
                  pl.BlockSpec((1, d), lambda i: (0, 0)),
                  pl.BlockSpec((MERGE_T, d // 2), lambda i: (i, 0)),
                  pl.BlockSpec((MERGE_T, d // 2), lambda i: (i + nb, 0))],
        out_specs=pl.BlockSpec((MERGE_T, d), lambda i: (i, 0)),
        out_shape=jax.ShapeDtypeStruct((n, d), F32),
        compiler_params=_cparams(1),
        name="combine_dense",
    )(gate_t, x1, g_final, yg, yg)


def _projection_weights(w):
    d = w.shape[0]
    blk = N_HEADS * HEAD_DIM
    kv = SWA_KV_HEADS * HEAD_DIM
    pa, pb, pc, pd = w[:, :3 * blk], w[:, 3 * blk:6 * blk], w[:, 6 * blk:9 * blk], w[:, 9 * blk:]
    half = N_HEADS * DIFF_QK

    def per_head(a, b):
        return jnp.stack([a.reshape(d, N_HEADS, DIFF_QK), b.reshape(d, N_HEADS, DIFF_QK)], axis=2).reshape(d, blk)

    s64, s32 = HEAD_DIM ** -0.5, DIFF_QK ** -0.5
    wn = jnp.concatenate([pa[:, blk:2 * blk], per_head(pb[:, 2 * half:3 * half], pb[:, 3 * half:4 * half]),
                          pc[:, blk:2 * blk], pd[:, blk:blk + kv]], axis=1)
    wt = jnp.concatenate([pa[:, :blk] * (s64 * LOG2E), pa[:, 2 * blk:],
                          per_head(pb[:, :half], pb[:, half:2 * half]) * (s32 * LOG2E), pb[:, 4 * half:],
                          pc[:, :blk] * (s64 * LOG2E), pc[:, 2 * blk:],
                          pd[:, :blk] * s64, pd[:, blk + kv:]], axis=1)
    assert wn.shape[1] == KN_COLS and wt.shape[1] == QV_ROWS
    return wn.astype(BF16), wt.T.astype(BF16)


def _router_weights(w_rg, b_rg, w_re, b_re):
    d = w_rg.shape[0]
    w = jnp.concatenate([w_rg.T, jnp.zeros((8 - N_GROUPS, d), F32), w_re.T], axis=0)
    b = jnp.concatenate([b_rg.astype(F32), jnp.full((8 - N_GROUPS,), NEG, F32), b_re.astype(F32)])[:, None]
    w_hi = w.astype(BF16)
    w_lo = (w - w_hi.astype(F32)).astype(BF16)
    return w_hi, w_lo, b


def _moe_plan(eid, rank, counts, n_rows_total):
    padded = (counts + MOE_ROWS - 1) // MOE_ROWS * MOE_ROWS
    pad_end = jnp.cumsum(padded)
    pad_start = pad_end - padded
    experts = jnp.arange(N_EXPERTS, dtype=jnp.int32)
    start_of = jnp.sum(jnp.where(eid[..., None] == experts, pad_start, 0), axis=-1)
    dest = start_of + rank
    n_blk = n_rows_total // MOE_ROWS
    n_used = (pad_end[-1] // MOE_ROWS).astype(jnp.int32)
    blk = jnp.minimum(jnp.arange(n_blk, dtype=jnp.int32), n_used - 1) * MOE_ROWS
    blk_expert = jnp.minimum(jnp.sum(pad_end[None, :] <= blk[:, None], axis=1), N_EXPERTS - 1).astype(jnp.int32)
    live_end = jnp.sum(jnp.where(blk_expert[:, None] == experts, pad_start + counts, 0), axis=-1)
    n_valid = jnp.clip(live_end - blk, 0, MOE_ROWS).astype(jnp.int32)
    return dest.astype(jnp.int32), blk_expert, n_used.reshape(1), n_valid


def kernel(x, rel_bias, g_mix, w_in, diff_lq1, diff_lk1, diff_lq2, diff_lk2, diff_subln, swa_sinks,
           w_gate, w_br, w_o, g_ffn, w_route_group, b_route_group, w_route_expert, b_route_expert,
           w1, w3, w2, g_final):
    batch, seq, d = x.shape
    n = batch * seq
    depth = w_in.shape[0]
    assert seq % TB == 0 and n % MERGE_T == 0 and TB == MOBA_BLOCK
    tab = rel_bias.T.astype(F32)
    tiles_moba = _causal_bias_tiles(tab[:N_HEADS])
    tiles_diff = _causal_bias_tiles(tab[N_HEADS:2 * N_HEADS])
    tile_cur, tile_prev = _swa_bias_tiles(tab[2 * N_HEADS:])
    n_rows_total = n * 2 + N_EXPERTS * MOE_ROWS
    row = lambda v: v.astype(F32)[None, :]

    x2 = x.reshape(n, d)
    for l in range(depth):
        lam_init = 0.8 - 0.6 * math.exp(-0.3 * l)
        wn, wt = _projection_weights(w_in[l])
        kn, qvt = _inproj(x2, row(g_mix[l]), wn, wt)
        o_a = _moba_attention(kn, qvt, tiles_moba, batch)
        o_b = _diff_attention(kn, qvt, tiles_diff, row(diff_lq1[l]), row(diff_lk1[l]), row(diff_lq2[l]),
                              row(diff_lk2[l]), diff_subln[l].astype(F32)[:, None], lam_init, batch)
        o_c = _sb_attention(kn, qvt, batch)
        o_d = _swa_attention(kn, qvt, tile_cur, tile_prev, swa_sinks[l].astype(F32), batch)
        x1 = _merge(x2, row(g_mix[l]), o_a, o_b, o_c, o_d, w_gate[l].astype(BF16), w_br[l].astype(BF16),
                    w_o[l].astype(BF16))
        w_hi, w_lo, r_bias = _router_weights(w_route_group[l], b_route_group[l], w_route_expert[l],
                                             b_route_expert[l])
        h2, eid, gate, rank, cnt = _router(x1, row(g_ffn[l]), w_hi, w_lo, r_bias)
        dest, blk_expert, n_used, n_valid = _moe_plan(eid, rank, cnt[:, 0].astype(jnp.int32), n_rows_total)
        xbuf = _sc_scatter_rows(h2, dest, n_rows_total)
        y = _experts(blk_expert, n_used, n_valid, xbuf, w1, w3, w2, l)
        yg = _sc_gather_rows(y, dest.reshape(-1))
        x2 = _combine_dense(gate.T, x1, row(g_final), yg, l == depth - 1)
    return x2.reshape(batch, seq, d)
```

```python
import functools
import math

import numpy as np
import jax
import jax.numpy as jnp
from jax import lax
from jax.experimental import pallas as pl
from jax.experimental.pallas import tpu as pltpu
from jax.experimental.pallas import tpu_sc as plsc

F32 = jnp.float32
BF16 = jnp.bfloat16

HEAD_DIM = 64
N_HEADS = 4
DIFF_QK = 32
SWA_KV_HEADS = 2
SWA_WINDOW = 128
MOBA_BLOCK = 256
MOBA_TOPK = 3
REL_BUCKETS = 32
REL_MAX_DIST = 128
N_GROUPS = 4
EXPERTS_PER_GROUP = 8
N_EXPERTS = N_GROUPS * EXPERTS_PER_GROUP
NORM_EPS = 1e-6

TB = 256
LANES = 128
ONES_ROWS = 16
NEG = -1e30
LOG2E = math.log2(math.e)
SB_EXIT = 104.0
MOE_ROWS = 512
VMEM_LIMIT = 56 * 1024 * 1024

QV_MOBA_Q, QV_MOBA_V, QV_DIFF_Q, QV_DIFF_V, QV_SB_Q, QV_SB_V, QV_SWA_Q = range(7)
QV_SWA_V_128 = 14
QV_ROWS = 7 * 256 + 128
KN_MOBA, KN_DIFF, KN_SB = range(3)
KN_SWA_128 = 6
KN_COLS = 3 * 256 + 128


def _cparams(n_grid):
    return pltpu.CompilerParams(dimension_semantics=("arbitrary",) * n_grid,
                                vmem_limit_bytes=VMEM_LIMIT)


def _rel_bucket_np(n):
    n = np.maximum(n, 0)
    max_exact = REL_BUCKETS // 2
    nf = np.maximum(n, 1).astype(np.float64)
    large = max_exact + (np.log(nf / max_exact) / math.log(REL_MAX_DIST / max_exact)
                         * (REL_BUCKETS - max_exact)).astype(np.int64)
    large = np.minimum(large, REL_BUCKETS - 1)
    return np.where(n < max_exact, n, large)


def _first_far_distance():
    d = np.arange(0, 4 * REL_MAX_DIST)
    b = _rel_bucket_np(d)
    return int(np.min(d[b == REL_BUCKETS - 1]))


def _toeplitz_bias(tab, rows, cols, base, valid_fn, shift_far, unit):
    length = rows + cols - 1
    off = np.concatenate([np.arange(0, cols), np.arange(cols - length, 0)])
    n = base + off
    onehot = np.zeros((REL_BUCKETS, length), np.float32)
    onehot[_rel_bucket_np(n), np.arange(length)] = 1.0
    vec = jnp.dot(tab, jnp.asarray(onehot), precision=lax.Precision.HIGHEST)
    if shift_far:
        vec = vec - tab[:, REL_BUCKETS - 1:]
    vec = jnp.where(jnp.asarray(valid_fn(n))[None, :], vec * unit, NEG).astype(F32)
    flat = jnp.tile(vec, (1, rows))[:, :rows * (length - 1)]
    return flat.reshape(tab.shape[0], rows, length - 1)[:, :, :cols]


def _causal_bias_tiles(tab):
    assert _first_far_distance() <= TB + 1
    tiles = [_toeplitz_bias(tab, TB, TB, d * TB, lambda n: n >= 0, True, LOG2E) for d in range(2)]
    return jnp.stack(tiles, axis=1)


def _swa_bias_tiles(tab):
    in_window = lambda n: (n >= 0) & (n < SWA_WINDOW)
    return (_toeplitz_bias(tab, TB, TB, 0, in_window, False, 1.0),
            _toeplitz_bias(tab, SWA_WINDOW, TB, SWA_WINDOW, in_window, False, 1.0))


def _pad_rows(q, off, total):
    n, t = q.shape
    parts = []
    if off:
        parts.append(jnp.zeros((off, t), q.dtype))
    parts.append(q)
    if total - off - n:
        parts.append(jnp.zeros((total - off - n, t), q.dtype))
    return jnp.concatenate(parts, axis=0) if len(parts) > 1 else q


def _with_ones(v):
    return jnp.concatenate([v, jnp.ones((ONES_ROWS, v.shape[1]), v.dtype)], axis=0)


def _pack_bf16_pairs(x):
    w = x.shape[1] // 2
    lo = lax.bitcast_convert_type(x[:, :w].astype(BF16).astype(F32), jnp.uint32)
    hi = lax.bitcast_convert_type(x[:, w:].astype(BF16).astype(F32), jnp.uint32)
    return hi | (lo >> 16)


def _unpack_bf16_pairs(u):
    lo = lax.bitcast_convert_type(u << 16, F32)
    hi = lax.bitcast_convert_type(u & jnp.uint32(0xFFFF0000), F32)
    return jnp.concatenate([lo, hi], axis=1)


def _rms(x, g_row):
    ms = jnp.mean(x * x, axis=-1, keepdims=True)
    return x * lax.rsqrt(ms + NORM_EPS) * g_row


IN_T = 512
IN_CHUNK = 384


def _inproj_kernel(x_ref, g_ref, wn_ref, wt_ref, kn_ref, qvt_ref):
    h = _rms(x_ref[...], g_ref[...]).astype(BF16)
    kn_ref[...] = jnp.dot(h, wn_ref[...], preferred_element_type=F32).astype(BF16)
    for r0 in range(0, QV_ROWS, IN_CHUNK):
        pt = lax.dot_general(wt_ref[r0:r0 + IN_CHUNK, :], h, (((1,), (1,)), ((), ())),
                             preferred_element_type=F32)
        for s in range(IN_T // TB):
            qvt_ref[s, r0:r0 + IN_CHUNK, :] = pt[:, s * TB:(s + 1) * TB].astype(BF16)


def _inproj(x2, g, wn, wt):
    n, d = x2.shape
    return pl.pallas_call(
        _inproj_kernel,
        grid=(n // IN_T,),
        in_specs=[pl.BlockSpec((IN_T, d), lambda i: (i, 0)),
                  pl.BlockSpec((1, d), lambda i: (0, 0)),
                  pl.BlockSpec((d, KN_COLS), lambda i: (0, 0)),
                  pl.BlockSpec((QV_ROWS, d), lambda i: (0, 0))],
        out_specs=[pl.BlockSpec((IN_T, KN_COLS), lambda i: (i, 0)),
                   pl.BlockSpec((IN_T // TB, QV_ROWS, TB), lambda i: (i, 0, 0))],
        out_shape=[jax.ShapeDtypeStruct((n, KN_COLS), BF16),
                   jax.ShapeDtypeStruct((n // TB, QV_ROWS, TB), BF16)],
        compiler_params=_cparams(1),
        name="inproj",
    )(x2, g, wn, wt)


FLASH_OVERFLOW = 100.0
FLASH_SKEW = 4


class _FlashScratch:
    def __init__(self, n_chain, qpad, m, acc, over, s):
        self.n_chain, self.qpad, self.m, self.acc, self.over, self.s = n_chain, qpad, m, acc, over, s

    @staticmethod
    def shapes(n_chain):
        return [pltpu.VMEM((n_chain, LANES, TB), BF16),
                pltpu.VMEM((n_chain, 1, TB), F32),
                pltpu.VMEM((n_chain, HEAD_DIM + ONES_ROWS, TB), F32),
                pltpu.VMEM((n_chain, 1, TB), F32),
                pltpu.VMEM((n_chain, TB, TB), F32)]


def _flash_two_pass(sc, chains):
    if len(chains) > sc.n_chain:
        for k in range(0, len(chains), sc.n_chain):
            _flash_two_pass(sc, chains[k:k + sc.n_chain])
        return
    block_max = []
    for c, kblk, add, _, colmask in chains:
        s = jnp.dot(kblk, sc.qpad[c], preferred_element_type=F32)
        if add is not None:
            s = s + add
        if colmask is not None:
            s = s + colmask
        sc.s[c] = s
        block_max.append(jnp.max(s, axis=0, keepdims=True))
    for (c, _, _, vext, _), mx in zip(chains, block_max):
        m_old = sc.m[c]
        m_new = jnp.maximum(m_old, mx)
        p = jnp.exp2(sc.s[c] - m_new).astype(BF16)
        sc.acc[c] = sc.acc[c] * jnp.exp2(m_old - m_new) + jnp.dot(vext, p, preferred_element_type=F32)
        sc.m[c] = m_new


def _flash_lagged(sc, chains, n_first=0):
    def finish(c, p, mx, ref, vext, first):
        pv = jnp.dot(vext, p, preferred_element_type=F32)
        ref_new = jnp.maximum(ref, mx)
        sc.acc[c] = (sc.acc[c] + pv) * jnp.exp2(ref - ref_new)
        sc.m[c] = ref_new
        sc.over[c] = jnp.maximum(sc.over[c], jnp.abs(mx - ref) if first else mx - ref)

    pending = []
    for idx, (c, kblk, add, vext, colmask) in enumerate(chains):
        s = jnp.dot(kblk, sc.qpad[c], preferred_element_type=F32)
        if add is not None:
            s = s + add
        if len(pending) >= min(FLASH_SKEW, sc.n_chain):
            finish(*pending.pop(0))
        ref = sc.m[c]
        mx = jnp.max(s, axis=0, keepdims=True)
        if colmask is None:
            p = jnp.exp2(s - ref)
        else:
            p = jnp.exp2(s - (ref - colmask))
            mx = mx + colmask
        pending.append((c, p.astype(BF16), mx, ref, vext, idx < n_first))
    for item in pending:
        finish(*item)


def _flash_causal(sc, i, block_chains, far_group):
    assert far_group & (far_group - 1) == 0

    def init(ref0):
        for c in range(sc.n_chain):
            sc.m[c] = jnp.full(sc.m.shape[1:], ref0, F32)
            sc.acc[c] = jnp.zeros(sc.acc.shape[1:], F32)
            sc.over[c] = jnp.full(sc.over.shape[1:], NEG, F32)

    def run(step):
        if step is _flash_two_pass:
            init(NEG)
            step(sc, block_chains(i, 0))

            @pl.when(i >= 1)
            def _():
                step(sc, block_chains(i - 1, 1))
        else:
            init(0.0)

            @pl.when(i == 0)
            def _():
                step(sc, block_chains(i, 0), sc.n_chain)

            @pl.when(i >= 1)
            def _():
                step(sc, block_chains(i, 0) + block_chains(i - 1, 1), sc.n_chain)

        n_far = jnp.maximum(i - 1, 0)

        def far_blocks(j, count):
            chains = []
            for k in range(count):
                chains += block_chains(j + k, None)
            step(sc, chains)

        if step is _flash_two_pass:
            lax.fori_loop(0, n_far, lambda j, carry: (far_blocks(j, 1), carry)[1], 0)
            return

        def far_group_step(t, carry):
            far_blocks(far_group * t, far_group)
            return carry

        n_group = n_far // far_group
        lax.fori_loop(0, n_group, far_group_step, 0)
        done = n_group * far_group
        size = far_group // 2
        while size >= 1:
            has_piece = (n_far & size) != 0

            @pl.when(has_piece)
            def _(done=done, size=size):
                far_blocks(done, size)

            done = done + jnp.where(has_piece, size, 0)
            size //= 2

    run(_flash_lagged)
    worst = sc.over[0]
    for c in range(1, sc.n_chain):
        worst = jnp.maximum(worst, sc.over[c])

    @pl.when(jnp.max(worst) > FLASH_OVERFLOW)
    def _():
        run(_flash_two_pass)


def _diff_kernel(lam_init, qt_ref, k_ref, vt_ref, tile_ref, lq1_ref, lk1_ref, lq2_ref, lk2_ref,
                 subln_ref, o_ref, *scratch):
    i = pl.program_id(1)
    sc = _FlashScratch(2 * N_HEADS, *scratch)
    acc_scr = sc.acc
    for h in range(N_HEADS):
        for mp in range(2):
            r0 = h * HEAD_DIM + mp * DIFF_QK
            sc.qpad[2 * h + mp] = _pad_rows(qt_ref[0, r0:r0 + DIFF_QK, :],
                                            (h % 2) * HEAD_DIM + mp * DIFF_QK, LANES)

    def block_chains(j, tile_idx):
        row = pl.multiple_of(j * TB, TB)
        chains = []
        for h in range(N_HEADS):
            g = h // 2
            kblk = k_ref[pl.ds(row, TB), g * LANES:(g + 1) * LANES]
            add = None if tile_idx is None else tile_ref[h, tile_idx]
            vext = _with_ones(vt_ref[j, h * HEAD_DIM:(h + 1) * HEAD_DIM, :])
            chains += [(2 * h + mp, kblk, add, vext, None) for mp in range(2)]
        return chains

    _flash_causal(sc, i, block_chains, far_group=8)

    lam =(jnp.exp(jnp.sum(lq1_ref[...] * lk1_ref[...], keepdims=True))
           - jnp.exp(jnp.sum(lq2_ref[...] * lk2_ref[...], keepdims=True)) + lam_init)
    for h in range(N_HEADS):
        a1 = acc_scr[2 * h]
        a2 = acc_scr[2 * h + 1]
        o1 = a1[:HEAD_DIM] / a1[HEAD_DIM:HEAD_DIM + 1]
        o2 = a2[:HEAD_DIM] / a2[HEAD_DIM:HEAD_DIM + 1]
        a = o1 - lam * o2
        ms = jnp.mean(a * a, axis=0, keepdims=True)
        y = a * lax.rsqrt(ms + NORM_EPS) * subln_ref[...] * (1.0 - lam_init)
        o_ref[0, h * HEAD_DIM:(h + 1) * HEAD_DIM, :] = y.astype(BF16)


def _attn_common_specs(nb, q_blk, k_blk, v_blk):
    return [pl.BlockSpec((1, TB, TB), lambda b, i: (b * nb + i, q_blk, 0)),
            pl.BlockSpec((nb * TB, TB), lambda b, i: (b, k_blk)),
            pl.BlockSpec((nb, TB, TB), lambda b, i: (b, v_blk, 0))]


def _small_spec(shape):
    return pl.BlockSpec(shape, lambda b, i: (0,) * len(shape))


def _diff_attention(kn, qvt, tiles, lq1, lk1, lq2, lk2, subln, lam_init, batch):
    nblk = qvt.shape[0]
    nb = nblk // batch
    n_chain = 2 * N_HEADS
    return pl.pallas_call(
        functools.partial(_diff_kernel, lam_init),
        grid=(batch, nb),
        in_specs=_attn_common_specs(nb, QV_DIFF_Q, KN_DIFF, QV_DIFF_V) + [
            _small_spec((N_HEADS, 2, TB, TB)),
            _small_spec((1, DIFF_QK)), _small_spec((1, DIFF_QK)),
            _small_spec((1, DIFF_QK)), _small_spec((1, DIFF_QK)),
            _small_spec((HEAD_DIM, 1))],
        out_specs=pl.BlockSpec((1, TB, TB), lambda b, i: (b * nb + i, 0, 0)),
        out_shape=jax.ShapeDtypeStruct((nblk, TB, TB), BF16),
        scratch_shapes=_FlashScratch.shapes(n_chain),
        compiler_params=_cparams(2),
        name="diff_attn",
    )(qvt, kn, qvt, tiles, lq1, lk1, lq2, lk2, subln)


def _moba_kernel(nb, nbp, qt_ref, k_ref, vt_ref, tile_ref, o_ref, kmean_scr, sel_scr, *scratch):
    i = pl.program_id(1)
    sc = _FlashScratch(N_HEADS, *scratch)
    qpad_scr, acc_scr = sc.qpad, sc.acc

    @pl.when(i == 0)
    def _():
        kmean_scr[...] = jnp.zeros(kmean_scr.shape, F32)
        for jb in range(nb):
            blk = k_ref[jb * TB:(jb + 1) * TB, :].astype(F32)
            kmean_scr[jb:jb + 1, :] = jnp.mean(blk, axis=0, keepdims=True)

    for h in range(N_HEADS):
        qpad_scr[h] = _pad_rows(qt_ref[0, h * HEAD_DIM:(h + 1) * HEAD_DIM, :], (h % 2) * HEAD_DIM, LANES)

    blk_id = lax.broadcasted_iota(jnp.int32, (nbp, TB), 0)
    for h in range(N_HEADS):
        g = h // 2
        km = kmean_scr[:, g * LANES:(g + 1) * LANES].astype(BF16)
        gate = jnp.dot(km, qpad_scr[h], preferred_element_type=F32)
        avail = blk_id < i
        sel = jnp.zeros((nbp, TB), jnp.bool_)
        for _ in range(MOBA_TOPK):
            gm = jnp.where(avail, gate, -jnp.inf)
            best = jnp.max(gm, axis=0, keepdims=True)
            is_best = avail & (gm == best)
            first = jnp.min(jnp.where(is_best, blk_id, nbp), axis=0, keepdims=True)
            pick = blk_id == first
            sel = sel | pick
            avail = avail & jnp.logical_not(pick)
        sel_scr[h] = jnp.where(sel, 0.0, NEG).astype(F32)

    def block_chains(j, tile_idx):
        row = pl.multiple_of(j * TB, TB)
        chains = []
        for h in range(N_HEADS):
            g = h // 2
            kblk = k_ref[pl.ds(row, TB), g * LANES:(g + 1) * LANES]
            add = None if tile_idx is None else tile_ref[h, tile_idx]
            colmask = None if tile_idx == 0 else sel_scr[h, pl.ds(j, 1), :]
            chains.append((h, kblk, add, _with_ones(vt_ref[j, h * HEAD_DIM:(h + 1) * HEAD_DIM, :]), colmask))
        return chains

    _flash_causal(sc, i, block_chains, far_group=16)

    for h in range(N_HEADS):
        a = acc_scr[h]
        o_ref[0, h * HEAD_DIM:(h + 1) * HEAD_DIM, :] = (a[:HEAD_DIM] / a[HEAD_DIM:HEAD_DIM + 1]).astype(BF16)


def _moba_attention(kn, qvt, tiles, batch):
    nblk = qvt.shape[0]
    nb = nblk // batch
    nbp = max(8, -(-nb // 8) * 8)
    return pl.pallas_call(
        functools.partial(_moba_kernel, nb, nbp),
        grid=(batch, nb),
        in_specs=_attn_common_specs(nb, QV_MOBA_Q, KN_MOBA, QV_MOBA_V) + [
            _small_spec((N_HEADS, 2, TB, TB))],
        out_specs=pl.BlockSpec((1, TB, TB), lambda b, i: (b * nb + i, 0, 0)),
        out_shape=jax.ShapeDtypeStruct((nblk, TB, TB), BF16),
        scratch_shapes=[pltpu.VMEM((nbp, TB), F32),
                        pltpu.VMEM((N_HEADS, nbp, TB), F32)] + _FlashScratch.shapes(N_HEADS),
        compiler_params=_cparams(2),
        name="moba_attn",
    )(qvt, kn, qvt, tiles)


def _sb_kernel(qt_ref, k_ref, vt_ref, o_ref, qpad_scr, acc_scr, c_scr):
    i = pl.program_id(1)
    rows = lax.broadcasted_iota(jnp.int32, (TB, TB), 0)
    cols = lax.broadcasted_iota(jnp.int32, (TB, TB), 1)
    upper = (cols > rows).astype(BF16)
    strict = cols > rows

    heads = range(N_HEADS)

    def step(j, first):
        row = pl.multiple_of(j * TB, TB)
        zs = [jnp.dot(k_ref[pl.ds(row, TB), (h // 2) * LANES:(h // 2 + 1) * LANES], qpad_scr[h],
                      preferred_element_type=F32) for h in heads]
        log_keeps = []
        for z in zs:
            lk = -(jnp.maximum(z, 0.0) + jnp.log2(1.0 + jnp.exp2(-jnp.abs(z))))
            log_keeps.append(jnp.where(strict, lk, 0.0) if first else lk)
        laters = []
        for h in heads:
            hi = log_keeps[h].astype(BF16)
            lo = (log_keeps[h] - hi.astype(F32)).astype(BF16)
            later = (jnp.dot(upper, hi, preferred_element_type=F32)
                     + jnp.dot(upper, lo, preferred_element_type=F32))
            laters.append(later if first else later + c_scr[h])
        c_max = None
        for h in heads:
            w = jnp.exp2(zs[h] + log_keeps[h] + laters[h])
            if first:
                w = jnp.where(strict, w, 0.0)
            pv = jnp.dot(vt_ref[j, h * HEAD_DIM:(h + 1) * HEAD_DIM, :], w.astype(BF16),
                         preferred_element_type=F32)
            acc_scr[h] = pv if first else acc_scr[h] + pv
            c_new = laters[h][0:1, :] + log_keeps[h][0:1, :]
            c_scr[h] = c_new
            c_max = c_new if c_max is None else jnp.maximum(c_max, c_new)
        return jnp.max(c_max) > -SB_EXIT * LOG2E

    for h in heads:
        qpad_scr[h] = _pad_rows(qt_ref[0, h * HEAD_DIM:(h + 1) * HEAD_DIM, :], (h % 2) * HEAD_DIM, LANES)

    def cond(carry):
        j, go = carry
        return jnp.logical_and(j >= 0, go)

    def body(carry):
        j, _ = carry
        return j - 1, step(j, False)

    lax.while_loop(cond, body, (i - 1, step(i, True)))
    for h in heads:
        o_ref[0, h * HEAD_DIM:(h + 1) * HEAD_DIM, :] = acc_scr[h].astype(BF16)


def _sb_attention(kn, qvt, batch):
    nblk = qvt.shape[0]
    nb = nblk // batch
    return pl.pallas_call(
        _sb_kernel,
        grid=(batch, nb),
        in_specs=_attn_common_specs(nb, QV_SB_Q, KN_SB, QV_SB_V),
        out_specs=pl.BlockSpec((1, TB, TB), lambda b, i: (b * nb + i, 0, 0)),
        out_shape=jax.ShapeDtypeStruct((nblk, TB, TB), BF16),
        scratch_shapes=[pltpu.VMEM((N_HEADS, LANES, TB), BF16),
                        pltpu.VMEM((N_HEADS, HEAD_DIM, TB), F32),
                        pltpu.VMEM((N_HEADS, 1, TB), F32)],
        compiler_params=_cparams(2),
        name="sb_attn",
    )(qvt, kn, qvt)


def _swa_kernel(sink_ref, qt_ref, kc_ref, kp_ref, vc_ref, vp_ref, tc_ref, tp_ref, o_ref):
    i = pl.program_id(1)
    kc = kc_ref[...]
    kp = kp_ref[...]
    vc = vc_ref[0]
    vp = vp_ref[0][:, TB - SWA_WINDOW:]
    no_prev = jnp.where(i > 0, 0.0, NEG)
    group = N_HEADS // SWA_KV_HEADS
    heads = range(N_HEADS)
    qpads = [_pad_rows(qt_ref[0, h * HEAD_DIM:(h + 1) * HEAD_DIM, :], (h // group) * HEAD_DIM, LANES)
             for h in heads]
    s_cur = [jnp.dot(kc, qpads[h], preferred_element_type=F32) + tc_ref[h] for h in heads]
    s_prev = [jnp.dot(kp, qpads[h], preferred_element_type=F32) + tp_ref[h] + no_prev for h in heads]
    ms = [jnp.maximum(jnp.maximum(jnp.max(s_cur[h], axis=0, keepdims=True),
                                  jnp.max(s_prev[h], axis=0, keepdims=True)), sink_ref[h]) for h in heads]
    p_cur = [jnp.exp(s_cur[h] - ms[h]).astype(BF16) for h in heads]
    p_prev = [jnp.exp(s_prev[h] - ms[h]).astype(BF16) for h in heads]
    for h in heads:
        kv = h // group
        o = (jnp.dot(_with_ones(vc[kv * HEAD_DIM:(kv + 1) * HEAD_DIM]), p_cur[h], preferred_element_type=F32)
             + jnp.dot(_with_ones(vp[kv * HEAD_DIM:(kv + 1) * HEAD_DIM]), p_prev[h],
                       preferred_element_type=F32))
        denom = o[HEAD_DIM:HEAD_DIM + 1] + jnp.exp(sink_ref[h] - ms[h])
        o_ref[0, h * HEAD_DIM:(h + 1) * HEAD_DIM, :] = (o[:HEAD_DIM] / denom).astype(BF16)


def _swa_attention(kn, qvt, tile_cur, tile_prev, sinks, batch):
    nblk = qvt.shape[0]
    nb = nblk // batch
    half = TB // SWA_WINDOW
    kv_rows = SWA_KV_HEADS * HEAD_DIM
    return pl.pallas_call(
        _swa_kernel,
        grid=(batch, nb),
        in_specs=[pl.BlockSpec(memory_space=pltpu.SMEM),
                  pl.BlockSpec((1, TB, TB), lambda b, i: (b * nb + i, QV_SWA_Q, 0)),
                  pl.BlockSpec((TB, kv_rows), lambda b, i: (b * nb + i, KN_SWA_128)),
                  pl.BlockSpec((SWA_WINDOW, kv_rows),
                               lambda b, i: (b * nb * half + jnp.maximum(half * i - 1, 0), KN_SWA_128)),
                  pl.BlockSpec((1, kv_rows, TB), lambda b, i: (b * nb + i, QV_SWA_V_128, 0)),
                  pl.BlockSpec((1, kv_rows, TB), lambda b, i: (b * nb + jnp.maximum(i - 1, 0), QV_SWA_V_128, 0)),
                  _small_spec((N_HEADS, TB, TB)),
                  _small_spec((N_HEADS, SWA_WINDOW, TB))],
        out_specs=pl.BlockSpec((1, TB, TB), lambda b, i: (b * nb + i, 0, 0)),
        out_shape=jax.ShapeDtypeStruct((nblk, TB, TB), BF16),
        compiler_params=_cparams(2),
        name="swa_attn",
    )(sinks, qvt, kn, kn, qvt, qvt, tile_cur, tile_prev)


MERGE_T = 512


def _merge_kernel(x_ref, g_ref, oa_ref, ob_ref, oc_ref, od_ref, wg_ref, wbr_ref, wo_ref, x1_ref):
    x = x_ref[...]
    h = _rms(x, g_ref[...]).astype(BF16)
    merged = None
    for bi, o_ref in enumerate((oa_ref, ob_ref, oc_ref, od_ref)):
        gate = jax.nn.sigmoid(jnp.dot(h, wg_ref[bi], preferred_element_type=F32))
        branch = jnp.concatenate(
            [lax.dot_general(o_ref[s], wbr_ref[bi], (((0,), (0,)), ((), ())), preferred_element_type=F32)
             for s in range(MERGE_T // TB)], axis=0)
        term = gate * branch
        merged = term if merged is None else merged + term
    x1_ref[...] = x + jnp.dot(merged.astype(BF16), wo_ref[...], preferred_element_type=F32)


def _merge(x2, g, o_a, o_b, o_c, o_d, wg, wbr, wo):
    n, d = x2.shape
    o_spec = pl.BlockSpec((MERGE_T // TB, TB, TB), lambda i: (i, 0, 0))
    return pl.pallas_call(
        _merge_kernel,
        grid=(n // MERGE_T,),
        in_specs=[pl.BlockSpec((MERGE_T, d), lambda i: (i, 0)),
                  pl.BlockSpec((1, d), lambda i: (0, 0)),
                  o_spec, o_spec, o_spec, o_spec,
                  pl.BlockSpec(wg.shape, lambda i: (0, 0, 0)),
                  pl.BlockSpec(wbr.shape, lambda i: (0, 0, 0)),
                  pl.BlockSpec(wo.shape, lambda i: (0, 0))],
        out_specs=pl.BlockSpec((MERGE_T, d), lambda i: (i, 0)),
        out_shape=jax.ShapeDtypeStruct((n, d), F32),
        compiler_params=_cparams(1),
        name="merge",
    )(x2, g, o_a, o_b, o_c, o_d, wg, wbr, wo)


ROUTER_T = 512
ROUTER_ROWS = 8 + N_EXPERTS


def _first_argmax_rows(v, n_rows):
    best = jnp.max(v, axis=0, keepdims=True)
    ids = lax.broadcasted_iota(jnp.int32, v.shape, 0)
    return best, jnp.min(jnp.where(v == best, ids, n_rows), axis=0, keepdims=True)


def _router_kernel(x_ref, g_ref, w_ref, b_ref, h2_ref, eid_ref, gate_ref, rank_ref, cnt_ref, base_scr):
    i = pl.program_id(0)

    @pl.when(i == 0)
    def _():
        base_scr[...] = jnp.zeros(base_scr.shape, F32)

    h2 = _rms(x_ref[...], g_ref[...])
    h2_ref[...] = _pack_bf16_pairs(h2)
    nt = (((1,), (1,)), ((), ()))
    logits = lax.dot_general(w_ref[...], h2.astype(BF16), nt, preferred_element_type=F32) + b_ref[...]
    gl = logits[0:8]
    gmax, grp = _first_argmax_rows(gl, 8)
    p_grp = 1.0 / jnp.sum(jnp.exp(gl - gmax), axis=0, keepdims=True)
    e_sel = jnp.zeros((EXPERTS_PER_GROUP, ROUTER_T), F32)
    for g in range(N_GROUPS):
        e_sel = jnp.where(grp == g, logits[8 + 8 * g:16 + 8 * g], e_sel)
    ids8 = lax.broadcasted_iota(jnp.int32, e_sel.shape, 0)
    v1, i1 = _first_argmax_rows(e_sel, EXPERTS_PER_GROUP)
    e_rest = jnp.where(ids8 == i1, -jnp.inf, e_sel)
    v2, i2 = _first_argmax_rows(e_rest, EXPERTS_PER_GROUP)
    r = jnp.exp(v2 - v1)
    s1 = 1.0 / (1.0 + r)
    gate_ref[0:1, :] = p_grp * s1
    gate_ref[1:2, :] = p_grp * (r * s1)
    e1 = grp * EXPERTS_PER_GROUP + i1
    e2 = grp * EXPERTS_PER_GROUP + i2
    eid_ref[0:1, :] = e1
    eid_ref[1:2, :] = e2

    ids_e = lax.broadcasted_iota(jnp.int32, (N_EXPERTS, ROUTER_T), 0)
    oh1 = ids_e == e1
    oh2 = ids_e == e2
    cnt = oh1.astype(F32) + oh2.astype(F32)
    tr = lax.broadcasted_iota(jnp.int32, (ROUTER_T, ROUTER_T), 0)
    tc = lax.broadcasted_iota(jnp.int32, (ROUTER_T, ROUTER_T), 1)
    before = (tr < tc).astype(BF16)
    prefix = jnp.dot(cnt.astype(BF16), before, preferred_element_type=F32) + base_scr[:, 0:1]
    rank_ref[0:1, :] = jnp.sum(jnp.where(oh1, prefix, 0.0), axis=0, keepdims=True).astype(jnp.int32)
    rank_ref[1:2, :] = jnp.sum(jnp.where(oh2, prefix, 0.0), axis=0, keepdims=True).astype(jnp.int32)
    base_scr[...] = base_scr[...] + jnp.sum(cnt, axis=1, keepdims=True)
    cnt_ref[...] = base_scr[...]


def _router(x1, g, w, bias):
    n, d = x1.shape
    row2 = lambda dt: jax.ShapeDtypeStruct((2, n), dt)
    spec2 = pl.BlockSpec((2, ROUTER_T), lambda i: (0, i))
    return pl.pallas_call(
        _router_kernel,
        grid=(n // ROUTER_T,),
        in_specs=[pl.BlockSpec((ROUTER_T, d), lambda i: (i, 0)),
                  pl.BlockSpec((1, d), lambda i: (0, 0)),
                  pl.BlockSpec((ROUTER_ROWS, d), lambda i: (0, 0)),
                  pl.BlockSpec((ROUTER_ROWS, 1), lambda i: (0, 0))],
        out_specs=[pl.BlockSpec((ROUTER_T, d // 2), lambda i: (i, 0)), spec2, spec2, spec2,
                   pl.BlockSpec((N_EXPERTS, LANES), lambda i: (0, 0))],
        out_shape=[jax.ShapeDtypeStruct((n, d // 2), jnp.uint32), row2(jnp.int32), row2(F32), row2(jnp.int32),
                   jax.ShapeDtypeStruct((N_EXPERTS, LANES), F32)],
        scratch_shapes=[pltpu.VMEM((N_EXPERTS, LANES), F32)],
        compiler_params=_cparams(1),
        name="router",
    )(x1, g, w, bias)


def _expert_kernel(be_ref, nu_ref, nv_ref, x_ref, w1_ref, w3_ref, w2_ref, y_ref):
    del be_ref
    used = pl.program_id(0) < nu_ref[0]

    @pl.when(used)
    def _():
        live = lax.broadcasted_iota(jnp.int32, x_ref.shape, 0) < nv_ref[pl.program_id(0)]
        xb = _unpack_bf16_pairs(jnp.where(live, x_ref[...], jnp.uint32(0))).astype(BF16)
        a = jnp.dot(xb, w1_ref[...].astype(BF16), preferred_element_type=F32)
        b = jnp.dot(xb, w3_ref[...].astype(BF16), preferred_element_type=F32)
        mid = (a * jax.nn.sigmoid(a) * b).astype(BF16)
        y_ref[...] = _pack_bf16_pairs(jnp.dot(mid, w2_ref[...].astype(BF16), preferred_element_type=F32))

    @pl.when(jnp.logical_not(used))
    def _():
        y_ref[...] = jnp.zeros(y_ref.shape, jnp.uint32)


def _experts(blk_expert, n_used, n_valid, xbuf, w1, w3, w2, layer):
    r = xbuf.shape[0]
    d = w1.shape[-2]
    de = w1.shape[-1]
    assert xbuf.shape[1] * 2 == d
    row_map = lambda i, be, nu, nv: (jnp.minimum(i, nu[0] - 1), 0)
    grid_spec = pltpu.PrefetchScalarGridSpec(
        num_scalar_prefetch=3,
        grid=(r // MOE_ROWS,),
        in_specs=[pl.BlockSpec((MOE_ROWS, d // 2), row_map),
                  pl.BlockSpec((None, None, d, de), lambda i, be, nu, nv: (layer, be[i], 0, 0)),
                  pl.BlockSpec((None, None, d, de), lambda i, be, nu, nv: (layer, be[i], 0, 0)),
                  pl.BlockSpec((None, None, de, d), lambda i, be, nu, nv: (layer, be[i], 0, 0))],
        out_specs=pl.BlockSpec((MOE_ROWS, d // 2), lambda i, be, nu, nv: (i, 0)),
    )
    return pl.pallas_call(
        _expert_kernel,
        grid_spec=grid_spec,
        out_shape=jax.ShapeDtypeStruct((r, d // 2), jnp.uint32),
        compiler_params=_cparams(1),
        name="experts",
    )(blk_expert, n_used, n_valid, xbuf, w1, w3, w2)


SC_CORES = 2
SC_SUBCORES = 16
SC_ROWS = 64


def _sc_gather_rows(table, idx):
    n_idx = idx.shape[0]
    d = table.shape[1]
    n_workers = SC_CORES * SC_SUBCORES
    per_worker = n_idx // n_workers
    n_chunk = per_worker // SC_ROWS
    assert per_worker * n_workers == n_idx and n_chunk * SC_ROWS == per_worker and n_chunk % 2 == 0
    mesh = plsc.VectorSubcoreMesh(core_axis_name="c", subcore_axis_name="s",
                                  num_cores=SC_CORES, num_subcores=SC_SUBCORES)

    def body(table_hbm, idx_hbm, out_hbm, idx_v, rows_v, gsem, wsem):
        worker = lax.axis_index("s") * SC_CORES + lax.axis_index("c")
        base = worker * per_worker
        pltpu.sync_copy(idx_hbm.at[pl.ds(base, per_worker)], idx_v)

        def gather(c, b):
            return pltpu.make_async_copy(table_hbm.at[idx_v.at[pl.ds(c * SC_ROWS, SC_ROWS)]],
                                         rows_v.at[b], gsem.at[b])

        def put(c, b):
            return pltpu.make_async_copy(rows_v.at[b], out_hbm.at[pl.ds(base + c * SC_ROWS, SC_ROWS)],
                                         wsem.at[b])

        gather(0, 0).start()

        @pl.loop(0, n_chunk, step=2)
        def _(c0):
            for b in range(2):
                c = c0 + b
                gather(c, b).wait()

                @pl.when(c + 1 < n_chunk)
                def _():
                    @pl.when(c >= 1)
                    def _():
                        put(c - 1, 1 - b).wait()

                    gather(c + 1, 1 - b).start()

                put(c, b).start()

        put(n_chunk - 2, 0).wait()
        put(n_chunk - 1, 1).wait()

    return pl.kernel(
        body,
        out_type=jax.ShapeDtypeStruct((n_idx, d), table.dtype),
        mesh=mesh,
        scratch_types=[pltpu.VMEM((per_worker,), jnp.int32),
                       pltpu.VMEM((2, SC_ROWS, d), table.dtype),
                       pltpu.SemaphoreType.DMA((2,)),
                       pltpu.SemaphoreType.DMA((2,))],
        name="sc_gather_rows",
    )(table, idx)


def _sc_scatter_rows(rows, dest, n_out):
    n, d = rows.shape
    n_workers = SC_CORES * SC_SUBCORES
    per_worker = n // n_workers
    n_chunk = per_worker // SC_ROWS
    assert per_worker * n_workers == n and n_chunk * SC_ROWS == per_worker and n_chunk % 2 == 0
    mesh = plsc.VectorSubcoreMesh(core_axis_name="c", subcore_axis_name="s",
                                  num_cores=SC_CORES, num_subcores=SC_SUBCORES)
    dest3 = dest.reshape(2, n // SC_ROWS, SC_ROWS)

    def body(rows_hbm, idx_hbm, out_hbm, idx_v, rows_v, lsem, ssem):
        worker = lax.axis_index("s") * SC_CORES + lax.axis_index("c")
        for k in range(2):
            pltpu.sync_copy(idx_hbm.at[k, pl.ds(worker * n_chunk, n_chunk)], idx_v.at[k])

        def load(c, b):
            return pltpu.make_async_copy(rows_hbm.at[pl.ds(worker * per_worker + c * SC_ROWS, SC_ROWS)],
                                         rows_v.at[b], lsem.at[b])

        def scatter(c, b, k):
            return pltpu.make_async_copy(rows_v.at[b], out_hbm.at[idx_v.at[k, c]], ssem.at[b])

        load(0, 0).start()

        @pl.loop(0, n_chunk, step=2)
        def _(c0):
            for b in range(2):
                c = c0 + b
                load(c, b).wait()

                @pl.when(c + 1 < n_chunk)
                def _():
                    @pl.when(c >= 1)
                    def _():
                        scatter(c - 1, 1 - b, 0).wait()
                        scatter(c - 1, 1 - b, 1).wait()

                    load(c + 1, 1 - b).start()

                scatter(c, b, 0).start()
                scatter(c, b, 1).start()

        for c, b in ((n_chunk - 2, 0), (n_chunk - 1, 1)):
            scatter(c, b, 0).wait()
            scatter(c, b, 1).wait()

    return pl.kernel(
        body,
        out_type=jax.ShapeDtypeStruct((n_out, d), rows.dtype),
        mesh=mesh,
        scratch_types=[pltpu.VMEM((2, n_chunk, SC_ROWS), jnp.int32),
                       pltpu.VMEM((2, SC_ROWS, d), rows.dtype),
                       pltpu.SemaphoreType.DMA((2,)),
                       pltpu.SemaphoreType.DMA((2,))],
        name="sc_scatter_rows",
    )(rows, dest3)


def _combine_dense_kernel(final, gate_ref, x1_ref, gf_ref, y0_ref, y1_ref, out_ref):
    gate = gate_ref[...]
    out = (x1_ref[...] + gate[:, 0:1] * _unpack_bf16_pairs(y0_ref[...])
           + gate[:, 1:2] * _unpack_bf16_pairs(y1_ref[...]))
    if final:
        out = _rms(out, gf_ref[...])
    out_ref[...] = out


def _combine_dense(gate_t, x1, g_final, yg, final):
    n, d = x1.shape
    nb = n // MERGE_T
    return pl.pallas_call(
        functools.partial(_combine_dense_kernel, final),
        grid=(nb,),
        in_specs=[pl.BlockSpec((MERGE_T, 2), lambda i: (i, 0)),
                  pl.BlockSpec((MERGE_T, d), lambda i: (i, 0)),
                  pl.BlockSpec((1, d), lambda i: (0, 0)),
                  pl.BlockSpec((MERGE_T, d // 2), lambda i: (i, 0)),
                  pl.BlockSpec((MERGE_T, d // 2), lambda i: (i + nb, 0))],
        out_specs=pl.BlockSpec((MERGE_T, d), lambda i: (i, 0)),
        out_shape=jax.ShapeDtypeStruct((n, d), F32),
        compiler_params=_cparams(1),
        name="combine_dense",
    )(gate_t, x1, g_final, yg, yg)


def _projection_weights(w):
    d = w.shape[0]
    blk = N_HEADS * HEAD_DIM
    kv = SWA_KV_HEADS * HEAD_DIM
    pa, pb, pc, pd = w[:, :3 * blk], w[:, 3 * blk:6 * blk], w[:, 6 * blk:9 * blk], w[:, 9 * blk:]
    half = N_HEADS * DIFF_QK

    def per_head(a, b):
        return jnp.stack([a.reshape(d, N_HEADS, DIFF_QK), b.reshape(d, N_HEADS, DIFF_QK)], axis=2).reshape(d, blk)

    s64, s32 = HEAD_DIM ** -0.5, DIFF_QK ** -0.5
    wn = jnp.concatenate([pa[:, blk:2 * blk], per_head(pb[:, 2 * half:3 * half], pb[:, 3 * half:4 * half]),
                          pc[:, blk:2 * blk], pd[:, blk:blk + kv]], axis=1)
    wt = jnp.concatenate([pa[:, :blk] * (s64 * LOG2E), pa[:, 2 * blk:],
                          per_head(pb[:, :half], pb[:, half:2 * half]) * (s32 * LOG2E), pb[:, 4 * half:],
                          pc[:, :blk] * (s64 * LOG2E), pc[:, 2 * blk:],
                          pd[:, :blk] * s64, pd[:, blk + kv:]], axis=1)
    assert wn.shape[1] == KN_COLS and wt.shape[1] == QV_ROWS
    return wn.astype(BF16), wt.T.astype(BF16)


def _router_weights(w_rg, b_rg, w_re, b_re):
    d = w_rg.shape[0]
    w = jnp.concatenate([w_rg.T, jnp.zeros((8 - N_GROUPS, d), F32), w_re.T], axis=0)
    b = jnp.concatenate([b_rg.astype(F32), jnp.full((8 - N_GROUPS,), NEG, F32), b_re.astype(F32)])[:, None]
    return w.astype(BF16), b


def _moe_plan(eid, rank, counts, n_rows_total):
    padded = (counts + MOE_ROWS - 1) // MOE_ROWS * MOE_ROWS
    pad_end = jnp.cumsum(padded)
    pad_start = pad_end - padded
    experts = jnp.arange(N_EXPERTS, dtype=jnp.int32)
    start_of = jnp.sum(jnp.where(eid[..., None] == experts, pad_start, 0), axis=-1)
    dest = start_of + rank
    n_blk = n_rows_total // MOE_ROWS
    n_used = (pad_end[-1] // MOE_ROWS).astype(jnp.int32)
    blk = jnp.minimum(jnp.arange(n_blk, dtype=jnp.int32), n_used - 1) * MOE_ROWS
    blk_expert = jnp.minimum(jnp.sum(pad_end[None, :] <= blk[:, None], axis=1), N_EXPERTS - 1).astype(jnp.int32)
    live_end = jnp.sum(jnp.where(blk_expert[:, None] == experts, pad_start + counts, 0), axis=-1)
    n_valid = jnp.clip(live_end - blk, 0, MOE_ROWS).astype(jnp.int32)
    return dest.astype(jnp.int32), blk_expert, n_used.reshape(1), n_valid


def kernel(x, rel_bias, g_mix, w_in, diff_lq1, diff_lk1, diff_lq2, diff_lk2, diff_subln, swa_sinks,
           w_gate, w_br, w_o, g_ffn, w_route_group, b_route_group, w_route_expert, b_route_expert,
           w1, w3, w2, g_final):
    batch, seq, d = x.shape
    n = batch * seq
    depth = w_in.shape[0]
    assert seq % TB == 0 and n % MERGE_T == 0 and TB == MOBA_BLOCK
    tab = rel_bias.T.astype(F32)
    tiles_moba = _causal_bias_tiles(tab[:N_HEADS])
    tiles_diff = _causal_bias_tiles(tab[N_HEADS:2 * N_HEADS])
    tile_cur, tile_prev = _swa_bias_tiles(tab[2 * N_HEADS:])
    n_rows_total = n * 2 + N_EXPERTS * MOE_ROWS
    row = lambda v: v.astype(F32)[None, :]

    x2 = x.reshape(n, d)
    for l in range(depth):
        lam_init = 0.8 - 0.6 * math.exp(-0.3 * l)
        wn, wt = _projection_weights(w_in[l])
        kn, qvt = _inproj(x2, row(g_mix[l]), wn, wt)
        o_a = _moba_attention(kn, qvt, tiles_moba, batch)
        o_b = _diff_attention(kn, qvt, tiles_diff, row(diff_lq1[l]), row(diff_lk1[l]), row(diff_lq2[l]),
                              row(diff_lk2[l]), diff_subln[l].astype(F32)[:, None], lam_init, batch)
        o_c = _sb_attention(kn, qvt, batch)
        o_d = _swa_attention(kn, qvt, tile_cur, tile_prev, swa_sinks[l].astype(F32), batch)
        x1 = _merge(x2, row(g_mix[l]), o_a, o_b, o_c, o_d, w_gate[l].astype(BF16), w_br[l].astype(BF16),
                    w_o[l].astype(BF16))
        w_route, r_bias = _router_weights(w_route_group[l], b_route_group[l], w_route_expert[l],
                                          b_route_expert[l])
        h2, eid, gate, rank, cnt = _router(x1, row(g_ffn[l]), w_route, r_bias)
        dest, blk_expert, n_used, n_valid = _moe_plan(eid, rank, cnt[:, 0].astype(jnp.int32), n_rows_total)
        xbuf = _sc_scatter_rows(h2, dest, n_rows_total)
        y = _experts(blk_expert, n_used, n_valid, xbuf, w1, w3, w2, l)
        yg = _sc_gather_rows(y, dest.reshape(-1))
        x2 = _combine_dense(gate.T, x1, row(g_final), yg, l == depth - 1)
    return x2.reshape(batch, seq, d)
```

```python
import functools
import math

import numpy as np
import jax
import jax.numpy as jnp
from jax import lax
from jax.experimental import pallas as pl
from jax.experimental.pallas import tpu as pltpu
from jax.experimental.pallas import tpu_sc as plsc

F32 = jnp.float32
BF16 = jnp.bfloat16

HEAD_DIM = 64
N_HEADS = 4
DIFF_QK = 32
SWA_KV_HEADS = 2
SWA_WINDOW = 128
MOBA_BLOCK = 256
MOBA_TOPK = 3
REL_BUCKETS = 32
REL_MAX_DIST = 128
N_GROUPS = 4
EXPERTS_PER_GROUP = 8
N_EXPERTS = N_GROUPS * EXPERTS_PER_GROUP
NORM_EPS = 1e-6

TB = 256
LANES = 128
ONES_ROWS = 16
NEG = -1e30
LOG2E = math.log2(math.e)
SB_EXIT = 104.0
MOE_ROWS = 512
VMEM_LIMIT = 56 * 1024 * 1024

QV_MOBA_Q, QV_MOBA_V, QV_DIFF_Q, QV_DIFF_V, QV_SB_Q, QV_SB_V, QV_SWA_Q = range(7)
QV_SWA_V_128 = 14
QV_ROWS = 7 * 256 + 128
KN_MOBA, KN_DIFF, KN_SB = range(3)
KN_SWA_128 = 6
KN_COLS = 3 * 256 + 128


def _cparams(n_grid):
    return pltpu.CompilerParams(dimension_semantics=("arbitrary",) * n_grid,
                                vmem_limit_bytes=VMEM_LIMIT)


def _rel_bucket_np(n):
    n = np.maximum(n, 0)
    max_exact = REL_BUCKETS // 2
    nf = np.maximum(n, 1).astype(np.float64)
    large = max_exact + (np.log(nf / max_exact) / math.log(REL_MAX_DIST / max_exact)
                         * (REL_BUCKETS - max_exact)).astype(np.int64)
    large = np.minimum(large, REL_BUCKETS - 1)
    return np.where(n < max_exact, n, large)


def _first_far_distance():
    d = np.arange(0, 4 * REL_MAX_DIST)
    b = _rel_bucket_np(d)
    return int(np.min(d[b == REL_BUCKETS - 1]))


def _toeplitz_bias(tab, rows, cols, base, valid_fn, shift_far, unit):
    length = rows + cols - 1
    off = np.concatenate([np.arange(0, cols), np.arange(cols - length, 0)])
    n = base + off
    onehot = np.zeros((REL_BUCKETS, length), np.float32)
    onehot[_rel_bucket_np(n), np.arange(length)] = 1.0
    vec = jnp.dot(tab, jnp.asarray(onehot), precision=lax.Precision.HIGHEST)
    if shift_far:
        vec = vec - tab[:, REL_BUCKETS - 1:]
    vec = jnp.where(jnp.asarray(valid_fn(n))[None, :], vec * unit, NEG).astype(F32)
    flat = jnp.tile(vec, (1, rows))[:, :rows * (length - 1)]
    return flat.reshape(tab.shape[0], rows, length - 1)[:, :, :cols]


def _causal_bias_tiles(tab):
    assert _first_far_distance() <= TB + 1
    tiles = [_toeplitz_bias(tab, TB, TB, d * TB, lambda n: n >= 0, True, LOG2E) for d in range(2)]
    return jnp.stack(tiles, axis=1)


def _swa_bias_tiles(tab):
    in_window = lambda n: (n >= 0) & (n < SWA_WINDOW)
    return (_toeplitz_bias(tab, TB, TB, 0, in_window, False, 1.0),
            _toeplitz_bias(tab, SWA_WINDOW, TB, SWA_WINDOW, in_window, False, 1.0))


def _pad_rows(q, off, total):
    n, t = q.shape
    parts = []
    if off:
        parts.append(jnp.zeros((off, t), q.dtype))
    parts.append(q)
    if total - off - n:
        parts.append(jnp.zeros((total - off - n, t), q.dtype))
    return jnp.concatenate(parts, axis=0) if len(parts) > 1 else q


def _with_ones(v):
    return jnp.concatenate([v, jnp.ones((ONES_ROWS, v.shape[1]), v.dtype)], axis=0)


def _pack_bf16_pairs(x):
    w = x.shape[1] // 2
    lo = lax.bitcast_convert_type(x[:, :w].astype(BF16).astype(F32), jnp.uint32)
    hi = lax.bitcast_convert_type(x[:, w:].astype(BF16).astype(F32), jnp.uint32)
    return hi | (lo >> 16)


def _unpack_bf16_pairs(u):
    lo = lax.bitcast_convert_type(u << 16, F32)
    hi = lax.bitcast_convert_type(u & jnp.uint32(0xFFFF0000), F32)
    return jnp.concatenate([lo, hi], axis=1)


def _rms(x, g_row):
    ms = jnp.mean(x * x, axis=-1, keepdims=True)
    return x * lax.rsqrt(ms + NORM_EPS) * g_row


IN_T = 512
IN_CHUNK = 384


def _inproj_kernel(x_ref, g_ref, wn_ref, wt_ref, kn_ref, qvt_ref):
    h = _rms(x_ref[...], g_ref[...]).astype(BF16)
    kn_ref[...] = jnp.dot(h, wn_ref[...], preferred_element_type=F32).astype(BF16)
    for r0 in range(0, QV_ROWS, IN_CHUNK):
        pt = lax.dot_general(wt_ref[r0:r0 + IN_CHUNK, :], h, (((1,), (1,)), ((), ())),
                             preferred_element_type=F32)
        for s in range(IN_T // TB):
            qvt_ref[s, r0:r0 + IN_CHUNK, :] = pt[:, s * TB:(s + 1) * TB].astype(BF16)


def _inproj(x2, g, wn, wt):
    n, d = x2.shape
    return pl.pallas_call(
        _inproj_kernel,
        grid=(n // IN_T,),
        in_specs=[pl.BlockSpec((IN_T, d), lambda i: (i, 0)),
                  pl.BlockSpec((1, d), lambda i: (0, 0)),
                  pl.BlockSpec((d, KN_COLS), lambda i: (0, 0)),
                  pl.BlockSpec((QV_ROWS, d), lambda i: (0, 0))],
        out_specs=[pl.BlockSpec((IN_T, KN_COLS), lambda i: (i, 0)),
                   pl.BlockSpec((IN_T // TB, QV_ROWS, TB), lambda i: (i, 0, 0))],
        out_shape=[jax.ShapeDtypeStruct((n, KN_COLS), BF16),
                   jax.ShapeDtypeStruct((n // TB, QV_ROWS, TB), BF16)],
        compiler_params=_cparams(1),
        name="inproj",
    )(x2, g, wn, wt)


FLASH_OVERFLOW = 100.0
FLASH_SKEW = 4


class _FlashScratch:
    def __init__(self, n_chain, qpad, m, acc, over, s):
        self.n_chain, self.qpad, self.m, self.acc, self.over, self.s = n_chain, qpad, m, acc, over, s

    @staticmethod
    def shapes(n_chain):
        return [pltpu.VMEM((n_chain, LANES, TB), BF16),
                pltpu.VMEM((n_chain, 1, TB), F32),
                pltpu.VMEM((n_chain, HEAD_DIM + ONES_ROWS, TB), F32),
                pltpu.VMEM((n_chain, 1, TB), F32),
                pltpu.VMEM((n_chain, TB, TB), F32)]


def _flash_two_pass(sc, chains):
    if len(chains) > sc.n_chain:
        for k in range(0, len(chains), sc.n_chain):
            _flash_two_pass(sc, chains[k:k + sc.n_chain])
        return
    block_max = []
    for c, kblk, add, _, colmask in chains:
        s = jnp.dot(kblk, sc.qpad[c], preferred_element_type=F32)
        if add is not None:
            s = s + add
        if colmask is not None:
            s = s + colmask
        sc.s[c] = s
        block_max.append(jnp.max(s, axis=0, keepdims=True))
    for (c, _, _, vext, _), mx in zip(chains, block_max):
        m_old = sc.m[c]
        m_new = jnp.maximum(m_old, mx)
        p = jnp.exp2(sc.s[c] - m_new).astype(BF16)
        sc.acc[c] = sc.acc[c] * jnp.exp2(m_old - m_new) + jnp.dot(vext, p, preferred_element_type=F32)
        sc.m[c] = m_new


def _flash_lagged(sc, chains, n_first=0):
    def finish(c, p, mx, ref, vext, first):
        pv = jnp.dot(vext, p, preferred_element_type=F32)
        ref_new = jnp.maximum(ref, mx)
        sc.acc[c] = (sc.acc[c] + pv) * jnp.exp2(ref - ref_new)
        sc.m[c] = ref_new
        sc.over[c] = jnp.maximum(sc.over[c], jnp.abs(mx - ref) if first else mx - ref)

    pending = []
    for idx, (c, kblk, add, vext, colmask) in enumerate(chains):
        s = jnp.dot(kblk, sc.qpad[c], preferred_element_type=F32)
        if add is not None:
            s = s + add
        if len(pending) >= min(FLASH_SKEW, sc.n_chain):
            finish(*pending.pop(0))
        ref = sc.m[c]
        mx = jnp.max(s, axis=0, keepdims=True)
        if colmask is None:
            p = jnp.exp2(s - ref)
        else:
            p = jnp.exp2(s - (ref - colmask))
            mx = mx + colmask
        pending.append((c, p.astype(BF16), mx, ref, vext, idx < n_first))
    for item in pending:
        finish(*item)


def _flash_causal(sc, i, block_chains, far_group):
    assert far_group & (far_group - 1) == 0

    def init(ref0):
        for c in range(sc.n_chain):
            sc.m[c] = jnp.full(sc.m.shape[1:], ref0, F32)
            sc.acc[c] = jnp.zeros(sc.acc.shape[1:], F32)
            sc.over[c] = jnp.full(sc.over.shape[1:], NEG, F32)

    def run(step):
        if step is _flash_two_pass:
            init(NEG)
            step(sc, block_chains(i, 0))

            @pl.when(i >= 1)
            def _():
                step(sc, block_chains(i - 1, 1))
        else:
            init(0.0)

            @pl.when(i == 0)
            def _():
                step(sc, block_chains(i, 0), sc.n_chain)

            @pl.when(i >= 1)
            def _():
                step(sc, block_chains(i, 0) + block_chains(i - 1, 1), sc.n_chain)

        n_far = jnp.maximum(i - 1, 0)

        def far_blocks(j, count):
            chains = []
            for k in range(count):
                chains += block_chains(j + k, None)
            step(sc, chains)

        if step is _flash_two_pass:
            lax.fori_loop(0, n_far, lambda j, carry: (far_blocks(j, 1), carry)[1], 0)
            return

        def far_group_step(t, carry):
            far_blocks(far_group * t, far_group)
            return carry

        n_group = n_far // far_group
        lax.fori_loop(0, n_group, far_group_step, 0)
        done = n_group * far_group
        size = far_group // 2
        while size >= 1:
            has_piece = (n_far & size) != 0

            @pl.when(has_piece)
            def _(done=done, size=size):
                far_blocks(done, size)

            done = done + jnp.where(has_piece, size, 0)
            size //= 2

    run(_flash_lagged)
    worst = sc.over[0]
    for c in range(1, sc.n_chain):
        worst = jnp.maximum(worst, sc.over[c])

    @pl.when(jnp.max(worst) > FLASH_OVERFLOW)
    def _():
        run(_flash_two_pass)


def _diff_kernel(lam_init, qt_ref, k_ref, vt_ref, tile_ref, lq1_ref, lk1_ref, lq2_ref, lk2_ref,
                 subln_ref, o_ref, *scratch):
    i = pl.program_id(1)
    sc = _FlashScratch(2 * N_HEADS, *scratch)
    acc_scr = sc.acc
    for h in range(N_HEADS):
        for mp in range(2):
            r0 = h * HEAD_DIM + mp * DIFF_QK
            sc.qpad[2 * h + mp] = _pad_rows(qt_ref[0, r0:r0 + DIFF_QK, :],
                                            (h % 2) * HEAD_DIM + mp * DIFF_QK, LANES)

    def block_chains(j, tile_idx):
        row = pl.multiple_of(j * TB, TB)
        chains = []
        for h in range(N_HEADS):
            g = h // 2
            kblk = k_ref[pl.ds(row, TB), g * LANES:(g + 1) * LANES]
            add = None if tile_idx is None else tile_ref[h, tile_idx]
            vext = _with_ones(vt_ref[j, h * HEAD_DIM:(h + 1) * HEAD_DIM, :])
            chains += [(2 * h + mp, kblk, add, vext, None) for mp in range(2)]
        return chains

    _flash_causal(sc, i, block_chains, far_group=8)

    lam =(jnp.exp(jnp.sum(lq1_ref[...] * lk1_ref[...], keepdims=True))
           - jnp.exp(jnp.sum(lq2_ref[...] * lk2_ref[...], keepdims=True)) + lam_init)
    for h in range(N_HEADS):
        a1 = acc_scr[2 * h]
        a2 = acc_scr[2 * h + 1]
        o1 = a1[:HEAD_DIM] / a1[HEAD_DIM:HEAD_DIM + 1]
        o2 = a2[:HEAD_DIM] / a2[HEAD_DIM:HEAD_DIM + 1]
        a = o1 - lam * o2
        ms = jnp.mean(a * a, axis=0, keepdims=True)
        y = a * lax.rsqrt(ms + NORM_EPS) * subln_ref[...] * (1.0 - lam_init)
        o_ref[0, h * HEAD_DIM:(h + 1) * HEAD_DIM, :] = y.astype(BF16)


def _attn_common_specs(nb, q_blk, k_blk, v_blk):
    return [pl.BlockSpec((1, TB, TB), lambda b, i: (b * nb + i, q_blk, 0)),
            pl.BlockSpec((nb * TB, TB), lambda b, i: (b, k_blk)),
            pl.BlockSpec((nb, TB, TB), lambda b, i: (b, v_blk, 0))]


def _small_spec(shape):
    return pl.BlockSpec(shape, lambda b, i: (0,) * len(shape))


def _diff_attention(kn, qvt, tiles, lq1, lk1, lq2, lk2, subln, lam_init, batch):
    nblk = qvt.shape[0]
    nb = nblk // batch
    n_chain = 2 * N_HEADS
    return pl.pallas_call(
        functools.partial(_diff_kernel, lam_init),
        grid=(batch, nb),
        in_specs=_attn_common_specs(nb, QV_DIFF_Q, KN_DIFF, QV_DIFF_V) + [
            _small_spec((N_HEADS, 2, TB, TB)),
            _small_spec((1, DIFF_QK)), _small_spec((1, DIFF_QK)),
            _small_spec((1, DIFF_QK)), _small_spec((1, DIFF_QK)),
            _small_spec((HEAD_DIM, 1))],
        out_specs=pl.BlockSpec((1, TB, TB), lambda b, i: (b * nb + i, 0, 0)),
        out_shape=jax.ShapeDtypeStruct((nblk, TB, TB), BF16),
        scratch_shapes=_FlashScratch.shapes(n_chain),
        compiler_params=_cparams(2),
        name="diff_attn",
    )(qvt, kn, qvt, tiles, lq1, lk1, lq2, lk2, subln)


def _moba_kernel(nb, nbp, qt_ref, k_ref, vt_ref, tile_ref, o_ref, kmean_scr, sel_scr, *scratch):
    i = pl.program_id(1)
    sc = _FlashScratch(N_HEADS, *scratch)
    qpad_scr, acc_scr = sc.qpad, sc.acc

    @pl.when(i == 0)
    def _():
        kmean_scr[...] = jnp.zeros(kmean_scr.shape, F32)
        for jb in range(nb):
            blk = k_ref[jb * TB:(jb + 1) * TB, :].astype(F32)
            kmean_scr[jb:jb + 1, :] = jnp.mean(blk, axis=0, keepdims=True)

    for h in range(N_HEADS):
        qpad_scr[h] = _pad_rows(qt_ref[0, h * HEAD_DIM:(h + 1) * HEAD_DIM, :], (h % 2) * HEAD_DIM, LANES)

    blk_id = lax.broadcasted_iota(jnp.int32, (nbp, TB), 0)
    for h in range(N_HEADS):
        g = h // 2
        km = kmean_scr[:, g * LANES:(g + 1) * LANES].astype(BF16)
        gate = jnp.dot(km, qpad_scr[h], preferred_element_type=F32)
        avail = blk_id < i
        sel = jnp.zeros((nbp, TB), jnp.bool_)
        for _ in range(MOBA_TOPK):
            gm = jnp.where(avail, gate, -jnp.inf)
            best = jnp.max(gm, axis=0, keepdims=True)
            is_best = avail & (gm == best)
            first = jnp.min(jnp.where(is_best, blk_id, nbp), axis=0, keepdims=True)
            pick = blk_id == first
            sel = sel | pick
            avail = avail & jnp.logical_not(pick)
        sel_scr[h] = jnp.where(sel, 0.0, NEG).astype(F32)

    def block_chains(j, tile_idx):
        row = pl.multiple_of(j * TB, TB)
        chains = []
        for h in range(N_HEADS):
            g = h // 2
            kblk = k_ref[pl.ds(row, TB), g * LANES:(g + 1) * LANES]
            add = None if tile_idx is None else tile_ref[h, tile_idx]
            colmask = None if tile_idx == 0 else sel_scr[h, pl.ds(j, 1), :]
            chains.append((h, kblk, add, _with_ones(vt_ref[j, h * HEAD_DIM:(h + 1) * HEAD_DIM, :]), colmask))
        return chains

    _flash_causal(sc, i, block_chains, far_group=16)

    for h in range(N_HEADS):
        a = acc_scr[h]
        o_ref[0, h * HEAD_DIM:(h + 1) * HEAD_DIM, :] = (a[:HEAD_DIM] / a[HEAD_DIM:HEAD_DIM + 1]).astype(BF16)


def _moba_attention(kn, qvt, tiles, batch):
    nblk = qvt.shape[0]
    nb = nblk // batch
    nbp = max(8, -(-nb // 8) * 8)
    return pl.pallas_call(
        functools.partial(_moba_kernel, nb, nbp),
        grid=(batch, nb),
        in_specs=_attn_common_specs(nb, QV_MOBA_Q, KN_MOBA, QV_MOBA_V) + [
            _small_spec((N_HEADS, 2, TB, TB))],
        out_specs=pl.BlockSpec((1, TB, TB), lambda b, i: (b * nb + i, 0, 0)),
        out_shape=jax.ShapeDtypeStruct((nblk, TB, TB), BF16),
        scratch_shapes=[pltpu.VMEM((nbp, TB), F32),
                        pltpu.VMEM((N_HEADS, nbp, TB), F32)] + _FlashScratch.shapes(N_HEADS),
        compiler_params=_cparams(2),
        name="moba_attn",
    )(qvt, kn, qvt, tiles)


def _sb_kernel(qt_ref, k_ref, vt_ref, o_ref, qpad_scr, acc_scr, c_scr):
    i = pl.program_id(1)
    rows = lax.broadcasted_iota(jnp.int32, (TB, TB), 0)
    cols = lax.broadcasted_iota(jnp.int32, (TB, TB), 1)
    upper = (cols > rows).astype(BF16)
    strict = cols > rows

    heads = range(N_HEADS)

    def step(j, first):
        row = pl.multiple_of(j * TB, TB)
        zs = [jnp.dot(k_ref[pl.ds(row, TB), (h // 2) * LANES:(h // 2 + 1) * LANES], qpad_scr[h],
                      preferred_element_type=F32) for h in heads]
        log_keeps = []
        for z in zs:
            lk = -(jnp.maximum(z, 0.0) + jnp.log2(1.0 + jnp.exp2(-jnp.abs(z))))
            log_keeps.append(jnp.where(strict, lk, 0.0) if first else lk)
        laters = []
        for h in heads:
            hi = log_keeps[h].astype(BF16)
            lo = (log_keeps[h] - hi.astype(F32)).astype(BF16)
            later = (jnp.dot(upper, hi, preferred_element_type=F32)
                     + jnp.dot(upper, lo, preferred_element_type=F32))
            laters.append(later if first else later + c_scr[h])
        c_max = None
        for h in heads:
            w = jnp.exp2(zs[h] + log_keeps[h] + laters[h])
            if first:
                w = jnp.where(strict, w, 0.0)
            pv = jnp.dot(vt_ref[j, h * HEAD_DIM:(h + 1) * HEAD_DIM, :], w.astype(BF16),
                         preferred_element_type=F32)
            acc_scr[h] = pv if first else acc_scr[h] + pv
            c_new = laters[h][0:1, :] + log_keeps[h][0:1, :]
            c_scr[h] = c_new
            c_max = c_new if c_max is None else jnp.maximum(c_max, c_new)
        return jnp.max(c_max) > -SB_EXIT * LOG2E

    for h in heads:
        qpad_scr[h] = _pad_rows(qt_ref[0, h * HEAD_DIM:(h + 1) * HEAD_DIM, :], (h % 2) * HEAD_DIM, LANES)

    def cond(carry):
        j, go = carry
        return jnp.logical_and(j >= 0, go)

    def body(carry):
        j, _ = carry
        return j - 1, step(j, False)

    lax.while_loop(cond, body, (i - 1, step(i, True)))
    for h in heads:
        o_ref[0, h * HEAD_DIM:(h + 1) * HEAD_DIM, :] = acc_scr[h].astype(BF16)


def _sb_attention(kn, qvt, batch):
    nblk = qvt.shape[0]
    nb = nblk // batch
    return pl.pallas_call(
        _sb_kernel,
        grid=(batch, nb),
        in_specs=_attn_common_specs(nb, QV_SB_Q, KN_SB, QV_SB_V),
        out_specs=pl.BlockSpec((1, TB, TB), lambda b, i: (b * nb + i, 0, 0)),
        out_shape=jax.ShapeDtypeStruct((nblk, TB, TB), BF16),
        scratch_shapes=[pltpu.VMEM((N_HEADS, LANES, TB), BF16),
                        pltpu.VMEM((N_HEADS, HEAD_DIM, TB), F32),
                        pltpu.VMEM((N_HEADS, 1, TB), F32)],
        compiler_params=_cparams(2),
        name="sb_attn",
    )(qvt, kn, qvt)


def _swa_kernel(sink_ref, qt_ref, kc_ref, kp_ref, vc_ref, vp_ref, tc_ref, tp_ref, o_ref):
    i = pl.program_id(1)
    kc = kc_ref[...]
    kp = kp_ref[...]
    vc = vc_ref[0]
    vp = vp_ref[0][:, TB - SWA_WINDOW:]
    no_prev = jnp.where(i > 0, 0.0, NEG)
    group = N_HEADS // SWA_KV_HEADS
    heads = range(N_HEADS)
    qpads = [_pad_rows(qt_ref[0, h * HEAD_DIM:(h + 1) * HEAD_DIM, :], (h // group) * HEAD_DIM, LANES)
             for h in heads]
    s_cur = [jnp.dot(kc, qpads[h], preferred_element_type=F32) + tc_ref[h] for h in heads]
    s_prev = [jnp.dot(kp, qpads[h], preferred_element_type=F32) + tp_ref[h] + no_prev for h in heads]
    ms = [jnp.maximum(jnp.maximum(jnp.max(s_cur[h], axis=0, keepdims=True),
                                  jnp.max(s_prev[h], axis=0, keepdims=True)), sink_ref[h]) for h in heads]
    p_cur = [jnp.exp(s_cur[h] - ms[h]).astype(BF16) for h in heads]
    p_prev = [jnp.exp(s_prev[h] - ms[h]).astype(BF16) for h in heads]
    for h in heads:
        kv = h // group
        o = (jnp.dot(_with_ones(vc[kv * HEAD_DIM:(kv + 1) * HEAD_DIM]), p_cur[h], preferred_element_type=F32)
             + jnp.dot(_with_ones(vp[kv * HEAD_DIM:(kv + 1) * HEAD_DIM]), p_prev[h],
                       preferred_element_type=F32))
        denom = o[HEAD_DIM:HEAD_DIM + 1] + jnp.exp(sink_ref[h] - ms[h])
        o_ref[0, h * HEAD_DIM:(h + 1) * HEAD_DIM, :] = (o[:HEAD_DIM] / denom).astype(BF16)


def _swa_attention(kn, qvt, tile_cur, tile_prev, sinks, batch):
    nblk = qvt.shape[0]
    nb = nblk // batch
    half = TB // SWA_WINDOW
    kv_rows = SWA_KV_HEADS * HEAD_DIM
    return pl.pallas_call(
        _swa_kernel,
        grid=(batch, nb),
        in_specs=[pl.BlockSpec(memory_space=pltpu.SMEM),
                  pl.BlockSpec((1, TB, TB), lambda b, i: (b * nb + i, QV_SWA_Q, 0)),
                  pl.BlockSpec((TB, kv_rows), lambda b, i: (b * nb + i, KN_SWA_128)),
                  pl.BlockSpec((SWA_WINDOW, kv_rows),
                               lambda b, i: (b * nb * half + jnp.maximum(half * i - 1, 0), KN_SWA_128)),
                  pl.BlockSpec((1, kv_rows, TB), lambda b, i: (b * nb + i, QV_SWA_V_128, 0)),
                  pl.BlockSpec((1, kv_rows, TB), lambda b, i: (b * nb + jnp.maximum(i - 1, 0), QV_SWA_V_128, 0)),
                  _small_spec((N_HEADS, TB, TB)),
                  _small_spec((N_HEADS, SWA_WINDOW, TB))],
        out_specs=pl.BlockSpec((1, TB, TB), lambda b, i: (b * nb + i, 0, 0)),
        out_shape=jax.ShapeDtypeStruct((nblk, TB, TB), BF16),
        compiler_params=_cparams(2),
        name="swa_attn",
    )(sinks, qvt, kn, kn, qvt, qvt, tile_cur, tile_prev)


MERGE_T = 512
MERGE_COLS = 256


def _merge_route_kernel(x_ref, g_ref, oa_ref, ob_ref, oc_ref, od_ref, wg_ref, wbr_ref, wo_ref,
                        gf_ref, wr_ref, br_ref,
                        x1_ref, h2_ref, eid_ref, gate_ref, rank_ref, cnt_ref, base_scr):
    x = x_ref[...]
    h = _rms(x, g_ref[...]).astype(BF16)
    d = x.shape[1]
    chunks = []
    for n0 in range(0, d, MERGE_COLS):
        acc = None
        for bi, o_ref in enumerate((oa_ref, ob_ref, oc_ref, od_ref)):
            gate = jax.nn.sigmoid(jnp.dot(h, wg_ref[bi, :, n0:n0 + MERGE_COLS], preferred_element_type=F32))
            branch = jnp.concatenate(
                [lax.dot_general(o_ref[s], wbr_ref[bi, :, n0:n0 + MERGE_COLS], (((0,), (0,)), ((), ())),
                                 preferred_element_type=F32) for s in range(MERGE_T // TB)], axis=0)
            term = gate * branch
            acc = term if acc is None else acc + term
        chunks.append(acc.astype(BF16))
    merged = jnp.concatenate(chunks, axis=1)
    x1 = x + jnp.dot(merged, wo_ref[...], preferred_element_type=F32)
    x1_ref[...] = x1
    _route(x1, gf_ref, wr_ref, br_ref, h2_ref, eid_ref, gate_ref, rank_ref, cnt_ref, base_scr)


ROUTER_T = MERGE_T
ROUTER_ROWS = 8 + N_EXPERTS


def _first_argmax_rows(v, n_rows):
    best = jnp.max(v, axis=0, keepdims=True)
    ids = lax.broadcasted_iota(jnp.int32, v.shape, 0)
    return best, jnp.min(jnp.where(v == best, ids, n_rows), axis=0, keepdims=True)


def _route(x1, g_ref, w_ref, b_ref, h2_ref, eid_ref, gate_ref, rank_ref, cnt_ref, base_scr):
    i = pl.program_id(0)

    @pl.when(i == 0)
    def _():
        base_scr[...] = jnp.zeros(base_scr.shape, F32)

    h2 = _rms(x1, g_ref[...])
    h2_ref[...] = _pack_bf16_pairs(h2)
    nt = (((1,), (1,)), ((), ()))
    logits = lax.dot_general(w_ref[...], h2.astype(BF16), nt, preferred_element_type=F32) + b_ref[...]
    gl = logits[0:8]
    gmax, grp = _first_argmax_rows(gl, 8)
    p_grp = 1.0 / jnp.sum(jnp.exp(gl - gmax), axis=0, keepdims=True)
    e_sel = jnp.zeros((EXPERTS_PER_GROUP, ROUTER_T), F32)
    for g in range(N_GROUPS):
        e_sel = jnp.where(grp == g, logits[8 + 8 * g:16 + 8 * g], e_sel)
    ids8 = lax.broadcasted_iota(jnp.int32, e_sel.shape, 0)
    v1, i1 = _first_argmax_rows(e_sel, EXPERTS_PER_GROUP)
    e_rest = jnp.where(ids8 == i1, -jnp.inf, e_sel)
    v2, i2 = _first_argmax_rows(e_rest, EXPERTS_PER_GROUP)
    r = jnp.exp(v2 - v1)
    s1 = 1.0 / (1.0 + r)
    gate_ref[0:1, :] = p_grp * s1
    gate_ref[1:2, :] = p_grp * (r * s1)
    e1 = grp * EXPERTS_PER_GROUP + i1
    e2 = grp * EXPERTS_PER_GROUP + i2
    eid_ref[0:1, :] = e1
    eid_ref[1:2, :] = e2

    ids_e = lax.broadcasted_iota(jnp.int32, (N_EXPERTS, ROUTER_T), 0)
    oh1 = ids_e == e1
    oh2 = ids_e == e2
    cnt = oh1.astype(F32) + oh2.astype(F32)
    tr = lax.broadcasted_iota(jnp.int32, (ROUTER_T, ROUTER_T), 0)
    tc = lax.broadcasted_iota(jnp.int32, (ROUTER_T, ROUTER_T), 1)
    before = (tr < tc).astype(BF16)
    prefix = jnp.dot(cnt.astype(BF16), before, preferred_element_type=F32) + base_scr[:, 0:1]
    rank_ref[0:1, :] = jnp.sum(jnp.where(oh1, prefix, 0.0), axis=0, keepdims=True).astype(jnp.int32)
    rank_ref[1:2, :] = jnp.sum(jnp.where(oh2, prefix, 0.0), axis=0, keepdims=True).astype(jnp.int32)
    base_scr[...] = base_scr[...] + jnp.sum(cnt, axis=1, keepdims=True)
    cnt_ref[...] = base_scr[...]


def _merge_route(x2, g, o_a, o_b, o_c, o_d, wg, wbr, wo, g_ffn, w_route, b_route):
    n, d = x2.shape
    o_spec = pl.BlockSpec((MERGE_T // TB, TB, TB), lambda i: (i, 0, 0))
    row2 = lambda dt: jax.ShapeDtypeStruct((2, n), dt)
    spec2 = pl.BlockSpec((2, MERGE_T), lambda i: (0, i))
    return pl.pallas_call(
        _merge_route_kernel,
        grid=(n // MERGE_T,),
        in_specs=[pl.BlockSpec((MERGE_T, d), lambda i: (i, 0)),
                  pl.BlockSpec((1, d), lambda i: (0, 0)),
                  o_spec, o_spec, o_spec, o_spec,
                  pl.BlockSpec(wg.shape, lambda i: (0, 0, 0)),
                  pl.BlockSpec(wbr.shape, lambda i: (0, 0, 0)),
                  pl.BlockSpec(wo.shape, lambda i: (0, 0)),
                  pl.BlockSpec((1, d), lambda i: (0, 0)),
                  pl.BlockSpec((ROUTER_ROWS, d), lambda i: (0, 0)),
                  pl.BlockSpec((ROUTER_ROWS, 1), lambda i: (0, 0))],
        out_specs=[pl.BlockSpec((MERGE_T, d), lambda i: (i, 0)),
                   pl.BlockSpec((MERGE_T, d // 2), lambda i: (i, 0)), spec2, spec2, spec2,
                   pl.BlockSpec((N_EXPERTS, LANES), lambda i: (0, 0))],
        out_shape=[jax.ShapeDtypeStruct((n, d), F32),
                   jax.ShapeDtypeStruct((n, d // 2), jnp.uint32), row2(jnp.int32), row2(F32), row2(jnp.int32),
                   jax.ShapeDtypeStruct((N_EXPERTS, LANES), F32)],
        scratch_shapes=[pltpu.VMEM((N_EXPERTS, LANES), F32)],
        compiler_params=_cparams(1),
        name="merge_route",
    )(x2, g, o_a, o_b, o_c, o_d, wg, wbr, wo, g_ffn, w_route, b_route)


def _expert_kernel(be_ref, nu_ref, nv_ref, x_ref, w1_ref, w3_ref, w2_ref, y_ref):
    del be_ref
    used = pl.program_id(0) < nu_ref[0]

    @pl.when(used)
    def _():
        live = lax.broadcasted_iota(jnp.int32, x_ref.shape, 0) < nv_ref[pl.program_id(0)]
        xb = _unpack_bf16_pairs(jnp.where(live, x_ref[...], jnp.uint32(0))).astype(BF16)
        a = jnp.dot(xb, w1_ref[...].astype(BF16), preferred_element_type=F32)
        b = jnp.dot(xb, w3_ref[...].astype(BF16), preferred_element_type=F32)
        mid = (a * jax.nn.sigmoid(a) * b).astype(BF16)
        y_ref[...] = _pack_bf16_pairs(jnp.dot(mid, w2_ref[...].astype(BF16), preferred_element_type=F32))

    @pl.when(jnp.logical_not(used))
    def _():
        y_ref[...] = jnp.zeros(y_ref.shape, jnp.uint32)


def _experts(blk_expert, n_used, n_valid, xbuf, w1, w3, w2, layer):
    r = xbuf.shape[0]
    d = w1.shape[-2]
    de = w1.shape[-1]
    assert xbuf.shape[1] * 2 == d
    row_map = lambda i, be, nu, nv: (jnp.minimum(i, nu[0] - 1), 0)
    grid_spec = pltpu.PrefetchScalarGridSpec(
        num_scalar_prefetch=3,
        grid=(r // MOE_ROWS,),
        in_specs=[pl.BlockSpec((MOE_ROWS, d // 2), row_map),
                  pl.BlockSpec((None, None, d, de), lambda i, be, nu, nv: (layer, be[i], 0, 0)),
                  pl.BlockSpec((None, None, d, de), lambda i, be, nu, nv: (layer, be[i], 0, 0)),
                  pl.BlockSpec((None, None, de, d), lambda i, be, nu, nv: (layer, be[i], 0, 0))],
        out_specs=pl.BlockSpec((MOE_ROWS, d // 2), lambda i, be, nu, nv: (i, 0)),
    )
    return pl.pallas_call(
        _expert_kernel,
        grid_spec=grid_spec,
        out_shape=jax.ShapeDtypeStruct((r, d // 2), jnp.uint32),
        compiler_params=_cparams(1),
        name="experts",
    )(blk_expert, n_used, n_valid, xbuf, w1, w3, w2)


SC_CORES = 2
SC_SUBCORES = 16
SC_ROWS = 64


def _sc_gather_rows(table, idx):
    n_idx = idx.shape[0]
    d = table.shape[1]
    n_workers = SC_CORES * SC_SUBCORES
    per_worker = n_idx // n_workers
    n_chunk = per_worker // SC_ROWS
    assert per_worker * n_workers == n_idx and n_chunk * SC_ROWS == per_worker and n_chunk % 2 == 0
    mesh = plsc.VectorSubcoreMesh(core_axis_name="c", subcore_axis_name="s",
                                  num_cores=SC_CORES, num_subcores=SC_SUBCORES)

    def body(table_hbm, idx_hbm, out_hbm, idx_v, rows_v, gsem, wsem):
        worker = lax.axis_index("s") * SC_CORES + lax.axis_index("c")
        base = worker * per_worker
        pltpu.sync_copy(idx_hbm.at[pl.ds(base, per_worker)], idx_v)

        def gather(c, b):
            return pltpu.make_async_copy(table_hbm.at[idx_v.at[pl.ds(c * SC_ROWS, SC_ROWS)]],
                                         rows_v.at[b], gsem.at[b])

        def put(c, b):
            return pltpu.make_async_copy(rows_v.at[b], out_hbm.at[pl.ds(base + c * SC_ROWS, SC_ROWS)],
                                         wsem.at[b])

        gather(0, 0).start()

        @pl.loop(0, n_chunk, step=2)
        def _(c0):
            for b in range(2):
                c = c0 + b
                gather(c, b).wait()

                @pl.when(c + 1 < n_chunk)
                def _():
                    @pl.when(c >= 1)
                    def _():
                        put(c - 1, 1 - b).wait()

                    gather(c + 1, 1 - b).start()

                put(c, b).start()

        put(n_chunk - 2, 0).wait()
        put(n_chunk - 1, 1).wait()

    return pl.kernel(
        body,
        out_type=jax.ShapeDtypeStruct((n_idx, d), table.dtype),
        mesh=mesh,
        scratch_types=[pltpu.VMEM((per_worker,), jnp.int32),
                       pltpu.VMEM((2, SC_ROWS, d), table.dtype),
                       pltpu.SemaphoreType.DMA((2,)),
                       pltpu.SemaphoreType.DMA((2,))],
        name="sc_gather_rows",
    )(table, idx)


def _sc_scatter_rows(rows, dest, n_out):
    n, d = rows.shape
    n_workers = SC_CORES * SC_SUBCORES
    per_worker = n // n_workers
    n_chunk = per_worker // SC_ROWS
    assert per_worker * n_workers == n and n_chunk * SC_ROWS == per_worker and n_chunk % 2 == 0
    mesh = plsc.VectorSubcoreMesh(core_axis_name="c", subcore_axis_name="s",
                                  num_cores=SC_CORES, num_subcores=SC_SUBCORES)
    dest3 = dest.reshape(2, n // SC_ROWS, SC_ROWS)

    def body(rows_hbm, idx_hbm, out_hbm, idx_v, rows_v, lsem, ssem):
        worker = lax.axis_index("s") * SC_CORES + lax.axis_index("c")
        for k in range(2):
            pltpu.sync_copy(idx_hbm.at[k, pl.ds(worker * n_chunk, n_chunk)], idx_v.at[k])

        def load(c, b):
            return pltpu.make_async_copy(rows_hbm.at[pl.ds(worker * per_worker + c * SC_ROWS, SC_ROWS)],
                                         rows_v.at[b], lsem.at[b])

        def scatter(c, b, k):
            return pltpu.make_async_copy(rows_v.at[b], out_hbm.at[idx_v.at[k, c]], ssem.at[b])

        load(0, 0).start()

        @pl.loop(0, n_chunk, step=2)
        def _(c0):
            for b in range(2):
                c = c0 + b
                load(c, b).wait()

                @pl.when(c + 1 < n_chunk)
                def _():
                    @pl.when(c >= 1)
                    def _():
                        scatter(c - 1, 1 - b, 0).wait()
                        scatter(c - 1, 1 - b, 1).wait()

                    load(c + 1, 1 - b).start()

                scatter(c, b, 0).start()
                scatter(c, b, 1).start()

        for c, b in ((n_chunk - 2, 0), (n_chunk - 1, 1)):
            scatter(c, b, 0).wait()
            scatter(c, b, 1).wait()

    return pl.kernel(
        body,
        out_type=jax.ShapeDtypeStruct((n_out, d), rows.dtype),
        mesh=mesh,
        scratch_types=[pltpu.VMEM((2, n_chunk, SC_ROWS), jnp.int32),
                       pltpu.VMEM((2, SC_ROWS, d), rows.dtype),
                       pltpu.SemaphoreType.DMA((2,)),
                       pltpu.SemaphoreType.DMA((2,))],
        name="sc_scatter_rows",
    )(rows, dest3)


def _combine_dense_kernel(final, gate_ref, x1_ref, gf_ref, y0_ref, y1_ref, out_ref):
    gate = gate_ref[...]
    out = (x1_ref[...] + gate[:, 0:1] * _unpack_bf16_pairs(y0_ref[...])
           + gate[:, 1:2] * _unpack_bf16_pairs(y1_ref[...]))
    if final:
        out = _rms(out, gf_ref[...])
    out_ref[...] = out


def _combine_dense(gate_t, x1, g_final, yg, final):
    n, d = x1.shape
    nb = n // MERGE_T
    return pl.pallas_call(
        functools.partial(_combine_dense_kernel, final),
        grid=(nb,),
        in_specs=[pl.BlockSpec((MERGE_T, 2), lambda i: (i, 0)),
                  pl.BlockSpec((MERGE_T, d), lambda i: (i, 0)),
                  pl.BlockSpec((1, d), lambda i: (0, 0)),
                  pl.BlockSpec((MERGE_T, d // 2), lambda i: (i, 0)),
                  pl.BlockSpec((MERGE_T, d // 2), lambda i: (i + nb, 0))],
        out_specs=pl.BlockSpec((MERGE_T, d), lambda i: (i, 0)),
        out_shape=jax.ShapeDtypeStruct((n, d), F32),
        compiler_params=_cparams(1),
        name="combine_dense",
    )(gate_t, x1, g_final, yg, yg)


def _projection_weights(w):
    d = w.shape[0]
    blk = N_HEADS * HEAD_DIM
    kv = SWA_KV_HEADS * HEAD_DIM
    pa, pb, pc, pd = w[:, :3 * blk], w[:, 3 * blk:6 * blk], w[:, 6 * blk:9 * blk], w[:, 9 * blk:]
    half = N_HEADS * DIFF_QK

    def per_head(a, b):
        return jnp.stack([a.reshape(d, N_HEADS, DIFF_QK), b.reshape(d, N_HEADS, DIFF_QK)], axis=2).reshape(d, blk)

    s64, s32 = HEAD_DIM ** -0.5, DIFF_QK ** -0.5
    wn = jnp.concatenate([pa[:, blk:2 * blk], per_head(pb[:, 2 * half:3 * half], pb[:, 3 * half:4 * half]),
                          pc[:, blk:2 * blk], pd[:, blk:blk + kv]], axis=1)
    wt = jnp.concatenate([pa[:, :blk] * (s64 * LOG2E), pa[:, 2 * blk:],
                          per_head(pb[:, :half], pb[:, half:2 * half]) * (s32 * LOG2E), pb[:, 4 * half:],
                          pc[:, :blk] * (s64 * LOG2E), pc[:, 2 * blk:],
                          pd[:, :blk] * s64, pd[:, blk + kv:]], axis=1)
    assert wn.shape[1] == KN_COLS and wt.shape[1] == QV_ROWS
    return wn.astype(BF16), wt.T.astype(BF16)


def _router_weights(w_rg, b_rg, w_re, b_re):
    d = w_rg.shape[0]
    w = jnp.concatenate([w_rg.T, jnp.zeros((8 - N_GROUPS, d), F32), w_re.T], axis=0)
    b = jnp.concatenate([b_rg.astype(F32), jnp.full((8 - N_GROUPS,), NEG, F32), b_re.astype(F32)])[:, None]
    return w.astype(BF16), b


def _moe_plan(eid, rank, counts, n_rows_total):
    padded = (counts + MOE_ROWS - 1) // MOE_ROWS * MOE_ROWS
    pad_end = jnp.cumsum(padded)
    pad_start = pad_end - padded
    experts = jnp.arange(N_EXPERTS, dtype=jnp.int32)
    start_of = jnp.sum(jnp.where(eid[..., None] == experts, pad_start, 0), axis=-1)
    dest = start_of + rank
    n_blk = n_rows_total // MOE_ROWS
    n_used = (pad_end[-1] // MOE_ROWS).astype(jnp.int32)
    blk = jnp.minimum(jnp.arange(n_blk, dtype=jnp.int32), n_used - 1) * MOE_ROWS
    blk_expert = jnp.minimum(jnp.sum(pad_end[None, :] <= blk[:, None], axis=1), N_EXPERTS - 1).astype(jnp.int32)
    live_end = jnp.sum(jnp.where(blk_expert[:, None] == experts, pad_start + counts, 0), axis=-1)
    n_valid = jnp.clip(live_end - blk, 0, MOE_ROWS).astype(jnp.int32)
    return dest.astype(jnp.int32), blk_expert, n_used.reshape(1), n_valid


def kernel(x, rel_bias, g_mix, w_in, diff_lq1, diff_lk1, diff_lq2, diff_lk2, diff_subln, swa_sinks,
           w_gate, w_br, w_o, g_ffn, w_route_group, b_route_group, w_route_expert, b_route_expert,
           w1, w3, w2, g_final):
    batch, seq, d = x.shape
    n = batch * seq
    depth = w_in.shape[0]
    assert seq % TB == 0 and n % MERGE_T == 0 and TB == MOBA_BLOCK
    tab = rel_bias.T.astype(F32)
    tiles_moba = _causal_bias_tiles(tab[:N_HEADS])
    tiles_diff = _causal_bias_tiles(tab[N_HEADS:2 * N_HEADS])
    tile_cur, tile_prev = _swa_bias_tiles(tab[2 * N_HEADS:])
    n_rows_total = n * 2 + N_EXPERTS * MOE_ROWS
    row = lambda v: v.astype(F32)[None, :]

    x2 = x.reshape(n, d)
    for l in range(depth):
        lam_init = 0.8 - 0.6 * math.exp(-0.3 * l)
        wn, wt = _projection_weights(w_in[l])
        kn, qvt = _inproj(x2, row(g_mix[l]), wn, wt)
        o_a = _moba_attention(kn, qvt, tiles_moba, batch)
        o_b = _diff_attention(kn, qvt, tiles_diff, row(diff_lq1[l]), row(diff_lk1[l]), row(diff_lq2[l]),
                              row(diff_lk2[l]), diff_subln[l].astype(F32)[:, None], lam_init, batch)
        o_c = _sb_attention(kn, qvt, batch)
        o_d = _swa_attention(kn, qvt, tile_cur, tile_prev, swa_sinks[l].astype(F32), batch)
        w_route, r_bias = _router_weights(w_route_group[l], b_route_group[l], w_route_expert[l],
                                          b_route_expert[l])
        x1, h2, eid, gate, rank, cnt = _merge_route(
            x2, row(g_mix[l]), o_a, o_b, o_c, o_d, w_gate[l].astype(BF16), w_br[l].astype(BF16),
            w_o[l].astype(BF16), row(g_ffn[l]), w_route, r_bias)
        dest, blk_expert, n_used, n_valid = _moe_plan(eid, rank, cnt[:, 0].astype(jnp.int32), n_rows_total)
        xbuf = _sc_scatter_rows(h2, dest, n_rows_total)
        y = _experts(blk_expert, n_used, n_valid, xbuf, w1, w3, w2, l)
        yg = _sc_gather_rows(y, dest.reshape(-1))
        x2 = _combine_dense(gate.T, x1, row(g_final), yg, l == depth - 1)
    return x2.reshape(batch, seq, d)
```

```python
import functools
import math

import numpy as np
import jax
import jax.numpy as jnp
from jax import lax
from jax.experimental import pallas as pl
from jax.experimental.pallas import tpu as pltpu
from jax.experimental.pallas import tpu_sc as plsc

F32 = jnp.float32
BF16 = jnp.bfloat16

HEAD_DIM = 64
N_HEADS = 4
DIFF_QK = 32
SWA_KV_HEADS = 2
SWA_WINDOW = 128
MOBA_BLOCK = 256
MOBA_TOPK = 3
REL_BUCKETS = 32
REL_MAX_DIST = 128
N_GROUPS = 4
EXPERTS_PER_GROUP = 8
N_EXPERTS = N_GROUPS * EXPERTS_PER_GROUP
NORM_EPS = 1e-6

TB = 256
LANES = 128
ONES_ROWS = 16
NEG = -1e30
LOG2E = math.log2(math.e)
SB_EXIT = 104.0
MOE_ROWS = 512
VMEM_LIMIT = 56 * 1024 * 1024

QV_MOBA_Q, QV_MOBA_V, QV_DIFF_Q, QV_DIFF_V, QV_SB_Q, QV_SB_V, QV_SWA_Q = range(7)
QV_SWA_V_128 = 14
QV_ROWS = 7 * 256 + 128
KN_MOBA, KN_DIFF, KN_SB = range(3)
KN_SWA_128 = 6
KN_COLS = 3 * 256 + 128


def _cparams(n_grid):
    return pltpu.CompilerParams(dimension_semantics=("arbitrary",) * n_grid,
                                vmem_limit_bytes=VMEM_LIMIT)


def _rel_bucket_np(n):
    n = np.maximum(n, 0)
    max_exact = REL_BUCKETS // 2
    nf = np.maximum(n, 1).astype(np.float64)
    large = max_exact + (np.log(nf / max_exact) / math.log(REL_MAX_DIST / max_exact)
                         * (REL_BUCKETS - max_exact)).astype(np.int64)
    large = np.minimum(large, REL_BUCKETS - 1)
    return np.where(n < max_exact, n, large)


def _first_far_distance():
    d = np.arange(0, 4 * REL_MAX_DIST)
    b = _rel_bucket_np(d)
    return int(np.min(d[b == REL_BUCKETS - 1]))


def _toeplitz_bias(tab, rows, cols, base, valid_fn, shift_far, unit):
    length = rows + cols - 1
    off = np.concatenate([np.arange(0, cols), np.arange(cols - length, 0)])
    n = base + off
    onehot = np.zeros((REL_BUCKETS, length), np.float32)
    onehot[_rel_bucket_np(n), np.arange(length)] = 1.0
    vec = jnp.dot(tab, jnp.asarray(onehot), precision=lax.Precision.HIGHEST)
    if shift_far:
        vec = vec - tab[:, REL_BUCKETS - 1:]
    vec = jnp.where(jnp.asarray(valid_fn(n))[None, :], vec * unit, NEG).astype(F32)
    flat = jnp.tile(vec, (1, rows))[:, :rows * (length - 1)]
    return flat.reshape(tab.shape[0], rows, length - 1)[:, :, :cols]


def _causal_bias_tiles(tab):
    assert _first_far_distance() <= TB + 1
    tiles = [_toeplitz_bias(tab, TB, TB, d * TB, lambda n: n >= 0, True, LOG2E) for d in range(2)]
    return jnp.stack(tiles, axis=1)


def _swa_bias_tiles(tab):
    in_window = lambda n: (n >= 0) & (n < SWA_WINDOW)
    return (_toeplitz_bias(tab, TB, TB, 0, in_window, False, 1.0),
            _toeplitz_bias(tab, SWA_WINDOW, TB, SWA_WINDOW, in_window, False, 1.0))


def _pad_rows(q, off, total):
    n, t = q.shape
    parts = []
    if off:
        parts.append(jnp.zeros((off, t), q.dtype))
    parts.append(q)
    if total - off - n:
        parts.append(jnp.zeros((total - off - n, t), q.dtype))
    return jnp.concatenate(parts, axis=0) if len(parts) > 1 else q


def _with_ones(v):
    return jnp.concatenate([v, jnp.ones((ONES_ROWS, v.shape[1]), v.dtype)], axis=0)


def _pack_bf16_pairs(x):
    w = x.shape[1] // 2
    lo = lax.bitcast_convert_type(x[:, :w].astype(BF16).astype(F32), jnp.uint32)
    hi = lax.bitcast_convert_type(x[:, w:].astype(BF16).astype(F32), jnp.uint32)
    return hi | (lo >> 16)


def _unpack_bf16_pairs(u):
    lo = lax.bitcast_convert_type(u << 16, F32)
    hi = lax.bitcast_convert_type(u & jnp.uint32(0xFFFF0000), F32)
    return jnp.concatenate([lo, hi], axis=1)


def _rms(x, g_row):
    ms = jnp.mean(x * x, axis=-1, keepdims=True)
    return x * lax.rsqrt(ms + NORM_EPS) * g_row


IN_T = 512
IN_CHUNK = 384


def _inproj_kernel(x_ref, g_ref, wn_ref, wt_ref, kn_ref, qvt_ref):
    h = _rms(x_ref[...], g_ref[...]).astype(BF16)
    kn_ref[...] = jnp.dot(h, wn_ref[...], preferred_element_type=F32).astype(BF16)
    for r0 in range(0, QV_ROWS, IN_CHUNK):
        pt = lax.dot_general(wt_ref[r0:r0 + IN_CHUNK, :], h, (((1,), (1,)), ((), ())),
                             preferred_element_type=F32)
        for s in range(IN_T // TB):
            qvt_ref[s, r0:r0 + IN_CHUNK, :] = pt[:, s * TB:(s + 1) * TB].astype(BF16)


def _inproj(x2, g, wn, wt):
    n, d = x2.shape
    return pl.pallas_call(
        _inproj_kernel,
        grid=(n // IN_T,),
        in_specs=[pl.BlockSpec((IN_T, d), lambda i: (i, 0)),
                  pl.BlockSpec((1, d), lambda i: (0, 0)),
                  pl.BlockSpec((d, KN_COLS), lambda i: (0, 0)),
                  pl.BlockSpec((QV_ROWS, d), lambda i: (0, 0))],
        out_specs=[pl.BlockSpec((IN_T, KN_COLS), lambda i: (i, 0)),
                   pl.BlockSpec((IN_T // TB, QV_ROWS, TB), lambda i: (i, 0, 0))],
        out_shape=[jax.ShapeDtypeStruct((n, KN_COLS), BF16),
                   jax.ShapeDtypeStruct((n // TB, QV_ROWS, TB), BF16)],
        compiler_params=_cparams(1),
        name="inproj",
    )(x2, g, wn, wt)


FLASH_OVERFLOW = 100.0
FLASH_SKEW = 4


class _FlashScratch:
    def __init__(self, n_chain, qpad, m, acc, over, s):
        self.n_chain, self.qpad, self.m, self.acc, self.over, self.s = n_chain, qpad, m, acc, over, s

    @staticmethod
    def shapes(n_chain):
        return [pltpu.VMEM((n_chain, LANES, TB), BF16),
                pltpu.VMEM((n_chain, 1, TB), F32),
                pltpu.VMEM((n_chain, HEAD_DIM + ONES_ROWS, TB), F32),
                pltpu.VMEM((n_chain, 1, TB), F32),
                pltpu.VMEM((n_chain, TB, TB), F32)]


def _flash_two_pass(sc, chains):
    if len(chains) > sc.n_chain:
        for k in range(0, len(chains), sc.n_chain):
            _flash_two_pass(sc, chains[k:k + sc.n_chain])
        return
    block_max = []
    for c, kblk, add, _, colmask in chains:
        s = jnp.dot(kblk, sc.qpad[c], preferred_element_type=F32)
        if add is not None:
            s = s + add
        if colmask is not None:
            s = s + colmask
        sc.s[c] = s
        block_max.append(jnp.max(s, axis=0, keepdims=True))
    for (c, _, _, vext, _), mx in zip(chains, block_max):
        m_old = sc.m[c]
        m_new = jnp.maximum(m_old, mx)
        p = jnp.exp2(sc.s[c] - m_new).astype(BF16)
        sc.acc[c] = sc.acc[c] * jnp.exp2(m_old - m_new) + jnp.dot(vext, p, preferred_element_type=F32)
        sc.m[c] = m_new


def _flash_lagged(sc, chains, n_first=0):
    def finish(c, p, mx, ref, vext, first):
        pv = jnp.dot(vext, p, preferred_element_type=F32)
        ref_new = jnp.maximum(ref, mx)
        sc.acc[c] = (sc.acc[c] + pv) * jnp.exp2(ref - ref_new)
        sc.m[c] = ref_new
        sc.over[c] = jnp.maximum(sc.over[c], jnp.abs(mx - ref) if first else mx - ref)

    pending = []
    for idx, (c, kblk, add, vext, colmask) in enumerate(chains):
        s = jnp.dot(kblk, sc.qpad[c], preferred_element_type=F32)
        if add is not None:
            s = s + add
        if len(pending) >= min(FLASH_SKEW, sc.n_chain):
            finish(*pending.pop(0))
        ref = sc.m[c]
        mx = jnp.max(s, axis=0, keepdims=True)
        if colmask is None:
            p = jnp.exp2(s - ref)
        else:
            p = jnp.exp2(s - (ref - colmask))
            mx = mx + colmask
        pending.append((c, p.astype(BF16), mx, ref, vext, idx < n_first))
    for item in pending:
        finish(*item)


def _flash_causal(sc, i, block_chains, far_group):
    assert far_group & (far_group - 1) == 0

    def init(ref0):
        for c in range(sc.n_chain):
            sc.m[c] = jnp.full(sc.m.shape[1:], ref0, F32)
            sc.acc[c] = jnp.zeros(sc.acc.shape[1:], F32)
            sc.over[c] = jnp.full(sc.over.shape[1:], NEG, F32)

    def run(step):
        if step is _flash_two_pass:
            init(NEG)
            step(sc, block_chains(i, 0))

            @pl.when(i >= 1)
            def _():
                step(sc, block_chains(i - 1, 1))
        else:
            init(0.0)

            @pl.when(i == 0)
            def _():
                step(sc, block_chains(i, 0), sc.n_chain)

            @pl.when(i >= 1)
            def _():
                step(sc, block_chains(i, 0) + block_chains(i - 1, 1), sc.n_chain)

        n_far = jnp.maximum(i - 1, 0)

        def far_blocks(j, count):
            chains = []
            for k in range(count):
                chains += block_chains(j + k, None)
            step(sc, chains)

        if step is _flash_two_pass:
            lax.fori_loop(0, n_far, lambda j, carry: (far_blocks(j, 1), carry)[1], 0)
            return

        def far_group_step(t, carry):
            far_blocks(far_group * t, far_group)
            return carry

        n_group = n_far // far_group
        lax.fori_loop(0, n_group, far_group_step, 0)
        done = n_group * far_group
        size = far_group // 2
        while size >= 1:
            has_piece = (n_far & size) != 0

            @pl.when(has_piece)
            def _(done=done, size=size):
                far_blocks(done, size)

            done = done + jnp.where(has_piece, size, 0)
            size //= 2

    run(_flash_lagged)
    worst = sc.over[0]
    for c in range(1, sc.n_chain):
        worst = jnp.maximum(worst, sc.over[c])

    @pl.when(jnp.max(worst) > FLASH_OVERFLOW)
    def _():
        run(_flash_two_pass)


def _diff_kernel(lam_init, qt_ref, k_ref, vt_ref, tile_ref, lq1_ref, lk1_ref, lq2_ref, lk2_ref,
                 subln_ref, o_ref, *scratch):
    i = pl.program_id(1)
    sc = _FlashScratch(2 * N_HEADS, *scratch)
    acc_scr = sc.acc
    for h in range(N_HEADS):
        for mp in range(2):
            r0 = h * HEAD_DIM + mp * DIFF_QK
            sc.qpad[2 * h + mp] = _pad_rows(qt_ref[0, r0:r0 + DIFF_QK, :],
                                            (h % 2) * HEAD_DIM + mp * DIFF_QK, LANES)

    def block_chains(j, tile_idx):
        row = pl.multiple_of(j * TB, TB)
        chains = []
        for h in range(N_HEADS):
            g = h // 2
            kblk = k_ref[pl.ds(row, TB), g * LANES:(g + 1) * LANES]
            add = None if tile_idx is None else tile_ref[h, tile_idx]
            vext = _with_ones(vt_ref[j, h * HEAD_DIM:(h + 1) * HEAD_DIM, :])
            chains += [(2 * h + mp, kblk, add, vext, None) for mp in range(2)]
        return chains

    _flash_causal(sc, i, block_chains, far_group=8)

    lam =(jnp.exp(jnp.sum(lq1_ref[...] * lk1_ref[...], keepdims=True))
           - jnp.exp(jnp.sum(lq2_ref[...] * lk2_ref[...], keepdims=True)) + lam_init)
    for h in range(N_HEADS):
        a1 = acc_scr[2 * h]
        a2 = acc_scr[2 * h + 1]
        o1 = a1[:HEAD_DIM] / a1[HEAD_DIM:HEAD_DIM + 1]
        o2 = a2[:HEAD_DIM] / a2[HEAD_DIM:HEAD_DIM + 1]
        a = o1 - lam * o2
        ms = jnp.mean(a * a, axis=0, keepdims=True)
        y = a * lax.rsqrt(ms + NORM_EPS) * subln_ref[...] * (1.0 - lam_init)
        o_ref[0, h * HEAD_DIM:(h + 1) * HEAD_DIM, :] = y.astype(BF16)


def _attn_common_specs(nb, q_blk, k_blk, v_blk):
    return [pl.BlockSpec((1, TB, TB), lambda b, i: (b * nb + i, q_blk, 0)),
            pl.BlockSpec((nb * TB, TB), lambda b, i: (b, k_blk)),
            pl.BlockSpec((nb, TB, TB), lambda b, i: (b, v_blk, 0))]


def _small_spec(shape):
    return pl.BlockSpec(shape, lambda b, i: (0,) * len(shape))


def _diff_attention(kn, qvt, tiles, lq1, lk1, lq2, lk2, subln, lam_init, batch):
    nblk = qvt.shape[0]
    nb = nblk // batch
    n_chain = 2 * N_HEADS
    return pl.pallas_call(
        functools.partial(_diff_kernel, lam_init),
        grid=(batch, nb),
        in_specs=_attn_common_specs(nb, QV_DIFF_Q, KN_DIFF, QV_DIFF_V) + [
            _small_spec((N_HEADS, 2, TB, TB)),
            _small_spec((1, DIFF_QK)), _small_spec((1, DIFF_QK)),
            _small_spec((1, DIFF_QK)), _small_spec((1, DIFF_QK)),
            _small_spec((HEAD_DIM, 1))],
        out_specs=pl.BlockSpec((1, TB, TB), lambda b, i: (b * nb + i, 0, 0)),
        out_shape=jax.ShapeDtypeStruct((nblk, TB, TB), BF16),
        scratch_shapes=_FlashScratch.shapes(n_chain),
        compiler_params=_cparams(2),
        name="diff_attn",
    )(qvt, kn, qvt, tiles, lq1, lk1, lq2, lk2, subln)


def _moba_kernel(nb, nbp, qt_ref, k_ref, vt_ref, tile_ref, o_ref, kmean_scr, sel_scr, *scratch):
    i = pl.program_id(1)
    sc = _FlashScratch(N_HEADS, *scratch)
    qpad_scr, acc_scr = sc.qpad, sc.acc

    @pl.when(i == 0)
    def _():
        kmean_scr[...] = jnp.zeros(kmean_scr.shape, F32)
        for jb in range(nb):
            blk = k_ref[jb * TB:(jb + 1) * TB, :].astype(F32)
            kmean_scr[jb:jb + 1, :] = jnp.mean(blk, axis=0, keepdims=True)

    for h in range(N_HEADS):
        qpad_scr[h] = _pad_rows(qt_ref[0, h * HEAD_DIM:(h + 1) * HEAD_DIM, :], (h % 2) * HEAD_DIM, LANES)

    blk_id = lax.broadcasted_iota(jnp.int32, (nbp, TB), 0)
    for h in range(N_HEADS):
        g = h // 2
        km = kmean_scr[:, g * LANES:(g + 1) * LANES].astype(BF16)
        gate = jnp.dot(km, qpad_scr[h], preferred_element_type=F32)
        avail = blk_id < i
        sel = jnp.zeros((nbp, TB), jnp.bool_)
        for _ in range(MOBA_TOPK):
            gm = jnp.where(avail, gate, -jnp.inf)
            best = jnp.max(gm, axis=0, keepdims=True)
            is_best = avail & (gm == best)
            first = jnp.min(jnp.where(is_best, blk_id, nbp), axis=0, keepdims=True)
            pick = blk_id == first
            sel = sel | pick
            avail = avail & jnp.logical_not(pick)
        sel_scr[h] = jnp.where(sel, 0.0, NEG).astype(F32)

    def block_chains(j, tile_idx):
        row = pl.multiple_of(j * TB, TB)
        chains = []
        for h in range(N_HEADS):
            g = h // 2
            kblk = k_ref[pl.ds(row, TB), g * LANES:(g + 1) * LANES]
            add = None if tile_idx is None else tile_ref[h, tile_idx]
            colmask = None if tile_idx == 0 else sel_scr[h, pl.ds(j, 1), :]
            chains.append((h, kblk, add, _with_ones(vt_ref[j, h * HEAD_DIM:(h + 1) * HEAD_DIM, :]), colmask))
        return chains

    _flash_causal(sc, i, block_chains, far_group=16)

    for h in range(N_HEADS):
        a = acc_scr[h]
        o_ref[0, h * HEAD_DIM:(h + 1) * HEAD_DIM, :] = (a[:HEAD_DIM] / a[HEAD_DIM:HEAD_DIM + 1]).astype(BF16)


def _moba_attention(kn, qvt, tiles, batch):
    nblk = qvt.shape[0]
    nb = nblk // batch
    nbp = max(8, -(-nb // 8) * 8)
    return pl.pallas_call(
        functools.partial(_moba_kernel, nb, nbp),
        grid=(batch, nb),
        in_specs=_attn_common_specs(nb, QV_MOBA_Q, KN_MOBA, QV_MOBA_V) + [
            _small_spec((N_HEADS, 2, TB, TB))],
        out_specs=pl.BlockSpec((1, TB, TB), lambda b, i: (b * nb + i, 0, 0)),
        out_shape=jax.ShapeDtypeStruct((nblk, TB, TB), BF16),
        scratch_shapes=[pltpu.VMEM((nbp, TB), F32),
                        pltpu.VMEM((N_HEADS, nbp, TB), F32)] + _FlashScratch.shapes(N_HEADS),
        compiler_params=_cparams(2),
        name="moba_attn",
    )(qvt, kn, qvt, tiles)


def _sb_kernel(qt_ref, k_ref, vt_ref, o_ref, qpad_scr, acc_scr, c_scr):
    i = pl.program_id(1)
    rows = lax.broadcasted_iota(jnp.int32, (TB, TB), 0)
    cols = lax.broadcasted_iota(jnp.int32, (TB, TB), 1)
    upper = (cols > rows).astype(BF16)
    strict = cols > rows

    heads = range(N_HEADS)

    def step(j, first):
        row = pl.multiple_of(j * TB, TB)
        zs = [jnp.dot(k_ref[pl.ds(row, TB), (h // 2) * LANES:(h // 2 + 1) * LANES], qpad_scr[h],
                      preferred_element_type=F32) for h in heads]
        log_keeps = []
        for z in zs:
            lk = -(jnp.maximum(z, 0.0) + jnp.log2(1.0 + jnp.exp2(-jnp.abs(z))))
            log_keeps.append(jnp.where(strict, lk, 0.0) if first else lk)
        laters = []
        for h in heads:
            hi = log_keeps[h].astype(BF16)
            lo = (log_keeps[h] - hi.astype(F32)).astype(BF16)
            later = (jnp.dot(upper, hi, preferred_element_type=F32)
                     + jnp.dot(upper, lo, preferred_element_type=F32))
            laters.append(later if first else later + c_scr[h])
        c_max = None
        for h in heads:
            w = jnp.exp2(zs[h] + log_keeps[h] + laters[h])
            if first:
                w = jnp.where(strict, w, 0.0)
            pv = jnp.dot(vt_ref[j, h * HEAD_DIM:(h + 1) * HEAD_DIM, :], w.astype(BF16),
                         preferred_element_type=F32)
            acc_scr[h] = pv if first else acc_scr[h] + pv
            c_new = laters[h][0:1, :] + log_keeps[h][0:1, :]
            c_scr[h] = c_new
            c_max = c_new if c_max is None else jnp.maximum(c_max, c_new)
        return jnp.max(c_max) > -SB_EXIT * LOG2E

    for h in heads:
        qpad_scr[h] = _pad_rows(qt_ref[0, h * HEAD_DIM:(h + 1) * HEAD_DIM, :], (h % 2) * HEAD_DIM, LANES)

    def cond(carry):
        j, go = carry
        return jnp.logical_and(j >= 0, go)

    def body(carry):
        j, _ = carry
        return j - 1, step(j, False)

    lax.while_loop(cond, body, (i - 1, step(i, True)))
    for h in heads:
        o_ref[0, h * HEAD_DIM:(h + 1) * HEAD_DIM, :] = acc_scr[h].astype(BF16)


def _sb_attention(kn, qvt, batch):
    nblk = qvt.shape[0]
    nb = nblk // batch
    return pl.pallas_call(
        _sb_kernel,
        grid=(batch, nb),
        in_specs=_attn_common_specs(nb, QV_SB_Q, KN_SB, QV_SB_V),
        out_specs=pl.BlockSpec((1, TB, TB), lambda b, i: (b * nb + i, 0, 0)),
        out_shape=jax.ShapeDtypeStruct((nblk, TB, TB), BF16),
        scratch_shapes=[pltpu.VMEM((N_HEADS, LANES, TB), BF16),
                        pltpu.VMEM((N_HEADS, HEAD_DIM, TB), F32),
                        pltpu.VMEM((N_HEADS, 1, TB), F32)],
        compiler_params=_cparams(2),
        name="sb_attn",
    )(qvt, kn, qvt)


def _swa_kernel(sink_ref, qt_ref, kc_ref, kp_ref, vc_ref, vp_ref, tc_ref, tp_ref, o_ref):
    i = pl.program_id(1)
    kc = kc_ref[...]
    kp = kp_ref[...]
    vc = vc_ref[0]
    vp = vp_ref[0][:, TB - SWA_WINDOW:]
    no_prev = jnp.where(i > 0, 0.0, NEG)
    group = N_HEADS // SWA_KV_HEADS
    heads = range(N_HEADS)
    qpads = [_pad_rows(qt_ref[0, h * HEAD_DIM:(h + 1) * HEAD_DIM, :], (h // group) * HEAD_DIM, LANES)
             for h in heads]
    s_cur = [jnp.dot(kc, qpads[h], preferred_element_type=F32) + tc_ref[h] for h in heads]
    s_prev = [jnp.dot(kp, qpads[h], preferred_element_type=F32) + tp_ref[h] + no_prev for h in heads]
    ms = [jnp.maximum(jnp.maximum(jnp.max(s_cur[h], axis=0, keepdims=True),
                                  jnp.max(s_prev[h], axis=0, keepdims=True)), sink_ref[h]) for h in heads]
    p_cur = [jnp.exp(s_cur[h] - ms[h]).astype(BF16) for h in heads]
    p_prev = [jnp.exp(s_prev[h] - ms[h]).astype(BF16) for h in heads]
    for h in heads:
        kv = h // group
        o = (jnp.dot(_with_ones(vc[kv * HEAD_DIM:(kv + 1) * HEAD_DIM]), p_cur[h], preferred_element_type=F32)
             + jnp.dot(_with_ones(vp[kv * HEAD_DIM:(kv + 1) * HEAD_DIM]), p_prev[h],
                       preferred_element_type=F32))
        denom = o[HEAD_DIM:HEAD_DIM + 1] + jnp.exp(sink_ref[h] - ms[h])
        o_ref[0, h * HEAD_DIM:(h + 1) * HEAD_DIM, :] = (o[:HEAD_DIM] / denom).astype(BF16)


def _swa_attention(kn, qvt, tile_cur, tile_prev, sinks, batch):
    nblk = qvt.shape[0]
    nb = nblk // batch
    half = TB // SWA_WINDOW
    kv_rows = SWA_KV_HEADS * HEAD_DIM
    return pl.pallas_call(
        _swa_kernel,
        grid=(batch, nb),
        in_specs=[pl.BlockSpec(memory_space=pltpu.SMEM),
                  pl.BlockSpec((1, TB, TB), lambda b, i: (b * nb + i, QV_SWA_Q, 0)),
                  pl.BlockSpec((TB, kv_rows), lambda b, i: (b * nb + i, KN_SWA_128)),
                  pl.BlockSpec((SWA_WINDOW, kv_rows),
                               lambda b, i: (b * nb * half + jnp.maximum(half * i - 1, 0), KN_SWA_128)),
                  pl.BlockSpec((1, kv_rows, TB), lambda b, i: (b * nb + i, QV_SWA_V_128, 0)),
                  pl.BlockSpec((1, kv_rows, TB), lambda b, i: (b * nb + jnp.maximum(i - 1, 0), QV_SWA_V_128, 0)),
                  _small_spec((N_HEADS, TB, TB)),
                  _small_spec((N_HEADS, SWA_WINDOW, TB))],
        out_specs=pl.BlockSpec((1, TB, TB), lambda b, i: (b * nb + i, 0, 0)),
        out_shape=jax.ShapeDtypeStruct((nblk, TB, TB), BF16),
        compiler_params=_cparams(2),
        name="swa_attn",
    )(sinks, qvt, kn, kn, qvt, qvt, tile_cur, tile_prev)


MERGE_T = 512
MERGE_COLS = 256


def _merge_route_kernel(x_ref, g_ref, oa_ref, ob_ref, oc_ref, od_ref, wg_ref, wbr_ref, wo_ref,
                        gf_ref, wr_ref, br_ref,
                        x1_ref, h2_ref, eid_ref, gate_ref, rank_ref, cnt_ref, base_scr):
    x = x_ref[...]
    h = _rms(x, g_ref[...]).astype(BF16)
    d = x.shape[1]
    chunks = []
    for n0 in range(0, d, MERGE_COLS):
        acc = None
        for bi, o_ref in enumerate((oa_ref, ob_ref, oc_ref, od_ref)):
            gate = jax.nn.sigmoid(jnp.dot(h, wg_ref[bi, :, n0:n0 + MERGE_COLS], preferred_element_type=F32))
            branch = jnp.concatenate(
                [lax.dot_general(o_ref[s], wbr_ref[bi, :, n0:n0 + MERGE_COLS], (((0,), (0,)), ((), ())),
                                 preferred_element_type=F32) for s in range(MERGE_T // TB)], axis=0)
            term = gate * branch
            acc = term if acc is None else acc + term
        chunks.append(acc.astype(BF16))
    merged = jnp.concatenate(chunks, axis=1)
    x1 = x + jnp.dot(merged, wo_ref[...], preferred_element_type=F32)
    x1_ref[...] = x1
    _route(x1, gf_ref, wr_ref, br_ref, h2_ref, eid_ref, gate_ref, rank_ref, cnt_ref, base_scr)


ROUTER_T = MERGE_T
ROUTER_ROWS = 8 + N_EXPERTS


def _first_argmax_rows(v, n_rows):
    best = jnp.max(v, axis=0, keepdims=True)
    ids = lax.broadcasted_iota(jnp.int32, v.shape, 0)
    return best, jnp.min(jnp.where(v == best, ids, n_rows), axis=0, keepdims=True)


def _route(x1, g_ref, w_ref, b_ref, h2_ref, eid_ref, gate_ref, rank_ref, cnt_ref, base_scr):
    i = pl.program_id(0)

    @pl.when(i == 0)
    def _():
        base_scr[...] = jnp.zeros(base_scr.shape, F32)

    h2 = _rms(x1, g_ref[...])
    h2_ref[...] = _pack_bf16_pairs(h2)
    nt = (((1,), (1,)), ((), ()))
    logits = lax.dot_general(w_ref[...], h2.astype(BF16), nt, preferred_element_type=F32) + b_ref[...]
    gl = logits[0:8]
    gmax, grp = _first_argmax_rows(gl, 8)
    p_grp = 1.0 / jnp.sum(jnp.exp(gl - gmax), axis=0, keepdims=True)
    e_sel = jnp.zeros((EXPERTS_PER_GROUP, ROUTER_T), F32)
    for g in range(N_GROUPS):
        e_sel = jnp.where(grp == g, logits[8 + 8 * g:16 + 8 * g], e_sel)
    ids8 = lax.broadcasted_iota(jnp.int32, e_sel.shape, 0)
    v1, i1 = _first_argmax_rows(e_sel, EXPERTS_PER_GROUP)
    e_rest = jnp.where(ids8 == i1, -jnp.inf, e_sel)
    v2, i2 = _first_argmax_rows(e_rest, EXPERTS_PER_GROUP)
    r = jnp.exp(v2 - v1)
    s1 = 1.0 / (1.0 + r)
    gate_ref[0:1, :] = p_grp * s1
    gate_ref[1:2, :] = p_grp * (r * s1)
    e1 = grp * EXPERTS_PER_GROUP + i1
    e2 = grp * EXPERTS_PER_GROUP + i2
    eid_ref[0:1, :] = e1
    eid_ref[1:2, :] = e2

    ids_e = lax.broadcasted_iota(jnp.int32, (N_EXPERTS, ROUTER_T), 0)
    oh1 = ids_e == e1
    oh2 = ids_e == e2
    cnt = oh1.astype(F32) + oh2.astype(F32)
    tr = lax.broadcasted_iota(jnp.int32, (ROUTER_T, ROUTER_T), 0)
    tc = lax.broadcasted_iota(jnp.int32, (ROUTER_T, ROUTER_T), 1)
    before = (tr < tc).astype(BF16)
    prefix = jnp.dot(cnt.astype(BF16), before, preferred_element_type=F32) + base_scr[:, 0:1]
    rank_ref[0:1, :] = jnp.sum(jnp.where(oh1, prefix, 0.0), axis=0, keepdims=True).astype(jnp.int32)
    rank_ref[1:2, :] = jnp.sum(jnp.where(oh2, prefix, 0.0), axis=0, keepdims=True).astype(jnp.int32)
    base_scr[...] = base_scr[...] + jnp.sum(cnt, axis=1, keepdims=True)
    cnt_ref[...] = base_scr[...]


def _merge_route(x2, g, o_a, o_b, o_c, o_d, wg, wbr, wo, g_ffn, w_route, b_route):
    n, d = x2.shape
    o_spec = pl.BlockSpec((MERGE_T // TB, TB, TB), lambda i: (i, 0, 0))
    row2 = lambda dt: jax.ShapeDtypeStruct((2, n), dt)
    spec2 = pl.BlockSpec((2, MERGE_T), lambda i: (0, i))
    return pl.pallas_call(
        _merge_route_kernel,
        grid=(n // MERGE_T,),
        in_specs=[pl.BlockSpec((MERGE_T, d), lambda i: (i, 0)),
                  pl.BlockSpec((1, d), lambda i: (0, 0)),
                  o_spec, o_spec, o_spec, o_spec,
                  pl.BlockSpec(wg.shape, lambda i: (0, 0, 0)),
                  pl.BlockSpec(wbr.shape, lambda i: (0, 0, 0)),
                  pl.BlockSpec(wo.shape, lambda i: (0, 0)),
                  pl.BlockSpec((1, d), lambda i: (0, 0)),
                  pl.BlockSpec((ROUTER_ROWS, d), lambda i: (0, 0)),
                  pl.BlockSpec((ROUTER_ROWS, 1), lambda i: (0, 0))],
        out_specs=[pl.BlockSpec((MERGE_T, d), lambda i: (i, 0)),
                   pl.BlockSpec((MERGE_T, d // 2), lambda i: (i, 0)), spec2, spec2, spec2,
                   pl.BlockSpec((N_EXPERTS, LANES), lambda i: (0, 0))],
        out_shape=[jax.ShapeDtypeStruct((n, d), F32),
                   jax.ShapeDtypeStruct((n, d // 2), jnp.uint32), row2(jnp.int32), row2(F32), row2(jnp.int32),
                   jax.ShapeDtypeStruct((N_EXPERTS, LANES), F32)],
        scratch_shapes=[pltpu.VMEM((N_EXPERTS, LANES), F32)],
        compiler_params=_cparams(1),
        name="merge_route",
    )(x2, g, o_a, o_b, o_c, o_d, wg, wbr, wo, g_ffn, w_route, b_route)


def _expert_kernel(be_ref, nu_ref, nv_ref, x_ref, w1_ref, w3_ref, w2_ref, y_ref, w1_scr, w3_scr, w2_scr):
    i = pl.program_id(0)
    used = i < nu_ref[0]

    @pl.when(jnp.logical_or(i == 0, be_ref[i] != be_ref[jnp.maximum(i - 1, 0)]))
    def _():
        w1_scr[...] = w1_ref[...].astype(BF16)
        w3_scr[...] = w3_ref[...].astype(BF16)
        w2_scr[...] = w2_ref[...].astype(BF16)

    @pl.when(used)
    def _():
        live = lax.broadcasted_iota(jnp.int32, x_ref.shape, 0) < nv_ref[i]
        xb = _unpack_bf16_pairs(jnp.where(live, x_ref[...], jnp.uint32(0))).astype(BF16)
        a = jnp.dot(xb, w1_scr[...], preferred_element_type=F32)
        b = jnp.dot(xb, w3_scr[...], preferred_element_type=F32)
        mid = (a * jax.nn.sigmoid(a) * b).astype(BF16)
        y_ref[...] = _pack_bf16_pairs(jnp.dot(mid, w2_scr[...], preferred_element_type=F32))

    @pl.when(jnp.logical_not(used))
    def _():
        y_ref[...] = jnp.zeros(y_ref.shape, jnp.uint32)


def _experts(blk_expert, n_used, n_valid, xbuf, w1, w3, w2, layer):
    r = xbuf.shape[0]
    d = w1.shape[-2]
    de = w1.shape[-1]
    assert xbuf.shape[1] * 2 == d
    row_map = lambda i, be, nu, nv: (jnp.minimum(i, nu[0] - 1), 0)
    grid_spec = pltpu.PrefetchScalarGridSpec(
        num_scalar_prefetch=3,
        grid=(r // MOE_ROWS,),
        in_specs=[pl.BlockSpec((MOE_ROWS, d // 2), row_map),
                  pl.BlockSpec((None, None, d, de), lambda i, be, nu, nv: (layer, be[i], 0, 0)),
                  pl.BlockSpec((None, None, d, de), lambda i, be, nu, nv: (layer, be[i], 0, 0)),
                  pl.BlockSpec((None, None, de, d), lambda i, be, nu, nv: (layer, be[i], 0, 0))],
        out_specs=pl.BlockSpec((MOE_ROWS, d // 2), lambda i, be, nu, nv: (i, 0)),
        scratch_shapes=[pltpu.VMEM((d, de), BF16), pltpu.VMEM((d, de), BF16), pltpu.VMEM((de, d), BF16)],
    )
    return pl.pallas_call(
        _expert_kernel,
        grid_spec=grid_spec,
        out_shape=jax.ShapeDtypeStruct((r, d // 2), jnp.uint32),
        compiler_params=_cparams(1),
        name="experts",
    )(blk_expert, n_used, n_valid, xbuf, w1, w3, w2)


SC_CORES = 2
SC_SUBCORES = 16
SC_ROWS = 64


def _sc_gather_rows(table, idx):
    n_idx = idx.shape[0]
    d = table.shape[1]
    n_workers = SC_CORES * SC_SUBCORES
    per_worker = n_idx // n_workers
    n_chunk = per_worker // SC_ROWS
    assert per_worker * n_workers == n_idx and n_chunk * SC_ROWS == per_worker and n_chunk % 2 == 0
    mesh = plsc.VectorSubcoreMesh(core_axis_name="c", subcore_axis_name="s",
                                  num_cores=SC_CORES, num_subcores=SC_SUBCORES)

    def body(table_hbm, idx_hbm, out_hbm, idx_v, rows_v, gsem, wsem):
        worker = lax.axis_index("s") * SC_CORES + lax.axis_index("c")
        base = worker * per_worker
        pltpu.sync_copy(idx_hbm.at[pl.ds(base, per_worker)], idx_v)

        def gather(c, b):
            return pltpu.make_async_copy(table_hbm.at[idx_v.at[pl.ds(c * SC_ROWS, SC_ROWS)]],
                                         rows_v.at[b], gsem.at[b])

        def put(c, b):
            return pltpu.make_async_copy(rows_v.at[b], out_hbm.at[pl.ds(base + c * SC_ROWS, SC_ROWS)],
                                         wsem.at[b])

        gather(0, 0).start()

        @pl.loop(0, n_chunk, step=2)
        def _(c0):
            for b in range(2):
                c = c0 + b
                gather(c, b).wait()

                @pl.when(c + 1 < n_chunk)
                def _():
                    @pl.when(c >= 1)
                    def _():
                        put(c - 1, 1 - b).wait()

                    gather(c + 1, 1 - b).start()

                put(c, b).start()

        put(n_chunk - 2, 0).wait()
        put(n_chunk - 1, 1).wait()

    return pl.kernel(
        body,
        out_type=jax.ShapeDtypeStruct((n_idx, d), table.dtype),
        mesh=mesh,
        scratch_types=[pltpu.VMEM((per_worker,), jnp.int32),
                       pltpu.VMEM((2, SC_ROWS, d), table.dtype),
                       pltpu.SemaphoreType.DMA((2,)),
                       pltpu.SemaphoreType.DMA((2,))],
        name="sc_gather_rows",
    )(table, idx)


def _sc_scatter_rows(rows, dest, n_out):
    n, d = rows.shape
    n_workers = SC_CORES * SC_SUBCORES
    per_worker = n // n_workers
    n_chunk = per_worker // SC_ROWS
    assert per_worker * n_workers == n and n_chunk * SC_ROWS == per_worker and n_chunk % 2 == 0
    mesh = plsc.VectorSubcoreMesh(core_axis_name="c", subcore_axis_name="s",
                                  num_cores=SC_CORES, num_subcores=SC_SUBCORES)
    dest3 = dest.reshape(2, n // SC_ROWS, SC_ROWS)

    def body(rows_hbm, idx_hbm, out_hbm, idx_v, rows_v, lsem, ssem):
        worker = lax.axis_index("s") * SC_CORES + lax.axis_index("c")
        for k in range(2):
            pltpu.sync_copy(idx_hbm.at[k, pl.ds(worker * n_chunk, n_chunk)], idx_v.at[k])

        def load(c, b):
            return pltpu.make_async_copy(rows_hbm.at[pl.ds(worker * per_worker + c * SC_ROWS, SC_ROWS)],
                                         rows_v.at[b], lsem.at[b])

        def scatter(c, b, k):
            return pltpu.make_async_copy(rows_v.at[b], out_hbm.at[idx_v.at[k, c]], ssem.at[b])

        load(0, 0).start()

        @pl.loop(0, n_chunk, step=2)
        def _(c0):
            for b in range(2):
                c = c0 + b
                load(c, b).wait()

                @pl.when(c + 1 < n_chunk)
                def _():
                    @pl.when(c >= 1)
                    def _():
                        scatter(c - 1, 1 - b, 0).wait()
                        scatter(c - 1, 1 - b, 1).wait()

                    load(c + 1, 1 - b).start()

                scatter(c, b, 0).start()
                scatter(c, b, 1).start()

        for c, b in ((n_chunk - 2, 0), (n_chunk - 1, 1)):
            scatter(c, b, 0).wait()
            scatter(c, b, 1).wait()

    return pl.kernel(
        body,
        out_type=jax.ShapeDtypeStruct((n_out, d), rows.dtype),
        mesh=mesh,
        scratch_types=[pltpu.VMEM((2, n_chunk, SC_ROWS), jnp.int32),
                       pltpu.VMEM((2, SC_ROWS, d), rows.dtype),
                       pltpu.SemaphoreType.DMA((2,)),
                       pltpu.SemaphoreType.DMA((2,))],
        name="sc_scatter_rows",
    )(rows, dest3)


def _combine_dense_kernel(final, gate_ref, x1_ref, gf_ref, y0_ref, y1_ref, out_ref):
    gate = gate_ref[...]
    out = (x1_ref[...] + gate[:, 0:1] * _unpack_bf16_pairs(y0_ref[...])
           + gate[:, 1:2] * _unpack_bf16_pairs(y1_ref[...]))
    if final:
        out = _rms(out, gf_ref[...])
    out_ref[...] = out


def _combine_dense(gate_t, x1, g_final, yg, final):
    n, d = x1.shape
    nb = n // MERGE_T
    return pl.pallas_call(
        functools.partial(_combine_dense_kernel, final),
        grid=(nb,),
        in_specs=[pl.BlockSpec((MERGE_T, 2), lambda i: (i, 0)),
                  pl.BlockSpec((MERGE_T, d), lambda i: (i, 0)),
                  pl.BlockSpec((1, d), lambda i: (0, 0)),
                  pl.BlockSpec((MERGE_T, d // 2), lambda i: (i, 0)),
                  pl.BlockSpec((MERGE_T, d // 2), lambda i: (i + nb, 0))],
        out_specs=pl.BlockSpec((MERGE_T, d), lambda i: (i, 0)),
        out_shape=jax.ShapeDtypeStruct((n, d), F32),
        compiler_params=_cparams(1),
        name="combine_dense",
    )(gate_t, x1, g_final, yg, yg)


def _projection_weights(w):
    d = w.shape[0]
    blk = N_HEADS * HEAD_DIM
    kv = SWA_KV_HEADS * HEAD_DIM
    pa, pb, pc, pd = w[:, :3 * blk], w[:, 3 * blk:6 * blk], w[:, 6 * blk:9 * blk], w[:, 9 * blk:]
    half = N_HEADS * DIFF_QK

    def per_head(a, b):
        return jnp.stack([a.reshape(d, N_HEADS, DIFF_QK), b.reshape(d, N_HEADS, DIFF_QK)], axis=2).reshape(d, blk)

    s64, s32 = HEAD_DIM ** -0.5, DIFF_QK ** -0.5
    wn = jnp.concatenate([pa[:, blk:2 * blk], per_head(pb[:, 2 * half:3 * half], pb[:, 3 * half:4 * half]),
                          pc[:, blk:2 * blk], pd[:, blk:blk + kv]], axis=1)
    wt = jnp.concatenate([pa[:, :blk] * (s64 * LOG2E), pa[:, 2 * blk:],
                          per_head(pb[:, :half], pb[:, half:2 * half]) * (s32 * LOG2E), pb[:, 4 * half:],
                          pc[:, :blk] * (s64 * LOG2E), pc[:, 2 * blk:],
                          pd[:, :blk] * s64, pd[:, blk + kv:]], axis=1)
    assert wn.shape[1] == KN_COLS and wt.shape[1] == QV_ROWS
    return wn.astype(BF16), wt.T.astype(BF16)


def _router_weights(w_rg, b_rg, w_re, b_re):
    d = w_rg.shape[0]
    w = jnp.concatenate([w_rg.T, jnp.zeros((8 - N_GROUPS, d), F32), w_re.T], axis=0)
    b = jnp.concatenate([b_rg.astype(F32), jnp.full((8 - N_GROUPS,), NEG, F32), b_re.astype(F32)])[:, None]
    return w.astype(BF16), b


def _moe_plan(eid, rank, counts, n_rows_total):
    padded = (counts + MOE_ROWS - 1) // MOE_ROWS * MOE_ROWS
    pad_end = jnp.cumsum(padded)
    pad_start = pad_end - padded
    experts = jnp.arange(N_EXPERTS, dtype=jnp.int32)
    start_of = jnp.sum(jnp.where(eid[..., None] == experts, pad_start, 0), axis=-1)
    dest = start_of + rank
    n_blk = n_rows_total // MOE_ROWS
    n_used = (pad_end[-1] // MOE_ROWS).astype(jnp.int32)
    blk = jnp.minimum(jnp.arange(n_blk, dtype=jnp.int32), n_used - 1) * MOE_ROWS
    blk_expert = jnp.minimum(jnp.sum(pad_end[None, :] <= blk[:, None], axis=1), N_EXPERTS - 1).astype(jnp.int32)
    live_end = jnp.sum(jnp.where(blk_expert[:, None] == experts, pad_start + counts, 0), axis=-1)
    n_valid = jnp.clip(live_end - blk, 0, MOE_ROWS).astype(jnp.int32)
    return dest.astype(jnp.int32), blk_expert, n_used.reshape(1), n_valid


def kernel(x, rel_bias, g_mix, w_in, diff_lq1, diff_lk1, diff_lq2, diff_lk2, diff_subln, swa_sinks,
           w_gate, w_br, w_o, g_ffn, w_route_group, b_route_group, w_route_expert, b_route_expert,
           w1, w3, w2, g_final):
    batch, seq, d = x.shape
    n = batch * seq
    depth = w_in.shape[0]
    assert seq % TB == 0 and n % MERGE_T == 0 and TB == MOBA_BLOCK
    tab = rel_bias.T.astype(F32)
    tiles_moba = _causal_bias_tiles(tab[:N_HEADS])
    tiles_diff = _causal_bias_tiles(tab[N_HEADS:2 * N_HEADS])
    tile_cur, tile_prev = _swa_bias_tiles(tab[2 * N_HEADS:])
    n_rows_total = n * 2 + N_EXPERTS * MOE_ROWS
    row = lambda v: v.astype(F32)[None, :]

    x2 = x.reshape(n, d)
    for l in range(depth):
        lam_init = 0.8 - 0.6 * math.exp(-0.3 * l)
        wn, wt = _projection_weights(w_in[l])
        kn, qvt = _inproj(x2, row(g_mix[l]), wn, wt)
        o_a = _moba_attention(kn, qvt, tiles_moba, batch)
        o_b = _diff_attention(kn, qvt, tiles_diff, row(diff_lq1[l]), row(diff_lk1[l]), row(diff_lq2[l]),
                              row(diff_lk2[l]), diff_subln[l].astype(F32)[:, None], lam_init, batch)
        o_c = _sb_attention(kn, qvt, batch)
        o_d = _swa_attention(kn, qvt, tile_cur, tile_prev, swa_sinks[l].astype(F32), batch)
        w_route, r_bias = _router_weights(w_route_group[l], b_route_group[l], w_route_expert[l],
                                          b_route_expert[l])
        x1, h2, eid, gate, rank, cnt = _merge_route(
            x2, row(g_mix[l]), o_a, o_b, o_c, o_d, w_gate[l].astype(BF16), w_br[l].astype(BF16),
            w_o[l].astype(BF16), row(g_ffn[l]), w_route, r_bias)
        dest, blk_expert, n_used, n_valid = _moe_plan(eid, rank, cnt[:, 0].astype(jnp.int32), n_rows_total)
        xbuf = _sc_scatter_rows(h2, dest, n_rows_total)
        y = _experts(blk_expert, n_used, n_valid, xbuf, w1, w3, w2, l)
        yg = _sc_gather_rows(y, dest.reshape(-1))
        x2 = _combine_dense(gate.T, x1, row(g_final), yg, l == depth - 1)
    return x2.reshape(batch, seq, d)
```

```python
import functools
import math

import numpy as np
import jax
import jax.numpy as jnp
from jax import lax
from jax.experimental import pallas as pl
from jax.experimental.pallas import tpu as pltpu
from jax.experimental.pallas import tpu_sc as plsc

F32 = jnp.float32
BF16 = jnp.bfloat16

HEAD_DIM = 64
N_HEADS = 4
DIFF_QK = 32
SWA_KV_HEADS = 2
SWA_WINDOW = 128
MOBA_BLOCK = 256
MOBA_TOPK = 3
REL_BUCKETS = 32
REL_MAX_DIST = 128
N_GROUPS = 4
EXPERTS_PER_GROUP = 8
N_EXPERTS = N_GROUPS * EXPERTS_PER_GROUP
NORM_EPS = 1e-6

TB = 256
LANES = 128
ONES_ROWS = 16
NEG = -1e30
LOG2E = math.log2(math.e)
SB_EXIT = 104.0
MOE_ROWS = 512
EXPERT_COLS = 256
VMEM_LIMIT = 56 * 1024 * 1024

QV_MOBA_Q, QV_MOBA_V, QV_DIFF_Q, QV_DIFF_V, QV_SB_Q, QV_SB_V, QV_SWA_Q = range(7)
QV_SWA_V_128 = 14
QV_ROWS = 7 * 256 + 128
KN_MOBA, KN_DIFF, KN_SB = range(3)
KN_SWA_128 = 6
KN_COLS = 3 * 256 + 128


def _cparams(n_grid):
    return pltpu.CompilerParams(dimension_semantics=("arbitrary",) * n_grid,
                                vmem_limit_bytes=VMEM_LIMIT)


def _rel_bucket_np(n):
    n = np.maximum(n, 0)
    max_exact = REL_BUCKETS // 2
    nf = np.maximum(n, 1).astype(np.float64)
    large = max_exact + (np.log(nf / max_exact) / math.log(REL_MAX_DIST / max_exact)
                         * (REL_BUCKETS - max_exact)).astype(np.int64)
    large = np.minimum(large, REL_BUCKETS - 1)
    return np.where(n < max_exact, n, large)


def _first_far_distance():
    d = np.arange(0, 4 * REL_MAX_DIST)
    b = _rel_bucket_np(d)
    return int(np.min(d[b == REL_BUCKETS - 1]))


def _toeplitz_bias(tab, rows, cols, base, valid_fn, shift_far, unit):
    length = rows + cols - 1
    off = np.concatenate([np.arange(0, cols), np.arange(cols - length, 0)])
    n = base + off
    onehot = np.zeros((REL_BUCKETS, length), np.float32)
    onehot[_rel_bucket_np(n), np.arange(length)] = 1.0
    vec = jnp.dot(tab, jnp.asarray(onehot), precision=lax.Precision.HIGHEST)
    if shift_far:
        vec = vec - tab[:, REL_BUCKETS - 1:]
    vec = jnp.where(jnp.asarray(valid_fn(n))[None, :], vec * unit, NEG).astype(F32)
    flat = jnp.tile(vec, (1, rows))[:, :rows * (length - 1)]
    return flat.reshape(tab.shape[0], rows, length - 1)[:, :, :cols]


def _causal_bias_tiles(tab):
    assert _first_far_distance() <= TB + 1
    tiles = [_toeplitz_bias(tab, TB, TB, d * TB, lambda n: n >= 0, True, LOG2E) for d in range(2)]
    return jnp.stack(tiles, axis=1)


def _swa_bias_tiles(tab):
    in_window = lambda n: (n >= 0) & (n < SWA_WINDOW)
    return (_toeplitz_bias(tab, TB, TB, 0, in_window, False, 1.0),
            _toeplitz_bias(tab, SWA_WINDOW, TB, SWA_WINDOW, in_window, False, 1.0))


def _pad_rows(q, off, total):
    n, t = q.shape
    parts = []
    if off:
        parts.append(jnp.zeros((off, t), q.dtype))
    parts.append(q)
    if total - off - n:
        parts.append(jnp.zeros((total - off - n, t), q.dtype))
    return jnp.concatenate(parts, axis=0) if len(parts) > 1 else q


def _with_ones(v):
    return jnp.concatenate([v, jnp.ones((ONES_ROWS, v.shape[1]), v.dtype)], axis=0)


def _pack_bf16_pairs(x):
    w = x.shape[1] // 2
    lo = lax.bitcast_convert_type(x[:, :w].astype(BF16).astype(F32), jnp.uint32)
    hi = lax.bitcast_convert_type(x[:, w:].astype(BF16).astype(F32), jnp.uint32)
    return hi | (lo >> 16)


def _unpack_bf16_pairs(u):
    lo = lax.bitcast_convert_type(u << 16, F32)
    hi = lax.bitcast_convert_type(u & jnp.uint32(0xFFFF0000), F32)
    return jnp.concatenate([lo, hi], axis=1)


def _rms(x, g_row):
    ms = jnp.mean(x * x, axis=-1, keepdims=True)
    return x * lax.rsqrt(ms + NORM_EPS) * g_row


IN_T = 512
IN_CHUNK = 384


def _inproj_kernel(x_ref, g_ref, wn_ref, wt_ref, kn_ref, qvt_ref):
    h = _rms(x_ref[...], g_ref[...]).astype(BF16)
    kn_ref[...] = jnp.dot(h, wn_ref[...], preferred_element_type=F32).astype(BF16)
    for r0 in range(0, QV_ROWS, IN_CHUNK):
        pt = lax.dot_general(wt_ref[r0:r0 + IN_CHUNK, :], h, (((1,), (1,)), ((), ())),
                             preferred_element_type=F32)
        for s in range(IN_T // TB):
            qvt_ref[s, r0:r0 + IN_CHUNK, :] = pt[:, s * TB:(s + 1) * TB].astype(BF16)


def _inproj(x2, g, wn, wt):
    n, d = x2.shape
    return pl.pallas_call(
        _inproj_kernel,
        grid=(n // IN_T,),
        in_specs=[pl.BlockSpec((IN_T, d), lambda i: (i, 0)),
                  pl.BlockSpec((1, d), lambda i: (0, 0)),
                  pl.BlockSpec((d, KN_COLS), lambda i: (0, 0)),
                  pl.BlockSpec((QV_ROWS, d), lambda i: (0, 0))],
        out_specs=[pl.BlockSpec((IN_T, KN_COLS), lambda i: (i, 0)),
                   pl.BlockSpec((IN_T // TB, QV_ROWS, TB), lambda i: (i, 0, 0))],
        out_shape=[jax.ShapeDtypeStruct((n, KN_COLS), BF16),
                   jax.ShapeDtypeStruct((n // TB, QV_ROWS, TB), BF16)],
        compiler_params=_cparams(1),
        name="inproj",
    )(x2, g, wn, wt)


FLASH_OVERFLOW = 100.0
FLASH_SKEW = 4


class _FlashScratch:
    def __init__(self, n_chain, qpad, m, acc, over, s):
        self.n_chain, self.qpad, self.m, self.acc, self.over, self.s = n_chain, qpad, m, acc, over, s

    @staticmethod
    def shapes(n_chain):
        return [pltpu.VMEM((n_chain, LANES, TB), BF16),
                pltpu.VMEM((n_chain, 1, TB), F32),
                pltpu.VMEM((n_chain, HEAD_DIM + ONES_ROWS, TB), F32),
                pltpu.VMEM((n_chain, 1, TB), F32),
                pltpu.VMEM((n_chain, TB, TB), F32)]


def _flash_two_pass(sc, chains):
    if len(chains) > sc.n_chain:
        for k in range(0, len(chains), sc.n_chain):
            _flash_two_pass(sc, chains[k:k + sc.n_chain])
        return
    block_max = []
    for c, kblk, add, _, colmask in chains:
        s = jnp.dot(kblk, sc.qpad[c], preferred_element_type=F32)
        if add is not None:
            s = s + add
        if colmask is not None:
            s = s + colmask
        sc.s[c] = s
        block_max.append(jnp.max(s, axis=0, keepdims=True))
    for (c, _, _, vext, _), mx in zip(chains, block_max):
        m_old = sc.m[c]
        m_new = jnp.maximum(m_old, mx)
        p = jnp.exp2(sc.s[c] - m_new).astype(BF16)
        sc.acc[c] = sc.acc[c] * jnp.exp2(m_old - m_new) + jnp.dot(vext, p, preferred_element_type=F32)
        sc.m[c] = m_new


def _flash_lagged(sc, chains, n_first=0):
    def finish(c, p, mx, ref, vext, first):
        pv = jnp.dot(vext, p, preferred_element_type=F32)
        ref_new = jnp.maximum(ref, mx)
        sc.acc[c] = (sc.acc[c] + pv) * jnp.exp2(ref - ref_new)
        sc.m[c] = ref_new
        sc.over[c] = jnp.maximum(sc.over[c], jnp.abs(mx - ref) if first else mx - ref)

    pending = []
    for idx, (c, kblk, add, vext, colmask) in enumerate(chains):
        s = jnp.dot(kblk, sc.qpad[c], preferred_element_type=F32)
        if add is not None:
            s = s + add
        if len(pending) >= min(FLASH_SKEW, sc.n_chain):
            finish(*pending.pop(0))
        ref = sc.m[c]
        mx = jnp.max(s, axis=0, keepdims=True)
        if colmask is None:
            p = jnp.exp2(s - ref)
        else:
            p = jnp.exp2(s - (ref - colmask))
            mx = mx + colmask
        pending.append((c, p.astype(BF16), mx, ref, vext, idx < n_first))
    for item in pending:
        finish(*item)


def _flash_causal(sc, i, block_chains, far_group):
    assert far_group & (far_group - 1) == 0

    def init(ref0):
        for c in range(sc.n_chain):
            sc.m[c] = jnp.full(sc.m.shape[1:], ref0, F32)
            sc.acc[c] = jnp.zeros(sc.acc.shape[1:], F32)
            sc.over[c] = jnp.full(sc.over.shape[1:], NEG, F32)

    def run(step):
        if step is _flash_two_pass:
            init(NEG)
            step(sc, block_chains(i, 0))

            @pl.when(i >= 1)
            def _():
                step(sc, block_chains(i - 1, 1))
        else:
            init(0.0)

            @pl.when(i == 0)
            def _():
                step(sc, block_chains(i, 0), sc.n_chain)

            @pl.when(i >= 1)
            def _():
                step(sc, block_chains(i, 0) + block_chains(i - 1, 1), sc.n_chain)

        n_far = jnp.maximum(i - 1, 0)

        def far_blocks(j, count):
            chains = []
            for k in range(count):
                chains += block_chains(j + k, None)
            step(sc, chains)

        if step is _flash_two_pass:
            lax.fori_loop(0, n_far, lambda j, carry: (far_blocks(j, 1), carry)[1], 0)
            return

        def far_group_step(t, carry):
            far_blocks(far_group * t, far_group)
            return carry

        n_group = n_far // far_group
        lax.fori_loop(0, n_group, far_group_step, 0)
        done = n_group * far_group
        size = far_group // 2
        while size >= 1:
            has_piece = (n_far & size) != 0

            @pl.when(has_piece)
            def _(done=done, size=size):
                far_blocks(done, size)

            done = done + jnp.where(has_piece, size, 0)
            size //= 2

    run(_flash_lagged)
    worst = sc.over[0]
    for c in range(1, sc.n_chain):
        worst = jnp.maximum(worst, sc.over[c])

    @pl.when(jnp.max(worst) > FLASH_OVERFLOW)
    def _():
        run(_flash_two_pass)


def _diff_kernel(lam_init, qt_ref, k_ref, vt_ref, tile_ref, lq1_ref, lk1_ref, lq2_ref, lk2_ref,
                 subln_ref, o_ref, *scratch):
    i = pl.program_id(1)
    sc = _FlashScratch(2 * N_HEADS, *scratch)
    acc_scr = sc.acc
    for h in range(N_HEADS):
        for mp in range(2):
            r0 = h * HEAD_DIM + mp * DIFF_QK
            sc.qpad[2 * h + mp] = _pad_rows(qt_ref[0, r0:r0 + DIFF_QK, :],
                                            (h % 2) * HEAD_DIM + mp * DIFF_QK, LANES)

    def block_chains(j, tile_idx):
        row = pl.multiple_of(j * TB, TB)
        chains = []
        for h in range(N_HEADS):
            g = h // 2
            kblk = k_ref[pl.ds(row, TB), g * LANES:(g + 1) * LANES]
            add = None if tile_idx is None else tile_ref[h, tile_idx]
            vext = _with_ones(vt_ref[j, h * HEAD_DIM:(h + 1) * HEAD_DIM, :])
            chains += [(2 * h + mp, kblk, add, vext, None) for mp in range(2)]
        return chains

    _flash_causal(sc, i, block_chains, far_group=8)

    lam =(jnp.exp(jnp.sum(lq1_ref[...] * lk1_ref[...], keepdims=True))
           - jnp.exp(jnp.sum(lq2_ref[...] * lk2_ref[...], keepdims=True)) + lam_init)
    for h in range(N_HEADS):
        a1 = acc_scr[2 * h]
        a2 = acc_scr[2 * h + 1]
        o1 = a1[:HEAD_DIM] / a1[HEAD_DIM:HEAD_DIM + 1]
        o2 = a2[:HEAD_DIM] / a2[HEAD_DIM:HEAD_DIM + 1]
        a = o1 - lam * o2
        ms = jnp.mean(a * a, axis=0, keepdims=True)
        y = a * lax.rsqrt(ms + NORM_EPS) * subln_ref[...] * (1.0 - lam_init)
        o_ref[0, h * HEAD_DIM:(h + 1) * HEAD_DIM, :] = y.astype(BF16)


def _attn_common_specs(nb, q_blk, k_blk, v_blk):
    return [pl.BlockSpec((1, TB, TB), lambda b, i: (b * nb + i, q_blk, 0)),
            pl.BlockSpec((nb * TB, TB), lambda b, i: (b, k_blk)),
            pl.BlockSpec((nb, TB, TB), lambda b, i: (b, v_blk, 0))]


def _small_spec(shape):
    return pl.BlockSpec(shape, lambda b, i: (0,) * len(shape))


def _diff_attention(kn, qvt, tiles, lq1, lk1, lq2, lk2, subln, lam_init, batch):
    nblk = qvt.shape[0]
    nb = nblk // batch
    n_chain = 2 * N_HEADS
    return pl.pallas_call(
        functools.partial(_diff_kernel, lam_init),
        grid=(batch, nb),
        in_specs=_attn_common_specs(nb, QV_DIFF_Q, KN_DIFF, QV_DIFF_V) + [
            _small_spec((N_HEADS, 2, TB, TB)),
            _small_spec((1, DIFF_QK)), _small_spec((1, DIFF_QK)),
            _small_spec((1, DIFF_QK)), _small_spec((1, DIFF_QK)),
            _small_spec((HEAD_DIM, 1))],
        out_specs=pl.BlockSpec((1, TB, TB), lambda b, i: (b * nb + i, 0, 0)),
        out_shape=jax.ShapeDtypeStruct((nblk, TB, TB), BF16),
        scratch_shapes=_FlashScratch.shapes(n_chain),
        compiler_params=_cparams(2),
        name="diff_attn",
    )(qvt, kn, qvt, tiles, lq1, lk1, lq2, lk2, subln)


def _moba_kernel(nb, nbp, qt_ref, k_ref, vt_ref, tile_ref, o_ref, kmean_scr, sel_scr, *scratch):
    i = pl.program_id(1)
    sc = _FlashScratch(N_HEADS, *scratch)
    qpad_scr, acc_scr = sc.qpad, sc.acc

    @pl.when(i == 0)
    def _():
        kmean_scr[...] = jnp.zeros(kmean_scr.shape, F32)
        for jb in range(nb):
            blk = k_ref[jb * TB:(jb + 1) * TB, :].astype(F32)
            kmean_scr[jb:jb + 1, :] = jnp.mean(blk, axis=0, keepdims=True)

    for h in range(N_HEADS):
        qpad_scr[h] = _pad_rows(qt_ref[0, h * HEAD_DIM:(h + 1) * HEAD_DIM, :], (h % 2) * HEAD_DIM, LANES)

    blk_id = lax.broadcasted_iota(jnp.int32, (nbp, TB), 0)
    for h in range(N_HEADS):
        g = h // 2
        km = kmean_scr[:, g * LANES:(g + 1) * LANES].astype(BF16)
        gate = jnp.dot(km, qpad_scr[h], preferred_element_type=F32)
        avail = blk_id < i
        sel = jnp.zeros((nbp, TB), jnp.bool_)
        for _ in range(MOBA_TOPK):
            gm = jnp.where(avail, gate, -jnp.inf)
            best = jnp.max(gm, axis=0, keepdims=True)
            is_best = avail & (gm == best)
            first = jnp.min(jnp.where(is_best, blk_id, nbp), axis=0, keepdims=True)
            pick = blk_id == first
            sel = sel | pick
            avail = avail & jnp.logical_not(pick)
        sel_scr[h] = jnp.where(sel, 0.0, NEG).astype(F32)

    def block_chains(j, tile_idx):
        row = pl.multiple_of(j * TB, TB)
        chains = []
        for h in range(N_HEADS):
            g = h // 2
            kblk = k_ref[pl.ds(row, TB), g * LANES:(g + 1) * LANES]
            add = None if tile_idx is None else tile_ref[h, tile_idx]
            colmask = None if tile_idx == 0 else sel_scr[h, pl.ds(j, 1), :]
            chains.append((h, kblk, add, _with_ones(vt_ref[j, h * HEAD_DIM:(h + 1) * HEAD_DIM, :]), colmask))
        return chains

    _flash_causal(sc, i, block_chains, far_group=16)

    for h in range(N_HEADS):
        a = acc_scr[h]
        o_ref[0, h * HEAD_DIM:(h + 1) * HEAD_DIM, :] = (a[:HEAD_DIM] / a[HEAD_DIM:HEAD_DIM + 1]).astype(BF16)


def _moba_attention(kn, qvt, tiles, batch):
    nblk = qvt.shape[0]
    nb = nblk // batch
    nbp = max(8, -(-nb // 8) * 8)
    return pl.pallas_call(
        functools.partial(_moba_kernel, nb, nbp),
        grid=(batch, nb),
        in_specs=_attn_common_specs(nb, QV_MOBA_Q, KN_MOBA, QV_MOBA_V) + [
            _small_spec((N_HEADS, 2, TB, TB))],
        out_specs=pl.BlockSpec((1, TB, TB), lambda b, i: (b * nb + i, 0, 0)),
        out_shape=jax.ShapeDtypeStruct((nblk, TB, TB), BF16),
        scratch_shapes=[pltpu.VMEM((nbp, TB), F32),
                        pltpu.VMEM((N_HEADS, nbp, TB), F32)] + _FlashScratch.shapes(N_HEADS),
        compiler_params=_cparams(2),
        name="moba_attn",
    )(qvt, kn, qvt, tiles)


def _sb_kernel(qt_ref, k_ref, vt_ref, o_ref, qpad_scr, acc_scr, c_scr):
    i = pl.program_id(1)
    rows = lax.broadcasted_iota(jnp.int32, (TB, TB), 0)
    cols = lax.broadcasted_iota(jnp.int32, (TB, TB), 1)
    upper = (cols > rows).astype(BF16)
    strict = cols > rows

    heads = range(N_HEADS)

    def step(j, first):
        row = pl.multiple_of(j * TB, TB)
        zs = [jnp.dot(k_ref[pl.ds(row, TB), (h // 2) * LANES:(h // 2 + 1) * LANES], qpad_scr[h],
                      preferred_element_type=F32) for h in heads]
        log_keeps = []
        for z in zs:
            lk = -(jnp.maximum(z, 0.0) + jnp.log2(1.0 + jnp.exp2(-jnp.abs(z))))
            log_keeps.append(jnp.where(strict, lk, 0.0) if first else lk)
        laters = []
        for h in heads:
            hi = log_keeps[h].astype(BF16)
            lo = (log_keeps[h] - hi.astype(F32)).astype(BF16)
            later = (jnp.dot(upper, hi, preferred_element_type=F32)
                     + jnp.dot(upper, lo, preferred_element_type=F32))
            laters.append(later if first else later + c_scr[h])
        c_max = None
        for h in heads:
            w = jnp.exp2(zs[h] + log_keeps[h] + laters[h])
            if first:
                w = jnp.where(strict, w, 0.0)
            pv = jnp.dot(vt_ref[j, h * HEAD_DIM:(h + 1) * HEAD_DIM, :], w.astype(BF16),
                         preferred_element_type=F32)
            acc_scr[h] = pv if first else acc_scr[h] + pv
            c_new = laters[h][0:1, :] + log_keeps[h][0:1, :]
            c_scr[h] = c_new
            c_max = c_new if c_max is None else jnp.maximum(c_max, c_new)
        return jnp.max(c_max) > -SB_EXIT * LOG2E

    for h in heads:
        qpad_scr[h] = _pad_rows(qt_ref[0, h * HEAD_DIM:(h + 1) * HEAD_DIM, :], (h % 2) * HEAD_DIM, LANES)

    def cond(carry):
        j, go = carry
        return jnp.logical_and(j >= 0, go)

    def body(carry):
        j, _ = carry
        return j - 1, step(j, False)

    lax.while_loop(cond, body, (i - 1, step(i, True)))
    for h in heads:
        o_ref[0, h * HEAD_DIM:(h + 1) * HEAD_DIM, :] = acc_scr[h].astype(BF16)


def _sb_attention(kn, qvt, batch):
    nblk = qvt.shape[0]
    nb = nblk // batch
    return pl.pallas_call(
        _sb_kernel,
        grid=(batch, nb),
        in_specs=_attn_common_specs(nb, QV_SB_Q, KN_SB, QV_SB_V),
        out_specs=pl.BlockSpec((1, TB, TB), lambda b, i: (b * nb + i, 0, 0)),
        out_shape=jax.ShapeDtypeStruct((nblk, TB, TB), BF16),
        scratch_shapes=[pltpu.VMEM((N_HEADS, LANES, TB), BF16),
                        pltpu.VMEM((N_HEADS, HEAD_DIM, TB), F32),
                        pltpu.VMEM((N_HEADS, 1, TB), F32)],
        compiler_params=_cparams(2),
        name="sb_attn",
    )(qvt, kn, qvt)


def _swa_kernel(sink_ref, qt_ref, kc_ref, kp_ref, vc_ref, vp_ref, tc_ref, tp_ref, o_ref):
    i = pl.program_id(1)
    kc = kc_ref[...]
    kp = kp_ref[...]
    vc = vc_ref[0]
    vp = vp_ref[0][:, TB - SWA_WINDOW:]
    no_prev = jnp.where(i > 0, 0.0, NEG)
    group = N_HEADS // SWA_KV_HEADS
    heads = range(N_HEADS)
    qpads = [_pad_rows(qt_ref[0, h * HEAD_DIM:(h + 1) * HEAD_DIM, :], (h // group) * HEAD_DIM, LANES)
             for h in heads]
    s_cur = [jnp.dot(kc, qpads[h], preferred_element_type=F32) + tc_ref[h] for h in heads]
    s_prev = [jnp.dot(kp, qpads[h], preferred_element_type=F32) + tp_ref[h] + no_prev for h in heads]
    ms = [jnp.maximum(jnp.maximum(jnp.max(s_cur[h], axis=0, keepdims=True),
                                  jnp.max(s_prev[h], axis=0, keepdims=True)), sink_ref[h]) for h in heads]
    p_cur = [jnp.exp(s_cur[h] - ms[h]).astype(BF16) for h in heads]
    p_prev = [jnp.exp(s_prev[h] - ms[h]).astype(BF16) for h in heads]
    for h in heads:
        kv = h // group
        o = (jnp.dot(_with_ones(vc[kv * HEAD_DIM:(kv + 1) * HEAD_DIM]), p_cur[h], preferred_element_type=F32)
             + jnp.dot(_with_ones(vp[kv * HEAD_DIM:(kv + 1) * HEAD_DIM]), p_prev[h],
                       preferred_element_type=F32))
        denom = o[HEAD_DIM:HEAD_DIM + 1] + jnp.exp(sink_ref[h] - ms[h])
        o_ref[0, h * HEAD_DIM:(h + 1) * HEAD_DIM, :] = (o[:HEAD_DIM] / denom).astype(BF16)


def _swa_attention(kn, qvt, tile_cur, tile_prev, sinks, batch):
    nblk = qvt.shape[0]
    nb = nblk // batch
    half = TB // SWA_WINDOW
    kv_rows = SWA_KV_HEADS * HEAD_DIM
    return pl.pallas_call(
        _swa_kernel,
        grid=(batch, nb),
        in_specs=[pl.BlockSpec(memory_space=pltpu.SMEM),
                  pl.BlockSpec((1, TB, TB), lambda b, i: (b * nb + i, QV_SWA_Q, 0)),
                  pl.BlockSpec((TB, kv_rows), lambda b, i: (b * nb + i, KN_SWA_128)),
                  pl.BlockSpec((SWA_WINDOW, kv_rows),
                               lambda b, i: (b * nb * half + jnp.maximum(half * i - 1, 0), KN_SWA_128)),
                  pl.BlockSpec((1, kv_rows, TB), lambda b, i: (b * nb + i, QV_SWA_V_128, 0)),
                  pl.BlockSpec((1, kv_rows, TB), lambda b, i: (b * nb + jnp.maximum(i - 1, 0), QV_SWA_V_128, 0)),
                  _small_spec((N_HEADS, TB, TB)),
                  _small_spec((N_HEADS, SWA_WINDOW, TB))],
        out_specs=pl.BlockSpec((1, TB, TB), lambda b, i: (b * nb + i, 0, 0)),
        out_shape=jax.ShapeDtypeStruct((nblk, TB, TB), BF16),
        compiler_params=_cparams(2),
        name="swa_attn",
    )(sinks, qvt, kn, kn, qvt, qvt, tile_cur, tile_prev)


MERGE_T = 512
MERGE_COLS = 256


def _merge_route_kernel(x_ref, g_ref, oa_ref, ob_ref, oc_ref, od_ref, wg_ref, wbr_ref, wo_ref,
                        gf_ref, wr_ref, br_ref,
                        x1_ref, h2_ref, eid_ref, gate_ref, rank_ref, cnt_ref, base_scr):
    x = x_ref[...]
    h = _rms(x, g_ref[...]).astype(BF16)
    d = x.shape[1]
    chunks = []
    for n0 in range(0, d, MERGE_COLS):
        acc = None
        for bi, o_ref in enumerate((oa_ref, ob_ref, oc_ref, od_ref)):
            gate = jax.nn.sigmoid(jnp.dot(h, wg_ref[bi, :, n0:n0 + MERGE_COLS], preferred_element_type=F32))
            branch = jnp.concatenate(
                [lax.dot_general(o_ref[s], wbr_ref[bi, :, n0:n0 + MERGE_COLS], (((0,), (0,)), ((), ())),
                                 preferred_element_type=F32) for s in range(MERGE_T // TB)], axis=0)
            term = gate * branch
            acc = term if acc is None else acc + term
        chunks.append(acc.astype(BF16))
    merged = jnp.concatenate(chunks, axis=1)
    x1 = x + jnp.dot(merged, wo_ref[...], preferred_element_type=F32)
    x1_ref[...] = x1
    _route(x1, gf_ref, wr_ref, br_ref, h2_ref, eid_ref, gate_ref, rank_ref, cnt_ref, base_scr)


ROUTER_T = MERGE_T
ROUTER_ROWS = 8 + N_EXPERTS


def _first_argmax_rows(v, n_rows):
    best = jnp.max(v, axis=0, keepdims=True)
    ids = lax.broadcasted_iota(jnp.int32, v.shape, 0)
    return best, jnp.min(jnp.where(v == best, ids, n_rows), axis=0, keepdims=True)


def _route(x1, g_ref, w_ref, b_ref, h2_ref, eid_ref, gate_ref, rank_ref, cnt_ref, base_scr):
    i = pl.program_id(0)

    @pl.when(i == 0)
    def _():
        base_scr[...] = jnp.zeros(base_scr.shape, F32)

    h2 = _rms(x1, g_ref[...])
    h2_ref[...] = _pack_bf16_pairs(h2)
    nt = (((1,), (1,)), ((), ()))
    logits = lax.dot_general(w_ref[...], h2.astype(BF16), nt, preferred_element_type=F32) + b_ref[...]
    gl = logits[0:8]
    gmax, grp = _first_argmax_rows(gl, 8)
    p_grp = 1.0 / jnp.sum(jnp.exp(gl - gmax), axis=0, keepdims=True)
    e_sel = jnp.zeros((EXPERTS_PER_GROUP, ROUTER_T), F32)
    for g in range(N_GROUPS):
        e_sel = jnp.where(grp == g, logits[8 + 8 * g:16 + 8 * g], e_sel)
    ids8 = lax.broadcasted_iota(jnp.int32, e_sel.shape, 0)
    v1, i1 = _first_argmax_rows(e_sel, EXPERTS_PER_GROUP)
    e_rest = jnp.where(ids8 == i1, -jnp.inf, e_sel)
    v2, i2 = _first_argmax_rows(e_rest, EXPERTS_PER_GROUP)
    r = jnp.exp(v2 - v1)
    s1 = 1.0 / (1.0 + r)
    gate_ref[0:1, :] = p_grp * s1
    gate_ref[1:2, :] = p_grp * (r * s1)
    e1 = grp * EXPERTS_PER_GROUP + i1
    e2 = grp * EXPERTS_PER_GROUP + i2
    eid_ref[0:1, :] = e1
    eid_ref[1:2, :] = e2

    ids_e = lax.broadcasted_iota(jnp.int32, (N_EXPERTS, ROUTER_T), 0)
    oh1 = ids_e == e1
    oh2 = ids_e == e2
    cnt = oh1.astype(F32) + oh2.astype(F32)
    tr = lax.broadcasted_iota(jnp.int32, (ROUTER_T, ROUTER_T), 0)
    tc = lax.broadcasted_iota(jnp.int32, (ROUTER_T, ROUTER_T), 1)
    before = (tr < tc).astype(BF16)
    prefix = jnp.dot(cnt.astype(BF16), before, preferred_element_type=F32) + base_scr[:, 0:1]
    rank_ref[0:1, :] = jnp.sum(jnp.where(oh1, prefix, 0.0), axis=0, keepdims=True).astype(jnp.int32)
    rank_ref[1:2, :] = jnp.sum(jnp.where(oh2, prefix, 0.0), axis=0, keepdims=True).astype(jnp.int32)
    base_scr[...] = base_scr[...] + jnp.sum(cnt, axis=1, keepdims=True)
    cnt_ref[...] = base_scr[...]


def _merge_route(x2, g, o_a, o_b, o_c, o_d, wg, wbr, wo, g_ffn, w_route, b_route):
    n, d = x2.shape
    o_spec = pl.BlockSpec((MERGE_T // TB, TB, TB), lambda i: (i, 0, 0))
    row2 = lambda dt: jax.ShapeDtypeStruct((2, n), dt)
    spec2 = pl.BlockSpec((2, MERGE_T), lambda i: (0, i))
    return pl.pallas_call(
        _merge_route_kernel,
        grid=(n // MERGE_T,),
        in_specs=[pl.BlockSpec((MERGE_T, d), lambda i: (i, 0)),
                  pl.BlockSpec((1, d), lambda i: (0, 0)),
                  o_spec, o_spec, o_spec, o_spec,
                  pl.BlockSpec(wg.shape, lambda i: (0, 0, 0)),
                  pl.BlockSpec(wbr.shape, lambda i: (0, 0, 0)),
                  pl.BlockSpec(wo.shape, lambda i: (0, 0)),
                  pl.BlockSpec((1, d), lambda i: (0, 0)),
                  pl.BlockSpec((ROUTER_ROWS, d), lambda i: (0, 0)),
                  pl.BlockSpec((ROUTER_ROWS, 1), lambda i: (0, 0))],
        out_specs=[pl.BlockSpec((MERGE_T, d), lambda i: (i, 0)),
                   pl.BlockSpec((MERGE_T, d // 2), lambda i: (i, 0)), spec2, spec2, spec2,
                   pl.BlockSpec((N_EXPERTS, LANES), lambda i: (0, 0))],
        out_shape=[jax.ShapeDtypeStruct((n, d), F32),
                   jax.ShapeDtypeStruct((n, d // 2), jnp.uint32), row2(jnp.int32), row2(F32), row2(jnp.int32),
                   jax.ShapeDtypeStruct((N_EXPERTS, LANES), F32)],
        scratch_shapes=[pltpu.VMEM((N_EXPERTS, LANES), F32)],
        compiler_params=_cparams(1),
        name="merge_route",
    )(x2, g, o_a, o_b, o_c, o_d, wg, wbr, wo, g_ffn, w_route, b_route)


def _expert_kernel(be_ref, nu_ref, nv_ref, x_ref, w1_ref, w3_ref, w2_ref, y_ref):
    del be_ref
    used = pl.program_id(0) < nu_ref[0]

    @pl.when(used)
    def _():
        live = lax.broadcasted_iota(jnp.int32, x_ref.shape, 0) < nv_ref[pl.program_id(0)]
        xb = _unpack_bf16_pairs(jnp.where(live, x_ref[...], jnp.uint32(0))).astype(BF16)
        y = None
        for c0 in range(0, w1_ref.shape[1], EXPERT_COLS):
            a = jnp.dot(xb, w1_ref[:, c0:c0 + EXPERT_COLS].astype(BF16), preferred_element_type=F32)
            b = jnp.dot(xb, w3_ref[:, c0:c0 + EXPERT_COLS].astype(BF16), preferred_element_type=F32)
            mid = (a * jax.nn.sigmoid(a) * b).astype(BF16)
            part = jnp.dot(mid, w2_ref[c0:c0 + EXPERT_COLS, :].astype(BF16), preferred_element_type=F32)
            y = part if y is None else y + part
        y_ref[...] = _pack_bf16_pairs(y)

    @pl.when(jnp.logical_not(used))
    def _():
        y_ref[...] = jnp.zeros(y_ref.shape, jnp.uint32)


def _experts(blk_expert, n_used, n_valid, xbuf, w1, w3, w2, layer):
    r = xbuf.shape[0]
    d = w1.shape[-2]
    de = w1.shape[-1]
    assert xbuf.shape[1] * 2 == d
    row_map = lambda i, be, nu, nv: (jnp.minimum(i, nu[0] - 1), 0)
    grid_spec = pltpu.PrefetchScalarGridSpec(
        num_scalar_prefetch=3,
        grid=(r // MOE_ROWS,),
        in_specs=[pl.BlockSpec((MOE_ROWS, d // 2), row_map),
                  pl.BlockSpec((None, None, d, de), lambda i, be, nu, nv: (layer, be[i], 0, 0)),
                  pl.BlockSpec((None, None, d, de), lambda i, be, nu, nv: (layer, be[i], 0, 0)),
                  pl.BlockSpec((None, None, de, d), lambda i, be, nu, nv: (layer, be[i], 0, 0))],
        out_specs=pl.BlockSpec((MOE_ROWS, d // 2), lambda i, be, nu, nv: (i, 0)),
    )
    return pl.pallas_call(
        _expert_kernel,
        grid_spec=grid_spec,
        out_shape=jax.ShapeDtypeStruct((r, d // 2), jnp.uint32),
        compiler_params=_cparams(1),
        name="experts",
    )(blk_expert, n_used, n_valid, xbuf, w1, w3, w2)


SC_CORES = 2
SC_SUBCORES = 16
SC_ROWS = 64


def _sc_gather_rows(table, idx):
    n_idx = idx.shape[0]
    d = table.shape[1]
    n_workers = SC_CORES * SC_SUBCORES
    per_worker = n_idx // n_workers
    n_chunk = per_worker // SC_ROWS
    assert per_worker * n_workers == n_idx and n_chunk * SC_ROWS == per_worker and n_chunk % 2 == 0
    mesh = plsc.VectorSubcoreMesh(core_axis_name="c", subcore_axis_name="s",
                                  num_cores=SC_CORES, num_subcores=SC_SUBCORES)

    def body(table_hbm, idx_hbm, out_hbm, idx_v, rows_v, gsem, wsem):
        worker = lax.axis_index("s") * SC_CORES + lax.axis_index("c")
        base = worker * per_worker
        pltpu.sync_copy(idx_hbm.at[pl.ds(base, per_worker)], idx_v)

        def gather(c, b):
            return pltpu.make_async_copy(table_hbm.at[idx_v.at[pl.ds(c * SC_ROWS, SC_ROWS)]],
                                         rows_v.at[b], gsem.at[b])

        def put(c, b):
            return pltpu.make_async_copy(rows_v.at[b], out_hbm.at[pl.ds(base + c * SC_ROWS, SC_ROWS)],
                                         wsem.at[b])

        gather(0, 0).start()

        @pl.loop(0, n_chunk, step=2)
        def _(c0):
            for b in range(2):
                c = c0 + b
                gather(c, b).wait()

                @pl.when(c + 1 < n_chunk)
                def _():
                    @pl.when(c >= 1)
                    def _():
                        put(c - 1, 1 - b).wait()

                    gather(c + 1, 1 - b).start()

                put(c, b).start()

        put(n_chunk - 2, 0).wait()
        put(n_chunk - 1, 1).wait()

    return pl.kernel(
        body,
        out_type=jax.ShapeDtypeStruct((n_idx, d), table.dtype),
        mesh=mesh,
        scratch_types=[pltpu.VMEM((per_worker,), jnp.int32),
                       pltpu.VMEM((2, SC_ROWS, d), table.dtype),
                       pltpu.SemaphoreType.DMA((2,)),
                       pltpu.SemaphoreType.DMA((2,))],
        name="sc_gather_rows",
    )(table, idx)


def _sc_scatter_rows(rows, dest, n_out):
    n, d = rows.shape
    n_workers = SC_CORES * SC_SUBCORES
    per_worker = n // n_workers
    n_chunk = per_worker // SC_ROWS
    assert per_worker * n_workers == n and n_chunk * SC_ROWS == per_worker and n_chunk % 2 == 0
    mesh = plsc.VectorSubcoreMesh(core_axis_name="c", subcore_axis_name="s",
                                  num_cores=SC_CORES, num_subcores=SC_SUBCORES)
    dest3 = dest.reshape(2, n // SC_ROWS, SC_ROWS)

    def body(rows_hbm, idx_hbm, out_hbm, idx_v, rows_v, lsem, ssem):
        worker = lax.axis_index("s") * SC_CORES + lax.axis_index("c")
        for k in range(2):
            pltpu.sync_copy(idx_hbm.at[k, pl.ds(worker * n_chunk, n_chunk)], idx_v.at[k])

        def load(c, b):
            return pltpu.make_async_copy(rows_hbm.at[pl.ds(worker * per_worker + c * SC_ROWS, SC_ROWS)],
                                         rows_v.at[b], lsem.at[b])

        def scatter(c, b, k):
            return pltpu.make_async_copy(rows_v.at[b], out_hbm.at[idx_v.at[k, c]], ssem.at[b])

        load(0, 0).start()

        @pl.loop(0, n_chunk, step=2)
        def _(c0):
            for b in range(2):
                c = c0 + b
                load(c, b).wait()

                @pl.when(c + 1 < n_chunk)
                def _():
                    @pl.when(c >= 1)
                    def _():
                        scatter(c - 1, 1 - b, 0).wait()
                        scatter(c - 1, 1 - b, 1).wait()

                    load(c + 1, 1 - b).start()

                scatter(c, b, 0).start()
                scatter(c, b, 1).start()

        for c, b in ((n_chunk - 2, 0), (n_chunk - 1, 1)):
            scatter(c, b, 0).wait()
            scatter(c, b, 1).wait()

    return pl.kernel(
        body,
        out_type=jax.ShapeDtypeStruct((n_out, d), rows.dtype),
        mesh=mesh,
        scratch_types=[pltpu.VMEM((2, n_chunk, SC_ROWS), jnp.int32),
                       pltpu.VMEM((2, SC_ROWS, d), rows.dtype),
                       pltpu.SemaphoreType.DMA((2,)),
                       pltpu.SemaphoreType.DMA((2,))],
        name="sc_scatter_rows",
    )(rows, dest3)


def _combine_dense_kernel(final, gate_ref, x1_ref, gf_ref, y0_ref, y1_ref, out_ref):
    gate = gate_ref[...]
    out = (x1_ref[...] + gate[:, 0:1] * _unpack_bf16_pairs(y0_ref[...])
           + gate[:, 1:2] * _unpack_bf16_pairs(y1_ref[...]))
    if final:
        out = _rms(out, gf_ref[...])
    out_ref[...] = out


def _combine_dense(gate_t, x1, g_final, yg, final):
    n, d = x1.shape
    nb = n // MERGE_T
    return pl.pallas_call(
        functools.partial(_combine_dense_kernel, final),
        grid=(nb,),
        in_specs=[pl.BlockSpec((MERGE_T, 2), lambda i: (i, 0)),
                  pl.BlockSpec((MERGE_T, d), lambda i: (i, 0)),
                  pl.BlockSpec((1, d), lambda i: (0, 0)),
                  pl.BlockSpec((MERGE_T, d // 2), lambda i: (i, 0)),
                  pl.BlockSpec((MERGE_T, d // 2), lambda i: (i + nb, 0))],
        out_specs=pl.BlockSpec((MERGE_T, d), lambda i: (i, 0)),
        out_shape=jax.ShapeDtypeStruct((n, d), F32),
        compiler_params=_cparams(1),
        name="combine_dense",
    )(gate_t, x1, g_final, yg, yg)


def _projection_weights(w):
    d = w.shape[0]
    blk = N_HEADS * HEAD_DIM
    kv = SWA_KV_HEADS * HEAD_DIM
    pa, pb, pc, pd = w[:, :3 * blk], w[:, 3 * blk:6 * blk], w[:, 6 * blk:9 * blk], w[:, 9 * blk:]
    half = N_HEADS * DIFF_QK

    def per_head(a, b):
        return jnp.stack([a.reshape(d, N_HEADS, DIFF_QK), b.reshape(d, N_HEADS, DIFF_QK)], axis=2).reshape(d, blk)

    s64, s32 = HEAD_DIM ** -0.5, DIFF_QK ** -0.5
    wn = jnp.concatenate([pa[:, blk:2 * blk], per_head(pb[:, 2 * half:3 * half], pb[:, 3 * half:4 * half]),
                          pc[:, blk:2 * blk], pd[:, blk:blk + kv]], axis=1)
    wt = jnp.concatenate([pa[:, :blk] * (s64 * LOG2E), pa[:, 2 * blk:],
                          per_head(pb[:, :half], pb[:, half:2 * half]) * (s32 * LOG2E), pb[:, 4 * half:],
                          pc[:, :blk] * (s64 * LOG2E), pc[:, 2 * blk:],
                          pd[:, :blk] * s64, pd[:, blk + kv:]], axis=1)
    assert wn.shape[1] == KN_COLS and wt.shape[1] == QV_ROWS
    return wn.astype(BF16), wt.T.astype(BF16)


def _router_weights(w_rg, b_rg, w_re, b_re):
    d = w_rg.shape[0]
    w = jnp.concatenate([w_rg.T, jnp.zeros((8 - N_GROUPS, d), F32), w_re.T], axis=0)
    b = jnp.concatenate([b_rg.astype(F32), jnp.full((8 - N_GROUPS,), NEG, F32), b_re.astype(F32)])[:, None]
    return w.astype(BF16), b


def _moe_plan(eid, rank, counts, n_rows_total):
    padded = (counts + MOE_ROWS - 1) // MOE_ROWS * MOE_ROWS
    pad_end = jnp.cumsum(padded)
    pad_start = pad_end - padded
    experts = jnp.arange(N_EXPERTS, dtype=jnp.int32)
    start_of = jnp.sum(jnp.where(eid[..., None] == experts, pad_start, 0), axis=-1)
    dest = start_of + rank
    n_blk = n_rows_total // MOE_ROWS
    n_used = (pad_end[-1] // MOE_ROWS).astype(jnp.int32)
    blk = jnp.minimum(jnp.arange(n_blk, dtype=jnp.int32), n_used - 1) * MOE_ROWS
    blk_expert = jnp.minimum(jnp.sum(pad_end[None, :] <= blk[:, None], axis=1), N_EXPERTS - 1).astype(jnp.int32)
    live_end = jnp.sum(jnp.where(blk_expert[:, None] == experts, pad_start + counts, 0), axis=-1)
    n_valid = jnp.clip(live_end - blk, 0, MOE_ROWS).astype(jnp.int32)
    return dest.astype(jnp.int32), blk_expert, n_used.reshape(1), n_valid


def kernel(x, rel_bias, g_mix, w_in, diff_lq1, diff_lk1, diff_lq2, diff_lk2, diff_subln, swa_sinks,
           w_gate, w_br, w_o, g_ffn, w_route_group, b_route_group, w_route_expert, b_route_expert,
           w1, w3, w2, g_final):
    batch, seq, d = x.shape
    n = batch * seq
    depth = w_in.shape[0]
    assert seq % TB == 0 and n % MERGE_T == 0 and TB == MOBA_BLOCK
    tab = rel_bias.T.astype(F32)
    tiles_moba = _causal_bias_tiles(tab[:N_HEADS])
    tiles_diff = _causal_bias_tiles(tab[N_HEADS:2 * N_HEADS])
    tile_cur, tile_prev = _swa_bias_tiles(tab[2 * N_HEADS:])
    n_rows_total = n * 2 + N_EXPERTS * MOE_ROWS
    row = lambda v: v.astype(F32)[None, :]

    x2 = x.reshape(n, d)
    for l in range(depth):
        lam_init = 0.8 - 0.6 * math.exp(-0.3 * l)
        wn, wt = _projection_weights(w_in[l])
        kn, qvt = _inproj(x2, row(g_mix[l]), wn, wt)
        o_a = _moba_attention(kn, qvt, tiles_moba, batch)
        o_b = _diff_attention(kn, qvt, tiles_diff, row(diff_lq1[l]), row(diff_lk1[l]), row(diff_lq2[l]),
                              row(diff_lk2[l]), diff_subln[l].astype(F32)[:, None], lam_init, batch)
        o_c = _sb_attention(kn, qvt, batch)
        o_d = _swa_attention(kn, qvt, tile_cur, tile_prev, swa_sinks[l].astype(F32), batch)
        w_route, r_bias = _router_weights(w_route_group[l], b_route_group[l], w_route_expert[l],
                                          b_route_expert[l])
        x1, h2, eid, gate, rank, cnt = _merge_route(
            x2, row(g_mix[l]), o_a, o_b, o_c, o_d, w_gate[l].astype(BF16), w_br[l].astype(BF16),
            w_o[l].astype(BF16), row(g_ffn[l]), w_route, r_bias)
        dest, blk_expert, n_used, n_valid = _moe_plan(eid, rank, cnt[:, 0].astype(jnp.int32), n_rows_total)
        xbuf = _sc_scatter_rows(h2, dest, n_rows_total)
        y = _experts(blk_expert, n_used, n_valid, xbuf, w1, w3, w2, l)
        yg = _sc_gather_rows(y, dest.reshape(-1))
        x2 = _combine_dense(gate.T, x1, row(g_final), yg, l == depth - 1)
    return x2.reshape(batch, seq, d)
```

```python
import functools
import math

import numpy as np
import jax
import jax.numpy as jnp
from jax import lax
from jax.experimental import pallas as pl
from jax.experimental.pallas import tpu as pltpu
from jax.experimental.pallas import tpu_sc as plsc

F32 = jnp.float32
BF16 = jnp.bfloat16

HEAD_DIM = 64
N_HEADS = 4
DIFF_QK = 32
SWA_KV_HEADS = 2
SWA_WINDOW = 128
MOBA_BLOCK = 256
MOBA_TOPK = 3
REL_BUCKETS = 32
REL_MAX_DIST = 128
N_GROUPS = 4
EXPERTS_PER_GROUP = 8
N_EXPERTS = N_GROUPS * EXPERTS_PER_GROUP
NORM_EPS = 1e-6

TB = 256
LANES = 128
ONES_ROWS = 16
NEG = -1e30
LOG2E = math.log2(math.e)
SB_EXIT = 104.0
MOE_ROWS = 512
VMEM_LIMIT = 56 * 1024 * 1024

QV_MOBA_Q, QV_MOBA_V, QV_DIFF_Q, QV_DIFF_V, QV_SB_Q, QV_SB_V, QV_SWA_Q = range(7)
QV_SWA_V_128 = 14
QV_ROWS = 7 * 256 + 128
KN_MOBA, KN_DIFF, KN_SB = range(3)
KN_SWA_128 = 6
KN_COLS = 3 * 256 + 128


def _cparams(n_grid):
    return pltpu.CompilerParams(dimension_semantics=("arbitrary",) * n_grid,
                                vmem_limit_bytes=VMEM_LIMIT)


def _rel_bucket_np(n):
    n = np.maximum(n, 0)
    max_exact = REL_BUCKETS // 2
    nf = np.maximum(n, 1).astype(np.float64)
    large = max_exact + (np.log(nf / max_exact) / math.log(REL_MAX_DIST / max_exact)
                         * (REL_BUCKETS - max_exact)).astype(np.int64)
    large = np.minimum(large, REL_BUCKETS - 1)
    return np.where(n < max_exact, n, large)


def _first_far_distance():
    d = np.arange(0, 4 * REL_MAX_DIST)
    b = _rel_bucket_np(d)
    return int(np.min(d[b == REL_BUCKETS - 1]))


def _toeplitz_bias(tab, rows, cols, base, valid_fn, shift_far, unit):
    length = rows + cols - 1
    off = np.concatenate([np.arange(0, cols), np.arange(cols - length, 0)])
    n = base + off
    onehot = np.zeros((REL_BUCKETS, length), np.float32)
    onehot[_rel_bucket_np(n), np.arange(length)] = 1.0
    vec = jnp.dot(tab, jnp.asarray(onehot), precision=lax.Precision.HIGHEST)
    if shift_far:
        vec = vec - tab[:, REL_BUCKETS - 1:]
    vec = jnp.where(jnp.asarray(valid_fn(n))[None, :], vec * unit, NEG).astype(F32)
    flat = jnp.tile(vec, (1, rows))[:, :rows * (length - 1)]
    return flat.reshape(tab.shape[0], rows, length - 1)[:, :, :cols]


def _causal_bias_tiles(tab):
    assert _first_far_distance() <= TB + 1
    tiles = [_toeplitz_bias(tab, TB, TB, d * TB, lambda n: n >= 0, True, LOG2E) for d in range(2)]
    return jnp.stack(tiles, axis=1)


def _swa_bias_tiles(tab):
    in_window = lambda n: (n >= 0) & (n < SWA_WINDOW)
    return (_toeplitz_bias(tab, TB, TB, 0, in_window, False, 1.0),
            _toeplitz_bias(tab, SWA_WINDOW, TB, SWA_WINDOW, in_window, False, 1.0))


def _pad_rows(q, off, total):
    n, t = q.shape
    parts = []
    if off:
        parts.append(jnp.zeros((off, t), q.dtype))
    parts.append(q)
    if total - off - n:
        parts.append(jnp.zeros((total - off - n, t), q.dtype))
    return jnp.concatenate(parts, axis=0) if len(parts) > 1 else q


def _with_ones(v):
    return jnp.concatenate([v, jnp.ones((ONES_ROWS, v.shape[1]), v.dtype)], axis=0)


def _pack_bf16_pairs(x):
    w = x.shape[1] // 2
    lo = lax.bitcast_convert_type(x[:, :w].astype(BF16).astype(F32), jnp.uint32)
    hi = lax.bitcast_convert_type(x[:, w:].astype(BF16).astype(F32), jnp.uint32)
    return hi | (lo >> 16)


def _unpack_bf16_pairs(u):
    lo = lax.bitcast_convert_type(u << 16, F32)
    hi = lax.bitcast_convert_type(u & jnp.uint32(0xFFFF0000), F32)
    return jnp.concatenate([lo, hi], axis=1)


def _rms(x, g_row):
    ms = jnp.mean(x * x, axis=-1, keepdims=True)
    return x * lax.rsqrt(ms + NORM_EPS) * g_row


IN_T = 512
IN_CHUNK = 384


def _inproj_kernel(x_ref, g_ref, wn_ref, wt_ref, kn_ref, qvt_ref):
    h = _rms(x_ref[...], g_ref[...]).astype(BF16)
    kn_ref[...] = jnp.dot(h, wn_ref[...], preferred_element_type=F32).astype(BF16)
    for r0 in range(0, QV_ROWS, IN_CHUNK):
        pt = lax.dot_general(wt_ref[r0:r0 + IN_CHUNK, :], h, (((1,), (1,)), ((), ())),
                             preferred_element_type=F32)
        for s in range(IN_T // TB):
            qvt_ref[s, r0:r0 + IN_CHUNK, :] = pt[:, s * TB:(s + 1) * TB].astype(BF16)


def _inproj(x2, g, wn, wt):
    n, d = x2.shape
    return pl.pallas_call(
        _inproj_kernel,
        grid=(n // IN_T,),
        in_specs=[pl.BlockSpec((IN_T, d), lambda i: (i, 0)),
                  pl.BlockSpec((1, d), lambda i: (0, 0)),
                  pl.BlockSpec((d, KN_COLS), lambda i: (0, 0)),
                  pl.BlockSpec((QV_ROWS, d), lambda i: (0, 0))],
        out_specs=[pl.BlockSpec((IN_T, KN_COLS), lambda i: (i, 0)),
                   pl.BlockSpec((IN_T // TB, QV_ROWS, TB), lambda i: (i, 0, 0))],
        out_shape=[jax.ShapeDtypeStruct((n, KN_COLS), BF16),
                   jax.ShapeDtypeStruct((n // TB, QV_ROWS, TB), BF16)],
        compiler_params=_cparams(1),
        name="inproj",
    )(x2, g, wn, wt)


FLASH_OVERFLOW = 100.0
FLASH_SKEW = 4


class _FlashScratch:
    def __init__(self, n_chain, qpad, m, acc, over, s):
        self.n_chain, self.qpad, self.m, self.acc, self.over, self.s = n_chain, qpad, m, acc, over, s

    @staticmethod
    def shapes(n_chain):
        return [pltpu.VMEM((n_chain, LANES, TB), BF16),
                pltpu.VMEM((n_chain, 1, TB), F32),
                pltpu.VMEM((n_chain, HEAD_DIM + ONES_ROWS, TB), F32),
                pltpu.VMEM((n_chain, 1, TB), F32),
                pltpu.VMEM((n_chain, TB, TB), F32)]


def _flash_two_pass(sc, chains):
    if len(chains) > sc.n_chain:
        for k in range(0, len(chains), sc.n_chain):
            _flash_two_pass(sc, chains[k:k + sc.n_chain])
        return
    block_max = []
    for c, kblk, add, _, colmask in chains:
        s = jnp.dot(kblk, sc.qpad[c], preferred_element_type=F32)
        if add is not None:
            s = s + add
        if colmask is not None:
            s = s + colmask
        sc.s[c] = s
        block_max.append(jnp.max(s, axis=0, keepdims=True))
    for (c, _, _, vext, _), mx in zip(chains, block_max):
        m_old = sc.m[c]
        m_new = jnp.maximum(m_old, mx)
        p = jnp.exp2(sc.s[c] - m_new).astype(BF16)
        sc.acc[c] = sc.acc[c] * jnp.exp2(m_old - m_new) + jnp.dot(vext, p, preferred_element_type=F32)
        sc.m[c] = m_new


def _flash_lagged(sc, chains, n_first=0):
    def finish(c, p, mx, ref, vext, first):
        pv = jnp.dot(vext, p, preferred_element_type=F32)
        ref_new = jnp.maximum(ref, mx)
        sc.acc[c] = (sc.acc[c] + pv) * jnp.exp2(ref - ref_new)
        sc.m[c] = ref_new
        sc.over[c] = jnp.maximum(sc.over[c], jnp.abs(mx - ref) if first else mx - ref)

    pending = []
    for idx, (c, kblk, add, vext, colmask) in enumerate(chains):
        s = jnp.dot(kblk, sc.qpad[c], preferred_element_type=F32)
        if add is not None:
            s = s + add
        if len(pending) >= min(FLASH_SKEW, sc.n_chain):
            finish(*pending.pop(0))
        ref = sc.m[c]
        mx = jnp.max(s, axis=0, keepdims=True)
        if colmask is None:
            p = jnp.exp2(s - ref)
        else:
            p = jnp.exp2(s - (ref - colmask))
            mx = mx + colmask
        pending.append((c, p.astype(BF16), mx, ref, vext, idx < n_first))
    for item in pending:
        finish(*item)


def _flash_causal(sc, i, block_chains, far_group):
    def init(ref0):
        for c in range(sc.n_chain):
            sc.m[c] = jnp.full(sc.m.shape[1:], ref0, F32)
            sc.acc[c] = jnp.zeros(sc.acc.shape[1:], F32)
            sc.over[c] = jnp.full(sc.over.shape[1:], NEG, F32)

    def far_blocks(step, j, count):
        chains = []
        for k in range(count):
            chains += block_chains(j + k, None)
        step(sc, chains)

    def run_two_pass():
        init(NEG)
        _flash_two_pass(sc, block_chains(i, 0))

        @pl.when(i >= 1)
        def _():
            _flash_two_pass(sc, block_chains(i - 1, 1))

        lax.fori_loop(0, jnp.maximum(i - 1, 0), lambda j, carry: (far_blocks(_flash_two_pass, j, 1), carry)[1], 0)

    def run_single_pass():
        init(0.0)

        @pl.when(i == 0)
        def _():
            _flash_lagged(sc, block_chains(i, 0), sc.n_chain)

        head = lax.rem(i + far_group - 1, far_group) + 2
        for size in range(2, far_group + 2):
            @pl.when(jnp.logical_and(i >= 1, head == size))
            def _(size=size):
                chains = block_chains(i, 0) + block_chains(i - 1, 1)
                for k in range(size - 2):
                    chains += block_chains(i - 2 - k, None)
                _flash_lagged(sc, chains, sc.n_chain)

        def far_group_step(t, carry):
            far_blocks(_flash_lagged, far_group * t, far_group)
            return carry

        lax.fori_loop(0, jnp.where(i >= 1, (i + 1 - head) // far_group, 0), far_group_step, 0)

    run_single_pass()
    worst = sc.over[0]
    for c in range(1, sc.n_chain):
        worst = jnp.maximum(worst, sc.over[c])

    @pl.when(jnp.max(worst) > FLASH_OVERFLOW)
    def _():
        run_two_pass()


def _diff_kernel(lam_init, qt_ref, k_ref, vt_ref, tile_ref, lq1_ref, lk1_ref, lq2_ref, lk2_ref,
                 subln_ref, o_ref, *scratch):
    i = pl.program_id(1)
    sc = _FlashScratch(2 * N_HEADS, *scratch)
    acc_scr = sc.acc
    for h in range(N_HEADS):
        for mp in range(2):
            r0 = h * HEAD_DIM + mp * DIFF_QK
            sc.qpad[2 * h + mp] = _pad_rows(qt_ref[0, r0:r0 + DIFF_QK, :],
                                            (h % 2) * HEAD_DIM + mp * DIFF_QK, LANES)

    def block_chains(j, tile_idx):
        row = pl.multiple_of(j * TB, TB)
        chains = []
        for h in range(N_HEADS):
            g = h // 2
            kblk = k_ref[pl.ds(row, TB), g * LANES:(g + 1) * LANES]
            add = None if tile_idx is None else tile_ref[h, tile_idx]
            vext = _with_ones(vt_ref[j, h * HEAD_DIM:(h + 1) * HEAD_DIM, :])
            chains += [(2 * h + mp, kblk, add, vext, None) for mp in range(2)]
        return chains

    _flash_causal(sc, i, block_chains, far_group=8)

    lam =(jnp.exp(jnp.sum(lq1_ref[...] * lk1_ref[...], keepdims=True))
           - jnp.exp(jnp.sum(lq2_ref[...] * lk2_ref[...], keepdims=True)) + lam_init)
    for h in range(N_HEADS):
        a1 = acc_scr[2 * h]
        a2 = acc_scr[2 * h + 1]
        o1 = a1[:HEAD_DIM] / a1[HEAD_DIM:HEAD_DIM + 1]
        o2 = a2[:HEAD_DIM] / a2[HEAD_DIM:HEAD_DIM + 1]
        a = o1 - lam * o2
        ms = jnp.mean(a * a, axis=0, keepdims=True)
        y = a * lax.rsqrt(ms + NORM_EPS) * subln_ref[...] * (1.0 - lam_init)
        o_ref[0, h * HEAD_DIM:(h + 1) * HEAD_DIM, :] = y.astype(BF16)


def _attn_common_specs(nb, q_blk, k_blk, v_blk):
    return [pl.BlockSpec((1, TB, TB), lambda b, i: (b * nb + i, q_blk, 0)),
            pl.BlockSpec((nb * TB, TB), lambda b, i: (b, k_blk)),
            pl.BlockSpec((nb, TB, TB), lambda b, i: (b, v_blk, 0))]


def _small_spec(shape):
    return pl.BlockSpec(shape, lambda b, i: (0,) * len(shape))


def _diff_attention(kn, qvt, tiles, lq1, lk1, lq2, lk2, subln, lam_init, batch):
    nblk = qvt.shape[0]
    nb = nblk // batch
    n_chain = 2 * N_HEADS
    return pl.pallas_call(
        functools.partial(_diff_kernel, lam_init),
        grid=(batch, nb),
        in_specs=_attn_common_specs(nb, QV_DIFF_Q, KN_DIFF, QV_DIFF_V) + [
            _small_spec((N_HEADS, 2, TB, TB)),
            _small_spec((1, DIFF_QK)), _small_spec((1, DIFF_QK)),
            _small_spec((1, DIFF_QK)), _small_spec((1, DIFF_QK)),
            _small_spec((HEAD_DIM, 1))],
        out_specs=pl.BlockSpec((1, TB, TB), lambda b, i: (b * nb + i, 0, 0)),
        out_shape=jax.ShapeDtypeStruct((nblk, TB, TB), BF16),
        scratch_shapes=_FlashScratch.shapes(n_chain),
        compiler_params=_cparams(2),
        name="diff_attn",
    )(qvt, kn, qvt, tiles, lq1, lk1, lq2, lk2, subln)


def _moba_kernel(nb, nbp, qt_ref, k_ref, vt_ref, tile_ref, o_ref, kmean_scr, sel_scr, *scratch):
    i = pl.program_id(1)
    sc = _FlashScratch(N_HEADS, *scratch)
    qpad_scr, acc_scr = sc.qpad, sc.acc

    @pl.when(i == 0)
    def _():
        kmean_scr[...] = jnp.zeros(kmean_scr.shape, F32)
        for jb in range(nb):
            blk = k_ref[jb * TB:(jb + 1) * TB, :].astype(F32)
            kmean_scr[jb:jb + 1, :] = jnp.mean(blk, axis=0, keepdims=True)

    for h in range(N_HEADS):
        qpad_scr[h] = _pad_rows(qt_ref[0, h * HEAD_DIM:(h + 1) * HEAD_DIM, :], (h % 2) * HEAD_DIM, LANES)

    blk_id = lax.broadcasted_iota(jnp.int32, (nbp, TB), 0)
    for h in range(N_HEADS):
        g = h // 2
        km = kmean_scr[:, g * LANES:(g + 1) * LANES].astype(BF16)
        gate = jnp.dot(km, qpad_scr[h], preferred_element_type=F32)
        avail = blk_id < i
        sel = jnp.zeros((nbp, TB), jnp.bool_)
        for _ in range(MOBA_TOPK):
            gm = jnp.where(avail, gate, -jnp.inf)
            best = jnp.max(gm, axis=0, keepdims=True)
            is_best = avail & (gm == best)
            first = jnp.min(jnp.where(is_best, blk_id, nbp), axis=0, keepdims=True)
            pick = blk_id == first
            sel = sel | pick
            avail = avail & jnp.logical_not(pick)
        sel_scr[h] = jnp.where(sel, 0.0, NEG).astype(F32)

    def block_chains(j, tile_idx):
        row = pl.multiple_of(j * TB, TB)
        chains = []
        for h in range(N_HEADS):
            g = h // 2
            kblk = k_ref[pl.ds(row, TB), g * LANES:(g + 1) * LANES]
            add = None if tile_idx is None else tile_ref[h, tile_idx]
            colmask = None if tile_idx == 0 else sel_scr[h, pl.ds(j, 1), :]
            chains.append((h, kblk, add, _with_ones(vt_ref[j, h * HEAD_DIM:(h + 1) * HEAD_DIM, :]), colmask))
        return chains

    _flash_causal(sc, i, block_chains, far_group=8)

    for h in range(N_HEADS):
        a = acc_scr[h]
        o_ref[0, h * HEAD_DIM:(h + 1) * HEAD_DIM, :] = (a[:HEAD_DIM] / a[HEAD_DIM:HEAD_DIM + 1]).astype(BF16)


def _moba_attention(kn, qvt, tiles, batch):
    nblk = qvt.shape[0]
    nb = nblk // batch
    nbp = max(8, -(-nb // 8) * 8)
    return pl.pallas_call(
        functools.partial(_moba_kernel, nb, nbp),
        grid=(batch, nb),
        in_specs=_attn_common_specs(nb, QV_MOBA_Q, KN_MOBA, QV_MOBA_V) + [
            _small_spec((N_HEADS, 2, TB, TB))],
        out_specs=pl.BlockSpec((1, TB, TB), lambda b, i: (b * nb + i, 0, 0)),
        out_shape=jax.ShapeDtypeStruct((nblk, TB, TB), BF16),
        scratch_shapes=[pltpu.VMEM((nbp, TB), F32),
                        pltpu.VMEM((N_HEADS, nbp, TB), F32)] + _FlashScratch.shapes(N_HEADS),
        compiler_params=_cparams(2),
        name="moba_attn",
    )(qvt, kn, qvt, tiles)


def _sb_kernel(qt_ref, k_ref, vt_ref, o_ref, qpad_scr, acc_scr, c_scr):
    i = pl.program_id(1)
    rows = lax.broadcasted_iota(jnp.int32, (TB, TB), 0)
    cols = lax.broadcasted_iota(jnp.int32, (TB, TB), 1)
    upper = (cols > rows).astype(BF16)
    strict = cols > rows

    heads = range(N_HEADS)

    def step(j, first):
        row = pl.multiple_of(j * TB, TB)
        zs = [jnp.dot(k_ref[pl.ds(row, TB), (h // 2) * LANES:(h // 2 + 1) * LANES], qpad_scr[h],
                      preferred_element_type=F32) for h in heads]
        log_keeps = []
        for z in zs:
            lk = -(jnp.maximum(z, 0.0) + jnp.log2(1.0 + jnp.exp2(-jnp.abs(z))))
            log_keeps.append(jnp.where(strict, lk, 0.0) if first else lk)
        laters = []
        for h in heads:
            hi = log_keeps[h].astype(BF16)
            lo = (log_keeps[h] - hi.astype(F32)).astype(BF16)
            later = (jnp.dot(upper, hi, preferred_element_type=F32)
                     + jnp.dot(upper, lo, preferred_element_type=F32))
            laters.append(later if first else later + c_scr[h])
        c_max = None
        for h in heads:
            w = jnp.exp2(zs[h] + log_keeps[h] + laters[h])
            if first:
                w = jnp.where(strict, w, 0.0)
            pv = jnp.dot(vt_ref[j, h * HEAD_DIM:(h + 1) * HEAD_DIM, :], w.astype(BF16),
                         preferred_element_type=F32)
            acc_scr[h] = pv if first else acc_scr[h] + pv
            c_new = laters[h][0:1, :] + log_keeps[h][0:1, :]
            c_scr[h] = c_new
            c_max = c_new if c_max is None else jnp.maximum(c_max, c_new)
        return jnp.max(c_max) > -SB_EXIT * LOG2E

    for h in heads:
        qpad_scr[h] = _pad_rows(qt_ref[0, h * HEAD_DIM:(h + 1) * HEAD_DIM, :], (h % 2) * HEAD_DIM, LANES)

    def cond(carry):
        j, go = carry
        return jnp.logical_and(j >= 0, go)

    def body(carry):
        j, _ = carry
        return j - 1, step(j, False)

    lax.while_loop(cond, body, (i - 1, step(i, True)))
    for h in heads:
        o_ref[0, h * HEAD_DIM:(h + 1) * HEAD_DIM, :] = acc_scr[h].astype(BF16)


def _sb_attention(kn, qvt, batch):
    nblk = qvt.shape[0]
    nb = nblk // batch
    return pl.pallas_call(
        _sb_kernel,
        grid=(batch, nb),
        in_specs=_attn_common_specs(nb, QV_SB_Q, KN_SB, QV_SB_V),
        out_specs=pl.BlockSpec((1, TB, TB), lambda b, i: (b * nb + i, 0, 0)),
        out_shape=jax.ShapeDtypeStruct((nblk, TB, TB), BF16),
        scratch_shapes=[pltpu.VMEM((N_HEADS, LANES, TB), BF16),
                        pltpu.VMEM((N_HEADS, HEAD_DIM, TB), F32),
                        pltpu.VMEM((N_HEADS, 1, TB), F32)],
        compiler_params=_cparams(2),
        name="sb_attn",
    )(qvt, kn, qvt)


def _swa_kernel(sink_ref, qt_ref, kc_ref, kp_ref, vc_ref, vp_ref, tc_ref, tp_ref, o_ref):
    i = pl.program_id(1)
    kc = kc_ref[...]
    kp = kp_ref[...]
    vc = vc_ref[0]
    vp = vp_ref[0][:, TB - SWA_WINDOW:]
    no_prev = jnp.where(i > 0, 0.0, NEG)
    group = N_HEADS // SWA_KV_HEADS
    heads = range(N_HEADS)
    qpads = [_pad_rows(qt_ref[0, h * HEAD_DIM:(h + 1) * HEAD_DIM, :], (h // group) * HEAD_DIM, LANES)
             for h in heads]
    s_cur = [jnp.dot(kc, qpads[h], preferred_element_type=F32) + tc_ref[h] for h in heads]
    s_prev = [jnp.dot(kp, qpads[h], preferred_element_type=F32) + tp_ref[h] + no_prev for h in heads]
    ms = [jnp.maximum(jnp.maximum(jnp.max(s_cur[h], axis=0, keepdims=True),
                                  jnp.max(s_prev[h], axis=0, keepdims=True)), sink_ref[h]) for h in heads]
    p_cur = [jnp.exp(s_cur[h] - ms[h]).astype(BF16) for h in heads]
    p_prev = [jnp.exp(s_prev[h] - ms[h]).astype(BF16) for h in heads]
    for h in heads:
        kv = h // group
        o = (jnp.dot(_with_ones(vc[kv * HEAD_DIM:(kv + 1) * HEAD_DIM]), p_cur[h], preferred_element_type=F32)
             + jnp.dot(_with_ones(vp[kv * HEAD_DIM:(kv + 1) * HEAD_DIM]), p_prev[h],
                       preferred_element_type=F32))
        denom = o[HEAD_DIM:HEAD_DIM + 1] + jnp.exp(sink_ref[h] - ms[h])
        o_ref[0, h * HEAD_DIM:(h + 1) * HEAD_DIM, :] = (o[:HEAD_DIM] / denom).astype(BF16)


def _swa_attention(kn, qvt, tile_cur, tile_prev, sinks, batch):
    nblk = qvt.shape[0]
    nb = nblk // batch
    half = TB // SWA_WINDOW
    kv_rows = SWA_KV_HEADS * HEAD_DIM
    return pl.pallas_call(
        _swa_kernel,
        grid=(batch, nb),
        in_specs=[pl.BlockSpec(memory_space=pltpu.SMEM),
                  pl.BlockSpec((1, TB, TB), lambda b, i: (b * nb + i, QV_SWA_Q, 0)),
                  pl.BlockSpec((TB, kv_rows), lambda b, i: (b * nb + i, KN_SWA_128)),
                  pl.BlockSpec((SWA_WINDOW, kv_rows),
                               lambda b, i: (b * nb * half + jnp.maximum(half * i - 1, 0), KN_SWA_128)),
                  pl.BlockSpec((1, kv_rows, TB), lambda b, i: (b * nb + i, QV_SWA_V_128, 0)),
                  pl.BlockSpec((1, kv_rows, TB), lambda b, i: (b * nb + jnp.maximum(i - 1, 0), QV_SWA_V_128, 0)),
                  _small_spec((N_HEADS, TB, TB)),
                  _small_spec((N_HEADS, SWA_WINDOW, TB))],
        out_specs=pl.BlockSpec((1, TB, TB), lambda b, i: (b * nb + i, 0, 0)),
        out_shape=jax.ShapeDtypeStruct((nblk, TB, TB), BF16),
        compiler_params=_cparams(2),
        name="swa_attn",
    )(sinks, qvt, kn, kn, qvt, qvt, tile_cur, tile_prev)


MERGE_T = 512
MERGE_COLS = 256


def _merge_route_kernel(x_ref, g_ref, oa_ref, ob_ref, oc_ref, od_ref, wg_ref, wbr_ref, wo_ref,
                        gf_ref, wr_ref, br_ref,
                        x1_ref, h2_ref, eid_ref, gate_ref, rank_ref, cnt_ref, base_scr):
    x = x_ref[...]
    h = _rms(x, g_ref[...]).astype(BF16)
    d = x.shape[1]
    chunks = []
    for n0 in range(0, d, MERGE_COLS):
        acc = None
        for bi, o_ref in enumerate((oa_ref, ob_ref, oc_ref, od_ref)):
            gate = jax.nn.sigmoid(jnp.dot(h, wg_ref[bi, :, n0:n0 + MERGE_COLS], preferred_element_type=F32))
            branch = jnp.concatenate(
                [lax.dot_general(o_ref[s], wbr_ref[bi, :, n0:n0 + MERGE_COLS], (((0,), (0,)), ((), ())),
                                 preferred_element_type=F32) for s in range(MERGE_T // TB)], axis=0)
            term = gate * branch
            acc = term if acc is None else acc + term
        chunks.append(acc.astype(BF16))
    merged = jnp.concatenate(chunks, axis=1)
    x1 = x + jnp.dot(merged, wo_ref[...], preferred_element_type=F32)
    x1_ref[...] = x1
    _route(x1, gf_ref, wr_ref, br_ref, h2_ref, eid_ref, gate_ref, rank_ref, cnt_ref, base_scr)


ROUTER_T = MERGE_T
ROUTER_ROWS = 8 + N_EXPERTS


def _first_argmax_rows(v, n_rows):
    best = jnp.max(v, axis=0, keepdims=True)
    ids = lax.broadcasted_iota(jnp.int32, v.shape, 0)
    return best, jnp.min(jnp.where(v == best, ids, n_rows), axis=0, keepdims=True)


def _route(x1, g_ref, w_ref, b_ref, h2_ref, eid_ref, gate_ref, rank_ref, cnt_ref, base_scr):
    i = pl.program_id(0)

    @pl.when(i == 0)
    def _():
        base_scr[...] = jnp.zeros(base_scr.shape, F32)

    h2 = _rms(x1, g_ref[...])
    h2_ref[...] = _pack_bf16_pairs(h2)
    nt = (((1,), (1,)), ((), ()))
    logits = lax.dot_general(w_ref[...], h2.astype(BF16), nt, preferred_element_type=F32) + b_ref[...]
    gl = logits[0:8]
    gmax, grp = _first_argmax_rows(gl, 8)
    p_grp = 1.0 / jnp.sum(jnp.exp(gl - gmax), axis=0, keepdims=True)
    e_sel = jnp.zeros((EXPERTS_PER_GROUP, ROUTER_T), F32)
    for g in range(N_GROUPS):
        e_sel = jnp.where(grp == g, logits[8 + 8 * g:16 + 8 * g], e_sel)
    ids8 = lax.broadcasted_iota(jnp.int32, e_sel.shape, 0)
    v1, i1 = _first_argmax_rows(e_sel, EXPERTS_PER_GROUP)
    e_rest = jnp.where(ids8 == i1, -jnp.inf, e_sel)
    v2, i2 = _first_argmax_rows(e_rest, EXPERTS_PER_GROUP)
    r = jnp.exp(v2 - v1)
    s1 = 1.0 / (1.0 + r)
    gate_ref[0:1, :] = p_grp * s1
    gate_ref[1:2, :] = p_grp * (r * s1)
    e1 = grp * EXPERTS_PER_GROUP + i1
    e2 = grp * EXPERTS_PER_GROUP + i2
    eid_ref[0:1, :] = e1
    eid_ref[1:2, :] = e2

    ids_e = lax.broadcasted_iota(jnp.int32, (N_EXPERTS, ROUTER_T), 0)
    oh1 = ids_e == e1
    oh2 = ids_e == e2
    cnt = oh1.astype(F32) + oh2.astype(F32)
    tr = lax.broadcasted_iota(jnp.int32, (ROUTER_T, ROUTER_T), 0)
    tc = lax.broadcasted_iota(jnp.int32, (ROUTER_T, ROUTER_T), 1)
    before = (tr < tc).astype(BF16)
    prefix = jnp.dot(cnt.astype(BF16), before, preferred_element_type=F32) + base_scr[:, 0:1]
    rank_ref[0:1, :] = jnp.sum(jnp.where(oh1, prefix, 0.0), axis=0, keepdims=True).astype(jnp.int32)
    rank_ref[1:2, :] = jnp.sum(jnp.where(oh2, prefix, 0.0), axis=0, keepdims=True).astype(jnp.int32)
    base_scr[...] = base_scr[...] + jnp.sum(cnt, axis=1, keepdims=True)
    cnt_ref[...] = base_scr[...]


def _merge_route(x2, g, o_a, o_b, o_c, o_d, wg, wbr, wo, g_ffn, w_route, b_route):
    n, d = x2.shape
    o_spec = pl.BlockSpec((MERGE_T // TB, TB, TB), lambda i: (i, 0, 0))
    row2 = lambda dt: jax.ShapeDtypeStruct((2, n), dt)
    spec2 = pl.BlockSpec((2, MERGE_T), lambda i: (0, i))
    return pl.pallas_call(
        _merge_route_kernel,
        grid=(n // MERGE_T,),
        in_specs=[pl.BlockSpec((MERGE_T, d), lambda i: (i, 0)),
                  pl.BlockSpec((1, d), lambda i: (0, 0)),
                  o_spec, o_spec, o_spec, o_spec,
                  pl.BlockSpec(wg.shape, lambda i: (0, 0, 0)),
                  pl.BlockSpec(wbr.shape, lambda i: (0, 0, 0)),
                  pl.BlockSpec(wo.shape, lambda i: (0, 0)),
                  pl.BlockSpec((1, d), lambda i: (0, 0)),
                  pl.BlockSpec((ROUTER_ROWS, d), lambda i: (0, 0)),
                  pl.BlockSpec((ROUTER_ROWS, 1), lambda i: (0, 0))],
        out_specs=[pl.BlockSpec((MERGE_T, d), lambda i: (i, 0)),
                   pl.BlockSpec((MERGE_T, d // 2), lambda i: (i, 0)), spec2, spec2, spec2,
                   pl.BlockSpec((N_EXPERTS, LANES), lambda i: (0, 0))],
        out_shape=[jax.ShapeDtypeStruct((n, d), F32),
                   jax.ShapeDtypeStruct((n, d // 2), jnp.uint32), row2(jnp.int32), row2(F32), row2(jnp.int32),
                   jax.ShapeDtypeStruct((N_EXPERTS, LANES), F32)],
        scratch_shapes=[pltpu.VMEM((N_EXPERTS, LANES), F32)],
        compiler_params=_cparams(1),
        name="merge_route",
    )(x2, g, o_a, o_b, o_c, o_d, wg, wbr, wo, g_ffn, w_route, b_route)


def _expert_kernel(be_ref, nu_ref, nv_ref, x_ref, w1_ref, w3_ref, w2_ref, y_ref):
    del be_ref
    used = pl.program_id(0) < nu_ref[0]

    @pl.when(used)
    def _():
        live = lax.broadcasted_iota(jnp.int32, x_ref.shape, 0) < nv_ref[pl.program_id(0)]
        xb = _unpack_bf16_pairs(jnp.where(live, x_ref[...], jnp.uint32(0))).astype(BF16)
        a = jnp.dot(xb, w1_ref[...].astype(BF16), preferred_element_type=F32)
        b = jnp.dot(xb, w3_ref[...].astype(BF16), preferred_element_type=F32)
        mid = (a * jax.nn.sigmoid(a) * b).astype(BF16)
        y_ref[...] = _pack_bf16_pairs(jnp.dot(mid, w2_ref[...].astype(BF16), preferred_element_type=F32))

    @pl.when(jnp.logical_not(used))
    def _():
        y_ref[...] = jnp.zeros(y_ref.shape, jnp.uint32)


def _experts(blk_expert, n_used, n_valid, xbuf, w1, w3, w2, layer):
    r = xbuf.shape[0]
    d = w1.shape[-2]
    de = w1.shape[-1]
    assert xbuf.shape[1] * 2 == d
    row_map = lambda i, be, nu, nv: (jnp.minimum(i, nu[0] - 1), 0)
    grid_spec = pltpu.PrefetchScalarGridSpec(
        num_scalar_prefetch=3,
        grid=(r // MOE_ROWS,),
        in_specs=[pl.BlockSpec((MOE_ROWS, d // 2), row_map),
                  pl.BlockSpec((None, None, d, de), lambda i, be, nu, nv: (layer, be[i], 0, 0)),
                  pl.BlockSpec((None, None, d, de), lambda i, be, nu, nv: (layer, be[i], 0, 0)),
                  pl.BlockSpec((None, None, de, d), lambda i, be, nu, nv: (layer, be[i], 0, 0))],
        out_specs=pl.BlockSpec((MOE_ROWS, d // 2), lambda i, be, nu, nv: (i, 0)),
    )
    return pl.pallas_call(
        _expert_kernel,
        grid_spec=grid_spec,
        out_shape=jax.ShapeDtypeStruct((r, d // 2), jnp.uint32),
        compiler_params=_cparams(1),
        name="experts",
    )(blk_expert, n_used, n_valid, xbuf, w1, w3, w2)


SC_CORES = 2
SC_SUBCORES = 16
SC_ROWS = 64


def _sc_gather_rows(table, idx):
    n_idx = idx.shape[0]
    d = table.shape[1]
    n_workers = SC_CORES * SC_SUBCORES
    per_worker = n_idx // n_workers
    n_chunk = per_worker // SC_ROWS
    assert per_worker * n_workers == n_idx and n_chunk * SC_ROWS == per_worker and n_chunk % 2 == 0
    mesh = plsc.VectorSubcoreMesh(core_axis_name="c", subcore_axis_name="s",
                                  num_cores=SC_CORES, num_subcores=SC_SUBCORES)

    def body(table_hbm, idx_hbm, out_hbm, idx_v, rows_v, gsem, wsem):
        worker = lax.axis_index("s") * SC_CORES + lax.axis_index("c")
        base = worker * per_worker
        pltpu.sync_copy(idx_hbm.at[pl.ds(base, per_worker)], idx_v)

        def gather(c, b):
            return pltpu.make_async_copy(table_hbm.at[idx_v.at[pl.ds(c * SC_ROWS, SC_ROWS)]],
                                         rows_v.at[b], gsem.at[b])

        def put(c, b):
            return pltpu.make_async_copy(rows_v.at[b], out_hbm.at[pl.ds(base + c * SC_ROWS, SC_ROWS)],
                                         wsem.at[b])

        gather(0, 0).start()

        @pl.loop(0, n_chunk, step=2)
        def _(c0):
            for b in range(2):
                c = c0 + b
                gather(c, b).wait()

                @pl.when(c + 1 < n_chunk)
                def _():
                    @pl.when(c >= 1)
                    def _():
                        put(c - 1, 1 - b).wait()

                    gather(c + 1, 1 - b).start()

                put(c, b).start()

        put(n_chunk - 2, 0).wait()
        put(n_chunk - 1, 1).wait()

    return pl.kernel(
        body,
        out_type=jax.ShapeDtypeStruct((n_idx, d), table.dtype),
        mesh=mesh,
        scratch_types=[pltpu.VMEM((per_worker,), jnp.int32),
                       pltpu.VMEM((2, SC_ROWS, d), table.dtype),
                       pltpu.SemaphoreType.DMA((2,)),
                       pltpu.SemaphoreType.DMA((2,))],
        name="sc_gather_rows",
    )(table, idx)


def _sc_scatter_rows(rows, dest, n_out):
    n, d = rows.shape
    n_workers = SC_CORES * SC_SUBCORES
    per_worker = n // n_workers
    n_chunk = per_worker // SC_ROWS
    assert per_worker * n_workers == n and n_chunk * SC_ROWS == per_worker and n_chunk % 2 == 0
    mesh = plsc.VectorSubcoreMesh(core_axis_name="c", subcore_axis_name="s",
                                  num_cores=SC_CORES, num_subcores=SC_SUBCORES)
    dest3 = dest.reshape(2, n // SC_ROWS, SC_ROWS)

    def body(rows_hbm, idx_hbm, out_hbm, idx_v, rows_v, lsem, ssem):
        worker = lax.axis_index("s") * SC_CORES + lax.axis_index("c")
        for k in range(2):
            pltpu.sync_copy(idx_hbm.at[k, pl.ds(worker * n_chunk, n_chunk)], idx_v.at[k])

        def load(c, b):
            return pltpu.make_async_copy(rows_hbm.at[pl.ds(worker * per_worker + c * SC_ROWS, SC_ROWS)],
                                         rows_v.at[b], lsem.at[b])

        def scatter(c, b, k):
            return pltpu.make_async_copy(rows_v.at[b], out_hbm.at[idx_v.at[k, c]], ssem.at[b])

        load(0, 0).start()

        @pl.loop(0, n_chunk, step=2)
        def _(c0):
            for b in range(2):
                c = c0 + b
                load(c, b).wait()

                @pl.when(c + 1 < n_chunk)
                def _():
                    @pl.when(c >= 1)
                    def _():
                        scatter(c - 1, 1 - b, 0).wait()
                        scatter(c - 1, 1 - b, 1).wait()

                    load(c + 1, 1 - b).start()

                scatter(c, b, 0).start()
                scatter(c, b, 1).start()

        for c, b in ((n_chunk - 2, 0), (n_chunk - 1, 1)):
            scatter(c, b, 0).wait()
            scatter(c, b, 1).wait()

    return pl.kernel(
        body,
        out_type=jax.ShapeDtypeStruct((n_out, d), rows.dtype),
        mesh=mesh,
        scratch_types=[pltpu.VMEM((2, n_chunk, SC_ROWS), jnp.int32),
                       pltpu.VMEM((2, SC_ROWS, d), rows.dtype),
                       pltpu.SemaphoreType.DMA((2,)),
                       pltpu.SemaphoreType.DMA((2,))],
        name="sc_scatter_rows",
    )(rows, dest3)


def _combine_dense_kernel(final, gate_ref, x1_ref, gf_ref, y0_ref, y1_ref, out_ref):
    gate = gate_ref[...]
    out = (x1_ref[...] + gate[:, 0:1] * _unpack_bf16_pairs(y0_ref[...])
           + gate[:, 1:2] * _unpack_bf16_pairs(y1_ref[...]))
    if final:
        out = _rms(out, gf_ref[...])
    out_ref[...] = out


def _combine_dense(gate_t, x1, g_final, yg, final):
    n, d = x1.shape
    nb = n // MERGE_T
    return pl.pallas_call(
        functools.partial(_combine_dense_kernel, final),
        grid=(nb,),
        in_specs=[pl.BlockSpec((MERGE_T, 2), lambda i: (i, 0)),
                  pl.BlockSpec((MERGE_T, d), lambda i: (i, 0)),
                  pl.BlockSpec((1, d), lambda i: (0, 0)),
                  pl.BlockSpec((MERGE_T, d // 2), lambda i: (i, 0)),
                  pl.BlockSpec((MERGE_T, d // 2), lambda i: (i + nb, 0))],
        out_specs=pl.BlockSpec((MERGE_T, d), lambda i: (i, 0)),
        out_shape=jax.ShapeDtypeStruct((n, d), F32),
        compiler_params=_cparams(1),
        name="combine_dense",
    )(gate_t, x1, g_final, yg, yg)


def _projection_weights(w):
    d = w.shape[0]
    blk = N_HEADS * HEAD_DIM
    kv = SWA_KV_HEADS * HEAD_DIM
    pa, pb, pc, pd = w[:, :3 * blk], w[:, 3 * blk:6 * blk], w[:, 6 * blk:9 * blk], w[:, 9 * blk:]
    half = N_HEADS * DIFF_QK

    def per_head(a, b):
        return jnp.stack([a.reshape(d, N_HEADS, DIFF_QK), b.reshape(d, N_HEADS, DIFF_QK)], axis=2).reshape(d, blk)

    s64, s32 = HEAD_DIM ** -0.5, DIFF_QK ** -0.5
    wn = jnp.concatenate([pa[:, blk:2 * blk], per_head(pb[:, 2 * half:3 * half], pb[:, 3 * half:4 * half]),
                          pc[:, blk:2 * blk], pd[:, blk:blk + kv]], axis=1)
    wt = jnp.concatenate([pa[:, :blk] * (s64 * LOG2E), pa[:, 2 * blk:],
                          per_head(pb[:, :half], pb[:, half:2 * half]) * (s32 * LOG2E), pb[:, 4 * half:],
                          pc[:, :blk] * (s64 * LOG2E), pc[:, 2 * blk:],
                          pd[:, :blk] * s64, pd[:, blk + kv:]], axis=1)
    assert wn.shape[1] == KN_COLS and wt.shape[1] == QV_ROWS
    return wn.astype(BF16), wt.T.astype(BF16)


def _router_weights(w_rg, b_rg, w_re, b_re):
    d = w_rg.shape[0]
    w = jnp.concatenate([w_rg.T, jnp.zeros((8 - N_GROUPS, d), F32), w_re.T], axis=0)
    b = jnp.concatenate([b_rg.astype(F32), jnp.full((8 - N_GROUPS,), NEG, F32), b_re.astype(F32)])[:, None]
    return w.astype(BF16), b


def _moe_plan(eid, rank, counts, n_rows_total):
    padded = (counts + MOE_ROWS - 1) // MOE_ROWS * MOE_ROWS
    pad_end = jnp.cumsum(padded)
    pad_start = pad_end - padded
    experts = jnp.arange(N_EXPERTS, dtype=jnp.int32)
    start_of = jnp.sum(jnp.where(eid[..., None] == experts, pad_start, 0), axis=-1)
    dest = start_of + rank
    n_blk = n_rows_total // MOE_ROWS
    n_used = (pad_end[-1] // MOE_ROWS).astype(jnp.int32)
    blk = jnp.minimum(jnp.arange(n_blk, dtype=jnp.int32), n_used - 1) * MOE_ROWS
    blk_expert = jnp.minimum(jnp.sum(pad_end[None, :] <= blk[:, None], axis=1), N_EXPERTS - 1).astype(jnp.int32)
    live_end = jnp.sum(jnp.where(blk_expert[:, None] == experts, pad_start + counts, 0), axis=-1)
    n_valid = jnp.clip(live_end - blk, 0, MOE_ROWS).astype(jnp.int32)
    return dest.astype(jnp.int32), blk_expert, n_used.reshape(1), n_valid


def kernel(x, rel_bias, g_mix, w_in, diff_lq1, diff_lk1, diff_lq2, diff_lk2, diff_subln, swa_sinks,
           w_gate, w_br, w_o, g_ffn, w_route_group, b_route_group, w_route_expert, b_route_expert,
           w1, w3, w2, g_final):
    batch, seq, d = x.shape
    n = batch * seq
    depth = w_in.shape[0]
    assert seq % TB == 0 and n % MERGE_T == 0 and TB == MOBA_BLOCK
    tab = rel_bias.T.astype(F32)
    tiles_moba = _causal_bias_tiles(tab[:N_HEADS])
    tiles_diff = _causal_bias_tiles(tab[N_HEADS:2 * N_HEADS])
    tile_cur, tile_prev = _swa_bias_tiles(tab[2 * N_HEADS:])
    n_rows_total = n * 2 + N_EXPERTS * MOE_ROWS
    row = lambda v: v.astype(F32)[None, :]

    x2 = x.reshape(n, d)
    for l in range(depth):
        lam_init = 0.8 - 0.6 * math.exp(-0.3 * l)
        wn, wt = _projection_weights(w_in[l])
        kn, qvt = _inproj(x2, row(g_mix[l]), wn, wt)
        o_a = _moba_attention(kn, qvt, tiles_moba, batch)
        o_b = _diff_attention(kn, qvt, tiles_diff, row(diff_lq1[l]), row(diff_lk1[l]), row(diff_lq2[l]),
                              row(diff_lk2[l]), diff_subln[l].astype(F32)[:, None], lam_init, batch)
        o_c = _sb_attention(kn, qvt, batch)
        o_d = _swa_attention(kn, qvt, tile_cur, tile_prev, swa_sinks[l].astype(F32), batch)
        w_route, r_bias = _router_weights(w_route_group[l], b_route_group[l], w_route_expert[l],
                                          b_route_expert[l])
        x1, h2, eid, gate, rank, cnt = _merge_route(
            x2, row(g_mix[l]), o_a, o_b, o_c, o_d, w_gate[l].astype(BF16), w_br[l].astype(BF16),
            w_o[l].astype(BF16), row(g_ffn[l]), w_route, r_bias)
        dest, blk_expert, n_used, n_valid = _moe_plan(eid, rank, cnt[:, 0].astype(jnp.int32), n_rows_total)
        xbuf = _sc_scatter_rows(h2, dest, n_rows_total)
        y = _experts(blk_expert, n_used, n_valid, xbuf, w1, w3, w2, l)
        yg = _sc_gather_rows(y, dest.reshape(-1))
        x2 = _combine_dense(gate.T, x1, row(g_final), yg, l == depth - 1)
    return x2.reshape(batch, seq, d)
```

```python
import functools
import math

import numpy as np
import jax
import jax.numpy as jnp
from jax import lax
from jax.experimental import pallas as pl
from jax.experimental.pallas import tpu as pltpu
from jax.experimental.pallas import tpu_sc as plsc

F32 = jnp.float32
BF16 = jnp.bfloat16

HEAD_DIM = 64
N_HEADS = 4
DIFF_QK = 32
SWA_KV_HEADS = 2
SWA_WINDOW = 128
MOBA_BLOCK = 256
MOBA_TOPK = 3
REL_BUCKETS = 32
REL_MAX_DIST = 128
N_GROUPS = 4
EXPERTS_PER_GROUP = 8
N_EXPERTS = N_GROUPS * EXPERTS_PER_GROUP
NORM_EPS = 1e-6

TB = 256
LANES = 128
ONES_ROWS = 16
NEG = -1e30
LOG2E = math.log2(math.e)
SB_EXIT = 104.0
MOE_ROWS = 512
VMEM_LIMIT = 56 * 1024 * 1024

QV_MOBA_Q, QV_MOBA_V, QV_DIFF_Q, QV_DIFF_V, QV_SB_Q, QV_SB_V, QV_SWA_Q = range(7)
QV_SWA_V_128 = 14
QV_ROWS = 7 * 256 + 128
KN_MOBA, KN_DIFF, KN_SB = range(3)
KN_SWA_128 = 6
KN_COLS = 3 * 256 + 128


def _cparams(n_grid):
    return pltpu.CompilerParams(dimension_semantics=("arbitrary",) * n_grid,
                                vmem_limit_bytes=VMEM_LIMIT)


def _rel_bucket_np(n):
    n = np.maximum(n, 0)
    max_exact = REL_BUCKETS // 2
    nf = np.maximum(n, 1).astype(np.float64)
    large = max_exact + (np.log(nf / max_exact) / math.log(REL_MAX_DIST / max_exact)
                         * (REL_BUCKETS - max_exact)).astype(np.int64)
    large = np.minimum(large, REL_BUCKETS - 1)
    return np.where(n < max_exact, n, large)


def _first_far_distance():
    d = np.arange(0, 4 * REL_MAX_DIST)
    b = _rel_bucket_np(d)
    return int(np.min(d[b == REL_BUCKETS - 1]))


def _toeplitz_bias(tab, rows, cols, base, valid_fn, shift_far, unit):
    length = rows + cols - 1
    off = np.concatenate([np.arange(0, cols), np.arange(cols - length, 0)])
    n = base + off
    onehot = np.zeros((REL_BUCKETS, length), np.float32)
    onehot[_rel_bucket_np(n), np.arange(length)] = 1.0
    vec = jnp.dot(tab, jnp.asarray(onehot), precision=lax.Precision.HIGHEST)
    if shift_far:
        vec = vec - tab[:, REL_BUCKETS - 1:]
    vec = jnp.where(jnp.asarray(valid_fn(n))[None, :], vec * unit, NEG).astype(F32)
    flat = jnp.tile(vec, (1, rows))[:, :rows * (length - 1)]
    return flat.reshape(tab.shape[0], rows, length - 1)[:, :, :cols]


def _causal_bias_tiles(tab):
    assert _first_far_distance() <= TB + 1
    tiles = [_toeplitz_bias(tab, TB, TB, d * TB, lambda n: n >= 0, True, LOG2E) for d in range(2)]
    return jnp.stack(tiles, axis=1)


def _swa_bias_tiles(tab):
    in_window = lambda n: (n >= 0) & (n < SWA_WINDOW)
    return (_toeplitz_bias(tab, TB, TB, 0, in_window, False, 1.0),
            _toeplitz_bias(tab, SWA_WINDOW, TB, SWA_WINDOW, in_window, False, 1.0))


def _pad_rows(q, off, total):
    n, t = q.shape
    parts = []
    if off:
        parts.append(jnp.zeros((off, t), q.dtype))
    parts.append(q)
    if total - off - n:
        parts.append(jnp.zeros((total - off - n, t), q.dtype))
    return jnp.concatenate(parts, axis=0) if len(parts) > 1 else q


def _with_ones(v):
    return jnp.concatenate([v, jnp.ones((ONES_ROWS, v.shape[1]), v.dtype)], axis=0)


def _pack_bf16_pairs(x):
    w = x.shape[1] // 2
    lo = lax.bitcast_convert_type(x[:, :w].astype(BF16).astype(F32), jnp.uint32)
    hi = lax.bitcast_convert_type(x[:, w:].astype(BF16).astype(F32), jnp.uint32)
    return hi | (lo >> 16)


def _unpack_bf16_pairs(u):
    lo = lax.bitcast_convert_type(u << 16, F32)
    hi = lax.bitcast_convert_type(u & jnp.uint32(0xFFFF0000), F32)
    return jnp.concatenate([lo, hi], axis=1)


def _rms(x, g_row):
    ms = jnp.mean(x * x, axis=-1, keepdims=True)
    return x * lax.rsqrt(ms + NORM_EPS) * g_row


IN_T = 512
IN_CHUNK = 384


def _inproj_kernel(x_ref, g_ref, wn_ref, wt_ref, kn_ref, qvt_ref):
    h = _rms(x_ref[...], g_ref[...]).astype(BF16)
    kn_ref[...] = jnp.dot(h, wn_ref[...], preferred_element_type=F32).astype(BF16)
    for r0 in range(0, QV_ROWS, IN_CHUNK):
        pt = lax.dot_general(wt_ref[r0:r0 + IN_CHUNK, :], h, (((1,), (1,)), ((), ())),
                             preferred_element_type=F32)
        for s in range(IN_T // TB):
            qvt_ref[s, r0:r0 + IN_CHUNK, :] = pt[:, s * TB:(s + 1) * TB].astype(BF16)


def _inproj(x2, g, wn, wt):
    n, d = x2.shape
    return pl.pallas_call(
        _inproj_kernel,
        grid=(n // IN_T,),
        in_specs=[pl.BlockSpec((IN_T, d), lambda i: (i, 0)),
                  pl.BlockSpec((1, d), lambda i: (0, 0)),
                  pl.BlockSpec((d, KN_COLS), lambda i: (0, 0)),
                  pl.BlockSpec((QV_ROWS, d), lambda i: (0, 0))],
        out_specs=[pl.BlockSpec((IN_T, KN_COLS), lambda i: (i, 0)),
                   pl.BlockSpec((IN_T // TB, QV_ROWS, TB), lambda i: (i, 0, 0))],
        out_shape=[jax.ShapeDtypeStruct((n, KN_COLS), BF16),
                   jax.ShapeDtypeStruct((n // TB, QV_ROWS, TB), BF16)],
        compiler_params=_cparams(1),
        name="inproj",
    )(x2, g, wn, wt)


FLASH_OVERFLOW = 100.0
FLASH_SKEW = 4


class _FlashScratch:
    def __init__(self, n_chain, qpad, m, acc, over, s):
        self.n_chain, self.qpad, self.m, self.acc, self.over, self.s = n_chain, qpad, m, acc, over, s

    @staticmethod
    def shapes(n_chain):
        return [pltpu.VMEM((n_chain, LANES, TB), BF16),
                pltpu.VMEM((n_chain, 1, TB), F32),
                pltpu.VMEM((n_chain, HEAD_DIM + ONES_ROWS, TB), F32),
                pltpu.VMEM((n_chain, 1, TB), F32),
                pltpu.VMEM((n_chain, TB, TB), F32)]


def _flash_two_pass(sc, chains):
    if len(chains) > sc.n_chain:
        for k in range(0, len(chains), sc.n_chain):
            _flash_two_pass(sc, chains[k:k + sc.n_chain])
        return
    block_max = []
    for c, kblk, add, _, colmask in chains:
        s = jnp.dot(kblk, sc.qpad[c], preferred_element_type=F32)
        if add is not None:
            s = s + add
        if colmask is not None:
            s = s + colmask
        sc.s[c] = s
        block_max.append(jnp.max(s, axis=0, keepdims=True))
    for (c, _, _, vext, _), mx in zip(chains, block_max):
        m_old = sc.m[c]
        m_new = jnp.maximum(m_old, mx)
        p = jnp.exp2(sc.s[c] - m_new).astype(BF16)
        sc.acc[c] = sc.acc[c] * jnp.exp2(m_old - m_new) + jnp.dot(vext, p, preferred_element_type=F32)
        sc.m[c] = m_new


def _flash_lagged(sc, chains, n_first=0):
    def finish(c, p, mx, ref, vext, first):
        pv = jnp.dot(vext, p, preferred_element_type=F32)
        ref_new = jnp.maximum(ref, mx)
        sc.acc[c] = (sc.acc[c] + pv) * jnp.exp2(ref - ref_new)
        sc.m[c] = ref_new
        sc.over[c] = jnp.maximum(sc.over[c], jnp.abs(mx - ref) if first else mx - ref)

    pending = []
    for idx, (c, kblk, add, vext, colmask) in enumerate(chains):
        s = jnp.dot(kblk, sc.qpad[c], preferred_element_type=F32)
        if add is not None:
            s = s + add
        if len(pending) >= min(FLASH_SKEW, sc.n_chain):
            finish(*pending.pop(0))
        ref = sc.m[c]
        mx = jnp.max(s, axis=0, keepdims=True)
        if colmask is None:
            p = jnp.exp2(s - ref)
        else:
            p = jnp.exp2(s - (ref - colmask))
            mx = mx + colmask
        pending.append((c, p.astype(BF16), mx, ref, vext, idx < n_first))
    for item in pending:
        finish(*item)


def _flash_causal(sc, i, block_chains, far_group):
    def init(ref0):
        for c in range(sc.n_chain):
            sc.m[c] = jnp.full(sc.m.shape[1:], ref0, F32)
            sc.acc[c] = jnp.zeros(sc.acc.shape[1:], F32)
            sc.over[c] = jnp.full(sc.over.shape[1:], NEG, F32)

    def far_blocks(step, j, count):
        chains = []
        for k in range(count):
            chains += block_chains(j + k, None)
        step(sc, chains)

    def run_two_pass():
        init(NEG)
        _flash_two_pass(sc, block_chains(i, 0))

        @pl.when(i >= 1)
        def _():
            _flash_two_pass(sc, block_chains(i - 1, 1))

        lax.fori_loop(0, jnp.maximum(i - 1, 0), lambda j, carry: (far_blocks(_flash_two_pass, j, 1), carry)[1], 0)

    def run_single_pass():
        init(0.0)

        @pl.when(i == 0)
        def _():
            _flash_lagged(sc, block_chains(i, 0), sc.n_chain)

        head = lax.rem(i + far_group - 1, far_group) + 2
        for size in range(2, far_group + 2):
            @pl.when(jnp.logical_and(i >= 1, head == size))
            def _(size=size):
                chains = block_chains(i, 0) + block_chains(i - 1, 1)
                for k in range(size - 2):
                    chains += block_chains(i - 2 - k, None)
                _flash_lagged(sc, chains, sc.n_chain)

        def far_group_step(t, carry):
            far_blocks(_flash_lagged, far_group * t, far_group)
            return carry

        lax.fori_loop(0, jnp.where(i >= 1, (i + 1 - head) // far_group, 0), far_group_step, 0)

    run_single_pass()
    worst = sc.over[0]
    for c in range(1, sc.n_chain):
        worst = jnp.maximum(worst, sc.over[c])

    @pl.when(jnp.max(worst) > FLASH_OVERFLOW)
    def _():
        run_two_pass()


def _diff_kernel(lam_init, qt_ref, k_ref, vt_ref, tile_ref, lq1_ref, lk1_ref, lq2_ref, lk2_ref,
                 subln_ref, o_ref, *scratch):
    i = pl.program_id(1)
    sc = _FlashScratch(2 * N_HEADS, *scratch)
    acc_scr = sc.acc
    for h in range(N_HEADS):
        for mp in range(2):
            r0 = h * HEAD_DIM + mp * DIFF_QK
            sc.qpad[2 * h + mp] = _pad_rows(qt_ref[0, r0:r0 + DIFF_QK, :],
                                            (h % 2) * HEAD_DIM + mp * DIFF_QK, LANES)

    def block_chains(j, tile_idx):
        row = pl.multiple_of(j * TB, TB)
        chains = []
        for h in range(N_HEADS):
            g = h // 2
            kblk = k_ref[pl.ds(row, TB), g * LANES:(g + 1) * LANES]
            add = None if tile_idx is None else tile_ref[h, tile_idx]
            vext = _with_ones(vt_ref[j, h * HEAD_DIM:(h + 1) * HEAD_DIM, :])
            chains += [(2 * h + mp, kblk, add, vext, None) for mp in range(2)]
        return chains

    _flash_causal(sc, i, block_chains, far_group=8)

    lam =(jnp.exp(jnp.sum(lq1_ref[...] * lk1_ref[...], keepdims=True))
           - jnp.exp(jnp.sum(lq2_ref[...] * lk2_ref[...], keepdims=True)) + lam_init)
    for h in range(N_HEADS):
        a1 = acc_scr[2 * h]
        a2 = acc_scr[2 * h + 1]
        o1 = a1[:HEAD_DIM] / a1[HEAD_DIM:HEAD_DIM + 1]
        o2 = a2[:HEAD_DIM] / a2[HEAD_DIM:HEAD_DIM + 1]
        a = o1 - lam * o2
        ms = jnp.mean(a * a, axis=0, keepdims=True)
        y = a * lax.rsqrt(ms + NORM_EPS) * subln_ref[...] * (1.0 - lam_init)
        o_ref[0, h * HEAD_DIM:(h + 1) * HEAD_DIM, :] = y.astype(BF16)


def _attn_common_specs(nb, q_blk, k_blk, v_blk):
    return [pl.BlockSpec((1, TB, TB), lambda b, i: (b * nb + i, q_blk, 0)),
            pl.BlockSpec((nb * TB, TB), lambda b, i: (b, k_blk)),
            pl.BlockSpec((nb, TB, TB), lambda b, i: (b, v_blk, 0))]


def _small_spec(shape):
    return pl.BlockSpec(shape, lambda b, i: (0,) * len(shape))


def _diff_attention(kn, qvt, tiles, lq1, lk1, lq2, lk2, subln, lam_init, batch):
    nblk = qvt.shape[0]
    nb = nblk // batch
    n_chain = 2 * N_HEADS
    return pl.pallas_call(
        functools.partial(_diff_kernel, lam_init),
        grid=(batch, nb),
        in_specs=_attn_common_specs(nb, QV_DIFF_Q, KN_DIFF, QV_DIFF_V) + [
            _small_spec((N_HEADS, 2, TB, TB)),
            _small_spec((1, DIFF_QK)), _small_spec((1, DIFF_QK)),
            _small_spec((1, DIFF_QK)), _small_spec((1, DIFF_QK)),
            _small_spec((HEAD_DIM, 1))],
        out_specs=pl.BlockSpec((1, TB, TB), lambda b, i: (b * nb + i, 0, 0)),
        out_shape=jax.ShapeDtypeStruct((nblk, TB, TB), BF16),
        scratch_shapes=_FlashScratch.shapes(n_chain),
        compiler_params=_cparams(2),
        name="diff_attn",
    )(qvt, kn, qvt, tiles, lq1, lk1, lq2, lk2, subln)


def _moba_kernel(nb, nbp, qt_ref, k_ref, vt_ref, tile_ref, o_ref, kmean_scr, sel_scr, *scratch):
    i = pl.program_id(1)
    sc = _FlashScratch(N_HEADS, *scratch)
    qpad_scr, acc_scr = sc.qpad, sc.acc

    @pl.when(i == 0)
    def _():
        kmean_scr[...] = jnp.zeros(kmean_scr.shape, F32)
        for jb in range(nb):
            blk = k_ref[jb * TB:(jb + 1) * TB, :].astype(F32)
            kmean_scr[jb:jb + 1, :] = jnp.mean(blk, axis=0, keepdims=True)

    for h in range(N_HEADS):
        qpad_scr[h] = _pad_rows(qt_ref[0, h * HEAD_DIM:(h + 1) * HEAD_DIM, :], (h % 2) * HEAD_DIM, LANES)

    blk_id = lax.broadcasted_iota(jnp.int32, (nbp, TB), 0)
    for h in range(N_HEADS):
        g = h // 2
        km = kmean_scr[:, g * LANES:(g + 1) * LANES].astype(BF16)
        gate = jnp.dot(km, qpad_scr[h], preferred_element_type=F32)
        avail = blk_id < i
        sel = jnp.zeros((nbp, TB), jnp.bool_)
        for _ in range(MOBA_TOPK):
            gm = jnp.where(avail, gate, -jnp.inf)
            best = jnp.max(gm, axis=0, keepdims=True)
            is_best = avail & (gm == best)
            first = jnp.min(jnp.where(is_best, blk_id, nbp), axis=0, keepdims=True)
            pick = blk_id == first
            sel = sel | pick
            avail = avail & jnp.logical_not(pick)
        sel_scr[h] = jnp.where(sel, 0.0, NEG).astype(F32)

    def block_chains(j, tile_idx):
        row = pl.multiple_of(j * TB, TB)
        chains = []
        for h in range(N_HEADS):
            g = h // 2
            kblk = k_ref[pl.ds(row, TB), g * LANES:(g + 1) * LANES]
            add = None if tile_idx is None else tile_ref[h, tile_idx]
            colmask = None if tile_idx == 0 else sel_scr[h, pl.ds(j, 1), :]
            chains.append((h, kblk, add, _with_ones(vt_ref[j, h * HEAD_DIM:(h + 1) * HEAD_DIM, :]), colmask))
        return chains

    _flash_causal(sc, i, block_chains, far_group=16)

    for h in range(N_HEADS):
        a = acc_scr[h]
        o_ref[0, h * HEAD_DIM:(h + 1) * HEAD_DIM, :] = (a[:HEAD_DIM] / a[HEAD_DIM:HEAD_DIM + 1]).astype(BF16)


def _moba_attention(kn, qvt, tiles, batch):
    nblk = qvt.shape[0]
    nb = nblk // batch
    nbp = max(8, -(-nb // 8) * 8)
    return pl.pallas_call(
        functools.partial(_moba_kernel, nb, nbp),
        grid=(batch, nb),
        in_specs=_attn_common_specs(nb, QV_MOBA_Q, KN_MOBA, QV_MOBA_V) + [
            _small_spec((N_HEADS, 2, TB, TB))],
        out_specs=pl.BlockSpec((1, TB, TB), lambda b, i: (b * nb + i, 0, 0)),
        out_shape=jax.ShapeDtypeStruct((nblk, TB, TB), BF16),
        scratch_shapes=[pltpu.VMEM((nbp, TB), F32),
                        pltpu.VMEM((N_HEADS, nbp, TB), F32)] + _FlashScratch.shapes(N_HEADS),
        compiler_params=_cparams(2),
        name="moba_attn",
    )(qvt, kn, qvt, tiles)


def _sb_kernel(qt_ref, k_ref, vt_ref, o_ref, qpad_scr, acc_scr, c_scr):
    i = pl.program_id(1)
    rows = lax.broadcasted_iota(jnp.int32, (TB, TB), 0)
    cols = lax.broadcasted_iota(jnp.int32, (TB, TB), 1)
    upper = (cols > rows).astype(BF16)
    strict = cols > rows

    heads = range(N_HEADS)

    def step(j, first):
        row = pl.multiple_of(j * TB, TB)
        zs = [jnp.dot(k_ref[pl.ds(row, TB), (h // 2) * LANES:(h // 2 + 1) * LANES], qpad_scr[h],
                      preferred_element_type=F32) for h in heads]
        log_keeps = []
        for z in zs:
            lk = -(jnp.maximum(z, 0.0) + jnp.log2(1.0 + jnp.exp2(-jnp.abs(z))))
            log_keeps.append(jnp.where(strict, lk, 0.0) if first else lk)
        laters = []
        for h in heads:
            hi = log_keeps[h].astype(BF16)
            lo = (log_keeps[h] - hi.astype(F32)).astype(BF16)
            later = (jnp.dot(upper, hi, preferred_element_type=F32)
                     + jnp.dot(upper, lo, preferred_element_type=F32))
            laters.append(later if first else later + c_scr[h])
        c_max = None
        for h in heads:
            w = jnp.exp2(zs[h] + log_keeps[h] + laters[h])
            if first:
                w = jnp.where(strict, w, 0.0)
            pv = jnp.dot(vt_ref[j, h * HEAD_DIM:(h + 1) * HEAD_DIM, :], w.astype(BF16),
                         preferred_element_type=F32)
            acc_scr[h] = pv if first else acc_scr[h] + pv
            c_new = laters[h][0:1, :] + log_keeps[h][0:1, :]
            c_scr[h] = c_new
            c_max = c_new if c_max is None else jnp.maximum(c_max, c_new)
        return jnp.max(c_max) > -SB_EXIT * LOG2E

    for h in heads:
        qpad_scr[h] = _pad_rows(qt_ref[0, h * HEAD_DIM:(h + 1) * HEAD_DIM, :], (h % 2) * HEAD_DIM, LANES)

    def cond(carry):
        j, go = carry
        return jnp.logical_and(j >= 0, go)

    def body(carry):
        j, _ = carry
        return j - 1, step(j, False)

    lax.while_loop(cond, body, (i - 1, step(i, True)))
    for h in heads:
        o_ref[0, h * HEAD_DIM:(h + 1) * HEAD_DIM, :] = acc_scr[h].astype(BF16)


def _sb_attention(kn, qvt, batch):
    nblk = qvt.shape[0]
    nb = nblk // batch
    return pl.pallas_call(
        _sb_kernel,
        grid=(batch, nb),
        in_specs=_attn_common_specs(nb, QV_SB_Q, KN_SB, QV_SB_V),
        out_specs=pl.BlockSpec((1, TB, TB), lambda b, i: (b * nb + i, 0, 0)),
        out_shape=jax.ShapeDtypeStruct((nblk, TB, TB), BF16),
        scratch_shapes=[pltpu.VMEM((N_HEADS, LANES, TB), BF16),
                        pltpu.VMEM((N_HEADS, HEAD_DIM, TB), F32),
                        pltpu.VMEM((N_HEADS, 1, TB), F32)],
        compiler_params=_cparams(2),
        name="sb_attn",
    )(qvt, kn, qvt)


def _swa_kernel(sink_ref, qt_ref, kc_ref, kp_ref, vc_ref, vp_ref, tc_ref, tp_ref, o_ref):
    i = pl.program_id(1)
    kc = kc_ref[...]
    kp = kp_ref[...]
    vc = vc_ref[0]
    vp = vp_ref[0][:, TB - SWA_WINDOW:]
    no_prev = jnp.where(i > 0, 0.0, NEG)
    group = N_HEADS // SWA_KV_HEADS
    heads = range(N_HEADS)
    qpads = [_pad_rows(qt_ref[0, h * HEAD_DIM:(h + 1) * HEAD_DIM, :], (h // group) * HEAD_DIM, LANES)
             for h in heads]
    s_cur = [jnp.dot(kc, qpads[h], preferred_element_type=F32) + tc_ref[h] for h in heads]
    s_prev = [jnp.dot(kp, qpads[h], preferred_element_type=F32) + tp_ref[h] + no_prev for h in heads]
    ms = [jnp.maximum(jnp.maximum(jnp.max(s_cur[h], axis=0, keepdims=True),
                                  jnp.max(s_prev[h], axis=0, keepdims=True)), sink_ref[h]) for h in heads]
    p_cur = [jnp.exp(s_cur[h] - ms[h]).astype(BF16) for h in heads]
    p_prev = [jnp.exp(s_prev[h] - ms[h]).astype(BF16) for h in heads]
    for h in heads:
        kv = h // group
        o = (jnp.dot(_with_ones(vc[kv * HEAD_DIM:(kv + 1) * HEAD_DIM]), p_cur[h], preferred_element_type=F32)
             + jnp.dot(_with_ones(vp[kv * HEAD_DIM:(kv + 1) * HEAD_DIM]), p_prev[h],
                       preferred_element_type=F32))
        denom = o[HEAD_DIM:HEAD_DIM + 1] + jnp.exp(sink_ref[h] - ms[h])
        o_ref[0, h * HEAD_DIM:(h + 1) * HEAD_DIM, :] = (o[:HEAD_DIM] / denom).astype(BF16)


def _swa_attention(kn, qvt, tile_cur, tile_prev, sinks, batch):
    nblk = qvt.shape[0]
    nb = nblk // batch
    half = TB // SWA_WINDOW
    kv_rows = SWA_KV_HEADS * HEAD_DIM
    return pl.pallas_call(
        _swa_kernel,
        grid=(batch, nb),
        in_specs=[pl.BlockSpec(memory_space=pltpu.SMEM),
                  pl.BlockSpec((1, TB, TB), lambda b, i: (b * nb + i, QV_SWA_Q, 0)),
                  pl.BlockSpec((TB, kv_rows), lambda b, i: (b * nb + i, KN_SWA_128)),
                  pl.BlockSpec((SWA_WINDOW, kv_rows),
                               lambda b, i: (b * nb * half + jnp.maximum(half * i - 1, 0), KN_SWA_128)),
                  pl.BlockSpec((1, kv_rows, TB), lambda b, i: (b * nb + i, QV_SWA_V_128, 0)),
                  pl.BlockSpec((1, kv_rows, TB), lambda b, i: (b * nb + jnp.maximum(i - 1, 0), QV_SWA_V_128, 0)),
                  _small_spec((N_HEADS, TB, TB)),
                  _small_spec((N_HEADS, SWA_WINDOW, TB))],
        out_specs=pl.BlockSpec((1, TB, TB), lambda b, i: (b * nb + i, 0, 0)),
        out_shape=jax.ShapeDtypeStruct((nblk, TB, TB), BF16),
        compiler_params=_cparams(2),
        name="swa_attn",
    )(sinks, qvt, kn, kn, qvt, qvt, tile_cur, tile_prev)


MERGE_T = 512
MERGE_COLS = 256


def _merge_route_kernel(x_ref, g_ref, oa_ref, ob_ref, oc_ref, od_ref, wg_ref, wbr_ref, wo_ref,
                        gf_ref, wr_ref, br_ref,
                        x1_ref, h2_ref, eid_ref, gate_ref, rank_ref, cnt_ref, base_scr):
    x = x_ref[...]
    h = _rms(x, g_ref[...]).astype(BF16)
    d = x.shape[1]
    chunks = []
    for n0 in range(0, d, MERGE_COLS):
        acc = None
        for bi, o_ref in enumerate((oa_ref, ob_ref, oc_ref, od_ref)):
            gate = jax.nn.sigmoid(jnp.dot(h, wg_ref[bi, :, n0:n0 + MERGE_COLS], preferred_element_type=F32))
            branch = jnp.concatenate(
                [lax.dot_general(o_ref[s], wbr_ref[bi, :, n0:n0 + MERGE_COLS], (((0,), (0,)), ((), ())),
                                 preferred_element_type=F32) for s in range(MERGE_T // TB)], axis=0)
            term = gate * branch
            acc = term if acc is None else acc + term
        chunks.append(acc.astype(BF16))
    merged = jnp.concatenate(chunks, axis=1)
    x1 = x + jnp.dot(merged, wo_ref[...], preferred_element_type=F32)
    x1_ref[...] = x1
    _route(x1, gf_ref, wr_ref, br_ref, h2_ref, eid_ref, gate_ref, rank_ref, cnt_ref, base_scr)


ROUTER_T = MERGE_T
ROUTER_ROWS = 8 + N_EXPERTS


def _first_argmax_rows(v, n_rows):
    best = jnp.max(v, axis=0, keepdims=True)
    ids = lax.broadcasted_iota(jnp.int32, v.shape, 0)
    return best, jnp.min(jnp.where(v == best, ids, n_rows), axis=0, keepdims=True)


def _route(x1, g_ref, w_ref, b_ref, h2_ref, eid_ref, gate_ref, rank_ref, cnt_ref, base_scr):
    i = pl.program_id(0)

    @pl.when(i == 0)
    def _():
        base_scr[...] = jnp.zeros(base_scr.shape, F32)

    h2 = _rms(x1, g_ref[...])
    h2_ref[...] = _pack_bf16_pairs(h2)
    nt = (((1,), (1,)), ((), ()))
    logits = lax.dot_general(w_ref[...], h2.astype(BF16), nt, preferred_element_type=F32) + b_ref[...]
    gl = logits[0:8]
    gmax, grp = _first_argmax_rows(gl, 8)
    p_grp = 1.0 / jnp.sum(jnp.exp(gl - gmax), axis=0, keepdims=True)
    e_sel = jnp.zeros((EXPERTS_PER_GROUP, ROUTER_T), F32)
    for g in range(N_GROUPS):
        e_sel = jnp.where(grp == g, logits[8 + 8 * g:16 + 8 * g], e_sel)
    ids8 = lax.broadcasted_iota(jnp.int32, e_sel.shape, 0)
    v1, i1 = _first_argmax_rows(e_sel, EXPERTS_PER_GROUP)
    e_rest = jnp.where(ids8 == i1, -jnp.inf, e_sel)
    v2, i2 = _first_argmax_rows(e_rest, EXPERTS_PER_GROUP)
    r = jnp.exp(v2 - v1)
    s1 = 1.0 / (1.0 + r)
    gate_ref[0:1, :] = p_grp * s1
    gate_ref[1:2, :] = p_grp * (r * s1)
    e1 = grp * EXPERTS_PER_GROUP + i1
    e2 = grp * EXPERTS_PER_GROUP + i2
    eid_ref[0:1, :] = e1
    eid_ref[1:2, :] = e2

    ids_e = lax.broadcasted_iota(jnp.int32, (N_EXPERTS, ROUTER_T), 0)
    oh1 = ids_e == e1
    oh2 = ids_e == e2
    cnt = oh1.astype(F32) + oh2.astype(F32)
    tr = lax.broadcasted_iota(jnp.int32, (ROUTER_T, ROUTER_T), 0)
    tc = lax.broadcasted_iota(jnp.int32, (ROUTER_T, ROUTER_T), 1)
    before = (tr < tc).astype(BF16)
    prefix = jnp.dot(cnt.astype(BF16), before, preferred_element_type=F32) + base_scr[:, 0:1]
    rank_ref[0:1, :] = jnp.sum(jnp.where(oh1, prefix, 0.0), axis=0, keepdims=True).astype(jnp.int32)
    rank_ref[1:2, :] = jnp.sum(jnp.where(oh2, prefix, 0.0), axis=0, keepdims=True).astype(jnp.int32)
    base_scr[...] = base_scr[...] + jnp.sum(cnt, axis=1, keepdims=True)
    cnt_ref[...] = base_scr[...]


def _merge_route(x2, g, o_a, o_b, o_c, o_d, wg, wbr, wo, g_ffn, w_route, b_route):
    n, d = x2.shape
    o_spec = pl.BlockSpec((MERGE_T // TB, TB, TB), lambda i: (i, 0, 0))
    row2 = lambda dt: jax.ShapeDtypeStruct((2, n), dt)
    spec2 = pl.BlockSpec((2, MERGE_T), lambda i: (0, i))
    return pl.pallas_call(
        _merge_route_kernel,
        grid=(n // MERGE_T,),
        in_specs=[pl.BlockSpec((MERGE_T, d), lambda i: (i, 0)),
                  pl.BlockSpec((1, d), lambda i: (0, 0)),
                  o_spec, o_spec, o_spec, o_spec,
                  pl.BlockSpec(wg.shape, lambda i: (0, 0, 0)),
                  pl.BlockSpec(wbr.shape, lambda i: (0, 0, 0)),
                  pl.BlockSpec(wo.shape, lambda i: (0, 0)),
                  pl.BlockSpec((1, d), lambda i: (0, 0)),
                  pl.BlockSpec((ROUTER_ROWS, d), lambda i: (0, 0)),
                  pl.BlockSpec((ROUTER_ROWS, 1), lambda i: (0, 0))],
        out_specs=[pl.BlockSpec((MERGE_T, d), lambda i: (i, 0)),
                   pl.BlockSpec((MERGE_T, d // 2), lambda i: (i, 0)), spec2, spec2, spec2,
                   pl.BlockSpec((N_EXPERTS, LANES), lambda i: (0, 0))],
        out_shape=[jax.ShapeDtypeStruct((n, d), F32),
                   jax.ShapeDtypeStruct((n, d // 2), jnp.uint32), row2(jnp.int32), row2(F32), row2(jnp.int32),
                   jax.ShapeDtypeStruct((N_EXPERTS, LANES), F32)],
        scratch_shapes=[pltpu.VMEM((N_EXPERTS, LANES), F32)],
        compiler_params=_cparams(1),
        name="merge_route",
    )(x2, g, o_a, o_b, o_c, o_d, wg, wbr, wo, g_ffn, w_route, b_route)


def _expert_kernel(be_ref, nu_ref, nv_ref, x_ref, w1_ref, w3_ref, w2_ref, y_ref):
    del be_ref
    used = pl.program_id(0) < nu_ref[0]

    @pl.when(used)
    def _():
        live = lax.broadcasted_iota(jnp.int32, x_ref.shape, 0) < nv_ref[pl.program_id(0)]
        xb = _unpack_bf16_pairs(jnp.where(live, x_ref[...], jnp.uint32(0))).astype(BF16)
        a = jnp.dot(xb, w1_ref[...].astype(BF16), preferred_element_type=F32)
        b = jnp.dot(xb, w3_ref[...].astype(BF16), preferred_element_type=F32)
        mid = (a * jax.nn.sigmoid(a) * b).astype(BF16)
        y_ref[...] = _pack_bf16_pairs(jnp.dot(mid, w2_ref[...].astype(BF16), preferred_element_type=F32))

    @pl.when(jnp.logical_not(used))
    def _():
        y_ref[...] = jnp.zeros(y_ref.shape, jnp.uint32)


def _experts(blk_expert, n_used, n_valid, xbuf, w1, w3, w2, layer):
    r = xbuf.shape[0]
    d = w1.shape[-2]
    de = w1.shape[-1]
    assert xbuf.shape[1] * 2 == d
    row_map = lambda i, be, nu, nv: (jnp.minimum(i, nu[0] - 1), 0)
    grid_spec = pltpu.PrefetchScalarGridSpec(
        num_scalar_prefetch=3,
        grid=(r // MOE_ROWS,),
        in_specs=[pl.BlockSpec((MOE_ROWS, d // 2), row_map),
                  pl.BlockSpec((None, None, d, de), lambda i, be, nu, nv: (layer, be[i], 0, 0)),
                  pl.BlockSpec((None, None, d, de), lambda i, be, nu, nv: (layer, be[i], 0, 0)),
                  pl.BlockSpec((None, None, de, d), lambda i, be, nu, nv: (layer, be[i], 0, 0))],
        out_specs=pl.BlockSpec((MOE_ROWS, d // 2), lambda i, be, nu, nv: (i, 0)),
    )
    return pl.pallas_call(
        _expert_kernel,
        grid_spec=grid_spec,
        out_shape=jax.ShapeDtypeStruct((r, d // 2), jnp.uint32),
        compiler_params=_cparams(1),
        name="experts",
    )(blk_expert, n_used, n_valid, xbuf, w1, w3, w2)


SC_CORES = 2
SC_SUBCORES = 16
SC_ROWS = 64


def _sc_gather_rows(table, idx):
    n_idx = idx.shape[0]
    d = table.shape[1]
    n_workers = SC_CORES * SC_SUBCORES
    per_worker = n_idx // n_workers
    n_chunk = per_worker // SC_ROWS
    assert per_worker * n_workers == n_idx and n_chunk * SC_ROWS == per_worker and n_chunk % 2 == 0
    mesh = plsc.VectorSubcoreMesh(core_axis_name="c", subcore_axis_name="s",
                                  num_cores=SC_CORES, num_subcores=SC_SUBCORES)

    def body(table_hbm, idx_hbm, out_hbm, idx_v, rows_v, gsem, wsem):
        worker = lax.axis_index("s") * SC_CORES + lax.axis_index("c")
        base = worker * per_worker
        pltpu.sync_copy(idx_hbm.at[pl.ds(base, per_worker)], idx_v)

        def gather(c, b):
            return pltpu.make_async_copy(table_hbm.at[idx_v.at[pl.ds(c * SC_ROWS, SC_ROWS)]],
                                         rows_v.at[b], gsem.at[b])

        def put(c, b):
            return pltpu.make_async_copy(rows_v.at[b], out_hbm.at[pl.ds(base + c * SC_ROWS, SC_ROWS)],
                                         wsem.at[b])

        gather(0, 0).start()

        @pl.loop(0, n_chunk, step=2)
        def _(c0):
            for b in range(2):
                c = c0 + b
                gather(c, b).wait()

                @pl.when(c + 1 < n_chunk)
                def _():
                    @pl.when(c >= 1)
                    def _():
                        put(c - 1, 1 - b).wait()

                    gather(c + 1, 1 - b).start()

                put(c, b).start()

        put(n_chunk - 2, 0).wait()
        put(n_chunk - 1, 1).wait()

    return pl.kernel(
        body,
        out_type=jax.ShapeDtypeStruct((n_idx, d), table.dtype),
        mesh=mesh,
        scratch_types=[pltpu.VMEM((per_worker,), jnp.int32),
                       pltpu.VMEM((2, SC_ROWS, d), table.dtype),
                       pltpu.SemaphoreType.DMA((2,)),
                       pltpu.SemaphoreType.DMA((2,))],
        name="sc_gather_rows",
    )(table, idx)


def _sc_scatter_rows(rows, dest, n_out):
    n, d = rows.shape
    n_workers = SC_CORES * SC_SUBCORES
    per_worker = n // n_workers
    n_chunk = per_worker // SC_ROWS
    assert per_worker * n_workers == n and n_chunk * SC_ROWS == per_worker and n_chunk % 2 == 0
    mesh = plsc.VectorSubcoreMesh(core_axis_name="c", subcore_axis_name="s",
                                  num_cores=SC_CORES, num_subcores=SC_SUBCORES)
    dest3 = dest.reshape(2, n // SC_ROWS, SC_ROWS)

    def body(rows_hbm, idx_hbm, out_hbm, idx_v, rows_v, lsem, ssem):
        worker = lax.axis_index("s") * SC_CORES + lax.axis_index("c")
        for k in range(2):
            pltpu.sync_copy(idx_hbm.at[k, pl.ds(worker * n_chunk, n_chunk)], idx_v.at[k])

        def load(c, b):
            return pltpu.make_async_copy(rows_hbm.at[pl.ds(worker * per_worker + c * SC_ROWS, SC_ROWS)],
                                         rows_v.at[b], lsem.at[b])

        def scatter(c, b, k):
            return pltpu.make_async_copy(rows_v.at[b], out_hbm.at[idx_v.at[k, c]], ssem.at[b])

        load(0, 0).start()

        @pl.loop(0, n_chunk, step=2)
        def _(c0):
            for b in range(2):
                c = c0 + b
                load(c, b).wait()

                @pl.when(c + 1 < n_chunk)
                def _():
                    @pl.when(c >= 1)
                    def _():
                        scatter(c - 1, 1 - b, 0).wait()
                        scatter(c - 1, 1 - b, 1).wait()

                    load(c + 1, 1 - b).start()

                scatter(c, b, 0).start()
                scatter(c, b, 1).start()

        for c, b in ((n_chunk - 2, 0), (n_chunk - 1, 1)):
            scatter(c, b, 0).wait()
            scatter(c, b, 1).wait()

    return pl.kernel(
        body,
        out_type=jax.ShapeDtypeStruct((n_out, d), rows.dtype),
        mesh=mesh,
        scratch_types=[pltpu.VMEM((2, n_chunk, SC_ROWS), jnp.int32),
                       pltpu.VMEM((2, SC_ROWS, d), rows.dtype),
                       pltpu.SemaphoreType.DMA((2,)),
                       pltpu.SemaphoreType.DMA((2,))],
        name="sc_scatter_rows",
    )(rows, dest3)


def _combine_dense_kernel(final, gate_ref, x1_ref, gf_ref, y0_ref, y1_ref, out_ref):
    gate = gate_ref[...]
    out = (x1_ref[...] + gate[:, 0:1] * _unpack_bf16_pairs(y0_ref[...])
           + gate[:, 1:2] * _unpack_bf16_pairs(y1_ref[...]))
    if final:
        out = _rms(out, gf_ref[...])
    out_ref[...] = out


def _combine_dense(gate_t, x1, g_final, yg, final):
    n, d = x1.shape
    nb = n // MERGE_T
    return pl.pallas_call(
        functools.partial(_combine_dense_kernel, final),
        grid=(nb,),
        in_specs=[pl.BlockSpec((MERGE_T, 2), lambda i: (i, 0)),
                  pl.BlockSpec((MERGE_T, d), lambda i: (i, 0)),
                  pl.BlockSpec((1, d), lambda i: (0, 0)),
                  pl.BlockSpec((MERGE_T, d // 2), lambda i: (i, 0)),
                  pl.BlockSpec((MERGE_T, d // 2), lambda i: (i + nb, 0))],
        out_specs=pl.BlockSpec((MERGE_T, d), lambda i: (i, 0)),
        out_shape=jax.ShapeDtypeStruct((n, d), F32),
        compiler_params=_cparams(1),
        name="combine_dense",
    )(gate_t, x1, g_final, yg, yg)


def _projection_weights(w):
    d = w.shape[0]
    blk = N_HEADS * HEAD_DIM
    kv = SWA_KV_HEADS * HEAD_DIM
    pa, pb, pc, pd = w[:, :3 * blk], w[:, 3 * blk:6 * blk], w[:, 6 * blk:9 * blk], w[:, 9 * blk:]
    half = N_HEADS * DIFF_QK

    def per_head(a, b):
        return jnp.stack([a.reshape(d, N_HEADS, DIFF_QK), b.reshape(d, N_HEADS, DIFF_QK)], axis=2).reshape(d, blk)

    s64, s32 = HEAD_DIM ** -0.5, DIFF_QK ** -0.5
    wn = jnp.concatenate([pa[:, blk:2 * blk], per_head(pb[:, 2 * half:3 * half], pb[:, 3 * half:4 * half]),
                          pc[:, blk:2 * blk], pd[:, blk:blk + kv]], axis=1)
    wt = jnp.concatenate([pa[:, :blk] * (s64 * LOG2E), pa[:, 2 * blk:],
                          per_head(pb[:, :half], pb[:, half:2 * half]) * (s32 * LOG2E), pb[:, 4 * half:],
                          pc[:, :blk] * (s64 * LOG2E), pc[:, 2 * blk:],
                          pd[:, :blk] * s64, pd[:, blk + kv:]], axis=1)
    assert wn.shape[1] == KN_COLS and wt.shape[1] == QV_ROWS
    return wn.astype(BF16), wt.T.astype(BF16)


def _router_weights(w_rg, b_rg, w_re, b_re):
    d = w_rg.shape[0]
    w = jnp.concatenate([w_rg.T, jnp.zeros((8 - N_GROUPS, d), F32), w_re.T], axis=0)
    b = jnp.concatenate([b_rg.astype(F32), jnp.full((8 - N_GROUPS,), NEG, F32), b_re.astype(F32)])[:, None]
    return w.astype(BF16), b


def _moe_plan(eid, rank, counts, n_rows_total):
    padded = (counts + MOE_ROWS - 1) // MOE_ROWS * MOE_ROWS
    pad_end = jnp.cumsum(padded)
    pad_start = pad_end - padded
    experts = jnp.arange(N_EXPERTS, dtype=jnp.int32)
    start_of = jnp.sum(jnp.where(eid[..., None] == experts, pad_start, 0), axis=-1)
    dest = start_of + rank
    n_blk = n_rows_total // MOE_ROWS
    n_used = (pad_end[-1] // MOE_ROWS).astype(jnp.int32)
    blk = jnp.minimum(jnp.arange(n_blk, dtype=jnp.int32), n_used - 1) * MOE_ROWS
    blk_expert = jnp.minimum(jnp.sum(pad_end[None, :] <= blk[:, None], axis=1), N_EXPERTS - 1).astype(jnp.int32)
    live_end = jnp.sum(jnp.where(blk_expert[:, None] == experts, pad_start + counts, 0), axis=-1)
    n_valid = jnp.clip(live_end - blk, 0, MOE_ROWS).astype(jnp.int32)
    return dest.astype(jnp.int32), blk_expert, n_used.reshape(1), n_valid


def kernel(x, rel_bias, g_mix, w_in, diff_lq1, diff_lk1, diff_lq2, diff_lk2, diff_subln, swa_sinks,
           w_gate, w_br, w_o, g_ffn, w_route_group, b_route_group, w_route_expert, b_route_expert,
           w1, w3, w2, g_final):
    batch, seq, d = x.shape
    n = batch * seq
    depth = w_in.shape[0]
    assert seq % TB == 0 and n % MERGE_T == 0 and TB == MOBA_BLOCK
    tab = rel_bias.T.astype(F32)
    tiles_moba = _causal_bias_tiles(tab[:N_HEADS])
    tiles_diff = _causal_bias_tiles(tab[N_HEADS:2 * N_HEADS])
    tile_cur, tile_prev = _swa_bias_tiles(tab[2 * N_HEADS:])
    n_rows_total = n * 2 + N_EXPERTS * MOE_ROWS
    row = lambda v: v.astype(F32)[None, :]

    x2 = x.reshape(n, d)
    for l in range(depth):
        lam_init = 0.8 - 0.6 * math.exp(-0.3 * l)
        wn, wt = _projection_weights(w_in[l])
        kn, qvt = _inproj(x2, row(g_mix[l]), wn, wt)
        o_a = _moba_attention(kn, qvt, tiles_moba, batch)
        o_b = _diff_attention(kn, qvt, tiles_diff, row(diff_lq1[l]), row(diff_lk1[l]), row(diff_lq2[l]),
                              row(diff_lk2[l]), diff_subln[l].astype(F32)[:, None], lam_init, batch)
        o_c = _sb_attention(kn, qvt, batch)
        o_d = _swa_attention(kn, qvt, tile_cur, tile_prev, swa_sinks[l].astype(F32), batch)
        w_route, r_bias = _router_weights(w_route_group[l], b_route_group[l], w_route_expert[l],
                                          b_route_expert[l])
        x1, h2, eid, gate, rank, cnt = _merge_route(
            x2, row(g_mix[l]), o_a, o_b, o_c, o_d, w_gate[l].astype(BF16), w_br[l].astype(BF16),
            w_o[l].astype(BF16), row(g_ffn[l]), w_route, r_bias)
        dest, blk_expert, n_used, n_valid = _moe_plan(eid, rank, cnt[:, 0].astype(jnp.int32), n_rows_total)
        xbuf = _sc_scatter_rows(h2, dest, n_rows_total)
        y = _experts(blk_expert, n_used, n_valid, xbuf, w1, w3, w2, l)
        yg = _sc_gather_rows(y, dest.reshape(-1))
        x2 = _combine_dense(gate.T, x1, row(g_final), yg, l == depth - 1)
    return x2.reshape(batch, seq, d)
```

```python
import functools
import math

import numpy as np
import jax
import jax.numpy as jnp
from jax import lax
from jax.experimental import pallas as pl
from jax.experimental.pallas import tpu as pltpu
from jax.experimental.pallas import tpu_sc as plsc

F32 = jnp.float32
BF16 = jnp.bfloat16

HEAD_DIM = 64
N_HEADS = 4
DIFF_QK = 32
SWA_KV_HEADS = 2
SWA_WINDOW = 128
MOBA_BLOCK = 256
MOBA_TOPK = 3
REL_BUCKETS = 32
REL_MAX_DIST = 128
N_GROUPS = 4
EXPERTS_PER_GROUP = 8
N_EXPERTS = N_GROUPS * EXPERTS_PER_GROUP
NORM_EPS = 1e-6

TB = 256
LANES = 128
ONES_ROWS = 16
NEG = -1e30
LOG2E = math.log2(math.e)
SB_EXIT = 104.0
MOE_ROWS = 512
VMEM_LIMIT = 56 * 1024 * 1024

QV_MOBA_Q, QV_MOBA_V, QV_DIFF_Q, QV_DIFF_V, QV_SB_Q, QV_SB_V, QV_SWA_Q = range(7)
QV_SWA_V_128 = 14
QV_ROWS = 7 * 256 + 128
KN_MOBA, KN_DIFF, KN_SB = range(3)
KN_SWA_128 = 6
KN_COLS = 3 * 256 + 128


def _cparams(n_grid):
    return pltpu.CompilerParams(dimension_semantics=("arbitrary",) * n_grid,
                                vmem_limit_bytes=VMEM_LIMIT)


def _rel_bucket_np(n):
    n = np.maximum(n, 0)
    max_exact = REL_BUCKETS // 2
    nf = np.maximum(n, 1).astype(np.float64)
    large = max_exact + (np.log(nf / max_exact) / math.log(REL_MAX_DIST / max_exact)
                         * (REL_BUCKETS - max_exact)).astype(np.int64)
    large = np.minimum(large, REL_BUCKETS - 1)
    return np.where(n < max_exact, n, large)


def _first_far_distance():
    d = np.arange(0, 4 * REL_MAX_DIST)
    b = _rel_bucket_np(d)
    return int(np.min(d[b == REL_BUCKETS - 1]))


def _toeplitz_bias(tab, rows, cols, base, valid_fn, shift_far, unit):
    length = rows + cols - 1
    off = np.concatenate([np.arange(0, cols), np.arange(cols - length, 0)])
    n = base + off
    onehot = np.zeros((REL_BUCKETS, length), np.float32)
    onehot[_rel_bucket_np(n), np.arange(length)] = 1.0
    vec = jnp.dot(tab, jnp.asarray(onehot), precision=lax.Precision.HIGHEST)
    if shift_far:
        vec = vec - tab[:, REL_BUCKETS - 1:]
    vec = jnp.where(jnp.asarray(valid_fn(n))[None, :], vec * unit, NEG).astype(F32)
    flat = jnp.tile(vec, (1, rows))[:, :rows * (length - 1)]
    return flat.reshape(tab.shape[0], rows, length - 1)[:, :, :cols]


def _causal_bias_tiles(tab):
    assert _first_far_distance() <= TB + 1
    tiles = [_toeplitz_bias(tab, TB, TB, d * TB, lambda n: n >= 0, True, LOG2E) for d in range(2)]
    return jnp.stack(tiles, axis=1)


def _swa_bias_tiles(tab):
    in_window = lambda n: (n >= 0) & (n < SWA_WINDOW)
    return (_toeplitz_bias(tab, TB, TB, 0, in_window, False, 1.0),
            _toeplitz_bias(tab, SWA_WINDOW, TB, SWA_WINDOW, in_window, False, 1.0))


def _pad_rows(q, off, total):
    n, t = q.shape
    parts = []
    if off:
        parts.append(jnp.zeros((off, t), q.dtype))
    parts.append(q)
    if total - off - n:
        parts.append(jnp.zeros((total - off - n, t), q.dtype))
    return jnp.concatenate(parts, axis=0) if len(parts) > 1 else q


def _with_ones(v):
    return jnp.concatenate([v, jnp.ones((ONES_ROWS, v.shape[1]), v.dtype)], axis=0)


def _pack_bf16_pairs(x):
    w = x.shape[1] // 2
    lo = lax.bitcast_convert_type(x[:, :w].astype(BF16).astype(F32), jnp.uint32)
    hi = lax.bitcast_convert_type(x[:, w:].astype(BF16).astype(F32), jnp.uint32)
    return hi | (lo >> 16)


def _unpack_bf16_pairs(u):
    lo = lax.bitcast_convert_type(u << 16, F32)
    hi = lax.bitcast_convert_type(u & jnp.uint32(0xFFFF0000), F32)
    return jnp.concatenate([lo, hi], axis=1)


def _rms(x, g_row):
    ms = jnp.mean(x * x, axis=-1, keepdims=True)
    return x * lax.rsqrt(ms + NORM_EPS) * g_row


IN_T = 512
IN_CHUNK = 384


def _inproj_kernel(x_ref, g_ref, wn_ref, wt_ref, kn_ref, qvt_ref):
    h = _rms(x_ref[...], g_ref[...]).astype(BF16)
    kn_ref[...] = jnp.dot(h, wn_ref[...], preferred_element_type=F32).astype(BF16)
    for r0 in range(0, QV_ROWS, IN_CHUNK):
        pt = lax.dot_general(wt_ref[r0:r0 + IN_CHUNK, :], h, (((1,), (1,)), ((), ())),
                             preferred_element_type=F32)
        for s in range(IN_T // TB):
            qvt_ref[s, r0:r0 + IN_CHUNK, :] = pt[:, s * TB:(s + 1) * TB].astype(BF16)


def _inproj(x2, g, wn, wt):
    n, d = x2.shape
    return pl.pallas_call(
        _inproj_kernel,
        grid=(n // IN_T,),
        in_specs=[pl.BlockSpec((IN_T, d), lambda i: (i, 0)),
                  pl.BlockSpec((1, d), lambda i: (0, 0)),
                  pl.BlockSpec((d, KN_COLS), lambda i: (0, 0)),
                  pl.BlockSpec((QV_ROWS, d), lambda i: (0, 0))],
        out_specs=[pl.BlockSpec((IN_T, KN_COLS), lambda i: (i, 0)),
                   pl.BlockSpec((IN_T // TB, QV_ROWS, TB), lambda i: (i, 0, 0))],
        out_shape=[jax.ShapeDtypeStruct((n, KN_COLS), BF16),
                   jax.ShapeDtypeStruct((n // TB, QV_ROWS, TB), BF16)],
        compiler_params=_cparams(1),
        name="inproj",
    )(x2, g, wn, wt)


FLASH_OVERFLOW = 100.0
FLASH_SKEW = 4


class _FlashScratch:
    def __init__(self, n_chain, qpad, m, acc, over, s):
        self.n_chain, self.qpad, self.m, self.acc, self.over, self.s = n_chain, qpad, m, acc, over, s

    @staticmethod
    def shapes(n_chain):
        return [pltpu.VMEM((n_chain, LANES, TB), BF16),
                pltpu.VMEM((n_chain, 1, TB), F32),
                pltpu.VMEM((n_chain, HEAD_DIM + ONES_ROWS, TB), F32),
                pltpu.VMEM((n_chain, 1, TB), F32),
                pltpu.VMEM((n_chain, TB, TB), F32)]


def _flash_two_pass(sc, chains):
    if len(chains) > sc.n_chain:
        for k in range(0, len(chains), sc.n_chain):
            _flash_two_pass(sc, chains[k:k + sc.n_chain])
        return
    block_max = []
    for c, kblk, add, _, colmask in chains:
        s = jnp.dot(kblk, sc.qpad[c], preferred_element_type=F32)
        if add is not None:
            s = s + add
        if colmask is not None:
            s = s + colmask
        sc.s[c] = s
        block_max.append(jnp.max(s, axis=0, keepdims=True))
    for (c, _, _, vext, _), mx in zip(chains, block_max):
        m_old = sc.m[c]
        m_new = jnp.maximum(m_old, mx)
        p = jnp.exp2(sc.s[c] - m_new).astype(BF16)
        sc.acc[c] = sc.acc[c] * jnp.exp2(m_old - m_new) + jnp.dot(vext, p, preferred_element_type=F32)
        sc.m[c] = m_new


def _flash_lagged(sc, chains, n_first=0):
    def finish(c, p, mx, ref, vext, first):
        pv = jnp.dot(vext, p, preferred_element_type=F32)
        ref_new = jnp.maximum(ref, mx)
        sc.acc[c] = (sc.acc[c] + pv) * jnp.exp2(ref - ref_new)
        sc.m[c] = ref_new
        sc.over[c] = jnp.maximum(sc.over[c], jnp.abs(mx - ref) if first else mx - ref)

    pending = []
    for idx, (c, kblk, add, vext, colmask) in enumerate(chains):
        s = jnp.dot(kblk, sc.qpad[c], preferred_element_type=F32)
        if add is not None:
            s = s + add
        if len(pending) >= min(FLASH_SKEW, sc.n_chain):
            finish(*pending.pop(0))
        ref = sc.m[c]
        mx = jnp.max(s, axis=0, keepdims=True)
        if colmask is None:
            p = jnp.exp2(s - ref)
        else:
            p = jnp.exp2(s - (ref - colmask))
            mx = mx + colmask
        pending.append((c, p.astype(BF16), mx, ref, vext, idx < n_first))
    for item in pending:
        finish(*item)


def _flash_causal(sc, i, block_chains, far_group, wide_trips=False):
    def init(ref0):
        for c in range(sc.n_chain):
            sc.m[c] = jnp.full(sc.m.shape[1:], ref0, F32)
            sc.acc[c] = jnp.zeros(sc.acc.shape[1:], F32)
            sc.over[c] = jnp.full(sc.over.shape[1:], NEG, F32)

    def far_blocks(step, j, count):
        chains = []
        for k in range(count):
            chains += block_chains(j + k, None)
        step(sc, chains)

    def run_two_pass():
        init(NEG)
        _flash_two_pass(sc, block_chains(i, 0))

        @pl.when(i >= 1)
        def _():
            _flash_two_pass(sc, block_chains(i - 1, 1))

        lax.fori_loop(0, jnp.maximum(i - 1, 0), lambda j, carry: (far_blocks(_flash_two_pass, j, 1), carry)[1], 0)

    def run_single_pass():
        init(0.0)

        @pl.when(i == 0)
        def _():
            _flash_lagged(sc, block_chains(i, 0), sc.n_chain)

        head = lax.rem(i + far_group - 1, far_group) + 2
        for size in range(2, far_group + 2):
            @pl.when(jnp.logical_and(i >= 1, head == size))
            def _(size=size):
                chains = block_chains(i, 0) + block_chains(i - 1, 1)
                for k in range(size - 2):
                    chains += block_chains(i - 2 - k, None)
                _flash_lagged(sc, chains, sc.n_chain)

        n_group = jnp.where(i >= 1, (i + 1 - head) // far_group, 0)
        if not wide_trips:
            def far_group_step(t, carry):
                far_blocks(_flash_lagged, far_group * t, far_group)
                return carry

            lax.fori_loop(0, n_group, far_group_step, 0)
        else:
            odd = lax.rem(n_group, 2)

            @pl.when(odd == 1)
            def _():
                far_blocks(_flash_lagged, 0, far_group)

            def far_wide_step(t, carry):
                far_blocks(_flash_lagged, far_group * (odd + 2 * t), 2 * far_group)
                return carry

            lax.fori_loop(0, n_group // 2, far_wide_step, 0)

    run_single_pass()
    worst = sc.over[0]
    for c in range(1, sc.n_chain):
        worst = jnp.maximum(worst, sc.over[c])

    @pl.when(jnp.max(worst) > FLASH_OVERFLOW)
    def _():
        run_two_pass()


def _diff_kernel(lam_init, qt_ref, k_ref, vt_ref, tile_ref, lq1_ref, lk1_ref, lq2_ref, lk2_ref,
                 subln_ref, o_ref, *scratch):
    i = pl.program_id(1)
    sc = _FlashScratch(2 * N_HEADS, *scratch)
    acc_scr = sc.acc
    for h in range(N_HEADS):
        for mp in range(2):
            r0 = h * HEAD_DIM + mp * DIFF_QK
            sc.qpad[2 * h + mp] = _pad_rows(qt_ref[0, r0:r0 + DIFF_QK, :],
                                            (h % 2) * HEAD_DIM + mp * DIFF_QK, LANES)

    def block_chains(j, tile_idx):
        row = pl.multiple_of(j * TB, TB)
        chains = []
        for h in range(N_HEADS):
            g = h // 2
            kblk = k_ref[pl.ds(row, TB), g * LANES:(g + 1) * LANES]
            add = None if tile_idx is None else tile_ref[h, tile_idx]
            vext = _with_ones(vt_ref[j, h * HEAD_DIM:(h + 1) * HEAD_DIM, :])
            chains += [(2 * h + mp, kblk, add, vext, None) for mp in range(2)]
        return chains

    _flash_causal(sc, i, block_chains, far_group=8)

    lam =(jnp.exp(jnp.sum(lq1_ref[...] * lk1_ref[...], keepdims=True))
           - jnp.exp(jnp.sum(lq2_ref[...] * lk2_ref[...], keepdims=True)) + lam_init)
    for h in range(N_HEADS):
        a1 = acc_scr[2 * h]
        a2 = acc_scr[2 * h + 1]
        o1 = a1[:HEAD_DIM] / a1[HEAD_DIM:HEAD_DIM + 1]
        o2 = a2[:HEAD_DIM] / a2[HEAD_DIM:HEAD_DIM + 1]
        a = o1 - lam * o2
        ms = jnp.mean(a * a, axis=0, keepdims=True)
        y = a * lax.rsqrt(ms + NORM_EPS) * subln_ref[...] * (1.0 - lam_init)
        o_ref[0, h * HEAD_DIM:(h + 1) * HEAD_DIM, :] = y.astype(BF16)


def _attn_common_specs(nb, q_blk, k_blk, v_blk):
    return [pl.BlockSpec((1, TB, TB), lambda b, i: (b * nb + i, q_blk, 0)),
            pl.BlockSpec((nb * TB, TB), lambda b, i: (b, k_blk)),
            pl.BlockSpec((nb, TB, TB), lambda b, i: (b, v_blk, 0))]


def _small_spec(shape):
    return pl.BlockSpec(shape, lambda b, i: (0,) * len(shape))


def _diff_attention(kn, qvt, tiles, lq1, lk1, lq2, lk2, subln, lam_init, batch):
    nblk = qvt.shape[0]
    nb = nblk // batch
    n_chain = 2 * N_HEADS
    return pl.pallas_call(
        functools.partial(_diff_kernel, lam_init),
        grid=(batch, nb),
        in_specs=_attn_common_specs(nb, QV_DIFF_Q, KN_DIFF, QV_DIFF_V) + [
            _small_spec((N_HEADS, 2, TB, TB)),
            _small_spec((1, DIFF_QK)), _small_spec((1, DIFF_QK)),
            _small_spec((1, DIFF_QK)), _small_spec((1, DIFF_QK)),
            _small_spec((HEAD_DIM, 1))],
        out_specs=pl.BlockSpec((1, TB, TB), lambda b, i: (b * nb + i, 0, 0)),
        out_shape=jax.ShapeDtypeStruct((nblk, TB, TB), BF16),
        scratch_shapes=_FlashScratch.shapes(n_chain),
        compiler_params=_cparams(2),
        name="diff_attn",
    )(qvt, kn, qvt, tiles, lq1, lk1, lq2, lk2, subln)


def _moba_kernel(nb, nbp, qt_ref, k_ref, vt_ref, tile_ref, o_ref, kmean_scr, sel_scr, *scratch):
    i = pl.program_id(1)
    sc = _FlashScratch(N_HEADS, *scratch)
    qpad_scr, acc_scr = sc.qpad, sc.acc

    @pl.when(i == 0)
    def _():
        kmean_scr[...] = jnp.zeros(kmean_scr.shape, F32)
        for jb in range(nb):
            blk = k_ref[jb * TB:(jb + 1) * TB, :].astype(F32)
            kmean_scr[jb:jb + 1, :] = jnp.mean(blk, axis=0, keepdims=True)

    for h in range(N_HEADS):
        qpad_scr[h] = _pad_rows(qt_ref[0, h * HEAD_DIM:(h + 1) * HEAD_DIM, :], (h % 2) * HEAD_DIM, LANES)

    blk_id = lax.broadcasted_iota(jnp.int32, (nbp, TB), 0)
    for h in range(N_HEADS):
        g = h // 2
        km = kmean_scr[:, g * LANES:(g + 1) * LANES].astype(BF16)
        gate = jnp.dot(km, qpad_scr[h], preferred_element_type=F32)
        avail = blk_id < i
        sel = jnp.zeros((nbp, TB), jnp.bool_)
        for _ in range(MOBA_TOPK):
            gm = jnp.where(avail, gate, -jnp.inf)
            best = jnp.max(gm, axis=0, keepdims=True)
            is_best = avail & (gm == best)
            first = jnp.min(jnp.where(is_best, blk_id, nbp), axis=0, keepdims=True)
            pick = blk_id == first
            sel = sel | pick
            avail = avail & jnp.logical_not(pick)
        sel_scr[h] = jnp.where(sel, 0.0, NEG).astype(F32)

    def block_chains(j, tile_idx):
        row = pl.multiple_of(j * TB, TB)
        chains = []
        for h in range(N_HEADS):
            g = h // 2
            kblk = k_ref[pl.ds(row, TB), g * LANES:(g + 1) * LANES]
            add = None if tile_idx is None else tile_ref[h, tile_idx]
            colmask = None if tile_idx == 0 else sel_scr[h, pl.ds(j, 1), :]
            chains.append((h, kblk, add, _with_ones(vt_ref[j, h * HEAD_DIM:(h + 1) * HEAD_DIM, :]), colmask))
        return chains

    _flash_causal(sc, i, block_chains, far_group=8, wide_trips=True)

    for h in range(N_HEADS):
        a = acc_scr[h]
        o_ref[0, h * HEAD_DIM:(h + 1) * HEAD_DIM, :] = (a[:HEAD_DIM] / a[HEAD_DIM:HEAD_DIM + 1]).astype(BF16)


def _moba_attention(kn, qvt, tiles, batch):
    nblk = qvt.shape[0]
    nb = nblk // batch
    nbp = max(8, -(-nb // 8) * 8)
    return pl.pallas_call(
        functools.partial(_moba_kernel, nb, nbp),
        grid=(batch, nb),
        in_specs=_attn_common_specs(nb, QV_MOBA_Q, KN_MOBA, QV_MOBA_V) + [
            _small_spec((N_HEADS, 2, TB, TB))],
        out_specs=pl.BlockSpec((1, TB, TB), lambda b, i: (b * nb + i, 0, 0)),
        out_shape=jax.ShapeDtypeStruct((nblk, TB, TB), BF16),
        scratch_shapes=[pltpu.VMEM((nbp, TB), F32),
                        pltpu.VMEM((N_HEADS, nbp, TB), F32)] + _FlashScratch.shapes(N_HEADS),
        compiler_params=_cparams(2),
        name="moba_attn",
    )(qvt, kn, qvt, tiles)


def _sb_kernel(qt_ref, k_ref, vt_ref, o_ref, qpad_scr, acc_scr, c_scr):
    i = pl.program_id(1)
    rows = lax.broadcasted_iota(jnp.int32, (TB, TB), 0)
    cols = lax.broadcasted_iota(jnp.int32, (TB, TB), 1)
    upper = (cols > rows).astype(BF16)
    strict = cols > rows

    heads = range(N_HEADS)

    def step(j, first):
        row = pl.multiple_of(j * TB, TB)
        zs = [jnp.dot(k_ref[pl.ds(row, TB), (h // 2) * LANES:(h // 2 + 1) * LANES], qpad_scr[h],
                      preferred_element_type=F32) for h in heads]
        log_keeps = []
        for z in zs:
            lk = -(jnp.maximum(z, 0.0) + jnp.log2(1.0 + jnp.exp2(-jnp.abs(z))))
            log_keeps.append(jnp.where(strict, lk, 0.0) if first else lk)
        laters = []
        for h in heads:
            hi = log_keeps[h].astype(BF16)
            lo = (log_keeps[h] - hi.astype(F32)).astype(BF16)
            later = (jnp.dot(upper, hi, preferred_element_type=F32)
                     + jnp.dot(upper, lo, preferred_element_type=F32))
            laters.append(later if first else later + c_scr[h])
        c_max = None
        for h in heads:
            w = jnp.exp2(zs[h] + log_keeps[h] + laters[h])
            if first:
                w = jnp.where(strict, w, 0.0)
            pv = jnp.dot(vt_ref[j, h * HEAD_DIM:(h + 1) * HEAD_DIM, :], w.astype(BF16),
                         preferred_element_type=F32)
            acc_scr[h] = pv if first else acc_scr[h] + pv
            c_new = laters[h][0:1, :] + log_keeps[h][0:1, :]
            c_scr[h] = c_new
            c_max = c_new if c_max is None else jnp.maximum(c_max, c_new)
        return jnp.max(c_max) > -SB_EXIT * LOG2E

    for h in heads:
        qpad_scr[h] = _pad_rows(qt_ref[0, h * HEAD_DIM:(h + 1) * HEAD_DIM, :], (h % 2) * HEAD_DIM, LANES)

    def cond(carry):
        j, go = carry
        return jnp.logical_and(j >= 0, go)

    def body(carry):
        j, _ = carry
        return j - 1, step(j, False)

    lax.while_loop(cond, body, (i - 1, step(i, True)))
    for h in heads:
        o_ref[0, h * HEAD_DIM:(h + 1) * HEAD_DIM, :] = acc_scr[h].astype(BF16)


def _sb_attention(kn, qvt, batch):
    nblk = qvt.shape[0]
    nb = nblk // batch
    return pl.pallas_call(
        _sb_kernel,
        grid=(batch, nb),
        in_specs=_attn_common_specs(nb, QV_SB_Q, KN_SB, QV_SB_V),
        out_specs=pl.BlockSpec((1, TB, TB), lambda b, i: (b * nb + i, 0, 0)),
        out_shape=jax.ShapeDtypeStruct((nblk, TB, TB), BF16),
        scratch_shapes=[pltpu.VMEM((N_HEADS, LANES, TB), BF16),
                        pltpu.VMEM((N_HEADS, HEAD_DIM, TB), F32),
                        pltpu.VMEM((N_HEADS, 1, TB), F32)],
        compiler_params=_cparams(2),
        name="sb_attn",
    )(qvt, kn, qvt)


def _swa_kernel(sink_ref, qt_ref, kc_ref, kp_ref, vc_ref, vp_ref, tc_ref, tp_ref, o_ref):
    i = pl.program_id(1)
    kc = kc_ref[...]
    kp = kp_ref[...]
    vc = vc_ref[0]
    vp = vp_ref[0][:, TB - SWA_WINDOW:]
    no_prev = jnp.where(i > 0, 0.0, NEG)
    group = N_HEADS // SWA_KV_HEADS
    heads = range(N_HEADS)
    qpads = [_pad_rows(qt_ref[0, h * HEAD_DIM:(h + 1) * HEAD_DIM, :], (h // group) * HEAD_DIM, LANES)
             for h in heads]
    s_cur = [jnp.dot(kc, qpads[h], preferred_element_type=F32) + tc_ref[h] for h in heads]
    s_prev = [jnp.dot(kp, qpads[h], preferred_element_type=F32) + tp_ref[h] + no_prev for h in heads]
    ms = [jnp.maximum(jnp.maximum(jnp.max(s_cur[h], axis=0, keepdims=True),
                                  jnp.max(s_prev[h], axis=0, keepdims=True)), sink_ref[h]) for h in heads]
    p_cur = [jnp.exp(s_cur[h] - ms[h]).astype(BF16) for h in heads]
    p_prev = [jnp.exp(s_prev[h] - ms[h]).astype(BF16) for h in heads]
    for h in heads:
        kv = h // group
        o = (jnp.dot(_with_ones(vc[kv * HEAD_DIM:(kv + 1) * HEAD_DIM]), p_cur[h], preferred_element_type=F32)
             + jnp.dot(_with_ones(vp[kv * HEAD_DIM:(kv + 1) * HEAD_DIM]), p_prev[h],
                       preferred_element_type=F32))
        denom = o[HEAD_DIM:HEAD_DIM + 1] + jnp.exp(sink_ref[h] - ms[h])
        o_ref[0, h * HEAD_DIM:(h + 1) * HEAD_DIM, :] = (o[:HEAD_DIM] / denom).astype(BF16)


def _swa_attention(kn, qvt, tile_cur, tile_prev, sinks, batch):
    nblk = qvt.shape[0]
    nb = nblk // batch
    half = TB // SWA_WINDOW
    kv_rows = SWA_KV_HEADS * HEAD_DIM
    return pl.pallas_call(
        _swa_kernel,
        grid=(batch, nb),
        in_specs=[pl.BlockSpec(memory_space=pltpu.SMEM),
                  pl.BlockSpec((1, TB, TB), lambda b, i: (b * nb + i, QV_SWA_Q, 0)),
                  pl.BlockSpec((TB, kv_rows), lambda b, i: (b * nb + i, KN_SWA_128)),
                  pl.BlockSpec((SWA_WINDOW, kv_rows),
                               lambda b, i: (b * nb * half + jnp.maximum(half * i - 1, 0), KN_SWA_128)),
                  pl.BlockSpec((1, kv_rows, TB), lambda b, i: (b * nb + i, QV_SWA_V_128, 0)),
                  pl.BlockSpec((1, kv_rows, TB), lambda b, i: (b * nb + jnp.maximum(i - 1, 0), QV_SWA_V_128, 0)),
                  _small_spec((N_HEADS, TB, TB)),
                  _small_spec((N_HEADS, SWA_WINDOW, TB))],
        out_specs=pl.BlockSpec((1, TB, TB), lambda b, i: (b * nb + i, 0, 0)),
        out_shape=jax.ShapeDtypeStruct((nblk, TB, TB), BF16),
        compiler_params=_cparams(2),
        name="swa_attn",
    )(sinks, qvt, kn, kn, qvt, qvt, tile_cur, tile_prev)


MERGE_T = 512
MERGE_COLS = 256


def _merge_route_kernel(x_ref, g_ref, oa_ref, ob_ref, oc_ref, od_ref, wg_ref, wbr_ref, wo_ref,
                        gf_ref, wr_ref, br_ref,
                        x1_ref, h2_ref, eid_ref, gate_ref, rank_ref, cnt_ref, base_scr):
    x = x_ref[...]
    h = _rms(x, g_ref[...]).astype(BF16)
    d = x.shape[1]
    chunks = []
    for n0 in range(0, d, MERGE_COLS):
        acc = None
        for bi, o_ref in enumerate((oa_ref, ob_ref, oc_ref, od_ref)):
            gate = jax.nn.sigmoid(jnp.dot(h, wg_ref[bi, :, n0:n0 + MERGE_COLS], preferred_element_type=F32))
            branch = jnp.concatenate(
                [lax.dot_general(o_ref[s], wbr_ref[bi, :, n0:n0 + MERGE_COLS], (((0,), (0,)), ((), ())),
                                 preferred_element_type=F32) for s in range(MERGE_T // TB)], axis=0)
            term = gate * branch
            acc = term if acc is None else acc + term
        chunks.append(acc.astype(BF16))
    merged = jnp.concatenate(chunks, axis=1)
    x1 = x + jnp.dot(merged, wo_ref[...], preferred_element_type=F32)
    x1_ref[...] = x1
    _route(x1, gf_ref, wr_ref, br_ref, h2_ref, eid_ref, gate_ref, rank_ref, cnt_ref, base_scr)


ROUTER_T = MERGE_T
ROUTER_ROWS = 8 + N_EXPERTS


def _first_argmax_rows(v, n_rows):
    best = jnp.max(v, axis=0, keepdims=True)
    ids = lax.broadcasted_iota(jnp.int32, v.shape, 0)
    return best, jnp.min(jnp.where(v == best, ids, n_rows), axis=0, keepdims=True)


def _route(x1, g_ref, w_ref, b_ref, h2_ref, eid_ref, gate_ref, rank_ref, cnt_ref, base_scr):
    i = pl.program_id(0)

    @pl.when(i == 0)
    def _():
        base_scr[...] = jnp.zeros(base_scr.shape, F32)

    h2 = _rms(x1, g_ref[...])
    h2_ref[...] = _pack_bf16_pairs(h2)
    nt = (((1,), (1,)), ((), ()))
    logits = lax.dot_general(w_ref[...], h2.astype(BF16), nt, preferred_element_type=F32) + b_ref[...]
    gl = logits[0:8]
    gmax, grp = _first_argmax_rows(gl, 8)
    p_grp = 1.0 / jnp.sum(jnp.exp(gl - gmax), axis=0, keepdims=True)
    e_sel = jnp.zeros((EXPERTS_PER_GROUP, ROUTER_T), F32)
    for g in range(N_GROUPS):
        e_sel = jnp.where(grp == g, logits[8 + 8 * g:16 + 8 * g], e_sel)
    ids8 = lax.broadcasted_iota(jnp.int32, e_sel.shape, 0)
    v1, i1 = _first_argmax_rows(e_sel, EXPERTS_PER_GROUP)
    e_rest = jnp.where(ids8 == i1, -jnp.inf, e_sel)
    v2, i2 = _first_argmax_rows(e_rest, EXPERTS_PER_GROUP)
    r = jnp.exp(v2 - v1)
    s1 = 1.0 / (1.0 + r)
    gate_ref[0:1, :] = p_grp * s1
    gate_ref[1:2, :] = p_grp * (r * s1)
    e1 = grp * EXPERTS_PER_GROUP + i1
    e2 = grp * EXPERTS_PER_GROUP + i2
    eid_ref[0:1, :] = e1
    eid_ref[1:2, :] = e2

    ids_e = lax.broadcasted_iota(jnp.int32, (N_EXPERTS, ROUTER_T), 0)
    oh1 = ids_e == e1
    oh2 = ids_e == e2
    cnt = oh1.astype(F32) + oh2.astype(F32)
    tr = lax.broadcasted_iota(jnp.int32, (ROUTER_T, ROUTER_T), 0)
    tc = lax.broadcasted_iota(jnp.int32, (ROUTER_T, ROUTER_T), 1)
    before = (tr < tc).astype(BF16)
    prefix = jnp.dot(cnt.astype(BF16), before, preferred_element_type=F32) + base_scr[:, 0:1]
    rank_ref[0:1, :] = jnp.sum(jnp.where(oh1, prefix, 0.0), axis=0, keepdims=True).astype(jnp.int32)
    rank_ref[1:2, :] = jnp.sum(jnp.where(oh2, prefix, 0.0), axis=0, keepdims=True).astype(jnp.int32)
    base_scr[...] = base_scr[...] + jnp.sum(cnt, axis=1, keepdims=True)
    cnt_ref[...] = base_scr[...]


def _merge_route(x2, g, o_a, o_b, o_c, o_d, wg, wbr, wo, g_ffn, w_route, b_route):
    n, d = x2.shape
    o_spec = pl.BlockSpec((MERGE_T // TB, TB, TB), lambda i: (i, 0, 0))
    row2 = lambda dt: jax.ShapeDtypeStruct((2, n), dt)
    spec2 = pl.BlockSpec((2, MERGE_T), lambda i: (0, i))
    return pl.pallas_call(
        _merge_route_kernel,
        grid=(n // MERGE_T,),
        in_specs=[pl.BlockSpec((MERGE_T, d), lambda i: (i, 0)),
                  pl.BlockSpec((1, d), lambda i: (0, 0)),
                  o_spec, o_spec, o_spec, o_spec,
                  pl.BlockSpec(wg.shape, lambda i: (0, 0, 0)),
                  pl.BlockSpec(wbr.shape, lambda i: (0, 0, 0)),
                  pl.BlockSpec(wo.shape, lambda i: (0, 0)),
                  pl.BlockSpec((1, d), lambda i: (0, 0)),
                  pl.BlockSpec((ROUTER_ROWS, d), lambda i: (0, 0)),
                  pl.BlockSpec((ROUTER_ROWS, 1), lambda i: (0, 0))],
        out_specs=[pl.BlockSpec((MERGE_T, d), lambda i: (i, 0)),
                   pl.BlockSpec((MERGE_T, d // 2), lambda i: (i, 0)), spec2, spec2, spec2,
                   pl.BlockSpec((N_EXPERTS, LANES), lambda i: (0, 0))],
        out_shape=[jax.ShapeDtypeStruct((n, d), F32),
                   jax.ShapeDtypeStruct((n, d // 2), jnp.uint32), row2(jnp.int32), row2(F32), row2(jnp.int32),
                   jax.ShapeDtypeStruct((N_EXPERTS, LANES), F32)],
        scratch_shapes=[pltpu.VMEM((N_EXPERTS, LANES), F32)],
        compiler_params=_cparams(1),
        name="merge_route",
    )(x2, g, o_a, o_b, o_c, o_d, wg, wbr, wo, g_ffn, w_route, b_route)


def _expert_kernel(be_ref, nu_ref, nv_ref, x_ref, w1_ref, w3_ref, w2_ref, y_ref):
    del be_ref
    used = pl.program_id(0) < nu_ref[0]

    @pl.when(used)
    def _():
        live = lax.broadcasted_iota(jnp.int32, x_ref.shape, 0) < nv_ref[pl.program_id(0)]
        xb = _unpack_bf16_pairs(jnp.where(live, x_ref[...], jnp.uint32(0))).astype(BF16)
        a = jnp.dot(xb, w1_ref[...].astype(BF16), preferred_element_type=F32)
        b = jnp.dot(xb, w3_ref[...].astype(BF16), preferred_element_type=F32)
        mid = (a * jax.nn.sigmoid(a) * b).astype(BF16)
        y_ref[...] = _pack_bf16_pairs(jnp.dot(mid, w2_ref[...].astype(BF16), preferred_element_type=F32))

    @pl.when(jnp.logical_not(used))
    def _():
        y_ref[...] = jnp.zeros(y_ref.shape, jnp.uint32)


def _experts(blk_expert, n_used, n_valid, xbuf, w1, w3, w2, layer):
    r = xbuf.shape[0]
    d = w1.shape[-2]
    de = w1.shape[-1]
    assert xbuf.shape[1] * 2 == d
    row_map = lambda i, be, nu, nv: (jnp.minimum(i, nu[0] - 1), 0)
    grid_spec = pltpu.PrefetchScalarGridSpec(
        num_scalar_prefetch=3,
        grid=(r // MOE_ROWS,),
        in_specs=[pl.BlockSpec((MOE_ROWS, d // 2), row_map),
                  pl.BlockSpec((None, None, d, de), lambda i, be, nu, nv: (layer, be[i], 0, 0)),
                  pl.BlockSpec((None, None, d, de), lambda i, be, nu, nv: (layer, be[i], 0, 0)),
                  pl.BlockSpec((None, None, de, d), lambda i, be, nu, nv: (layer, be[i], 0, 0))],
        out_specs=pl.BlockSpec((MOE_ROWS, d // 2), lambda i, be, nu, nv: (i, 0)),
    )
    return pl.pallas_call(
        _expert_kernel,
        grid_spec=grid_spec,
        out_shape=jax.ShapeDtypeStruct((r, d // 2), jnp.uint32),
        compiler_params=_cparams(1),
        name="experts",
    )(blk_expert, n_used, n_valid, xbuf, w1, w3, w2)


SC_CORES = 2
SC_SUBCORES = 16
SC_ROWS = 64


def _sc_gather_rows(table, idx):
    n_idx = idx.shape[0]
    d = table.shape[1]
    n_workers = SC_CORES * SC_SUBCORES
    per_worker = n_idx // n_workers
    n_chunk = per_worker // SC_ROWS
    assert per_worker * n_workers == n_idx and n_chunk * SC_ROWS == per_worker and n_chunk % 2 == 0
    mesh = plsc.VectorSubcoreMesh(core_axis_name="c", subcore_axis_name="s",
                                  num_cores=SC_CORES, num_subcores=SC_SUBCORES)

    def body(table_hbm, idx_hbm, out_hbm, idx_v, rows_v, gsem, wsem):
        worker = lax.axis_index("s") * SC_CORES + lax.axis_index("c")
        base = worker * per_worker
        pltpu.sync_copy(idx_hbm.at[pl.ds(base, per_worker)], idx_v)

        def gather(c, b):
            return pltpu.make_async_copy(table_hbm.at[idx_v.at[pl.ds(c * SC_ROWS, SC_ROWS)]],
                                         rows_v.at[b], gsem.at[b])

        def put(c, b):
            return pltpu.make_async_copy(rows_v.at[b], out_hbm.at[pl.ds(base + c * SC_ROWS, SC_ROWS)],
                                         wsem.at[b])

        gather(0, 0).start()

        @pl.loop(0, n_chunk, step=2)
        def _(c0):
            for b in range(2):
                c = c0 + b
                gather(c, b).wait()

                @pl.when(c + 1 < n_chunk)
                def _():
                    @pl.when(c >= 1)
                    def _():
                        put(c - 1, 1 - b).wait()

                    gather(c + 1, 1 - b).start()

                put(c, b).start()

        put(n_chunk - 2, 0).wait()
        put(n_chunk - 1, 1).wait()

    return pl.kernel(
        body,
        out_type=jax.ShapeDtypeStruct((n_idx, d), table.dtype),
        mesh=mesh,
        scratch_types=[pltpu.VMEM((per_worker,), jnp.int32),
                       pltpu.VMEM((2, SC_ROWS, d), table.dtype),
                       pltpu.SemaphoreType.DMA((2,)),
                       pltpu.SemaphoreType.DMA((2,))],
        name="sc_gather_rows",
    )(table, idx)


def _sc_scatter_rows(rows, dest, n_out):
    n, d = rows.shape
    n_workers = SC_CORES * SC_SUBCORES
    per_worker = n // n_workers
    n_chunk = per_worker // SC_ROWS
    assert per_worker * n_workers == n and n_chunk * SC_ROWS == per_worker and n_chunk % 2 == 0
    mesh = plsc.VectorSubcoreMesh(core_axis_name="c", subcore_axis_name="s",
                                  num_cores=SC_CORES, num_subcores=SC_SUBCORES)
    dest3 = dest.reshape(2, n // SC_ROWS, SC_ROWS)

    def body(rows_hbm, idx_hbm, out_hbm, idx_v, rows_v, lsem, ssem):
        worker = lax.axis_index("s") * SC_CORES + lax.axis_index("c")
        for k in range(2):
            pltpu.sync_copy(idx_hbm.at[k, pl.ds(worker * n_chunk, n_chunk)], idx_v.at[k])

        def load(c, b):
            return pltpu.make_async_copy(rows_hbm.at[pl.ds(worker * per_worker + c * SC_ROWS, SC_ROWS)],
                                         rows_v.at[b], lsem.at[b])

        def scatter(c, b, k):
            return pltpu.make_async_copy(rows_v.at[b], out_hbm.at[idx_v.at[k, c]], ssem.at[b])

        load(0, 0).start()

        @pl.loop(0, n_chunk, step=2)
        def _(c0):
            for b in range(2):
                c = c0 + b
                load(c, b).wait()

                @pl.when(c + 1 < n_chunk)
                def _():
                    @pl.when(c >= 1)
                    def _():
                        scatter(c - 1, 1 - b, 0).wait()
                        scatter(c - 1, 1 - b, 1).wait()

                    load(c + 1, 1 - b).start()

                scatter(c, b, 0).start()
                scatter(c, b, 1).start()

        for c, b in ((n_chunk - 2, 0), (n_chunk - 1, 1)):
            scatter(c, b, 0).wait()
            scatter(c, b, 1).wait()

    return pl.kernel(
        body,
        out_type=jax.ShapeDtypeStruct((n_out, d), rows.dtype),
        mesh=mesh,
        scratch_types=[pltpu.VMEM((2, n_chunk, SC_ROWS), jnp.int32),
                       pltpu.VMEM((2, SC_ROWS, d), rows.dtype),
                       pltpu.SemaphoreType.DMA((2,)),
                       pltpu.SemaphoreType.DMA((2,))],
        name="sc_scatter_rows",
    )(rows, dest3)


def _combine_dense_kernel(final, gate_ref, x1_ref, gf_ref, y0_ref, y1_ref, out_ref):
    gate = gate_ref[...]
    out = (x1_ref[...] + gate[:, 0:1] * _unpack_bf16_pairs(y0_ref[...])
           + gate[:, 1:2] * _unpack_bf16_pairs(y1_ref[...]))
    if final:
        out = _rms(out, gf_ref[...])
    out_ref[...] = out


def _combine_dense(gate_t, x1, g_final, yg, final):
    n, d = x1.shape
    nb = n // MERGE_T
    return pl.pallas_call(
        functools.partial(_combine_dense_kernel, final),
        grid=(nb,),
        in_specs=[pl.BlockSpec((MERGE_T, 2), lambda i: (i, 0)),
                  pl.BlockSpec((MERGE_T, d), lambda i: (i, 0)),
                  pl.BlockSpec((1, d), lambda i: (0, 0)),
                  pl.BlockSpec((MERGE_T, d // 2), lambda i: (i, 0)),
                  pl.BlockSpec((MERGE_T, d // 2), lambda i: (i + nb, 0))],
        out_specs=pl.BlockSpec((MERGE_T, d), lambda i: (i, 0)),
        out_shape=jax.ShapeDtypeStruct((n, d), F32),
        compiler_params=_cparams(1),
        name="combine_dense",
    )(gate_t, x1, g_final, yg, yg)


def _projection_weights(w):
    d = w.shape[0]
    blk = N_HEADS * HEAD_DIM
    kv = SWA_KV_HEADS * HEAD_DIM
    pa, pb, pc, pd = w[:, :3 * blk], w[:, 3 * blk:6 * blk], w[:, 6 * blk:9 * blk], w[:, 9 * blk:]
    half = N_HEADS * DIFF_QK

    def per_head(a, b):
        return jnp.stack([a.reshape(d, N_HEADS, DIFF_QK), b.reshape(d, N_HEADS, DIFF_QK)], axis=2).reshape(d, blk)

    s64, s32 = HEAD_DIM ** -0.5, DIFF_QK ** -0.5
    wn = jnp.concatenate([pa[:, blk:2 * blk], per_head(pb[:, 2 * half:3 * half], pb[:, 3 * half:4 * half]),
                          pc[:, blk:2 * blk], pd[:, blk:blk + kv]], axis=1)
    wt = jnp.concatenate([pa[:, :blk] * (s64 * LOG2E), pa[:, 2 * blk:],
                          per_head(pb[:, :half], pb[:, half:2 * half]) * (s32 * LOG2E), pb[:, 4 * half:],
                          pc[:, :blk] * (s64 * LOG2E), pc[:, 2 * blk:],
                          pd[:, :blk] * s64, pd[:, blk + kv:]], axis=1)
    assert wn.shape[1] == KN_COLS and wt.shape[1] == QV_ROWS
    return wn.astype(BF16), wt.T.astype(BF16)


def _router_weights(w_rg, b_rg, w_re, b_re):
    d = w_rg.shape[0]
    w = jnp.concatenate([w_rg.T, jnp.zeros((8 - N_GROUPS, d), F32), w_re.T], axis=0)
    b = jnp.concatenate([b_rg.astype(F32), jnp.full((8 - N_GROUPS,), NEG, F32), b_re.astype(F32)])[:, None]
    return w.astype(BF16), b


def _moe_plan(eid, rank, counts, n_rows_total):
    padded = (counts + MOE_ROWS - 1) // MOE_ROWS * MOE_ROWS
    pad_end = jnp.cumsum(padded)
    pad_start = pad_end - padded
    experts = jnp.arange(N_EXPERTS, dtype=jnp.int32)
    start_of = jnp.sum(jnp.where(eid[..., None] == experts, pad_start, 0), axis=-1)
    dest = start_of + rank
    n_blk = n_rows_total // MOE_ROWS
    n_used = (pad_end[-1] // MOE_ROWS).astype(jnp.int32)
    blk = jnp.minimum(jnp.arange(n_blk, dtype=jnp.int32), n_used - 1) * MOE_ROWS
    blk_expert = jnp.minimum(jnp.sum(pad_end[None, :] <= blk[:, None], axis=1), N_EXPERTS - 1).astype(jnp.int32)
    live_end = jnp.sum(jnp.where(blk_expert[:, None] == experts, pad_start + counts, 0), axis=-1)
    n_valid = jnp.clip(live_end - blk, 0, MOE_ROWS).astype(jnp.int32)
    return dest.astype(jnp.int32), blk_expert, n_used.reshape(1), n_valid


def kernel(x, rel_bias, g_mix, w_in, diff_lq1, diff_lk1, diff_lq2, diff_lk2, diff_subln, swa_sinks,
           w_gate, w_br, w_o, g_ffn, w_route_group, b_route_group, w_route_expert, b_route_expert,
           w1, w3, w2, g_final):
    batch, seq, d = x.shape
    n = batch * seq
    depth = w_in.shape[0]
    assert seq % TB == 0 and n % MERGE_T == 0 and TB == MOBA_BLOCK
    tab = rel_bias.T.astype(F32)
    tiles_moba = _causal_bias_tiles(tab[:N_HEADS])
    tiles_diff = _causal_bias_tiles(tab[N_HEADS:2 * N_HEADS])
    tile_cur, tile_prev = _swa_bias_tiles(tab[2 * N_HEADS:])
    n_rows_total = n * 2 + N_EXPERTS * MOE_ROWS
    row = lambda v: v.astype(F32)[None, :]

    x2 = x.reshape(n, d)
    for l in range(depth):
        lam_init = 0.8 - 0.6 * math.exp(-0.3 * l)
        wn, wt = _projection_weights(w_in[l])
        kn, qvt = _inproj(x2, row(g_mix[l]), wn, wt)
        o_a = _moba_attention(kn, qvt, tiles_moba, batch)
        o_b = _diff_attention(kn, qvt, tiles_diff, row(diff_lq1[l]), row(diff_lk1[l]), row(diff_lq2[l]),
                              row(diff_lk2[l]), diff_subln[l].astype(F32)[:, None], lam_init, batch)
        o_c = _sb_attention(kn, qvt, batch)
        o_d = _swa_attention(kn, qvt, tile_cur, tile_prev, swa_sinks[l].astype(F32), batch)
        w_route, r_bias = _router_weights(w_route_group[l], b_route_group[l], w_route_expert[l],
                                          b_route_expert[l])
        x1, h2, eid, gate, rank, cnt = _merge_route(
            x2, row(g_mix[l]), o_a, o_b, o_c, o_d, w_gate[l].astype(BF16), w_br[l].astype(BF16),
            w_o[l].astype(BF16), row(g_ffn[l]), w_route, r_bias)
        dest, blk_expert, n_used, n_valid = _moe_plan(eid, rank, cnt[:, 0].astype(jnp.int32), n_rows_total)
        xbuf = _sc_scatter_rows(h2, dest, n_rows_total)
        y = _experts(blk_expert, n_used, n_valid, xbuf, w1, w3, w2, l)
        yg = _sc_gather_rows(y, dest.reshape(-1))
        x2 = _combine_dense(gate.T, x1, row(g_final), yg, l == depth - 1)
    return x2.reshape(batch, seq, d)
```
